```python
import math
import jax, jax.numpy as jnp
from jax import lax
import numpy as np

D_MODEL = 1024
BATCH = 16
SEQ = 2048
DEPTH = 1

D_MIX = D_MODEL
GDN_HEADS = 4
GDN_HEAD_DIM = 128
GDN_WIDTH = GDN_HEADS * GDN_HEAD_DIM
DIFF_HEADS = 4
DIFF_QK_DIM = 64
DIFF_V_DIM = 2 * DIFF_QK_DIM
DIFF_WIDTH = DIFF_HEADS * DIFF_V_DIM
CONV_K = 4
CHUNK = 64
Q_BLOCK = 128
NORM_EPS = 1e-6
SPLITS = (GDN_WIDTH, GDN_WIDTH, GDN_WIDTH, GDN_WIDTH, GDN_HEADS, GDN_HEADS,
          DIFF_HEADS * 2 * DIFF_QK_DIM, DIFF_HEADS * 2 * DIFF_QK_DIM, DIFF_WIDTH, DIFF_WIDTH)
D_IN_PROJ = 4 * GDN_WIDTH + 2 * GDN_HEADS + 4 * DIFF_HEADS * 2 * DIFF_QK_DIM

kernel_name = "hybrid_gdn_diffattn_parallel_heads"


def rmsnorm(x, w, eps=NORM_EPS):
    x32 = x.astype(jnp.float32)
    y = x32 * lax.rsqrt(jnp.mean(x32 * x32, axis=-1, keepdims=True) + eps)
    return (y * w.astype(jnp.float32)).astype(x.dtype)


def l2norm(x, eps=1e-6):
    return x * lax.rsqrt(jnp.sum(x * x, axis=-1, keepdims=True) + eps)


def causal_depthwise_conv(x, w):
    k_len, c = w.shape
    return lax.conv_general_dilated(
        x, w[:, None, :].astype(x.dtype), window_strides=(1,), padding=[(k_len - 1, 0)],
        dimension_numbers=("NWC", "WIO", "NWC"), feature_group_count=c)


def alibi_slopes(n_heads):
    return jnp.asarray(2.0 ** (-8.0 * (np.arange(n_heads) + 1) / n_heads), dtype=jnp.float32)


def chunk_gated_delta_rule(q, k, v, g, beta):
    b_sz, seq_len, n_h, dk = q.shape
    dv = v.shape[-1]
    n_chunks = seq_len // CHUNK

    def to_chunks(t):
        t = jnp.moveaxis(t, 2, 1)
        return t.reshape((b_sz, n_h, n_chunks, CHUNK) + t.shape[3:])

    q, k, v, g, beta = (to_chunks(t) for t in (q, k, v, g, beta))
    q = q * (dk ** -0.5)
    g_cum = jnp.cumsum(g, axis=-1)
    idx = jnp.arange(CHUNK)
    tril = idx[:, None] >= idx[None, :]
    strict = idx[:, None] > idx[None, :]
    decay = jnp.exp(jnp.where(tril, g_cum[..., :, None] - g_cum[..., None, :], -jnp.inf))
    k_beta = k * beta[..., None]
    v_beta = v * beta[..., None]
    m = jnp.where(strict, jnp.einsum("bhncd,bhnsd->bhncs", k_beta, k) * decay, 0.0)
    eye = jnp.eye(CHUNK, dtype=jnp.float32)
    t_inv = lax.linalg.triangular_solve(eye + m, jnp.broadcast_to(eye, m.shape),
                                        left_side=True, lower=True)
    u = jnp.einsum("bhncs,bhnsd->bhncd", t_inv, v_beta)
    w = jnp.einsum("bhncs,bhnsd->bhncd", t_inv, k_beta * jnp.exp(g_cum)[..., None])
    a_qk = jnp.einsum("bhncd,bhnsd->bhncs", q, k) * decay
    q_dec = q * jnp.exp(g_cum)[..., None]
    g_last = g_cum[..., -1]
    k_dec = k * jnp.exp(g_last[..., None] - g_cum)[..., None]

    def step(state, xs):
        u_i, w_i, a_i, qd_i, kd_i, gl_i = xs
        v_new = u_i - jnp.einsum("bhcd,bhde->bhce", w_i, state)
        o_i = (jnp.einsum("bhcd,bhde->bhce", qd_i, state)
               + jnp.einsum("bhcs,bhse->bhce", a_i, v_new))
        state = (state * jnp.exp(gl_i)[..., None, None]
                 + jnp.einsum("bhcd,bhce->bhde", kd_i, v_new))
        return state, o_i

    xs = tuple(jnp.moveaxis(t, 2, 0) for t in (u, w, a_qk, q_dec, k_dec, g_last))
    s0 = jnp.zeros((b_sz, n_h, dk, dv), dtype=jnp.float32)
    _, o = lax.scan(step, s0, xs)
    o = jnp.moveaxis(o, 0, 2).reshape(b_sz, n_h, seq_len, dv)
    return jnp.moveaxis(o, 1, 2)


def diff_attention(q, k, v, slopes, lam):
    b_sz, seq_len, n_h, _, dq = q.shape
    dv = v.shape[-1]
    n_blocks = seq_len // Q_BLOCK
    scale = dq ** -0.5
    kpos = jnp.arange(seq_len)
    q_blocks = jnp.moveaxis(q.reshape(b_sz, n_blocks, Q_BLOCK, n_h, 2, dq), 1, 0)

    def block(args):
        qb, start = args
        s = jnp.einsum("bqhcd,bkhcd->bhcqk", qb, k).astype(jnp.float32) * scale
        qpos = start + jnp.arange(Q_BLOCK)
        dist = (qpos[:, None] - kpos[None, :]).astype(jnp.float32)
        bias = -slopes[:, None, None, None] * dist
        s = jnp.where(dist >= 0, s + bias[None], -jnp.inf)
        p = jax.nn.softmax(s, axis=-1)
        p = p[:, :, 0] - lam * p[:, :, 1]
        return jnp.einsum("bhqk,bkhd->bqhd", p.astype(v.dtype), v)

    starts = jnp.arange(n_blocks) * Q_BLOCK
    o = lax.map(block, (q_blocks, starts))
    return jnp.moveaxis(o, 0, 1).reshape(b_sz, seq_len, n_h, dv)


def hybrid_layer(x, norm_w, w_in, conv_w, a_log, dt_bias, gdn_norm_w,
                 lambda_q1, lambda_k1, lambda_q2, lambda_k2, diff_norm_w, w_out, layer_idx):
    b_sz, seq_len, _ = x.shape
    f32 = jnp.float32
    h = rmsnorm(x, norm_w)
    proj = h @ w_in
    split_pts = np.cumsum(SPLITS)[:-1]
    gq, gk, gv, gz, gb, ga, dq, dk, dvv, dz = jnp.split(proj, split_pts, axis=-1)

    qkv = jax.nn.silu(causal_depthwise_conv(jnp.concatenate([gq, gk, gv], axis=-1), conv_w))
    gq, gk, gv = jnp.split(qkv.astype(f32), 3, axis=-1)
    shp = (b_sz, seq_len, GDN_HEADS, GDN_HEAD_DIM)
    gq = l2norm(gq.reshape(shp))
    gk = l2norm(gk.reshape(shp))
    gv = gv.reshape(shp)
    beta = jax.nn.sigmoid(gb.astype(f32))
    g = -jnp.exp(a_log.astype(f32)) * jax.nn.softplus(ga.astype(f32) + dt_bias.astype(f32))
    o_a = chunk_gated_delta_rule(gq, gk, gv, g, beta)
    o_a = rmsnorm(o_a, gdn_norm_w) * jax.nn.silu(gz.astype(f32).reshape(shp))
    o_a = o_a.reshape(b_sz, seq_len, GDN_WIDTH).astype(x.dtype)

    lam_init = 0.8 - 0.6 * math.exp(-0.3 * layer_idx)
    lam = (jnp.exp(jnp.sum(lambda_q1.astype(f32) * lambda_k1.astype(f32)))
           - jnp.exp(jnp.sum(lambda_q2.astype(f32) * lambda_k2.astype(f32))) + lam_init)
    dq = dq.reshape(b_sz, seq_len, DIFF_HEADS, 2, DIFF_QK_DIM)
    dk = dk.reshape(b_sz, seq_len, DIFF_HEADS, 2, DIFF_QK_DIM)
    dvv = dvv.reshape(b_sz, seq_len, DIFF_HEADS, DIFF_V_DIM)
    o_b = diff_attention(dq, dk, dvv, alibi_slopes(DIFF_HEADS), lam)
    o_b = rmsnorm(o_b.astype(f32), diff_norm_w) * (1.0 - lam_init)
    o_b = o_b.reshape(b_sz, seq_len, DIFF_WIDTH) * jax.nn.silu(dz.astype(f32))
    o_b = o_b.astype(x.dtype)

    mix = jnp.concatenate([o_a, o_b], axis=-1) @ w_out
    return x + mix


def setup_inputs(seed: int = 0) -> dict:
    key = jax.random.key(seed)
    ks = jax.random.split(key, 16)
    f32 = jnp.float32
    x = jax.random.normal(ks[0], (BATCH, SEQ, D_MODEL), f32)
    norm_w = 1.0 + 0.01 * jax.random.normal(ks[1], (DEPTH, D_MODEL), f32)
    w_in = jax.random.normal(ks[2], (DEPTH, D_MODEL, D_IN_PROJ), f32) * D_MODEL ** -0.5
    conv_w = jax.random.normal(ks[3], (DEPTH, CONV_K, 3 * GDN_WIDTH), f32) * CONV_K ** -0.5
    a_log = jnp.log(jax.random.uniform(ks[4], (DEPTH, GDN_HEADS), f32, 1.0, 16.0))
    dt = jnp.exp(jax.random.uniform(ks[5], (DEPTH, GDN_HEADS), f32,
                                    math.log(1e-3), math.log(1e-1)))
    dt_bias = dt + jnp.log(-jnp.expm1(-dt))
    gdn_norm_w = 1.0 + 0.01 * jax.random.normal(ks[6], (DEPTH, GDN_HEAD_DIM), f32)
    lambda_q1 = 0.1 * jax.random.normal(ks[7], (DEPTH, DIFF_QK_DIM), f32)
    lambda_k1 = 0.1 * jax.random.normal(ks[8], (DEPTH, DIFF_QK_DIM), f32)
    lambda_q2 = 0.1 * jax.random.normal(ks[9], (DEPTH, DIFF_QK_DIM), f32)
    lambda_k2 = 0.1 * jax.random.normal(ks[10], (DEPTH, DIFF_QK_DIM), f32)
    diff_norm_w = 1.0 + 0.01 * jax.random.normal(ks[11], (DEPTH, DIFF_V_DIM), f32)
    w_out = jax.random.normal(ks[12], (DEPTH, D_MIX, D_MODEL), f32) * D_MIX ** -0.5
    final_norm_w = 1.0 + 0.01 * jax.random.normal(ks[13], (D_MODEL,), f32)
    return {"x": x, "norm_w": norm_w, "w_in": w_in, "conv_w": conv_w, "a_log": a_log,
            "dt_bias": dt_bias, "gdn_norm_w": gdn_norm_w, "lambda_q1": lambda_q1,
            "lambda_k1": lambda_k1, "lambda_q2": lambda_q2, "lambda_k2": lambda_k2,
            "diff_norm_w": diff_norm_w, "w_out": w_out, "final_norm_w": final_norm_w}


def reference(x, norm_w, w_in, conv_w, a_log, dt_bias, gdn_norm_w, lambda_q1, lambda_k1,
              lambda_q2, lambda_k2, diff_norm_w, w_out, final_norm_w):
    for i in range(DEPTH):
        x = hybrid_layer(x, norm_w[i], w_in[i], conv_w[i], a_log[i], dt_bias[i], gdn_norm_w[i],
                         lambda_q1[i], lambda_k1[i], lambda_q2[i], lambda_k2[i],
                         diff_norm_w[i], w_out[i], layer_idx=i)
    return rmsnorm(x, final_norm_w)
```

```python
import functools
import math

import jax
import jax.numpy as jnp
from jax import lax
from jax.experimental import pallas as pl
from jax.experimental.pallas import tpu as pltpu

F32 = jnp.float32
BF16 = jnp.bfloat16

D_MODEL = 1024
GDN_HEADS = 4
HEAD_DIM = 128
GDN_WIDTH = GDN_HEADS * HEAD_DIM
DIFF_HEADS = 4
DIFF_QK_DIM = 64
DIFF_WIDTH = DIFF_HEADS * HEAD_DIM
CONV_K = 4
NORM_EPS = 1e-6
N_GATE = 2 * GDN_HEADS
N_MAIN = 4 * GDN_WIDTH + 4 * DIFF_WIDTH
N_CONV = 3 * GDN_WIDTH
LANES = 128
MASK_VALUE = -1e30

COL_GQ, COL_GK, COL_GV, COL_GZ = 0, 4, 8, 12
COL_DQ, COL_DK, COL_DV, COL_DZ = 16, 20, 24, 28

TM_IN = 512
NC_IN = 256
TAIL = 16
TM_OUT = 512
SC = 256
CHUNK = 64
BQ = 256
BK = 256
VMEM_LIMIT = 48 * 1024 * 1024


def _sigmoid(x):
    return 1.0 / (1.0 + jnp.exp(-x))


def _softplus(x):
    return jnp.maximum(x, 0.0) + jnp.log1p(jnp.exp(-jnp.abs(x)))


def _dot(a, b):
    return jnp.dot(a, b, preferred_element_type=F32)


def _dot_nt(a, b):
    return lax.dot_general(a, b, (((1,), (1,)), ((), ())), preferred_element_type=F32)


def _dot_tn(a, b):
    return lax.dot_general(a, b, (((0,), (0,)), ((), ())), preferred_element_type=F32)


def _inproj_kernel(tiles_per_seq, x_ref, nw_ref, w_ref, wg_ref, wgt_ref, cw_ref,
                   main_ref, gate_ref, gate_t_ref, tail_ref):
    i = pl.program_id(0)
    x = x_ref[...]
    ms = jnp.mean(x * x, axis=-1, keepdims=True)
    h = (x * lax.rsqrt(ms + NORM_EPS) * nw_ref[...]).astype(BF16)

    gate_ref[...] = _dot(h, wg_ref[...])
    gate_t_ref[...] = _dot_nt(wgt_ref[...], h)

    @pl.when(i % tiles_per_seq == 0)
    def _():
        tail_ref[...] = jnp.zeros_like(tail_ref)

    tm = x.shape[0]
    row = lax.broadcasted_iota(jnp.int32, (TAIL, NC_IN), 0)
    for n0 in range(0, N_CONV, NC_IN):
        cols = slice(n0, n0 + NC_IN)
        acc = _dot(h, w_ref[:, cols])
        last = acc[tm - TAIL:, :]
        delta = tail_ref[:, cols] - last
        tail_ref[:, cols] = last
        y = acc * cw_ref[CONV_K - 1:CONV_K, cols]
        fix = jnp.zeros((TAIL, NC_IN), F32)
        for s in range(1, CONV_K):
            wk = cw_ref[CONV_K - 1 - s:CONV_K - s, cols]
            y = y + pltpu.roll(acc, s, axis=0) * wk
            fix = fix + jnp.where(row < s, pltpu.roll(delta, s, axis=0), 0.0) * wk
        main_ref[:, cols] = (y * _sigmoid(y)).astype(BF16)
        y0 = y[:TAIL, :] + fix
        main_ref[0:TAIL, cols] = (y0 * _sigmoid(y0)).astype(BF16)

    for n0 in range(N_CONV, N_MAIN, NC_IN):
        cols = slice(n0, n0 + NC_IN)
        main_ref[:, cols] = _dot(h, w_ref[:, cols]).astype(BF16)


def _inproj(xf, norm_w, w_main, w_gate, w_gate_t, conv_w, seq_len):
    t = xf.shape[0]
    tiles_per_seq = seq_len // TM_IN
    return pl.pallas_call(
        functools.partial(_inproj_kernel, tiles_per_seq),
        grid=(t // TM_IN,),
        in_specs=[
            pl.BlockSpec((TM_IN, D_MODEL), lambda i: (i, 0)),
            pl.BlockSpec((1, D_MODEL), lambda i: (0, 0)),
            pl.BlockSpec((D_MODEL, N_MAIN), lambda i: (0, 0)),
            pl.BlockSpec((D_MODEL, LANES), lambda i: (0, 0)),
            pl.BlockSpec((N_GATE, D_MODEL), lambda i: (0, 0)),
            pl.BlockSpec((CONV_K, N_CONV), lambda i: (0, 0)),
        ],
        out_specs=[
            pl.BlockSpec((TM_IN, N_MAIN), lambda i: (i, 0)),
            pl.BlockSpec((TM_IN, LANES), lambda i: (i, 0)),
            pl.BlockSpec((N_GATE, TM_IN), lambda i: (0, i)),
        ],
        out_shape=[
            jax.ShapeDtypeStruct((t, N_MAIN), BF16),
            jax.ShapeDtypeStruct((t, LANES), F32),
            jax.ShapeDtypeStruct((N_GATE, t), F32),
        ],
        scratch_shapes=[pltpu.VMEM((TAIL, N_CONV), F32)],
        compiler_params=pltpu.CompilerParams(
            dimension_semantics=("arbitrary",), vmem_limit_bytes=VMEM_LIMIT),
        name="inproj",
    )(xf, norm_w, w_main, w_gate, w_gate_t, conv_w)


def _gdn_kernel(act_ref, gate_ref, gate_t_ref, alog_l_ref, dtb_l_ref, alog_s_ref, dtb_s_ref,
                gnw_ref, o_ref, state_ref, negmask_ref, strict_ref):
    seq_len = act_ref.shape[0]
    r = lax.broadcasted_iota(jnp.int32, (SC, SC), 0)
    c = lax.broadcasted_iota(jnp.int32, (SC, SC), 1)
    same = (r & -CHUNK) == (c & -CHUNK)
    negmask_ref[...] = jnp.where(same & (r >= c), 0.0, MASK_VALUE)
    strict_ref[...] = jnp.where(same & (r > c), 1.0, 0.0)
    state_ref[...] = jnp.zeros_like(state_ref)
    eye = jnp.where(r == c, 1.0, 0.0)

    sub_in_chunk = lax.broadcasted_iota(jnp.int32, (SC, LANES), 0) & (CHUNK - 1)
    lane_in_chunk = lax.broadcasted_iota(jnp.int32, (N_GATE, SC), 1) & (CHUNK - 1)
    neg_a_l = -jnp.exp(alog_l_ref[...])
    neg_a_s = -jnp.exp(alog_s_ref[...])
    scale = HEAD_DIM ** -0.5

    def sc_body(sc, carry):
        r0 = pl.multiple_of(sc * SC, SC)
        rows = pl.ds(r0, SC)
        gate = gate_ref[rows, :]
        beta_l = _sigmoid(gate)
        gc_l = neg_a_l * _softplus(gate + dtb_l_ref[...])
        gate_t = gate_t_ref[:, rows]
        beta_s = _sigmoid(gate_t)
        gc_s = neg_a_s * _softplus(gate_t + dtb_s_ref[...])
        step = 1
        while step < CHUNK:
            gc_l = gc_l + jnp.where(sub_in_chunk >= step, pltpu.roll(gc_l, step, axis=0), 0.0)
            gc_s = gc_s + jnp.where(lane_in_chunk >= step, pltpu.roll(gc_s, step, axis=1), 0.0)
            step *= 2

        for h in range(GDN_HEADS):
            hc = slice(h * HEAD_DIM, (h + 1) * HEAD_DIM)
            q = act_ref[rows, COL_GQ * HEAD_DIM + h * HEAD_DIM:COL_GQ * HEAD_DIM + (h + 1) * HEAD_DIM].astype(F32)
            k = act_ref[rows, COL_GK * HEAD_DIM + h * HEAD_DIM:COL_GK * HEAD_DIM + (h + 1) * HEAD_DIM].astype(F32)
            v = act_ref[rows, COL_GV * HEAD_DIM + h * HEAD_DIM:COL_GV * HEAD_DIM + (h + 1) * HEAD_DIM]
            q = q * (lax.rsqrt(jnp.sum(q * q, axis=-1, keepdims=True) + 1e-6) * scale)
            k = k * lax.rsqrt(jnp.sum(k * k, axis=-1, keepdims=True) + 1e-6)
            gcol = gc_l[:, GDN_HEADS + h:GDN_HEADS + h + 1]
            bcol = beta_l[:, h:h + 1]
            grow = gc_s[GDN_HEADS + h:GDN_HEADS + h + 1, :]
            brow = beta_s[h:h + 1, :]

            decay = jnp.exp(gcol - grow + negmask_ref[...])
            kb = k.astype(BF16)
            qb = q.astype(BF16)
            n_mat = -(_dot_nt(kb, kb) * decay * bcol * strict_ref[...])
            p_mat = eye + n_mat
            n_pow = n_mat
            span = 2
            while span < CHUNK:
                nb = n_pow.astype(BF16)
                n_pow = _dot(nb, nb)
                p_mat = p_mat + _dot(p_mat.astype(BF16), n_pow.astype(BF16))
                span *= 2
            t_beta = p_mat * brow
            u = _dot(t_beta.astype(BF16), v)
            w = _dot((t_beta * jnp.exp(grow)).astype(BF16), kb).astype(BF16)
            a_qk = (_dot_nt(qb, kb) * decay).astype(BF16)
            q_dec = (q * jnp.exp(gcol)).astype(BF16)

            z = act_ref[rows, COL_GZ * HEAD_DIM + h * HEAD_DIM:COL_GZ * HEAD_DIM + (h + 1) * HEAD_DIM].astype(F32)
            gate_out = z * _sigmoid(z) * gnw_ref[...]
            for ci in range(SC // CHUNK):
                cr = slice(ci * CHUNK, (ci + 1) * CHUNK)
                g_last = gcol[(ci + 1) * CHUNK - 1:(ci + 1) * CHUNK, :]
                k_dec = (k[cr, :] * jnp.exp(g_last - gcol[cr, :])).astype(BF16)
                state = state_ref[h]
                sb = state.astype(BF16)
                v_new = (u[cr, :] - _dot(w[cr, :], sb)).astype(BF16)
                o = _dot(q_dec[cr, :], sb) + _dot(a_qk[cr, ci * CHUNK:(ci + 1) * CHUNK], v_new)
                state_ref[h] = state * jnp.exp(g_last) + _dot_tn(k_dec, v_new)
                o = o * lax.rsqrt(jnp.mean(o * o, axis=-1, keepdims=True) + NORM_EPS)
                o_ref[pl.ds(r0 + ci * CHUNK, CHUNK), hc] = (o * gate_out[cr, :]).astype(BF16)
        return carry

    lax.fori_loop(0, seq_len // SC, sc_body, 0)


def _gdn(act, gate, gate_t, alog_l, dtb_l, alog_s, dtb_s, gdn_norm_w, batch, seq_len):
    t = act.shape[0]
    small = lambda shape: pl.BlockSpec(shape, lambda b: (0, 0))
    return pl.pallas_call(
        _gdn_kernel,
        grid=(batch,),
        in_specs=[
            pl.BlockSpec((seq_len, 4 * GDN_WIDTH), lambda b: (b, 0)),
            pl.BlockSpec((seq_len, LANES), lambda b: (b, 0)),
            pl.BlockSpec((N_GATE, seq_len), lambda b: (0, b)),
            small((1, LANES)), small((1, LANES)), small((N_GATE, 1)), small((N_GATE, 1)),
            small((1, HEAD_DIM)),
        ],
        out_specs=pl.BlockSpec((seq_len, GDN_WIDTH), lambda b: (b, 0)),
        out_shape=jax.ShapeDtypeStruct((t, GDN_WIDTH), BF16),
        scratch_shapes=[
            pltpu.VMEM((GDN_HEADS, HEAD_DIM, HEAD_DIM), F32),
            pltpu.VMEM((SC, SC), F32),
            pltpu.VMEM((SC, SC), F32),
        ],
        compiler_params=pltpu.CompilerParams(
            dimension_semantics=("arbitrary",), vmem_limit_bytes=VMEM_LIMIT),
        name="gdn",
    )(act, gate, gate_t, alog_l, dtb_l, alog_s, dtb_s, gdn_norm_w)


def _attn_kernel(lam_init, q_ref, k_ref, v_ref, z_ref, lq1_ref, lk1_ref, lq2_ref, lk2_ref,
                 nw_ref, o_ref, acc_ref, m_ref, l_ref):
    h = pl.program_id(1)
    qi = pl.program_id(2)
    slope = lax.shift_left(jnp.int32(1), 2 * (DIFF_HEADS - 1 - h)).astype(F32) * (1.0 / 256.0)
    lane = lax.broadcasted_iota(jnp.int32, (1, HEAD_DIM), 1)
    q = q_ref[...].astype(F32) * (DIFF_QK_DIM ** -0.5)
    q_ext = (
        jnp.where(lane < DIFF_QK_DIM, q, jnp.where(lane < DIFF_QK_DIM + 2, 1.0, 0.0)).astype(BF16),
        jnp.where(lane >= DIFF_QK_DIM, q, jnp.where(lane < 2, 1.0, 0.0)).astype(BF16),
    )
    m_ref[...] = jnp.full_like(m_ref, MASK_VALUE)
    l_ref[...] = jnp.zeros_like(l_ref)
    acc_ref[...] = jnp.zeros_like(acc_ref)
    key_in_block = lax.broadcasted_iota(jnp.int32, (BK, 1), 0).astype(F32) * slope

    def block(kb, masked):
        rows = pl.ds(pl.multiple_of(kb * BK, BK), BK)
        k = k_ref[rows, :].astype(F32)
        v = v_ref[rows, :]
        block_off = (kb * BK).astype(F32) * slope
        k_ext = (
            jnp.where(lane < DIFF_QK_DIM, k,
                      jnp.where(lane == DIFF_QK_DIM, key_in_block,
                                jnp.where(lane == DIFF_QK_DIM + 1, block_off, 0.0))).astype(BF16),
            jnp.where(lane >= DIFF_QK_DIM, k,
                      jnp.where(lane == 0, key_in_block,
                                jnp.where(lane == 1, block_off, 0.0))).astype(BF16),
        )
        for c in range(2):
            s = _dot_nt(q_ext[c], k_ext[c])
            if masked:
                rr = lax.broadcasted_iota(jnp.int32, (BQ, BK), 0)
                cc = lax.broadcasted_iota(jnp.int32, (BQ, BK), 1)
                s = jnp.where(cc <= rr, s, MASK_VALUE)
            m_old = m_ref[c]
            m_new = jnp.maximum(m_old, jnp.max(s, axis=-1, keepdims=True))
            alpha = jnp.exp(m_old - m_new)
            p = jnp.exp(s - m_new)
            l_ref[c] = alpha * l_ref[c] + jnp.sum(p, axis=-1, keepdims=True)
            acc_ref[c] = alpha * acc_ref[c] + _dot(p.astype(BF16), v)
            m_ref[c] = m_new

    def body(kb, carry):
        block(kb, False)
        return carry

    lax.fori_loop(0, qi, body, 0)
    block(qi, True)

    lam = (jnp.exp(jnp.sum(lq1_ref[...] * lk1_ref[...], axis=-1, keepdims=True))
           - jnp.exp(jnp.sum(lq2_ref[...] * lk2_ref[...], axis=-1, keepdims=True)) + lam_init)
    o = acc_ref[0] / l_ref[0] - lam * (acc_ref[1] / l_ref[1])
    o = o * lax.rsqrt(jnp.mean(o * o, axis=-1, keepdims=True) + NORM_EPS)
    o = o * nw_ref[...] * (1.0 - lam_init)
    z = z_ref[...].astype(F32)
    o_ref[...] = (o * (z * _sigmoid(z))).astype(BF16)


def _attn(act, lq1, lk1, lq2, lk2, diff_norm_w, lam_init, batch, seq_len):
    t = act.shape[0]
    nq = seq_len // BQ
    small = lambda shape: pl.BlockSpec(shape, lambda b, h, i: (0, 0))
    return pl.pallas_call(
        functools.partial(_attn_kernel, lam_init),
        grid=(batch, DIFF_HEADS, nq),
        in_specs=[
            pl.BlockSpec((BQ, HEAD_DIM), lambda b, h, i: (b * nq + i, COL_DQ + h)),
            pl.BlockSpec((seq_len, HEAD_DIM), lambda b, h, i: (b, COL_DK + h)),
            pl.BlockSpec((seq_len, HEAD_DIM), lambda b, h, i: (b, COL_DV + h)),
            pl.BlockSpec((BQ, HEAD_DIM), lambda b, h, i: (b * nq + i, COL_DZ + h)),
            small((1, DIFF_QK_DIM)), small((1, DIFF_QK_DIM)),
            small((1, DIFF_QK_DIM)), small((1, DIFF_QK_DIM)),
            small((1, HEAD_DIM)),
        ],
        out_specs=pl.BlockSpec((BQ, HEAD_DIM), lambda b, h, i: (b * nq + i, h)),
        out_shape=jax.ShapeDtypeStruct((t, DIFF_WIDTH), BF16),
        scratch_shapes=[
            pltpu.VMEM((2, BQ, HEAD_DIM), F32),
            pltpu.VMEM((2, BQ, 1), F32),
            pltpu.VMEM((2, BQ, 1), F32),
        ],
        compiler_params=pltpu.CompilerParams(
            dimension_semantics=("arbitrary", "arbitrary", "arbitrary"),
            vmem_limit_bytes=VMEM_LIMIT),
        name="diffattn",
    )(act, act, act, act, lq1, lk1, lq2, lk2, diff_norm_w)


def _outproj_kernel(oa_ref, ob_ref, x_ref, w_ref, fw_ref, out_ref):
    mix = _dot(oa_ref[...], w_ref[0:GDN_WIDTH, :]) + _dot(ob_ref[...], w_ref[GDN_WIDTH:, :])
    y = x_ref[...] + mix
    ms = jnp.mean(y * y, axis=-1, keepdims=True)
    out_ref[...] = y * lax.rsqrt(ms + NORM_EPS) * fw_ref[...]


def _outproj(o_a, o_b, xf, w_out, final_norm_w):
    t = xf.shape[0]
    return pl.pallas_call(
        _outproj_kernel,
        grid=(t // TM_OUT,),
        in_specs=[
            pl.BlockSpec((TM_OUT, GDN_WIDTH), lambda i: (i, 0)),
            pl.BlockSpec((TM_OUT, DIFF_WIDTH), lambda i: (i, 0)),
            pl.BlockSpec((TM_OUT, D_MODEL), lambda i: (i, 0)),
            pl.BlockSpec((D_MODEL, D_MODEL), lambda i: (0, 0)),
            pl.BlockSpec((1, D_MODEL), lambda i: (0, 0)),
        ],
        out_specs=pl.BlockSpec((TM_OUT, D_MODEL), lambda i: (i, 0)),
        out_shape=jax.ShapeDtypeStruct((t, D_MODEL), F32),
        compiler_params=pltpu.CompilerParams(
            dimension_semantics=("arbitrary",), vmem_limit_bytes=VMEM_LIMIT),
        name="outproj",
    )(o_a, o_b, xf, w_out, final_norm_w)


def kernel(x, norm_w, w_in, conv_w, a_log, dt_bias, gdn_norm_w, lambda_q1, lambda_k1,
           lambda_q2, lambda_k2, diff_norm_w, w_out, final_norm_w):
    batch, seq_len, d_model = x.shape
    depth = norm_w.shape[0]
    assert depth == 1 and d_model == D_MODEL
    assert seq_len % TM_IN == 0 and seq_len % SC == 0 and seq_len % BQ == 0 and BQ == BK
    n_wide_a = 4 * GDN_WIDTH
    xf = x.reshape(batch * seq_len, d_model)

    w = w_in[0]
    w_main = jnp.concatenate([w[:, :n_wide_a], w[:, n_wide_a + N_GATE:]], axis=1).astype(BF16)
    w_gate_cols = w[:, n_wide_a:n_wide_a + N_GATE]
    w_gate = jnp.pad(w_gate_cols, ((0, 0), (0, LANES - N_GATE))).astype(BF16)
    w_gate_t = w_gate_cols.T.astype(BF16)

    act, gate, gate_t = _inproj(xf, norm_w[0][None, :], w_main, w_gate, w_gate_t, conv_w[0], seq_len)

    pad_l = lambda vec: jnp.pad(vec, (GDN_HEADS, LANES - N_GATE))[None, :]
    pad_s = lambda vec: jnp.pad(vec, (GDN_HEADS, 0))[:, None]
    o_a = _gdn(act, gate, gate_t, pad_l(a_log[0]), pad_l(dt_bias[0]), pad_s(a_log[0]),
               pad_s(dt_bias[0]), gdn_norm_w[0][None, :], batch, seq_len)

    lam_init = 0.8 - 0.6 * math.exp(-0.3 * 0)
    o_b = _attn(act, lambda_q1[0][None, :], lambda_k1[0][None, :], lambda_q2[0][None, :],
                lambda_k2[0][None, :], diff_norm_w[0][None, :], lam_init, batch, seq_len)

    out = _outproj(o_a, o_b, xf, w_out[0].astype(BF16), final_norm_w[None, :])
    return out.reshape(batch, seq_len, d_model)
```

```python
import functools
import math

import jax
import jax.numpy as jnp
from jax import lax
from jax.experimental import pallas as pl
from jax.experimental.pallas import tpu as pltpu

F32 = jnp.float32
BF16 = jnp.bfloat16

D_MODEL = 1024
GDN_HEADS = 4
HEAD_DIM = 128
GDN_WIDTH = GDN_HEADS * HEAD_DIM
DIFF_HEADS = 4
DIFF_QK_DIM = 64
DIFF_WIDTH = DIFF_HEADS * HEAD_DIM
CONV_K = 4
NORM_EPS = 1e-6
N_GATE = 2 * GDN_HEADS
N_MAIN = 4 * GDN_WIDTH + 3 * DIFF_WIDTH
N_CONV = 3 * GDN_WIDTH
LANES = 128
MASK_VALUE = -1e30

COL_GQ, COL_GK, COL_GV, COL_GZ = 0, 4, 8, 12
COL_DQ, COL_DK, COL_DZ = 16, 20, 24

TM_IN = 512
NC_IN = 256
TAIL = 16
TM_OUT = 512
SC = 256
CHUNK = 64
BQ = 512
BK = 256
VMEM_LIMIT = 48 * 1024 * 1024


def _sigmoid(x):
    return 1.0 / (1.0 + jnp.exp(-x))


def _softplus(x):
    return jnp.maximum(x, 0.0) + jnp.log1p(jnp.exp(-jnp.abs(x)))


def _dot(a, b):
    return jnp.dot(a, b, preferred_element_type=F32)


def _dot_nt(a, b):
    return lax.dot_general(a, b, (((1,), (1,)), ((), ())), preferred_element_type=F32)


def _dot_tn(a, b):
    return lax.dot_general(a, b, (((0,), (0,)), ((), ())), preferred_element_type=F32)


def _inproj_kernel(tiles_per_seq, x_ref, nw_ref, w_ref, wg_ref, wgt_ref, wvt_ref, cw_ref,
                   main_ref, gate_ref, gate_t_ref, vt_ref, tail_ref):
    i = pl.program_id(0)
    x = x_ref[...]
    ms = jnp.mean(x * x, axis=-1, keepdims=True)
    h = (x * lax.rsqrt(ms + NORM_EPS) * nw_ref[...]).astype(BF16)

    gate_ref[...] = _dot(h, wg_ref[...])
    gate_t_ref[...] = _dot_nt(wgt_ref[...], h)
    vt_ref[...] = _dot_nt(wvt_ref[...], h).astype(BF16)

    @pl.when(i % tiles_per_seq == 0)
    def _():
        tail_ref[...] = jnp.zeros_like(tail_ref)

    tm = x.shape[0]
    row = lax.broadcasted_iota(jnp.int32, (TAIL, NC_IN), 0)
    for n0 in range(0, N_CONV, NC_IN):
        cols = slice(n0, n0 + NC_IN)
        acc = _dot(h, w_ref[:, cols])
        last = acc[tm - TAIL:, :]
        delta = tail_ref[:, cols] - last
        tail_ref[:, cols] = last
        y = acc * cw_ref[CONV_K - 1:CONV_K, cols]
        fix = jnp.zeros((TAIL, NC_IN), F32)
        for s in range(1, CONV_K):
            wk = cw_ref[CONV_K - 1 - s:CONV_K - s, cols]
            y = y + pltpu.roll(acc, s, axis=0) * wk
            fix = fix + jnp.where(row < s, pltpu.roll(delta, s, axis=0), 0.0) * wk
        main_ref[:, cols] = (y * _sigmoid(y)).astype(BF16)
        y0 = y[:TAIL, :] + fix
        main_ref[0:TAIL, cols] = (y0 * _sigmoid(y0)).astype(BF16)

    for n0 in range(N_CONV, N_MAIN, NC_IN):
        cols = slice(n0, n0 + NC_IN)
        main_ref[:, cols] = _dot(h, w_ref[:, cols]).astype(BF16)


def _inproj(xf, norm_w, w_main, w_gate, w_gate_t, w_v_t, conv_w, seq_len):
    t = xf.shape[0]
    tiles_per_seq = seq_len // TM_IN
    return pl.pallas_call(
        functools.partial(_inproj_kernel, tiles_per_seq),
        grid=(t // TM_IN,),
        in_specs=[
            pl.BlockSpec((TM_IN, D_MODEL), lambda i: (i, 0)),
            pl.BlockSpec((1, D_MODEL), lambda i: (0, 0)),
            pl.BlockSpec((D_MODEL, N_MAIN), lambda i: (0, 0)),
            pl.BlockSpec((D_MODEL, LANES), lambda i: (0, 0)),
            pl.BlockSpec((N_GATE, D_MODEL), lambda i: (0, 0)),
            pl.BlockSpec((DIFF_WIDTH, D_MODEL), lambda i: (0, 0)),
            pl.BlockSpec((CONV_K, N_CONV), lambda i: (0, 0)),
        ],
        out_specs=[
            pl.BlockSpec((TM_IN, N_MAIN), lambda i: (i, 0)),
            pl.BlockSpec((TM_IN, LANES), lambda i: (i, 0)),
            pl.BlockSpec((N_GATE, TM_IN), lambda i: (0, i)),
            pl.BlockSpec((DIFF_WIDTH, TM_IN), lambda i: (0, i)),
        ],
        out_shape=[
            jax.ShapeDtypeStruct((t, N_MAIN), BF16),
            jax.ShapeDtypeStruct((t, LANES), F32),
            jax.ShapeDtypeStruct((N_GATE, t), F32),
            jax.ShapeDtypeStruct((DIFF_WIDTH, t), BF16),
        ],
        scratch_shapes=[pltpu.VMEM((TAIL, N_CONV), F32)],
        compiler_params=pltpu.CompilerParams(
            dimension_semantics=("arbitrary",), vmem_limit_bytes=VMEM_LIMIT),
        name="inproj",
    )(xf, norm_w, w_main, w_gate, w_gate_t, w_v_t, conv_w)


def _gdn_kernel(act_ref, gate_ref, gate_t_ref, alog_l_ref, dtb_l_ref, alog_s_ref, dtb_s_ref,
                gnw_ref, o_ref, state_ref, negmask_ref, strict_ref):
    seq_len = act_ref.shape[0]
    r = lax.broadcasted_iota(jnp.int32, (SC, SC), 0)
    c = lax.broadcasted_iota(jnp.int32, (SC, SC), 1)
    same = (r & -CHUNK) == (c & -CHUNK)
    negmask_ref[...] = jnp.where(same & (r >= c), 0.0, MASK_VALUE)
    strict_ref[...] = jnp.where(same & (r > c), 1.0, 0.0)
    state_ref[...] = jnp.zeros_like(state_ref)
    eye = jnp.where(r == c, 1.0, 0.0)

    sub_in_chunk = lax.broadcasted_iota(jnp.int32, (SC, LANES), 0) & (CHUNK - 1)
    lane_in_chunk = lax.broadcasted_iota(jnp.int32, (N_GATE, SC), 1) & (CHUNK - 1)
    neg_a_l = -jnp.exp(alog_l_ref[...])
    neg_a_s = -jnp.exp(alog_s_ref[...])
    scale = HEAD_DIM ** -0.5

    def sc_body(sc, carry):
        r0 = pl.multiple_of(sc * SC, SC)
        rows = pl.ds(r0, SC)
        gate = gate_ref[rows, :]
        beta_l = _sigmoid(gate)
        gc_l = neg_a_l * _softplus(gate + dtb_l_ref[...])
        gate_t = gate_t_ref[:, rows]
        beta_s = _sigmoid(gate_t)
        gc_s = neg_a_s * _softplus(gate_t + dtb_s_ref[...])
        step = 1
        while step < CHUNK:
            gc_l = gc_l + jnp.where(sub_in_chunk >= step, pltpu.roll(gc_l, step, axis=0), 0.0)
            gc_s = gc_s + jnp.where(lane_in_chunk >= step, pltpu.roll(gc_s, step, axis=1), 0.0)
            step *= 2

        for h in range(GDN_HEADS):
            hc = slice(h * HEAD_DIM, (h + 1) * HEAD_DIM)
            q = act_ref[rows, COL_GQ * HEAD_DIM + h * HEAD_DIM:COL_GQ * HEAD_DIM + (h + 1) * HEAD_DIM].astype(F32)
            k = act_ref[rows, COL_GK * HEAD_DIM + h * HEAD_DIM:COL_GK * HEAD_DIM + (h + 1) * HEAD_DIM].astype(F32)
            v = act_ref[rows, COL_GV * HEAD_DIM + h * HEAD_DIM:COL_GV * HEAD_DIM + (h + 1) * HEAD_DIM]
            q = q * (lax.rsqrt(jnp.sum(q * q, axis=-1, keepdims=True) + 1e-6) * scale)
            k = k * lax.rsqrt(jnp.sum(k * k, axis=-1, keepdims=True) + 1e-6)
            gcol = gc_l[:, GDN_HEADS + h:GDN_HEADS + h + 1]
            bcol = beta_l[:, h:h + 1]
            grow = gc_s[GDN_HEADS + h:GDN_HEADS + h + 1, :]
            brow = beta_s[h:h + 1, :]

            decay = jnp.exp(gcol - grow + negmask_ref[...])
            kb = k.astype(BF16)
            qb = q.astype(BF16)
            n_mat = -(_dot_nt(kb, kb) * decay * bcol * strict_ref[...])
            p_mat = eye + n_mat
            n_pow = n_mat
            span = 2
            while span < CHUNK:
                nb = n_pow.astype(BF16)
                n_pow = _dot(nb, nb)
                p_mat = p_mat + _dot(p_mat.astype(BF16), n_pow.astype(BF16))
                span *= 2
            t_beta = p_mat * brow
            u = _dot(t_beta.astype(BF16), v)
            w = _dot((t_beta * jnp.exp(grow)).astype(BF16), kb).astype(BF16)
            a_qk = (_dot_nt(qb, kb) * decay).astype(BF16)
            q_dec = (q * jnp.exp(gcol)).astype(BF16)

            z = act_ref[rows, COL_GZ * HEAD_DIM + h * HEAD_DIM:COL_GZ * HEAD_DIM + (h + 1) * HEAD_DIM].astype(F32)
            gate_out = z * _sigmoid(z) * gnw_ref[...]
            for ci in range(SC // CHUNK):
                cr = slice(ci * CHUNK, (ci + 1) * CHUNK)
                g_last = gcol[(ci + 1) * CHUNK - 1:(ci + 1) * CHUNK, :]
                k_dec = (k[cr, :] * jnp.exp(g_last - gcol[cr, :])).astype(BF16)
                state = state_ref[h]
                sb = state.astype(BF16)
                v_new = (u[cr, :] - _dot(w[cr, :], sb)).astype(BF16)
                o = _dot(q_dec[cr, :], sb) + _dot(a_qk[cr, ci * CHUNK:(ci + 1) * CHUNK], v_new)
                state_ref[h] = state * jnp.exp(g_last) + _dot_tn(k_dec, v_new)
                o = o * lax.rsqrt(jnp.mean(o * o, axis=-1, keepdims=True) + NORM_EPS)
                o_ref[pl.ds(r0 + ci * CHUNK, CHUNK), hc] = (o * gate_out[cr, :]).astype(BF16)
        return carry

    lax.fori_loop(0, seq_len // SC, sc_body, 0)


def _gdn(act, gate, gate_t, alog_l, dtb_l, alog_s, dtb_s, gdn_norm_w, batch, seq_len):
    t = act.shape[0]
    small = lambda shape: pl.BlockSpec(shape, lambda b: (0, 0))
    return pl.pallas_call(
        _gdn_kernel,
        grid=(batch,),
        in_specs=[
            pl.BlockSpec((seq_len, 4 * GDN_WIDTH), lambda b: (b, 0)),
            pl.BlockSpec((seq_len, LANES), lambda b: (b, 0)),
            pl.BlockSpec((N_GATE, seq_len), lambda b: (0, b)),
            small((1, LANES)), small((1, LANES)), small((N_GATE, 1)), small((N_GATE, 1)),
            small((1, HEAD_DIM)),
        ],
        out_specs=pl.BlockSpec((seq_len, GDN_WIDTH), lambda b: (b, 0)),
        out_shape=jax.ShapeDtypeStruct((t, GDN_WIDTH), BF16),
        scratch_shapes=[
            pltpu.VMEM((GDN_HEADS, HEAD_DIM, HEAD_DIM), F32),
            pltpu.VMEM((SC, SC), F32),
            pltpu.VMEM((SC, SC), F32),
        ],
        compiler_params=pltpu.CompilerParams(
            dimension_semantics=("arbitrary",), vmem_limit_bytes=VMEM_LIMIT),
        name="gdn",
    )(act, gate, gate_t, alog_l, dtb_l, alog_s, dtb_s, gdn_norm_w)


def _split3(x):
    hi = x.astype(BF16).astype(F32)
    r1 = x - hi
    mid = r1.astype(BF16).astype(F32)
    return hi, mid, r1 - mid


def _attn_kernel(lam_init, q_ref, k_ref, vt_ref, z_ref, lq1_ref, lk1_ref, lq2_ref, lk2_ref,
                 nw_ref, o_ref, bias_ref, kext_ref, s_ref, acc_ref, m_ref, l_ref):
    h = pl.program_id(0)
    b = pl.program_id(1)
    qi = pl.program_id(2)
    seq_len = k_ref.shape[0]
    lane = lax.broadcasted_iota(jnp.int32, (1, HEAD_DIM), 1)
    n_bias = 6
    log2e = math.log2(math.e)

    @pl.when((b == 0) & (qi == 0))
    def _():
        slope = lax.shift_left(jnp.int32(1), 2 * (DIFF_HEADS - 1 - h)).astype(F32) * (log2e / 256.0)
        pos = lax.broadcasted_iota(jnp.int32, (seq_len, 1), 0)
        in_block = (pos & (BK - 1)).astype(F32) * slope
        block_off = (pos & -BK).astype(F32) * slope
        terms = _split3(in_block) + _split3(block_off)
        tile = jnp.zeros((seq_len, HEAD_DIM), F32)
        for j, term in enumerate(terms):
            tile = jnp.where((lane == j) | (lane == DIFF_QK_DIM + j), term, tile)
        bias_ref[...] = tile

    @pl.when(qi == 0)
    def _():
        k = k_ref[...].astype(F32)
        kext_ref[0] = jnp.where(lane < DIFF_QK_DIM, k, bias_ref[...]).astype(BF16)
        kext_ref[1] = jnp.where(lane >= DIFF_QK_DIM, k, bias_ref[...]).astype(BF16)

    q = q_ref[...].astype(F32) * (DIFF_QK_DIM ** -0.5 * log2e)
    q_ext = (
        jnp.where(lane < DIFF_QK_DIM, q, jnp.where(lane < DIFF_QK_DIM + n_bias, 1.0, 0.0)).astype(BF16),
        jnp.where(lane >= DIFF_QK_DIM, q, jnp.where(lane < n_bias, 1.0, 0.0)).astype(BF16),
    )
    m_ref[...] = jnp.full_like(m_ref, MASK_VALUE)
    l_ref[...] = jnp.zeros_like(l_ref)
    acc_ref[...] = jnp.zeros_like(acc_ref)

    def scores(kb, slot):
        rows = pl.ds(pl.multiple_of(kb * BK, BK), BK)
        for c in range(2):
            s_ref[slot, c] = _dot_nt(kext_ref[c, rows, :], q_ext[c])

    def accumulate(kb, slot, masked):
        rows = pl.ds(pl.multiple_of(kb * BK, BK), BK)
        vt = vt_ref[:, rows]
        for c in range(2):
            s = s_ref[slot, c]
            if masked:
                kpos = kb * BK + lax.broadcasted_iota(jnp.int32, (BK, BQ), 0)
                qpos = qi * BQ + lax.broadcasted_iota(jnp.int32, (BK, BQ), 1)
                s = jnp.where(kpos <= qpos, s, MASK_VALUE)
            m_old = m_ref[c]
            m_new = jnp.maximum(m_old, jnp.max(s, axis=0, keepdims=True))
            alpha = jnp.exp2(m_old - m_new)
            p = jnp.exp2(s - m_new)
            l_ref[c] = alpha * l_ref[c] + jnp.sum(p, axis=0, keepdims=True)
            acc_ref[c] = alpha * acc_ref[c] + _dot(vt, p.astype(BF16))
            m_ref[c] = m_new

    assert BQ == 2 * BK
    scores(0, 0)

    def body(j, carry):
        scores(2 * j + 1, 1)
        accumulate(2 * j, 0, False)
        scores(2 * j + 2, 0)
        accumulate(2 * j + 1, 1, False)
        return carry

    lax.fori_loop(0, qi, body, 0)
    scores(2 * qi + 1, 1)
    accumulate(2 * qi, 0, True)
    accumulate(2 * qi + 1, 1, True)

    lam = (jnp.exp(jnp.sum(lq1_ref[...] * lk1_ref[...], axis=-1, keepdims=True))
           - jnp.exp(jnp.sum(lq2_ref[...] * lk2_ref[...], axis=-1, keepdims=True)) + lam_init)
    o_t = acc_ref[0] / l_ref[0] - lam * (acc_ref[1] / l_ref[1])
    o = o_t.T
    o = o * lax.rsqrt(jnp.mean(o * o, axis=-1, keepdims=True) + NORM_EPS)
    o = o * nw_ref[...] * (1.0 - lam_init)
    z = z_ref[...].astype(F32)
    o_ref[...] = (o * (z * _sigmoid(z))).astype(BF16)


def _attn(act, v_t, lq1, lk1, lq2, lk2, diff_norm_w, lam_init, batch, seq_len):
    t = act.shape[0]
    nq = seq_len // BQ
    small = lambda shape: pl.BlockSpec(shape, lambda h, b, i: (0, 0))
    return pl.pallas_call(
        functools.partial(_attn_kernel, lam_init),
        grid=(DIFF_HEADS, batch, nq),
        in_specs=[
            pl.BlockSpec((BQ, HEAD_DIM), lambda h, b, i: (b * nq + i, COL_DQ + h)),
            pl.BlockSpec((seq_len, HEAD_DIM), lambda h, b, i: (b, COL_DK + h)),
            pl.BlockSpec((HEAD_DIM, seq_len), lambda h, b, i: (h, b)),
            pl.BlockSpec((BQ, HEAD_DIM), lambda h, b, i: (b * nq + i, COL_DZ + h)),
            small((1, DIFF_QK_DIM)), small((1, DIFF_QK_DIM)),
            small((1, DIFF_QK_DIM)), small((1, DIFF_QK_DIM)),
            small((1, HEAD_DIM)),
        ],
        out_specs=pl.BlockSpec((BQ, HEAD_DIM), lambda h, b, i: (b * nq + i, h)),
        out_shape=jax.ShapeDtypeStruct((t, DIFF_WIDTH), BF16),
        scratch_shapes=[
            pltpu.VMEM((seq_len, HEAD_DIM), F32),
            pltpu.VMEM((2, seq_len, HEAD_DIM), BF16),
            pltpu.VMEM((2, 2, BK, BQ), F32),
            pltpu.VMEM((2, HEAD_DIM, BQ), F32),
            pltpu.VMEM((2, 1, BQ), F32),
            pltpu.VMEM((2, 1, BQ), F32),
        ],
        compiler_params=pltpu.CompilerParams(
            dimension_semantics=("arbitrary", "arbitrary", "arbitrary"),
            vmem_limit_bytes=VMEM_LIMIT),
        name="diffattn",
    )(act, act, v_t, act, lq1, lk1, lq2, lk2, diff_norm_w)


def _outproj_kernel(oa_ref, ob_ref, x_ref, w_ref, fw_ref, out_ref):
    mix = _dot(oa_ref[...], w_ref[0:GDN_WIDTH, :]) + _dot(ob_ref[...], w_ref[GDN_WIDTH:, :])
    y = x_ref[...] + mix
    ms = jnp.mean(y * y, axis=-1, keepdims=True)
    out_ref[...] = y * lax.rsqrt(ms + NORM_EPS) * fw_ref[...]


def _outproj(o_a, o_b, xf, w_out, final_norm_w):
    t = xf.shape[0]
    return pl.pallas_call(
        _outproj_kernel,
        grid=(t // TM_OUT,),
        in_specs=[
            pl.BlockSpec((TM_OUT, GDN_WIDTH), lambda i: (i, 0)),
            pl.BlockSpec((TM_OUT, DIFF_WIDTH), lambda i: (i, 0)),
            pl.BlockSpec((TM_OUT, D_MODEL), lambda i: (i, 0)),
            pl.BlockSpec((D_MODEL, D_MODEL), lambda i: (0, 0)),
            pl.BlockSpec((1, D_MODEL), lambda i: (0, 0)),
        ],
        out_specs=pl.BlockSpec((TM_OUT, D_MODEL), lambda i: (i, 0)),
        out_shape=jax.ShapeDtypeStruct((t, D_MODEL), F32),
        compiler_params=pltpu.CompilerParams(
            dimension_semantics=("arbitrary",), vmem_limit_bytes=VMEM_LIMIT),
        name="outproj",
    )(o_a, o_b, xf, w_out, final_norm_w)


def kernel(x, norm_w, w_in, conv_w, a_log, dt_bias, gdn_norm_w, lambda_q1, lambda_k1,
           lambda_q2, lambda_k2, diff_norm_w, w_out, final_norm_w):
    batch, seq_len, d_model = x.shape
    depth = norm_w.shape[0]
    assert depth == 1 and d_model == D_MODEL
    assert seq_len % TM_IN == 0 and seq_len % SC == 0 and seq_len % BQ == 0 and BQ % BK == 0
    n_wide_a = 4 * GDN_WIDTH
    xf = x.reshape(batch * seq_len, d_model)

    w = w_in[0]
    c_dq = n_wide_a + N_GATE
    c_dv = c_dq + 2 * DIFF_WIDTH
    c_dz = c_dv + DIFF_WIDTH
    w_main = jnp.concatenate([w[:, :n_wide_a], w[:, c_dq:c_dv], w[:, c_dz:]], axis=1).astype(BF16)
    w_v_t = w[:, c_dv:c_dz].T.astype(BF16)
    w_gate_cols = w[:, n_wide_a:n_wide_a + N_GATE]
    w_gate = jnp.pad(w_gate_cols, ((0, 0), (0, LANES - N_GATE))).astype(BF16)
    w_gate_t = w_gate_cols.T.astype(BF16)

    act, gate, gate_t, v_t = _inproj(xf, norm_w[0][None, :], w_main, w_gate, w_gate_t, w_v_t,
                                     conv_w[0], seq_len)

    pad_l = lambda vec: jnp.pad(vec, (GDN_HEADS, LANES - N_GATE))[None, :]
    pad_s = lambda vec: jnp.pad(vec, (GDN_HEADS, 0))[:, None]
    o_a = _gdn(act, gate, gate_t, pad_l(a_log[0]), pad_l(dt_bias[0]), pad_s(a_log[0]),
               pad_s(dt_bias[0]), gdn_norm_w[0][None, :], batch, seq_len)

    lam_init = 0.8 - 0.6 * math.exp(-0.3 * 0)
    o_b = _attn(act, v_t, lambda_q1[0][None, :], lambda_k1[0][None, :], lambda_q2[0][None, :],
                lambda_k2[0][None, :], diff_norm_w[0][None, :], lam_init, batch, seq_len)

    out = _outproj(o_a, o_b, xf, w_out[0].astype(BF16), final_norm_w[None, :])
    return out.reshape(batch, seq_len, d_model)
```

```python
import functools
import math

import jax
import jax.numpy as jnp
from jax import lax
from jax.experimental import pallas as pl
from jax.experimental.pallas import tpu as pltpu

F32 = jnp.float32
BF16 = jnp.bfloat16

D_MODEL = 1024
GDN_HEADS = 4
HEAD_DIM = 128
GDN_WIDTH = GDN_HEADS * HEAD_DIM
DIFF_HEADS = 4
DIFF_QK_DIM = 64
DIFF_WIDTH = DIFF_HEADS * HEAD_DIM
CONV_K = 4
NORM_EPS = 1e-6
N_GATE = 2 * GDN_HEADS
N_MAIN = 4 * GDN_WIDTH + 3 * DIFF_WIDTH
N_CONV = 3 * GDN_WIDTH
LANES = 128
MASK_VALUE = -1e30

COL_GQ, COL_GK, COL_GV, COL_GZ = 0, 4, 8, 12
COL_DQ, COL_DK, COL_DZ = 16, 20, 24

TM_IN = 512
NC_IN = 256
TAIL = 16
TM_OUT = 512
SC = 256
CHUNK = 64
BQ = 512
BK = 256
VMEM_LIMIT = 48 * 1024 * 1024


def _sigmoid(x):
    return 1.0 / (1.0 + jnp.exp(-x))


def _softplus(x):
    return jnp.maximum(x, 0.0) + jnp.log1p(jnp.exp(-jnp.abs(x)))


def _dot(a, b):
    return jnp.dot(a, b, preferred_element_type=F32)


def _dot_nt(a, b):
    return lax.dot_general(a, b, (((1,), (1,)), ((), ())), preferred_element_type=F32)


def _dot_tn(a, b):
    return lax.dot_general(a, b, (((0,), (0,)), ((), ())), preferred_element_type=F32)


def _inproj_kernel(tiles_per_seq, x_ref, nw_ref, w_ref, wg_ref, wgt_ref, wvt_ref, cw_ref,
                   main_ref, gate_ref, gate_t_ref, vt_ref, tail_ref):
    i = pl.program_id(0)
    x = x_ref[...]
    ms = jnp.mean(x * x, axis=-1, keepdims=True)
    h = (x * lax.rsqrt(ms + NORM_EPS) * nw_ref[...]).astype(BF16)

    gate_ref[...] = _dot(h, wg_ref[...])
    gate_t_ref[...] = _dot_nt(wgt_ref[...], h)
    vt_ref[...] = _dot_nt(wvt_ref[...], h).astype(BF16)

    @pl.when(i % tiles_per_seq == 0)
    def _():
        tail_ref[...] = jnp.zeros_like(tail_ref)

    tm = x.shape[0]
    row = lax.broadcasted_iota(jnp.int32, (TAIL, NC_IN), 0)
    for n0 in range(0, N_CONV, NC_IN):
        cols = slice(n0, n0 + NC_IN)
        acc = _dot(h, w_ref[:, cols])
        last = acc[tm - TAIL:, :]
        delta = tail_ref[:, cols] - last
        tail_ref[:, cols] = last
        y = acc * cw_ref[CONV_K - 1:CONV_K, cols]
        fix = jnp.zeros((TAIL, NC_IN), F32)
        for s in range(1, CONV_K):
            wk = cw_ref[CONV_K - 1 - s:CONV_K - s, cols]
            y = y + pltpu.roll(acc, s, axis=0) * wk
            fix = fix + jnp.where(row < s, pltpu.roll(delta, s, axis=0), 0.0) * wk
        main_ref[:, cols] = (y * _sigmoid(y)).astype(BF16)
        y0 = y[:TAIL, :] + fix
        main_ref[0:TAIL, cols] = (y0 * _sigmoid(y0)).astype(BF16)

    for n0 in range(N_CONV, N_MAIN, NC_IN):
        cols = slice(n0, n0 + NC_IN)
        main_ref[:, cols] = _dot(h, w_ref[:, cols]).astype(BF16)


def _inproj(xf, norm_w, w_main, w_gate, w_gate_t, w_v_t, conv_w, seq_len):
    t = xf.shape[0]
    tiles_per_seq = seq_len // TM_IN
    return pl.pallas_call(
        functools.partial(_inproj_kernel, tiles_per_seq),
        grid=(t // TM_IN,),
        in_specs=[
            pl.BlockSpec((TM_IN, D_MODEL), lambda i: (i, 0)),
            pl.BlockSpec((1, D_MODEL), lambda i: (0, 0)),
            pl.BlockSpec((D_MODEL, N_MAIN), lambda i: (0, 0)),
            pl.BlockSpec((D_MODEL, LANES), lambda i: (0, 0)),
            pl.BlockSpec((N_GATE, D_MODEL), lambda i: (0, 0)),
            pl.BlockSpec((DIFF_WIDTH, D_MODEL), lambda i: (0, 0)),
            pl.BlockSpec((CONV_K, N_CONV), lambda i: (0, 0)),
        ],
        out_specs=[
            pl.BlockSpec((TM_IN, N_MAIN), lambda i: (i, 0)),
            pl.BlockSpec((TM_IN, LANES), lambda i: (i, 0)),
            pl.BlockSpec((N_GATE, TM_IN), lambda i: (0, i)),
            pl.BlockSpec((DIFF_WIDTH, TM_IN), lambda i: (0, i)),
        ],
        out_shape=[
            jax.ShapeDtypeStruct((t, N_MAIN), BF16),
            jax.ShapeDtypeStruct((t, LANES), F32),
            jax.ShapeDtypeStruct((N_GATE, t), F32),
            jax.ShapeDtypeStruct((DIFF_WIDTH, t), BF16),
        ],
        scratch_shapes=[pltpu.VMEM((TAIL, N_CONV), F32)],
        compiler_params=pltpu.CompilerParams(
            dimension_semantics=("arbitrary",), vmem_limit_bytes=VMEM_LIMIT),
        name="inproj",
    )(xf, norm_w, w_main, w_gate, w_gate_t, w_v_t, conv_w)


def _gdn_kernel(act_ref, gate_ref, gate_t_ref, alog_l_ref, dtb_l_ref, alog_s_ref, dtb_s_ref,
                gnw_ref, o_ref, state_ref, mask_ref, bd_ref):
    seq_len = act_ref.shape[0]
    n_chunks = SC // CHUNK
    heads = range(GDN_HEADS)
    i_cat = lax.broadcasted_iota(jnp.int32, (CHUNK, SC), 0)
    j_cat = lax.broadcasted_iota(jnp.int32, (CHUNK, SC), 1) & (CHUNK - 1)
    mask_ref[0] = jnp.where(i_cat >= j_cat, 0.0, MASK_VALUE)
    mask_ref[1] = jnp.where(i_cat > j_cat, -1.0, 0.0)
    mask_ref[2] = jnp.where(i_cat == j_cat, 1.0, 0.0)
    r = lax.broadcasted_iota(jnp.int32, (SC, SC), 0)
    c = lax.broadcasted_iota(jnp.int32, (SC, SC), 1)
    bd_ref[...] = jnp.where((r & -CHUNK) == (c & -CHUNK), 1.0, 0.0).astype(BF16)
    state_ref[...] = jnp.zeros_like(state_ref)
    lane_chunk = lax.broadcasted_iota(jnp.int32, (1, SC), 1) & -CHUNK

    sub_in_chunk = lax.broadcasted_iota(jnp.int32, (SC, LANES), 0) & (CHUNK - 1)
    lane_in_chunk = lax.broadcasted_iota(jnp.int32, (N_GATE, SC), 1) & (CHUNK - 1)
    neg_a_l = -jnp.exp(alog_l_ref[...])
    neg_a_s = -jnp.exp(alog_s_ref[...])
    scale = HEAD_DIM ** -0.5

    def cat(x):
        out = x[(n_chunks - 1) * CHUNK:, :]
        for ci in range(n_chunks - 2, -1, -1):
            out = jnp.where(lane_chunk == ci * CHUNK, x[ci * CHUNK:(ci + 1) * CHUNK, :], out)
        return out

    def block_diag(x_cat):
        return jnp.concatenate([x_cat.astype(BF16)] * n_chunks, axis=0) * bd_ref[...]

    def head_cols(col, h):
        return slice((col + h) * HEAD_DIM, (col + h + 1) * HEAD_DIM)

    def sc_body(sc, carry):
        r0 = pl.multiple_of(sc * SC, SC)
        rows = pl.ds(r0, SC)
        gate = gate_ref[rows, :]
        beta_l = _sigmoid(gate)
        logbeta_l = -_softplus(-gate)
        gc_l = neg_a_l * _softplus(gate + dtb_l_ref[...])
        gc_s = neg_a_s * _softplus(gate_t_ref[:, rows] + dtb_s_ref[...])
        step = 1
        while step < CHUNK:
            gc_l = gc_l + jnp.where(sub_in_chunk >= step, pltpu.roll(gc_l, step, axis=0), 0.0)
            gc_s = gc_s + jnp.where(lane_in_chunk >= step, pltpu.roll(gc_s, step, axis=1), 0.0)
            step *= 2
        causal, neg_strict, eye = mask_ref[0], mask_ref[1], mask_ref[2]

        n_pow, a_cat, rhs, q_dec, k_dec, g_last = [], [], [], [], [], []
        for h in heads:
            q = act_ref[rows, head_cols(COL_GQ, h)].astype(F32)
            k = act_ref[rows, head_cols(COL_GK, h)].astype(F32)
            v = act_ref[rows, head_cols(COL_GV, h)].astype(F32)
            q = q * (lax.rsqrt(jnp.sum(q * q, axis=-1, keepdims=True) + 1e-6) * scale)
            k = k * lax.rsqrt(jnp.sum(k * k, axis=-1, keepdims=True) + 1e-6)
            gcol = gc_l[:, GDN_HEADS + h:GDN_HEADS + h + 1]
            bcol = beta_l[:, h:h + 1]
            grow = gc_s[GDN_HEADS + h:GDN_HEADS + h + 1, :]
            kb = k.astype(BF16)
            gram = _dot_nt(jnp.concatenate([kb, q.astype(BF16)], axis=0), kb)
            e_cat = cat(gcol) - grow + causal
            n_pow.append(cat(gram[:SC]) * jnp.exp(e_cat + cat(logbeta_l[:, h:h + 1])) * neg_strict)
            a_cat.append(cat(gram[SC:]) * jnp.exp(e_cat))
            e_g = jnp.exp(gcol)
            rhs.append(jnp.concatenate([bcol * v, (bcol * e_g) * k], axis=1).astype(BF16))
            q_dec.append(q * e_g)
            g_last.append([gcol[(ci + 1) * CHUNK - 1:(ci + 1) * CHUNK, :] for ci in range(n_chunks)])
            k_dec.append([(k[ci * CHUNK:(ci + 1) * CHUNK, :]
                           * jnp.exp(g_last[h][ci] - gcol[ci * CHUNK:(ci + 1) * CHUNK, :])).astype(BF16)
                          for ci in range(n_chunks)])

        t_cat = [eye + n_pow[h] for h in heads]
        n_pow = [_dot(n_pow[h].astype(BF16), block_diag(n_pow[h])) for h in heads]
        span = 2
        while span < CHUNK // 2:
            both = [_dot(jnp.concatenate([t_cat[h], n_pow[h]], axis=0).astype(BF16), block_diag(n_pow[h]))
                    for h in heads]
            t_cat = [t_cat[h] + both[h][:CHUNK] for h in heads]
            n_pow = [both[h][CHUNK:] for h in heads]
            span *= 2
        t_cat = [t_cat[h] + _dot(t_cat[h].astype(BF16), block_diag(n_pow[h])) for h in heads]

        uw = [_dot(block_diag(t_cat[h]), rhs[h]) for h in heads]
        kt = [[_dot_tn(k_dec[h][ci], uw[h][ci * CHUNK:(ci + 1) * CHUNK, :].astype(BF16))
               for ci in range(n_chunks)] for h in heads]
        state = [state_ref[h] for h in heads]
        state_in = [[None] * n_chunks for _ in heads]
        for ci in range(n_chunks):
            for h in heads:
                sb = state[h].astype(BF16)
                state_in[h][ci] = sb
                state[h] = (state[h] * jnp.exp(g_last[h][ci])
                            - _dot(kt[h][ci][:, HEAD_DIM:].astype(BF16), sb) + kt[h][ci][:, :HEAD_DIM])
        for h in heads:
            state_ref[h] = state[h]

        for h in heads:
            inter = []
            for ci in range(n_chunks):
                cr = slice(ci * CHUNK, (ci + 1) * CHUNK)
                lhs = jnp.concatenate([uw[h][cr, HEAD_DIM:], q_dec[h][cr, :]], axis=0).astype(BF16)
                inter.append(_dot(lhs, state_in[h][ci]))
            v_new = jnp.concatenate([uw[h][ci * CHUNK:(ci + 1) * CHUNK, :HEAD_DIM] - inter[ci][:CHUNK]
                                     for ci in range(n_chunks)], axis=0).astype(BF16)
            o = (jnp.concatenate([inter[ci][CHUNK:] for ci in range(n_chunks)], axis=0)
                 + _dot(block_diag(a_cat[h]), v_new))
            o = o * lax.rsqrt(jnp.mean(o * o, axis=-1, keepdims=True) + NORM_EPS)
            z = act_ref[rows, head_cols(COL_GZ, h)].astype(F32)
            o_ref[rows, head_cols(0, h)] = (o * (z * _sigmoid(z) * gnw_ref[...])).astype(BF16)
        return carry

    lax.fori_loop(0, seq_len // SC, sc_body, 0)


def _gdn(act, gate, gate_t, alog_l, dtb_l, alog_s, dtb_s, gdn_norm_w, batch, seq_len):
    t = act.shape[0]
    small = lambda shape: pl.BlockSpec(shape, lambda b: (0, 0))
    return pl.pallas_call(
        _gdn_kernel,
        grid=(batch,),
        in_specs=[
            pl.BlockSpec((seq_len, 4 * GDN_WIDTH), lambda b: (b, 0)),
            pl.BlockSpec((seq_len, LANES), lambda b: (b, 0)),
            pl.BlockSpec((N_GATE, seq_len), lambda b: (0, b)),
            small((1, LANES)), small((1, LANES)), small((N_GATE, 1)), small((N_GATE, 1)),
            small((1, HEAD_DIM)),
        ],
        out_specs=pl.BlockSpec((seq_len, GDN_WIDTH), lambda b: (b, 0)),
        out_shape=jax.ShapeDtypeStruct((t, GDN_WIDTH), BF16),
        scratch_shapes=[
            pltpu.VMEM((GDN_HEADS, HEAD_DIM, HEAD_DIM), F32),
            pltpu.VMEM((3, CHUNK, SC), F32),
            pltpu.VMEM((SC, SC), BF16),
        ],
        compiler_params=pltpu.CompilerParams(
            dimension_semantics=("arbitrary",), vmem_limit_bytes=VMEM_LIMIT),
        name="gdn",
    )(act, gate, gate_t, alog_l, dtb_l, alog_s, dtb_s, gdn_norm_w)


def _split3(x):
    hi = x.astype(BF16).astype(F32)
    r1 = x - hi
    mid = r1.astype(BF16).astype(F32)
    return hi, mid, r1 - mid


def _attn_kernel(lam_init, q_ref, k_ref, vt_ref, z_ref, lq1_ref, lk1_ref, lq2_ref, lk2_ref,
                 nw_ref, o_ref, bias_ref, kext_ref, s_ref, acc_ref, m_ref, l_ref):
    h = pl.program_id(0)
    b = pl.program_id(1)
    qi = pl.program_id(2)
    seq_len = k_ref.shape[0]
    lane = lax.broadcasted_iota(jnp.int32, (1, HEAD_DIM), 1)
    n_bias = 6
    log2e = math.log2(math.e)

    @pl.when((b == 0) & (qi == 0))
    def _():
        slope = lax.shift_left(jnp.int32(1), 2 * (DIFF_HEADS - 1 - h)).astype(F32) * (log2e / 256.0)
        pos = lax.broadcasted_iota(jnp.int32, (seq_len, 1), 0)
        in_block = (pos & (BK - 1)).astype(F32) * slope
        block_off = (pos & -BK).astype(F32) * slope
        terms = _split3(in_block) + _split3(block_off)
        tile = jnp.zeros((seq_len, HEAD_DIM), F32)
        for j, term in enumerate(terms):
            tile = jnp.where((lane == j) | (lane == DIFF_QK_DIM + j), term, tile)
        bias_ref[...] = tile

    @pl.when(qi == 0)
    def _():
        k = k_ref[...].astype(F32)
        kext_ref[0] = jnp.where(lane < DIFF_QK_DIM, k, bias_ref[...]).astype(BF16)
        kext_ref[1] = jnp.where(lane >= DIFF_QK_DIM, k, bias_ref[...]).astype(BF16)

    q = q_ref[...].astype(F32) * (DIFF_QK_DIM ** -0.5 * log2e)
    q_ext = (
        jnp.where(lane < DIFF_QK_DIM, q, jnp.where(lane < DIFF_QK_DIM + n_bias, 1.0, 0.0)).astype(BF16),
        jnp.where(lane >= DIFF_QK_DIM, q, jnp.where(lane < n_bias, 1.0, 0.0)).astype(BF16),
    )
    m_ref[...] = jnp.full_like(m_ref, MASK_VALUE)
    l_ref[...] = jnp.zeros_like(l_ref)
    acc_ref[...] = jnp.zeros_like(acc_ref)

    def scores(kb, slot):
        rows = pl.ds(pl.multiple_of(kb * BK, BK), BK)
        for c in range(2):
            s_ref[slot, c] = _dot_nt(kext_ref[c, rows, :], q_ext[c])

    def accumulate(kb, slot, masked):
        rows = pl.ds(pl.multiple_of(kb * BK, BK), BK)
        vt = vt_ref[:, rows]
        for c in range(2):
            s = s_ref[slot, c]
            if masked:
                kpos = kb * BK + lax.broadcasted_iota(jnp.int32, (BK, BQ), 0)
                qpos = qi * BQ + lax.broadcasted_iota(jnp.int32, (BK, BQ), 1)
                s = jnp.where(kpos <= qpos, s, MASK_VALUE)
            m_old = m_ref[c]
            m_new = jnp.maximum(m_old, jnp.max(s, axis=0, keepdims=True))
            alpha = jnp.exp2(m_old - m_new)
            p = jnp.exp2(s - m_new)
            l_ref[c] = alpha * l_ref[c] + jnp.sum(p, axis=0, keepdims=True)
            acc_ref[c] = alpha * acc_ref[c] + _dot(vt, p.astype(BF16))
            m_ref[c] = m_new

    assert BQ == 2 * BK
    scores(0, 0)

    def body(j, carry):
        scores(2 * j + 1, 1)
        accumulate(2 * j, 0, False)
        scores(2 * j + 2, 0)
        accumulate(2 * j + 1, 1, False)
        return carry

    lax.fori_loop(0, qi, body, 0)
    scores(2 * qi + 1, 1)
    accumulate(2 * qi, 0, True)
    accumulate(2 * qi + 1, 1, True)

    lam = (jnp.exp(jnp.sum(lq1_ref[...] * lk1_ref[...], axis=-1, keepdims=True))
           - jnp.exp(jnp.sum(lq2_ref[...] * lk2_ref[...], axis=-1, keepdims=True)) + lam_init)
    o_t = acc_ref[0] / l_ref[0] - lam * (acc_ref[1] / l_ref[1])
    o = o_t.T
    o = o * lax.rsqrt(jnp.mean(o * o, axis=-1, keepdims=True) + NORM_EPS)
    o = o * nw_ref[...] * (1.0 - lam_init)
    z = z_ref[...].astype(F32)
    o_ref[...] = (o * (z * _sigmoid(z))).astype(BF16)


def _attn(act, v_t, lq1, lk1, lq2, lk2, diff_norm_w, lam_init, batch, seq_len):
    t = act.shape[0]
    nq = seq_len // BQ
    small = lambda shape: pl.BlockSpec(shape, lambda h, b, i: (0, 0))
    return pl.pallas_call(
        functools.partial(_attn_kernel, lam_init),
        grid=(DIFF_HEADS, batch, nq),
        in_specs=[
            pl.BlockSpec((BQ, HEAD_DIM), lambda h, b, i: (b * nq + i, COL_DQ + h)),
            pl.BlockSpec((seq_len, HEAD_DIM), lambda h, b, i: (b, COL_DK + h)),
            pl.BlockSpec((HEAD_DIM, seq_len), lambda h, b, i: (h, b)),
            pl.BlockSpec((BQ, HEAD_DIM), lambda h, b, i: (b * nq + i, COL_DZ + h)),
            small((1, DIFF_QK_DIM)), small((1, DIFF_QK_DIM)),
            small((1, DIFF_QK_DIM)), small((1, DIFF_QK_DIM)),
            small((1, HEAD_DIM)),
        ],
        out_specs=pl.BlockSpec((BQ, HEAD_DIM), lambda h, b, i: (b * nq + i, h)),
        out_shape=jax.ShapeDtypeStruct((t, DIFF_WIDTH), BF16),
        scratch_shapes=[
            pltpu.VMEM((seq_len, HEAD_DIM), F32),
            pltpu.VMEM((2, seq_len, HEAD_DIM), BF16),
            pltpu.VMEM((2, 2, BK, BQ), F32),
            pltpu.VMEM((2, HEAD_DIM, BQ), F32),
            pltpu.VMEM((2, 1, BQ), F32),
            pltpu.VMEM((2, 1, BQ), F32),
        ],
        compiler_params=pltpu.CompilerParams(
            dimension_semantics=("arbitrary", "arbitrary", "arbitrary"),
            vmem_limit_bytes=VMEM_LIMIT),
        name="diffattn",
    )(act, act, v_t, act, lq1, lk1, lq2, lk2, diff_norm_w)


def _outproj_kernel(oa_ref, ob_ref, x_ref, w_ref, fw_ref, out_ref):
    mix = _dot(oa_ref[...], w_ref[0:GDN_WIDTH, :]) + _dot(ob_ref[...], w_ref[GDN_WIDTH:, :])
    y = x_ref[...] + mix
    ms = jnp.mean(y * y, axis=-1, keepdims=True)
    out_ref[...] = y * lax.rsqrt(ms + NORM_EPS) * fw_ref[...]


def _outproj(o_a, o_b, xf, w_out, final_norm_w):
    t = xf.shape[0]
    return pl.pallas_call(
        _outproj_kernel,
        grid=(t // TM_OUT,),
        in_specs=[
            pl.BlockSpec((TM_OUT, GDN_WIDTH), lambda i: (i, 0)),
            pl.BlockSpec((TM_OUT, DIFF_WIDTH), lambda i: (i, 0)),
            pl.BlockSpec((TM_OUT, D_MODEL), lambda i: (i, 0)),
            pl.BlockSpec((D_MODEL, D_MODEL), lambda i: (0, 0)),
            pl.BlockSpec((1, D_MODEL), lambda i: (0, 0)),
        ],
        out_specs=pl.BlockSpec((TM_OUT, D_MODEL), lambda i: (i, 0)),
        out_shape=jax.ShapeDtypeStruct((t, D_MODEL), F32),
        compiler_params=pltpu.CompilerParams(
            dimension_semantics=("arbitrary",), vmem_limit_bytes=VMEM_LIMIT),
        name="outproj",
    )(o_a, o_b, xf, w_out, final_norm_w)


def kernel(x, norm_w, w_in, conv_w, a_log, dt_bias, gdn_norm_w, lambda_q1, lambda_k1,
           lambda_q2, lambda_k2, diff_norm_w, w_out, final_norm_w):
    batch, seq_len, d_model = x.shape
    depth = norm_w.shape[0]
    assert depth == 1 and d_model == D_MODEL
    assert seq_len % TM_IN == 0 and seq_len % SC == 0 and seq_len % BQ == 0 and BQ % BK == 0
    n_wide_a = 4 * GDN_WIDTH
    xf = x.reshape(batch * seq_len, d_model)

    w = w_in[0]
    c_dq = n_wide_a + N_GATE
    c_dv = c_dq + 2 * DIFF_WIDTH
    c_dz = c_dv + DIFF_WIDTH
    w_main = jnp.concatenate([w[:, :n_wide_a], w[:, c_dq:c_dv], w[:, c_dz:]], axis=1).astype(BF16)
    w_v_t = w[:, c_dv:c_dz].T.astype(BF16)
    w_gate_cols = w[:, n_wide_a:n_wide_a + N_GATE]
    w_gate = jnp.pad(w_gate_cols, ((0, 0), (0, LANES - N_GATE))).astype(BF16)
    w_gate_t = w_gate_cols.T.astype(BF16)

    act, gate, gate_t, v_t = _inproj(xf, norm_w[0][None, :], w_main, w_gate, w_gate_t, w_v_t,
                                     conv_w[0], seq_len)

    pad_l = lambda vec: jnp.pad(vec, (GDN_HEADS, LANES - N_GATE))[None, :]
    pad_s = lambda vec: jnp.pad(vec, (GDN_HEADS, 0))[:, None]
    o_a = _gdn(act, gate, gate_t, pad_l(a_log[0]), pad_l(dt_bias[0]), pad_s(a_log[0]),
               pad_s(dt_bias[0]), gdn_norm_w[0][None, :], batch, seq_len)

    lam_init = 0.8 - 0.6 * math.exp(-0.3 * 0)
    o_b = _attn(act, v_t, lambda_q1[0][None, :], lambda_k1[0][None, :], lambda_q2[0][None, :],
                lambda_k2[0][None, :], diff_norm_w[0][None, :], lam_init, batch, seq_len)

    out = _outproj(o_a, o_b, xf, w_out[0].astype(BF16), final_norm_w[None, :])
    return out.reshape(batch, seq_len, d_model)
```

```python
import functools
import math

import jax
import jax.numpy as jnp
from jax import lax
from jax.experimental import pallas as pl
from jax.experimental.pallas import tpu as pltpu

F32 = jnp.float32
BF16 = jnp.bfloat16

D_MODEL = 1024
GDN_HEADS = 4
HEAD_DIM = 128
GDN_WIDTH = GDN_HEADS * HEAD_DIM
DIFF_HEADS = 4
DIFF_QK_DIM = 64
DIFF_WIDTH = DIFF_HEADS * HEAD_DIM
CONV_K = 4
NORM_EPS = 1e-6
N_GATE = 2 * GDN_HEADS
N_MAIN = 4 * GDN_WIDTH + 3 * DIFF_WIDTH
N_CONV = 3 * GDN_WIDTH
LANES = 128
MASK_VALUE = -1e30

COL_GQ, COL_GK, COL_GV, COL_GZ = 0, 4, 8, 12
COL_DQ, COL_DK, COL_DZ = 16, 20, 24

TM_IN = 1024
NC_IN = 256
TAIL = 16
TM_OUT = 1024
SC = 256
CHUNK = 64
BQ = 512
BK = 256
V_PAD = 16
VMEM_LIMIT = 48 * 1024 * 1024


def _sigmoid(x):
    return 1.0 / (1.0 + jnp.exp(-x))


def _silu(x):
    half = 0.5 * x
    return half + half * jnp.tanh(half)


def _softplus(x):
    return jnp.maximum(x, 0.0) + jnp.log1p(jnp.exp(-jnp.abs(x)))


def _dot(a, b):
    return jnp.dot(a, b, preferred_element_type=F32)


def _dot_nt(a, b):
    return lax.dot_general(a, b, (((1,), (1,)), ((), ())), preferred_element_type=F32)


def _dot_tn(a, b):
    return lax.dot_general(a, b, (((0,), (0,)), ((), ())), preferred_element_type=F32)


def _inproj_kernel(tiles_per_seq, x_ref, nw_ref, w_ref, wg_ref, wgt_ref, wvt_ref, cw_ref,
                   main_ref, gate_ref, gate_t_ref, vt_ref, tail_ref):
    i = pl.program_id(0)
    x = x_ref[...]
    ms = jnp.mean(x * x, axis=-1, keepdims=True)
    h = (x * lax.rsqrt(ms + NORM_EPS) * nw_ref[...]).astype(BF16)

    gate_ref[...] = _dot(h, wg_ref[...])
    gate_t_ref[...] = _dot_nt(wgt_ref[...], h)
    vt_ref[...] = _dot_nt(wvt_ref[...], h).astype(BF16)

    @pl.when(i % tiles_per_seq == 0)
    def _():
        tail_ref[...] = jnp.zeros_like(tail_ref)

    tm = x.shape[0]
    row = lax.broadcasted_iota(jnp.int32, (TAIL, LANES), 0)

    def conv_silu(acc, n0):
        cols = slice(n0, n0 + LANES)
        last = acc[tm - TAIL:, :]
        delta = tail_ref[:, cols] - last
        tail_ref[:, cols] = last
        y = acc * cw_ref[CONV_K - 1:CONV_K, cols]
        fix = jnp.zeros((TAIL, LANES), F32)
        for s in range(1, CONV_K):
            wk = cw_ref[CONV_K - 1 - s:CONV_K - s, cols]
            y = y + pltpu.roll(acc, s, axis=0) * wk
            fix = fix + jnp.where(row < s, pltpu.roll(delta, s, axis=0), 0.0) * wk
        main_ref[:, cols] = _silu(y).astype(BF16)
        main_ref[0:TAIL, cols] = _silu(y[:TAIL, :] + fix).astype(BF16)

    plain = [slice(n0, n0 + NC_IN) for n0 in range(N_CONV, N_MAIN, NC_IN)]
    pending = []
    for n0 in range(0, N_CONV, NC_IN):
        acc = _dot(h, w_ref[:, n0:n0 + NC_IN])
        if pending:
            conv_silu(*pending.pop(0))
        if plain:
            pc = plain.pop(0)
            main_ref[:, pc] = _dot(h, w_ref[:, pc]).astype(BF16)
        if pending:
            conv_silu(*pending.pop(0))
        pending = [(acc[:, j:j + LANES], n0 + j) for j in range(0, NC_IN, LANES)]
    for piece in pending:
        if plain:
            pc = plain.pop(0)
            main_ref[:, pc] = _dot(h, w_ref[:, pc]).astype(BF16)
        conv_silu(*piece)
    for pc in plain:
        main_ref[:, pc] = _dot(h, w_ref[:, pc]).astype(BF16)


def _inproj(xf, norm_w, w_main, w_gate, w_gate_t, w_v_t, conv_w, seq_len):
    t = xf.shape[0]
    tiles_per_seq = seq_len // TM_IN
    return pl.pallas_call(
        functools.partial(_inproj_kernel, tiles_per_seq),
        grid=(t // TM_IN,),
        in_specs=[
            pl.BlockSpec((TM_IN, D_MODEL), lambda i: (i, 0)),
            pl.BlockSpec((1, D_MODEL), lambda i: (0, 0)),
            pl.BlockSpec((D_MODEL, N_MAIN), lambda i: (0, 0)),
            pl.BlockSpec((D_MODEL, LANES), lambda i: (0, 0)),
            pl.BlockSpec((N_GATE, D_MODEL), lambda i: (0, 0)),
            pl.BlockSpec((DIFF_WIDTH, D_MODEL), lambda i: (0, 0)),
            pl.BlockSpec((CONV_K, N_CONV), lambda i: (0, 0)),
        ],
        out_specs=[
            pl.BlockSpec((TM_IN, N_MAIN), lambda i: (i, 0)),
            pl.BlockSpec((TM_IN, LANES), lambda i: (i, 0)),
            pl.BlockSpec((N_GATE, TM_IN), lambda i: (0, i)),
            pl.BlockSpec((DIFF_WIDTH, TM_IN), lambda i: (0, i)),
        ],
        out_shape=[
            jax.ShapeDtypeStruct((t, N_MAIN), BF16),
            jax.ShapeDtypeStruct((t, LANES), F32),
            jax.ShapeDtypeStruct((N_GATE, t), F32),
            jax.ShapeDtypeStruct((DIFF_WIDTH, t), BF16),
        ],
        scratch_shapes=[pltpu.VMEM((TAIL, N_CONV), F32)],
        compiler_params=pltpu.CompilerParams(
            dimension_semantics=("arbitrary",), vmem_limit_bytes=VMEM_LIMIT),
        name="inproj",
    )(xf, norm_w, w_main, w_gate, w_gate_t, w_v_t, conv_w)


def _gdn_kernel(act_ref, gate_ref, gate_t_ref, alog_l_ref, dtb_l_ref, alog_s_ref, dtb_s_ref,
                gnw_ref, o_ref, state_ref, mask_ref, bd_ref):
    seq_len = act_ref.shape[0]
    n_chunks = SC // CHUNK
    heads = range(GDN_HEADS)
    i_cat = lax.broadcasted_iota(jnp.int32, (CHUNK, SC), 0)
    j_cat = lax.broadcasted_iota(jnp.int32, (CHUNK, SC), 1) & (CHUNK - 1)
    mask_ref[0] = jnp.where(i_cat >= j_cat, 0.0, MASK_VALUE)
    mask_ref[1] = jnp.where(i_cat > j_cat, -1.0, 0.0)
    mask_ref[2] = jnp.where(i_cat == j_cat, 1.0, 0.0)
    r = lax.broadcasted_iota(jnp.int32, (SC, SC), 0)
    c = lax.broadcasted_iota(jnp.int32, (SC, SC), 1)
    bd_ref[...] = jnp.where((r & -CHUNK) == (c & -CHUNK), 1.0, 0.0).astype(BF16)
    state_ref[...] = jnp.zeros_like(state_ref)
    lane_chunk = lax.broadcasted_iota(jnp.int32, (1, SC), 1) & -CHUNK

    sub_in_chunk = lax.broadcasted_iota(jnp.int32, (SC, LANES), 0) & (CHUNK - 1)
    lane_in_chunk = lax.broadcasted_iota(jnp.int32, (N_GATE, SC), 1) & (CHUNK - 1)
    neg_a_l = -jnp.exp(alog_l_ref[...])
    neg_a_s = -jnp.exp(alog_s_ref[...])
    scale = HEAD_DIM ** -0.5

    def cat(x):
        out = x[(n_chunks - 1) * CHUNK:, :]
        for ci in range(n_chunks - 2, -1, -1):
            out = jnp.where(lane_chunk == ci * CHUNK, x[ci * CHUNK:(ci + 1) * CHUNK, :], out)
        return out

    def block_diag(x_cat):
        return jnp.concatenate([x_cat.astype(BF16)] * n_chunks, axis=0) * bd_ref[...]

    def head_cols(col, h):
        return slice((col + h) * HEAD_DIM, (col + h + 1) * HEAD_DIM)

    def sc_body(sc, carry):
        r0 = pl.multiple_of(sc * SC, SC)
        rows = pl.ds(r0, SC)
        gate = gate_ref[rows, :]
        beta_l = _sigmoid(gate)
        logbeta_l = -_softplus(-gate)
        gc_l = neg_a_l * _softplus(gate + dtb_l_ref[...])
        gc_s = neg_a_s * _softplus(gate_t_ref[:, rows] + dtb_s_ref[...])
        step = 1
        while step < CHUNK:
            gc_l = gc_l + jnp.where(sub_in_chunk >= step, pltpu.roll(gc_l, step, axis=0), 0.0)
            gc_s = gc_s + jnp.where(lane_in_chunk >= step, pltpu.roll(gc_s, step, axis=1), 0.0)
            step *= 2
        causal, neg_strict, eye = mask_ref[0], mask_ref[1], mask_ref[2]

        n_pow, a_cat, rhs, q_dec, k_dec, g_last = [], [], [], [], [], []
        for h in heads:
            q = act_ref[rows, head_cols(COL_GQ, h)].astype(F32)
            k = act_ref[rows, head_cols(COL_GK, h)].astype(F32)
            v = act_ref[rows, head_cols(COL_GV, h)].astype(F32)
            q = q * (lax.rsqrt(jnp.sum(q * q, axis=-1, keepdims=True) + 1e-6) * scale)
            k = k * lax.rsqrt(jnp.sum(k * k, axis=-1, keepdims=True) + 1e-6)
            gcol = gc_l[:, GDN_HEADS + h:GDN_HEADS + h + 1]
            bcol = beta_l[:, h:h + 1]
            grow = gc_s[GDN_HEADS + h:GDN_HEADS + h + 1, :]
            kb = k.astype(BF16)
            gram = _dot_nt(jnp.concatenate([kb, q.astype(BF16)], axis=0), kb)
            e_cat = cat(gcol) - grow + causal
            n_pow.append(cat(gram[:SC]) * jnp.exp(e_cat + cat(logbeta_l[:, h:h + 1])) * neg_strict)
            a_cat.append(cat(gram[SC:]) * jnp.exp(e_cat))
            e_g = jnp.exp(gcol)
            rhs.append(jnp.concatenate([bcol * v, (bcol * e_g) * k], axis=1).astype(BF16))
            q_dec.append(q * e_g)
            g_last.append([gcol[(ci + 1) * CHUNK - 1:(ci + 1) * CHUNK, :] for ci in range(n_chunks)])
            k_dec.append([(k[ci * CHUNK:(ci + 1) * CHUNK, :]
                           * jnp.exp(g_last[h][ci] - gcol[ci * CHUNK:(ci + 1) * CHUNK, :])).astype(BF16)
                          for ci in range(n_chunks)])

        t_cat = [eye + n_pow[h] for h in heads]
        n_pow = [_dot(n_pow[h].astype(BF16), block_diag(n_pow[h])) for h in heads]
        span = 2
        while span < CHUNK // 2:
            both = [_dot(jnp.concatenate([t_cat[h], n_pow[h]], axis=0).astype(BF16), block_diag(n_pow[h]))
                    for h in heads]
            t_cat = [t_cat[h] + both[h][:CHUNK] for h in heads]
            n_pow = [both[h][CHUNK:] for h in heads]
            span *= 2
        t_cat = [t_cat[h] + _dot(t_cat[h].astype(BF16), block_diag(n_pow[h])) for h in heads]

        uw = [_dot(block_diag(t_cat[h]), rhs[h]) for h in heads]
        kt = [[_dot_tn(k_dec[h][ci], uw[h][ci * CHUNK:(ci + 1) * CHUNK, :].astype(BF16))
               for ci in range(n_chunks)] for h in heads]
        state = [state_ref[h] for h in heads]
        state_in = [[None] * n_chunks for _ in heads]
        for ci in range(n_chunks):
            for h in heads:
                sb = state[h].astype(BF16)
                state_in[h][ci] = sb
                state[h] = (state[h] * jnp.exp(g_last[h][ci])
                            - _dot(kt[h][ci][:, HEAD_DIM:].astype(BF16), sb) + kt[h][ci][:, :HEAD_DIM])
        for h in heads:
            state_ref[h] = state[h]

        for h in heads:
            inter = []
            for ci in range(n_chunks):
                cr = slice(ci * CHUNK, (ci + 1) * CHUNK)
                lhs = jnp.concatenate([uw[h][cr, HEAD_DIM:], q_dec[h][cr, :]], axis=0).astype(BF16)
                inter.append(_dot(lhs, state_in[h][ci]))
            v_new = jnp.concatenate([uw[h][ci * CHUNK:(ci + 1) * CHUNK, :HEAD_DIM] - inter[ci][:CHUNK]
                                     for ci in range(n_chunks)], axis=0).astype(BF16)
            o = (jnp.concatenate([inter[ci][CHUNK:] for ci in range(n_chunks)], axis=0)
                 + _dot(block_diag(a_cat[h]), v_new))
            o = o * lax.rsqrt(jnp.mean(o * o, axis=-1, keepdims=True) + NORM_EPS)
            z = act_ref[rows, head_cols(COL_GZ, h)].astype(F32)
            o_ref[rows, head_cols(0, h)] = (o * (_silu(z) * gnw_ref[...])).astype(BF16)
        return carry

    lax.fori_loop(0, seq_len // SC, sc_body, 0)


def _gdn(act, gate, gate_t, alog_l, dtb_l, alog_s, dtb_s, gdn_norm_w, batch, seq_len):
    t = act.shape[0]
    small = lambda shape: pl.BlockSpec(shape, lambda b: (0, 0))
    return pl.pallas_call(
        _gdn_kernel,
        grid=(batch,),
        in_specs=[
            pl.BlockSpec((seq_len, 4 * GDN_WIDTH), lambda b: (b, 0)),
            pl.BlockSpec((seq_len, LANES), lambda b: (b, 0)),
            pl.BlockSpec((N_GATE, seq_len), lambda b: (0, b)),
            small((1, LANES)), small((1, LANES)), small((N_GATE, 1)), small((N_GATE, 1)),
            small((1, HEAD_DIM)),
        ],
        out_specs=pl.BlockSpec((seq_len, GDN_WIDTH), lambda b: (b, 0)),
        out_shape=jax.ShapeDtypeStruct((t, GDN_WIDTH), BF16),
        scratch_shapes=[
            pltpu.VMEM((GDN_HEADS, HEAD_DIM, HEAD_DIM), F32),
            pltpu.VMEM((3, CHUNK, SC), F32),
            pltpu.VMEM((SC, SC), BF16),
        ],
        compiler_params=pltpu.CompilerParams(
            dimension_semantics=("arbitrary",), vmem_limit_bytes=VMEM_LIMIT),
        name="gdn",
    )(act, gate, gate_t, alog_l, dtb_l, alog_s, dtb_s, gdn_norm_w)


def _split3(x):
    hi = x.astype(BF16).astype(F32)
    r1 = x - hi
    mid = r1.astype(BF16).astype(F32)
    return hi, mid, r1 - mid


def _attn_kernel(lam_init, q_ref, k_ref, vt_ref, z_ref, lq1_ref, lk1_ref, lq2_ref, lk2_ref,
                 nw_ref, o_ref, bias_ref, kext_ref, vtx_ref, s_ref, acc_ref, m_ref):
    h = pl.program_id(0)
    b = pl.program_id(1)
    qi = pl.program_id(2)
    seq_len = k_ref.shape[0]
    lane = lax.broadcasted_iota(jnp.int32, (1, HEAD_DIM), 1)
    n_bias = 6
    log2e = math.log2(math.e)

    @pl.when((b == 0) & (qi == 0))
    def _():
        slope = lax.shift_left(jnp.int32(1), 2 * (DIFF_HEADS - 1 - h)).astype(F32) * (log2e / 256.0)
        pos = lax.broadcasted_iota(jnp.int32, (seq_len, 1), 0)
        in_block = (pos & (BK - 1)).astype(F32) * slope
        block_off = (pos & -BK).astype(F32) * slope
        terms = _split3(in_block) + _split3(block_off)
        tile = jnp.zeros((seq_len, HEAD_DIM), F32)
        for j, term in enumerate(terms):
            tile = jnp.where((lane == j) | (lane == DIFF_QK_DIM + j), term, tile)
        bias_ref[...] = tile

    @pl.when(qi == 0)
    def _():
        k = k_ref[...].astype(F32)
        kext_ref[0] = jnp.where(lane < DIFF_QK_DIM, k, bias_ref[...]).astype(BF16)
        kext_ref[1] = jnp.where(lane >= DIFF_QK_DIM, k, bias_ref[...]).astype(BF16)
        vtx_ref[0:HEAD_DIM, :] = vt_ref[...]
        ones_row = lax.broadcasted_iota(jnp.int32, (V_PAD, seq_len), 0) == 0
        vtx_ref[HEAD_DIM:, :] = jnp.where(ones_row, 1.0, 0.0).astype(BF16)

    q = q_ref[...].astype(F32) * (DIFF_QK_DIM ** -0.5 * log2e)
    q_ext = (
        jnp.where(lane < DIFF_QK_DIM, q, jnp.where(lane < DIFF_QK_DIM + n_bias, 1.0, 0.0)).astype(BF16),
        jnp.where(lane >= DIFF_QK_DIM, q, jnp.where(lane < n_bias, 1.0, 0.0)).astype(BF16),
    )
    m_ref[...] = jnp.full_like(m_ref, MASK_VALUE)
    acc_ref[...] = jnp.zeros_like(acc_ref)

    def scores(kb, slot, lo=0):
        rows = pl.ds(pl.multiple_of(kb * BK, BK), BK)
        for c in range(2):
            s_ref[slot, c, :, lo:] = _dot_nt(kext_ref[c, rows, :], q_ext[c][lo:, :])

    def accumulate(kb, slot, masked, lo=0):
        rows = pl.ds(pl.multiple_of(kb * BK, BK), BK)
        vt = vtx_ref[:, rows]
        for c in range(2):
            s = s_ref[slot, c, :, lo:]
            if masked:
                krel = lax.broadcasted_iota(jnp.int32, (BK, BQ - lo), 0)
                qrel = lax.broadcasted_iota(jnp.int32, (BK, BQ - lo), 1)
                s = jnp.where(krel <= qrel, s, MASK_VALUE)
            m_old = m_ref[c, :, lo:]
            m_new = jnp.maximum(m_old, jnp.max(s, axis=0, keepdims=True))
            alpha = jnp.exp2(m_old - m_new)
            p = jnp.exp2(s - m_new)
            acc_ref[c, :, lo:] = alpha * acc_ref[c, :, lo:] + _dot(vt, p.astype(BF16))
            m_ref[c, :, lo:] = m_new

    assert BQ == 2 * BK
    scores(0, 0)

    def body(j, carry):
        scores(2 * j + 1, 1)
        accumulate(2 * j, 0, False)
        scores(2 * j + 2, 0)
        accumulate(2 * j + 1, 1, False)
        return carry

    lax.fori_loop(0, qi, body, 0)
    scores(2 * qi + 1, 1, BK)
    accumulate(2 * qi, 0, True)
    accumulate(2 * qi + 1, 1, True, BK)

    lam = (jnp.exp(jnp.sum(lq1_ref[...] * lk1_ref[...], axis=-1, keepdims=True))
           - jnp.exp(jnp.sum(lq2_ref[...] * lk2_ref[...], axis=-1, keepdims=True)) + lam_init)
    num = (acc_ref[0, 0:HEAD_DIM, :], acc_ref[1, 0:HEAD_DIM, :])
    den = (acc_ref[0, HEAD_DIM:HEAD_DIM + 1, :], acc_ref[1, HEAD_DIM:HEAD_DIM + 1, :])
    o_t = num[0] / den[0] - lam * (num[1] / den[1])
    o = o_t.T
    o = o * lax.rsqrt(jnp.mean(o * o, axis=-1, keepdims=True) + NORM_EPS)
    o = o * nw_ref[...] * (1.0 - lam_init)
    z = z_ref[...].astype(F32)
    o_ref[...] = (o * _silu(z)).astype(BF16)


def _attn(act, v_t, lq1, lk1, lq2, lk2, diff_norm_w, lam_init, batch, seq_len):
    t = act.shape[0]
    nq = seq_len // BQ
    small = lambda shape: pl.BlockSpec(shape, lambda h, b, i: (0, 0))
    return pl.pallas_call(
        functools.partial(_attn_kernel, lam_init),
        grid=(DIFF_HEADS, batch, nq),
        in_specs=[
            pl.BlockSpec((BQ, HEAD_DIM), lambda h, b, i: (b * nq + i, COL_DQ + h)),
            pl.BlockSpec((seq_len, HEAD_DIM), lambda h, b, i: (b, COL_DK + h)),
            pl.BlockSpec((HEAD_DIM, seq_len), lambda h, b, i: (h, b)),
            pl.BlockSpec((BQ, HEAD_DIM), lambda h, b, i: (b * nq + i, COL_DZ + h)),
            small((1, DIFF_QK_DIM)), small((1, DIFF_QK_DIM)),
            small((1, DIFF_QK_DIM)), small((1, DIFF_QK_DIM)),
            small((1, HEAD_DIM)),
        ],
        out_specs=pl.BlockSpec((BQ, HEAD_DIM), lambda h, b, i: (b * nq + i, h)),
        out_shape=jax.ShapeDtypeStruct((t, DIFF_WIDTH), BF16),
        scratch_shapes=[
            pltpu.VMEM((seq_len, HEAD_DIM), F32),
            pltpu.VMEM((2, seq_len, HEAD_DIM), BF16),
            pltpu.VMEM((HEAD_DIM + V_PAD, seq_len), BF16),
            pltpu.VMEM((2, 2, BK, BQ), F32),
            pltpu.VMEM((2, HEAD_DIM + V_PAD, BQ), F32),
            pltpu.VMEM((2, 1, BQ), F32),
        ],
        compiler_params=pltpu.CompilerParams(
            dimension_semantics=("arbitrary", "arbitrary", "arbitrary"),
            vmem_limit_bytes=VMEM_LIMIT),
        name="diffattn",
    )(act, act, v_t, act, lq1, lk1, lq2, lk2, diff_norm_w)


def _outproj_kernel(oa_ref, ob_ref, x_ref, w_ref, fw_ref, out_ref):
    mix = _dot(oa_ref[...], w_ref[0:GDN_WIDTH, :]) + _dot(ob_ref[...], w_ref[GDN_WIDTH:, :])
    y = x_ref[...] + mix
    ms = jnp.mean(y * y, axis=-1, keepdims=True)
    out_ref[...] = y * lax.rsqrt(ms + NORM_EPS) * fw_ref[...]


def _outproj(o_a, o_b, xf, w_out, final_norm_w):
    t = xf.shape[0]
    return pl.pallas_call(
        _outproj_kernel,
        grid=(t // TM_OUT,),
        in_specs=[
            pl.BlockSpec((TM_OUT, GDN_WIDTH), lambda i: (i, 0)),
            pl.BlockSpec((TM_OUT, DIFF_WIDTH), lambda i: (i, 0)),
            pl.BlockSpec((TM_OUT, D_MODEL), lambda i: (i, 0)),
            pl.BlockSpec((D_MODEL, D_MODEL), lambda i: (0, 0)),
            pl.BlockSpec((1, D_MODEL), lambda i: (0, 0)),
        ],
        out_specs=pl.BlockSpec((TM_OUT, D_MODEL), lambda i: (i, 0)),
        out_shape=jax.ShapeDtypeStruct((t, D_MODEL), F32),
        compiler_params=pltpu.CompilerParams(
            dimension_semantics=("arbitrary",), vmem_limit_bytes=VMEM_LIMIT),
        name="outproj",
    )(o_a, o_b, xf, w_out, final_norm_w)


def kernel(x, norm_w, w_in, conv_w, a_log, dt_bias, gdn_norm_w, lambda_q1, lambda_k1,
           lambda_q2, lambda_k2, diff_norm_w, w_out, final_norm_w):
    batch, seq_len, d_model = x.shape
    depth = norm_w.shape[0]
    assert depth == 1 and d_model == D_MODEL
    assert seq_len % TM_IN == 0 and seq_len % SC == 0 and seq_len % BQ == 0 and BQ % BK == 0
    n_wide_a = 4 * GDN_WIDTH
    xf = x.reshape(batch * seq_len, d_model)

    w = w_in[0]
    c_dq = n_wide_a + N_GATE
    c_dv = c_dq + 2 * DIFF_WIDTH
    c_dz = c_dv + DIFF_WIDTH
    w_main = jnp.concatenate([w[:, :n_wide_a], w[:, c_dq:c_dv], w[:, c_dz:]], axis=1).astype(BF16)
    w_v_t = w[:, c_dv:c_dz].T.astype(BF16)
    w_gate_cols = w[:, n_wide_a:n_wide_a + N_GATE]
    w_gate = jnp.pad(w_gate_cols, ((0, 0), (0, LANES - N_GATE))).astype(BF16)
    w_gate_t = w_gate_cols.T.astype(BF16)

    act, gate, gate_t, v_t = _inproj(xf, norm_w[0][None, :], w_main, w_gate, w_gate_t, w_v_t,
                                     conv_w[0], seq_len)

    pad_l = lambda vec: jnp.pad(vec, (GDN_HEADS, LANES - N_GATE))[None, :]
    pad_s = lambda vec: jnp.pad(vec, (GDN_HEADS, 0))[:, None]
    o_a = _gdn(act, gate, gate_t, pad_l(a_log[0]), pad_l(dt_bias[0]), pad_s(a_log[0]),
               pad_s(dt_bias[0]), gdn_norm_w[0][None, :], batch, seq_len)

    lam_init = 0.8 - 0.6 * math.exp(-0.3 * 0)
    o_b = _attn(act, v_t, lambda_q1[0][None, :], lambda_k1[0][None, :], lambda_q2[0][None, :],
                lambda_k2[0][None, :], diff_norm_w[0][None, :], lam_init, batch, seq_len)

    out = _outproj(o_a, o_b, xf, w_out[0].astype(BF16), final_norm_w[None, :])
    return out.reshape(batch, seq_len, d_model)
```

```python
import functools
import math

import jax
import jax.numpy as jnp
from jax import lax
from jax.experimental import pallas as pl
from jax.experimental.pallas import tpu as pltpu

F32 = jnp.float32
BF16 = jnp.bfloat16

D_MODEL = 1024
GDN_HEADS = 4
HEAD_DIM = 128
GDN_WIDTH = GDN_HEADS * HEAD_DIM
DIFF_HEADS = 4
DIFF_QK_DIM = 64
DIFF_WIDTH = DIFF_HEADS * HEAD_DIM
CONV_K = 4
NORM_EPS = 1e-6
N_GATE = 2 * GDN_HEADS
N_MAIN = 4 * GDN_WIDTH + 3 * DIFF_WIDTH
N_CONV = 3 * GDN_WIDTH
LANES = 128
MASK_VALUE = -1e30

COL_GQ, COL_GK, COL_GV, COL_GZ = 0, 4, 8, 12
COL_DQ, COL_DK, COL_DZ = 16, 20, 24

TM_IN = 1024
NC_IN = 256
TAIL = 16
TM_OUT = 1024
SC = 256
CHUNK = 64
BQ = 512
BK = 256
V_PAD = 16
VMEM_LIMIT = 48 * 1024 * 1024


def _sigmoid(x):
    return 1.0 / (1.0 + jnp.exp(-x))


def _silu(x):
    half = 0.5 * x
    return half + half * jnp.tanh(half)


def _softplus(x):
    return jnp.maximum(x, 0.0) + jnp.log1p(jnp.exp(-jnp.abs(x)))


def _dot(a, b):
    return jnp.dot(a, b, preferred_element_type=F32)


def _dot_nt(a, b):
    return lax.dot_general(a, b, (((1,), (1,)), ((), ())), preferred_element_type=F32)


def _dot_tn(a, b):
    return lax.dot_general(a, b, (((0,), (0,)), ((), ())), preferred_element_type=F32)


def _inproj_kernel(tiles_per_seq, x_ref, nw_ref, w_ref, wg_ref, wgt_ref, wvt_ref, cw_ref,
                   main_ref, gate_ref, gate_t_ref, vt_ref, tail_ref):
    i = pl.program_id(0)
    x = x_ref[...]
    ms = jnp.mean(x * x, axis=-1, keepdims=True)
    h = (x * lax.rsqrt(ms + NORM_EPS) * nw_ref[...]).astype(BF16)

    gate_ref[...] = _dot(h, wg_ref[...])
    gate_t_ref[...] = _dot_nt(wgt_ref[...], h)
    vt_ref[...] = _dot_nt(wvt_ref[...], h).astype(BF16)

    @pl.when(i % tiles_per_seq == 0)
    def _():
        tail_ref[...] = jnp.zeros_like(tail_ref)

    tm = x.shape[0]
    row = lax.broadcasted_iota(jnp.int32, (TAIL, LANES), 0)

    def conv_silu(acc, n0):
        cols = slice(n0, n0 + LANES)
        last = acc[tm - TAIL:, :]
        delta = tail_ref[:, cols] - last
        tail_ref[:, cols] = last
        y = acc * cw_ref[CONV_K - 1:CONV_K, cols]
        fix = jnp.zeros((TAIL, LANES), F32)
        for s in range(1, CONV_K):
            wk = cw_ref[CONV_K - 1 - s:CONV_K - s, cols]
            y = y + pltpu.roll(acc, s, axis=0) * wk
            fix = fix + jnp.where(row < s, pltpu.roll(delta, s, axis=0), 0.0) * wk
        main_ref[:, cols] = _silu(y).astype(BF16)
        main_ref[0:TAIL, cols] = _silu(y[:TAIL, :] + fix).astype(BF16)

    plain = [slice(n0, n0 + NC_IN) for n0 in range(N_CONV, N_MAIN, NC_IN)]
    pending = []
    for n0 in range(0, N_CONV, NC_IN):
        acc = _dot(h, w_ref[:, n0:n0 + NC_IN])
        if pending:
            conv_silu(*pending.pop(0))
        if plain:
            pc = plain.pop(0)
            main_ref[:, pc] = _dot(h, w_ref[:, pc]).astype(BF16)
        if pending:
            conv_silu(*pending.pop(0))
        pending = [(acc[:, j:j + LANES], n0 + j) for j in range(0, NC_IN, LANES)]
    for piece in pending:
        if plain:
            pc = plain.pop(0)
            main_ref[:, pc] = _dot(h, w_ref[:, pc]).astype(BF16)
        conv_silu(*piece)
    for pc in plain:
        main_ref[:, pc] = _dot(h, w_ref[:, pc]).astype(BF16)


def _inproj(xf, norm_w, w_main, w_gate, w_gate_t, w_v_t, conv_w, seq_len):
    t = xf.shape[0]
    tiles_per_seq = seq_len // TM_IN
    return pl.pallas_call(
        functools.partial(_inproj_kernel, tiles_per_seq),
        grid=(t // TM_IN,),
        in_specs=[
            pl.BlockSpec((TM_IN, D_MODEL), lambda i: (i, 0)),
            pl.BlockSpec((1, D_MODEL), lambda i: (0, 0)),
            pl.BlockSpec((D_MODEL, N_MAIN), lambda i: (0, 0)),
            pl.BlockSpec((D_MODEL, LANES), lambda i: (0, 0)),
            pl.BlockSpec((N_GATE, D_MODEL), lambda i: (0, 0)),
            pl.BlockSpec((DIFF_WIDTH, D_MODEL), lambda i: (0, 0)),
            pl.BlockSpec((CONV_K, N_CONV), lambda i: (0, 0)),
        ],
        out_specs=[
            pl.BlockSpec((TM_IN, N_MAIN), lambda i: (i, 0)),
            pl.BlockSpec((TM_IN, LANES), lambda i: (i, 0)),
            pl.BlockSpec((N_GATE, TM_IN), lambda i: (0, i)),
            pl.BlockSpec((DIFF_WIDTH, TM_IN), lambda i: (0, i)),
        ],
        out_shape=[
            jax.ShapeDtypeStruct((t, N_MAIN), BF16),
            jax.ShapeDtypeStruct((t, LANES), F32),
            jax.ShapeDtypeStruct((N_GATE, t), F32),
            jax.ShapeDtypeStruct((DIFF_WIDTH, t), BF16),
        ],
        scratch_shapes=[pltpu.VMEM((TAIL, N_CONV), F32)],
        compiler_params=pltpu.CompilerParams(
            dimension_semantics=("arbitrary",), vmem_limit_bytes=VMEM_LIMIT),
        name="inproj",
    )(xf, norm_w, w_main, w_gate, w_gate_t, w_v_t, conv_w)


def _gdn_kernel(act_ref, gate_ref, gate_t_ref, alog_l_ref, dtb_l_ref, alog_s_ref, dtb_s_ref,
                gnw_ref, o_ref, state_ref, mask_ref, bd_ref):
    seq_len = act_ref.shape[0]
    n_chunks = SC // CHUNK
    heads = range(GDN_HEADS)
    i_cat = lax.broadcasted_iota(jnp.int32, (CHUNK, SC), 0)
    j_cat = lax.broadcasted_iota(jnp.int32, (CHUNK, SC), 1) & (CHUNK - 1)
    mask_ref[0] = jnp.where(i_cat >= j_cat, 0.0, MASK_VALUE)
    mask_ref[1] = jnp.where(i_cat > j_cat, -1.0, 0.0)
    mask_ref[2] = jnp.where(i_cat == j_cat, 1.0, 0.0)
    r = lax.broadcasted_iota(jnp.int32, (SC, SC), 0)
    c = lax.broadcasted_iota(jnp.int32, (SC, SC), 1)
    bd_ref[...] = jnp.where((r & -CHUNK) == (c & -CHUNK), 1.0, 0.0).astype(BF16)
    state_ref[...] = jnp.zeros_like(state_ref)
    lane_chunk = lax.broadcasted_iota(jnp.int32, (1, SC), 1) & -CHUNK

    sub_in_chunk = lax.broadcasted_iota(jnp.int32, (SC, LANES), 0) & (CHUNK - 1)
    lane_in_chunk = lax.broadcasted_iota(jnp.int32, (N_GATE, SC), 1) & (CHUNK - 1)
    neg_a_l = -jnp.exp(alog_l_ref[...])
    neg_a_s = -jnp.exp(alog_s_ref[...])
    scale = HEAD_DIM ** -0.5

    def cat(x):
        out = x[(n_chunks - 1) * CHUNK:, :]
        for ci in range(n_chunks - 2, -1, -1):
            out = jnp.where(lane_chunk == ci * CHUNK, x[ci * CHUNK:(ci + 1) * CHUNK, :], out)
        return out

    def block_diag(x_cat):
        return jnp.concatenate([x_cat.astype(BF16)] * n_chunks, axis=0) * bd_ref[...]

    def head_cols(col, h):
        return slice((col + h) * HEAD_DIM, (col + h + 1) * HEAD_DIM)

    def sc_body(sc, carry):
        r0 = pl.multiple_of(sc * SC, SC)
        rows = pl.ds(r0, SC)
        gate = gate_ref[rows, :]
        beta_l = _sigmoid(gate)
        logbeta_l = -_softplus(-gate)
        gc_l = neg_a_l * _softplus(gate + dtb_l_ref[...])
        gc_s = neg_a_s * _softplus(gate_t_ref[:, rows] + dtb_s_ref[...])
        step = 1
        while step < CHUNK:
            gc_l = gc_l + jnp.where(sub_in_chunk >= step, pltpu.roll(gc_l, step, axis=0), 0.0)
            gc_s = gc_s + jnp.where(lane_in_chunk >= step, pltpu.roll(gc_s, step, axis=1), 0.0)
            step *= 2
        causal, neg_strict, eye = mask_ref[0], mask_ref[1], mask_ref[2]

        n_pow, a_cat, rhs, q_dec, k_dec, g_last = [], [], [], [], [], []
        for h in heads:
            q = act_ref[rows, head_cols(COL_GQ, h)].astype(F32)
            k = act_ref[rows, head_cols(COL_GK, h)].astype(F32)
            v = act_ref[rows, head_cols(COL_GV, h)].astype(F32)
            q = q * (lax.rsqrt(jnp.sum(q * q, axis=-1, keepdims=True) + 1e-6) * scale)
            k = k * lax.rsqrt(jnp.sum(k * k, axis=-1, keepdims=True) + 1e-6)
            gcol = gc_l[:, GDN_HEADS + h:GDN_HEADS + h + 1]
            bcol = beta_l[:, h:h + 1]
            grow = gc_s[GDN_HEADS + h:GDN_HEADS + h + 1, :]
            kb = k.astype(BF16)
            gram = _dot_nt(jnp.concatenate([kb, q.astype(BF16)], axis=0), kb)
            e_cat = cat(gcol) - grow + causal
            n_pow.append(cat(gram[:SC]) * jnp.exp(e_cat + cat(logbeta_l[:, h:h + 1])) * neg_strict)
            a_cat.append(cat(gram[SC:]) * jnp.exp(e_cat))
            e_g = jnp.exp(gcol)
            rhs.append(jnp.concatenate([bcol * v, (bcol * e_g) * k], axis=1).astype(BF16))
            q_dec.append(q * e_g)
            g_last.append([gcol[(ci + 1) * CHUNK - 1:(ci + 1) * CHUNK, :] for ci in range(n_chunks)])
            k_dec.append([(k[ci * CHUNK:(ci + 1) * CHUNK, :]
                           * jnp.exp(g_last[h][ci] - gcol[ci * CHUNK:(ci + 1) * CHUNK, :])).astype(BF16)
                          for ci in range(n_chunks)])

        t_cat = [eye + n_pow[h] for h in heads]
        n_pow = [_dot(n_pow[h].astype(BF16), block_diag(n_pow[h])) for h in heads]
        span = 2
        while span < CHUNK // 2:
            both = [_dot(jnp.concatenate([t_cat[h], n_pow[h]], axis=0).astype(BF16), block_diag(n_pow[h]))
                    for h in heads]
            t_cat = [t_cat[h] + both[h][:CHUNK] for h in heads]
            n_pow = [both[h][CHUNK:] for h in heads]
            span *= 2
        t_cat = [t_cat[h] + _dot(t_cat[h].astype(BF16), block_diag(n_pow[h])) for h in heads]

        uw = [_dot(block_diag(t_cat[h]), rhs[h]) for h in heads]
        kt = [[_dot_tn(k_dec[h][ci], uw[h][ci * CHUNK:(ci + 1) * CHUNK, :].astype(BF16))
               for ci in range(n_chunks)] for h in heads]
        state = [state_ref[h] for h in heads]
        state_in = [[None] * n_chunks for _ in heads]
        for ci in range(n_chunks):
            for h in heads:
                sb = state[h].astype(BF16)
                state_in[h][ci] = sb
                state[h] = (state[h] * jnp.exp(g_last[h][ci])
                            - _dot(kt[h][ci][:, HEAD_DIM:].astype(BF16), sb) + kt[h][ci][:, :HEAD_DIM])
        for h in heads:
            state_ref[h] = state[h]

        for h in heads:
            inter = []
            for ci in range(n_chunks):
                cr = slice(ci * CHUNK, (ci + 1) * CHUNK)
                lhs = jnp.concatenate([uw[h][cr, HEAD_DIM:], q_dec[h][cr, :]], axis=0).astype(BF16)
                inter.append(_dot(lhs, state_in[h][ci]))
            v_new = jnp.concatenate([uw[h][ci * CHUNK:(ci + 1) * CHUNK, :HEAD_DIM] - inter[ci][:CHUNK]
                                     for ci in range(n_chunks)], axis=0).astype(BF16)
            o = (jnp.concatenate([inter[ci][CHUNK:] for ci in range(n_chunks)], axis=0)
                 + _dot(block_diag(a_cat[h]), v_new))
            o = o * lax.rsqrt(jnp.mean(o * o, axis=-1, keepdims=True) + NORM_EPS)
            z = act_ref[rows, head_cols(COL_GZ, h)].astype(F32)
            o_ref[rows, head_cols(0, h)] = (o * (_silu(z) * gnw_ref[...])).astype(BF16)
        return carry

    lax.fori_loop(0, seq_len // SC, sc_body, 0)


def _gdn(act, gate, gate_t, alog_l, dtb_l, alog_s, dtb_s, gdn_norm_w, batch, seq_len):
    t = act.shape[0]
    small = lambda shape: pl.BlockSpec(shape, lambda b: (0, 0))
    return pl.pallas_call(
        _gdn_kernel,
        grid=(batch,),
        in_specs=[
            pl.BlockSpec((seq_len, 4 * GDN_WIDTH), lambda b: (b, 0)),
            pl.BlockSpec((seq_len, LANES), lambda b: (b, 0)),
            pl.BlockSpec((N_GATE, seq_len), lambda b: (0, b)),
            small((1, LANES)), small((1, LANES)), small((N_GATE, 1)), small((N_GATE, 1)),
            small((1, HEAD_DIM)),
        ],
        out_specs=pl.BlockSpec((seq_len, GDN_WIDTH), lambda b: (b, 0)),
        out_shape=jax.ShapeDtypeStruct((t, GDN_WIDTH), BF16),
        scratch_shapes=[
            pltpu.VMEM((GDN_HEADS, HEAD_DIM, HEAD_DIM), F32),
            pltpu.VMEM((3, CHUNK, SC), F32),
            pltpu.VMEM((SC, SC), BF16),
        ],
        compiler_params=pltpu.CompilerParams(
            dimension_semantics=("arbitrary",), vmem_limit_bytes=VMEM_LIMIT),
        name="gdn",
    )(act, gate, gate_t, alog_l, dtb_l, alog_s, dtb_s, gdn_norm_w)


def _split3(x):
    hi = x.astype(BF16).astype(F32)
    r1 = x - hi
    mid = r1.astype(BF16).astype(F32)
    return hi, mid, r1 - mid


def _attn_kernel(lam_init, q_ref, k_ref, vt_ref, z_ref, lq1_ref, lk1_ref, lq2_ref, lk2_ref,
                 nw_ref, o_ref, bias_ref, kext_ref, vtx_ref, s_ref, acc_ref, m_ref):
    h = pl.program_id(0)
    b = pl.program_id(1)
    seq_len = k_ref.shape[0]
    lane = lax.broadcasted_iota(jnp.int32, (1, HEAD_DIM), 1)
    n_bias = 6
    log2e = math.log2(math.e)
    assert BQ == 2 * BK

    @pl.when(b == 0)
    def _():
        slope = lax.shift_left(jnp.int32(1), 2 * (DIFF_HEADS - 1 - h)).astype(F32) * (log2e / 256.0)
        pos = lax.broadcasted_iota(jnp.int32, (seq_len, 1), 0)
        in_block = (pos & (BK - 1)).astype(F32) * slope
        block_off = (pos & -BK).astype(F32) * slope
        terms = _split3(in_block) + _split3(block_off)
        tile = jnp.zeros((seq_len, HEAD_DIM), F32)
        for j, term in enumerate(terms):
            tile = jnp.where((lane == j) | (lane == DIFF_QK_DIM + j), term, tile)
        bias_ref[...] = tile
        ones_row = lax.broadcasted_iota(jnp.int32, (V_PAD, seq_len), 0) == 0
        vtx_ref[HEAD_DIM:, :] = jnp.where(ones_row, 1.0, 0.0).astype(BF16)

    k = k_ref[...].astype(F32)
    kext_ref[0] = jnp.where(lane < DIFF_QK_DIM, k, bias_ref[...]).astype(BF16)
    kext_ref[1] = jnp.where(lane >= DIFF_QK_DIM, k, bias_ref[...]).astype(BF16)
    vtx_ref[0:HEAD_DIM, :] = vt_ref[...]

    lam = (jnp.exp(jnp.sum(lq1_ref[...] * lk1_ref[...], axis=-1, keepdims=True))
           - jnp.exp(jnp.sum(lq2_ref[...] * lk2_ref[...], axis=-1, keepdims=True)) + lam_init)

    def scores(q_ext, kb, slot, lo):
        rows = slice(kb * BK, (kb + 1) * BK)
        for c in range(2):
            s_ref[slot, c, :, lo:] = _dot_nt(kext_ref[c, rows, :], q_ext[c][lo:, :])

    def accumulate(par, kb, slot, masked, lo):
        vt = vtx_ref[:, kb * BK:(kb + 1) * BK]
        for c in range(2):
            s = s_ref[slot, c, :, lo:]
            if masked:
                krel = lax.broadcasted_iota(jnp.int32, (BK, BQ - lo), 0)
                qrel = lax.broadcasted_iota(jnp.int32, (BK, BQ - lo), 1)
                s = jnp.where(krel <= qrel, s, MASK_VALUE)
            m_old = m_ref[par, c, :, lo:]
            m_new = jnp.maximum(m_old, jnp.max(s, axis=0, keepdims=True))
            alpha = jnp.exp2(m_old - m_new)
            p = jnp.exp2(s - m_new)
            acc_ref[par, c, :, lo:] = alpha * acc_ref[par, c, :, lo:] + _dot(vt, p.astype(BF16))
            m_ref[par, c, :, lo:] = m_new

    slot = 0
    for qi in range(seq_len // BQ):
        par = qi % 2
        qrows = slice(qi * BQ, (qi + 1) * BQ)
        q = q_ref[qrows, :].astype(F32) * (DIFF_QK_DIM ** -0.5 * log2e)
        q_ext = (
            jnp.where(lane < DIFF_QK_DIM, q,
                      jnp.where(lane < DIFF_QK_DIM + n_bias, 1.0, 0.0)).astype(BF16),
            jnp.where(lane >= DIFF_QK_DIM, q, jnp.where(lane < n_bias, 1.0, 0.0)).astype(BF16),
        )
        m_ref[par] = jnp.full(m_ref.shape[1:], MASK_VALUE, F32)
        acc_ref[par] = jnp.zeros(acc_ref.shape[1:], F32)
        n_blocks = 2 * qi + 2
        lo_of = lambda kb: BK if kb == n_blocks - 1 else 0
        scores(q_ext, 0, slot, lo_of(0))
        for kb in range(n_blocks):
            if kb + 1 < n_blocks:
                scores(q_ext, kb + 1, 1 - slot, lo_of(kb + 1))
            accumulate(par, kb, slot, kb >= 2 * qi, lo_of(kb))
            slot = 1 - slot

        num = (acc_ref[par, 0, 0:HEAD_DIM, :], acc_ref[par, 1, 0:HEAD_DIM, :])
        den = (acc_ref[par, 0, HEAD_DIM:HEAD_DIM + 1, :], acc_ref[par, 1, HEAD_DIM:HEAD_DIM + 1, :])
        o_t = num[0] / den[0] - lam * (num[1] / den[1])
        o = o_t.T
        o = o * lax.rsqrt(jnp.mean(o * o, axis=-1, keepdims=True) + NORM_EPS)
        o = o * nw_ref[...] * (1.0 - lam_init)
        o_ref[qrows, :] = (o * _silu(z_ref[qrows, :].astype(F32))).astype(BF16)


def _attn(act, v_t, lq1, lk1, lq2, lk2, diff_norm_w, lam_init, batch, seq_len):
    t = act.shape[0]
    small = lambda shape: pl.BlockSpec(shape, lambda h, b: (0, 0))
    return pl.pallas_call(
        functools.partial(_attn_kernel, lam_init),
        grid=(DIFF_HEADS, batch),
        in_specs=[
            pl.BlockSpec((seq_len, HEAD_DIM), lambda h, b: (b, COL_DQ + h)),
            pl.BlockSpec((seq_len, HEAD_DIM), lambda h, b: (b, COL_DK + h)),
            pl.BlockSpec((HEAD_DIM, seq_len), lambda h, b: (h, b)),
            pl.BlockSpec((seq_len, HEAD_DIM), lambda h, b: (b, COL_DZ + h)),
            small((1, DIFF_QK_DIM)), small((1, DIFF_QK_DIM)),
            small((1, DIFF_QK_DIM)), small((1, DIFF_QK_DIM)),
            small((1, HEAD_DIM)),
        ],
        out_specs=pl.BlockSpec((seq_len, HEAD_DIM), lambda h, b: (b, h)),
        out_shape=jax.ShapeDtypeStruct((t, DIFF_WIDTH), BF16),
        scratch_shapes=[
            pltpu.VMEM((seq_len, HEAD_DIM), F32),
            pltpu.VMEM((2, seq_len, HEAD_DIM), BF16),
            pltpu.VMEM((HEAD_DIM + V_PAD, seq_len), BF16),
            pltpu.VMEM((2, 2, BK, BQ), F32),
            pltpu.VMEM((2, 2, HEAD_DIM + V_PAD, BQ), F32),
            pltpu.VMEM((2, 2, 1, BQ), F32),
        ],
        compiler_params=pltpu.CompilerParams(
            dimension_semantics=("arbitrary", "arbitrary"), vmem_limit_bytes=VMEM_LIMIT),
        name="diffattn",
    )(act, act, v_t, act, lq1, lk1, lq2, lk2, diff_norm_w)


def _outproj_kernel(oa_ref, ob_ref, x_ref, w_ref, fw_ref, out_ref):
    mix = _dot(oa_ref[...], w_ref[0:GDN_WIDTH, :]) + _dot(ob_ref[...], w_ref[GDN_WIDTH:, :])
    y = x_ref[...] + mix
    ms = jnp.mean(y * y, axis=-1, keepdims=True)
    out_ref[...] = y * lax.rsqrt(ms + NORM_EPS) * fw_ref[...]


def _outproj(o_a, o_b, xf, w_out, final_norm_w):
    t = xf.shape[0]
    return pl.pallas_call(
        _outproj_kernel,
        grid=(t // TM_OUT,),
        in_specs=[
            pl.BlockSpec((TM_OUT, GDN_WIDTH), lambda i: (i, 0)),
            pl.BlockSpec((TM_OUT, DIFF_WIDTH), lambda i: (i, 0)),
            pl.BlockSpec((TM_OUT, D_MODEL), lambda i: (i, 0)),
            pl.BlockSpec((D_MODEL, D_MODEL), lambda i: (0, 0)),
            pl.BlockSpec((1, D_MODEL), lambda i: (0, 0)),
        ],
        out_specs=pl.BlockSpec((TM_OUT, D_MODEL), lambda i: (i, 0)),
        out_shape=jax.ShapeDtypeStruct((t, D_MODEL), F32),
        compiler_params=pltpu.CompilerParams(
            dimension_semantics=("arbitrary",), vmem_limit_bytes=VMEM_LIMIT),
        name="outproj",
    )(o_a, o_b, xf, w_out, final_norm_w)


def kernel(x, norm_w, w_in, conv_w, a_log, dt_bias, gdn_norm_w, lambda_q1, lambda_k1,
           lambda_q2, lambda_k2, diff_norm_w, w_out, final_norm_w):
    batch, seq_len, d_model = x.shape
    depth = norm_w.shape[0]
    assert depth == 1 and d_model == D_MODEL
    assert seq_len % TM_IN == 0 and seq_len % SC == 0 and seq_len % BQ == 0 and BQ % BK == 0
    n_wide_a = 4 * GDN_WIDTH
    xf = x.reshape(batch * seq_len, d_model)

    w = w_in[0]
    c_dq = n_wide_a + N_GATE
    c_dv = c_dq + 2 * DIFF_WIDTH
    c_dz = c_dv + DIFF_WIDTH
    w_main = jnp.concatenate([w[:, :n_wide_a], w[:, c_dq:c_dv], w[:, c_dz:]], axis=1).astype(BF16)
    w_v_t = w[:, c_dv:c_dz].T.astype(BF16)
    w_gate_cols = w[:, n_wide_a:n_wide_a + N_GATE]
    w_gate = jnp.pad(w_gate_cols, ((0, 0), (0, LANES - N_GATE))).astype(BF16)
    w_gate_t = w_gate_cols.T.astype(BF16)

    act, gate, gate_t, v_t = _inproj(xf, norm_w[0][None, :], w_main, w_gate, w_gate_t, w_v_t,
                                     conv_w[0], seq_len)

    pad_l = lambda vec: jnp.pad(vec, (GDN_HEADS, LANES - N_GATE))[None, :]
    pad_s = lambda vec: jnp.pad(vec, (GDN_HEADS, 0))[:, None]
    o_a = _gdn(act, gate, gate_t, pad_l(a_log[0]), pad_l(dt_bias[0]), pad_s(a_log[0]),
               pad_s(dt_bias[0]), gdn_norm_w[0][None, :], batch, seq_len)

    lam_init = 0.8 - 0.6 * math.exp(-0.3 * 0)
    o_b = _attn(act, v_t, lambda_q1[0][None, :], lambda_k1[0][None, :], lambda_q2[0][None, :],
                lambda_k2[0][None, :], diff_norm_w[0][None, :], lam_init, batch, seq_len)

    out = _outproj(o_a, o_b, xf, w_out[0].astype(BF16), final_norm_w[None, :])
    return out.reshape(batch, seq_len, d_model)
```

```python
import functools
import math

import jax
import jax.numpy as jnp
from jax import lax
from jax.experimental import pallas as pl
from jax.experimental.pallas import tpu as pltpu

F32 = jnp.float32
BF16 = jnp.bfloat16

D_MODEL = 1024
GDN_HEADS = 4
HEAD_DIM = 128
GDN_WIDTH = GDN_HEADS * HEAD_DIM
DIFF_HEADS = 4
DIFF_QK_DIM = 64
DIFF_WIDTH = DIFF_HEADS * HEAD_DIM
CONV_K = 4
NORM_EPS = 1e-6
N_GATE = 2 * GDN_HEADS
N_MAIN = 4 * GDN_WIDTH + 3 * DIFF_WIDTH
N_CONV = 3 * GDN_WIDTH
LANES = 128
MASK_VALUE = -1e30

COL_GQ, COL_GK, COL_GV, COL_GZ = 0, 4, 8, 12
COL_DQ, COL_DK, COL_DZ = 16, 20, 24

TM_IN = 1024
NC_IN = 256
TAIL = 16
TM_OUT = 1024
SC = 256
CHUNK = 64
BQ = 512
BK = 256
V_PAD = 16
VMEM_LIMIT = 48 * 1024 * 1024


def _sigmoid(x):
    return 1.0 / (1.0 + jnp.exp(-x))


def _silu(x):
    half = 0.5 * x
    return half + half * jnp.tanh(half)


def _softplus(x):
    return jnp.maximum(x, 0.0) + jnp.log1p(jnp.exp(-jnp.abs(x)))


def _dot(a, b):
    return jnp.dot(a, b, preferred_element_type=F32)


def _dot_nt(a, b):
    return lax.dot_general(a, b, (((1,), (1,)), ((), ())), preferred_element_type=F32)


def _dot_tn(a, b):
    return lax.dot_general(a, b, (((0,), (0,)), ((), ())), preferred_element_type=F32)


def _inproj_kernel(tiles_per_seq, x_ref, nw_ref, w_ref, wg_ref, wgt_ref, wvt_ref, cw_ref,
                   main_ref, gate_ref, gate_t_ref, vt_ref, tail_ref):
    i = pl.program_id(0)
    x = x_ref[...]
    ms = jnp.mean(x * x, axis=-1, keepdims=True)
    h = (x * lax.rsqrt(ms + NORM_EPS) * nw_ref[...]).astype(BF16)

    gate_ref[...] = _dot(h, wg_ref[...])
    gate_t_ref[...] = _dot_nt(wgt_ref[...], h)
    vt_ref[...] = _dot_nt(wvt_ref[...], h).astype(BF16)

    @pl.when(i % tiles_per_seq == 0)
    def _():
        tail_ref[...] = jnp.zeros_like(tail_ref)

    tm = x.shape[0]
    row = lax.broadcasted_iota(jnp.int32, (TAIL, LANES), 0)

    def conv_silu(acc, n0):
        cols = slice(n0, n0 + LANES)
        last = acc[tm - TAIL:, :]
        delta = tail_ref[:, cols] - last
        tail_ref[:, cols] = last
        y = acc * cw_ref[CONV_K - 1:CONV_K, cols]
        fix = jnp.zeros((TAIL, LANES), F32)
        for s in range(1, CONV_K):
            wk = cw_ref[CONV_K - 1 - s:CONV_K - s, cols]
            y = y + pltpu.roll(acc, s, axis=0) * wk
            fix = fix + jnp.where(row < s, pltpu.roll(delta, s, axis=0), 0.0) * wk
        main_ref[:, cols] = _silu(y).astype(BF16)
        main_ref[0:TAIL, cols] = _silu(y[:TAIL, :] + fix).astype(BF16)

    plain = [slice(n0, n0 + NC_IN) for n0 in range(N_CONV, N_MAIN, NC_IN)]
    pending = []
    for n0 in range(0, N_CONV, NC_IN):
        acc = _dot(h, w_ref[:, n0:n0 + NC_IN])
        if pending:
            conv_silu(*pending.pop(0))
        if plain:
            pc = plain.pop(0)
            main_ref[:, pc] = _dot(h, w_ref[:, pc]).astype(BF16)
        if pending:
            conv_silu(*pending.pop(0))
        pending = [(acc[:, j:j + LANES], n0 + j) for j in range(0, NC_IN, LANES)]
    for piece in pending:
        if plain:
            pc = plain.pop(0)
            main_ref[:, pc] = _dot(h, w_ref[:, pc]).astype(BF16)
        conv_silu(*piece)
    for pc in plain:
        main_ref[:, pc] = _dot(h, w_ref[:, pc]).astype(BF16)


def _inproj(xf, norm_w, w_main, w_gate, w_gate_t, w_v_t, conv_w, seq_len):
    t = xf.shape[0]
    tiles_per_seq = seq_len // TM_IN
    return pl.pallas_call(
        functools.partial(_inproj_kernel, tiles_per_seq),
        grid=(t // TM_IN,),
        in_specs=[
            pl.BlockSpec((TM_IN, D_MODEL), lambda i: (i, 0)),
            pl.BlockSpec((1, D_MODEL), lambda i: (0, 0)),
            pl.BlockSpec((D_MODEL, N_MAIN), lambda i: (0, 0)),
            pl.BlockSpec((D_MODEL, LANES), lambda i: (0, 0)),
            pl.BlockSpec((N_GATE, D_MODEL), lambda i: (0, 0)),
            pl.BlockSpec((DIFF_WIDTH, D_MODEL), lambda i: (0, 0)),
            pl.BlockSpec((CONV_K, N_CONV), lambda i: (0, 0)),
        ],
        out_specs=[
            pl.BlockSpec((TM_IN, N_MAIN), lambda i: (i, 0)),
            pl.BlockSpec((TM_IN, LANES), lambda i: (i, 0)),
            pl.BlockSpec((N_GATE, TM_IN), lambda i: (0, i)),
            pl.BlockSpec((DIFF_WIDTH, TM_IN), lambda i: (0, i)),
        ],
        out_shape=[
            jax.ShapeDtypeStruct((t, N_MAIN), BF16),
            jax.ShapeDtypeStruct((t, LANES), F32),
            jax.ShapeDtypeStruct((N_GATE, t), F32),
            jax.ShapeDtypeStruct((DIFF_WIDTH, t), BF16),
        ],
        scratch_shapes=[pltpu.VMEM((TAIL, N_CONV), F32)],
        compiler_params=pltpu.CompilerParams(
            dimension_semantics=("arbitrary",), vmem_limit_bytes=VMEM_LIMIT),
        name="inproj",
    )(xf, norm_w, w_main, w_gate, w_gate_t, w_v_t, conv_w)


def _gdn_kernel(act_ref, gate_ref, gate_t_ref, alog_l_ref, dtb_l_ref, alog_s_ref, dtb_s_ref,
                gnw_ref, o_ref, mask_ref, bd_ref):
    seq_len = act_ref.shape[0]
    n_chunks = SC // CHUNK
    heads = range(GDN_HEADS)
    i_cat = lax.broadcasted_iota(jnp.int32, (CHUNK, SC), 0)
    j_cat = lax.broadcasted_iota(jnp.int32, (CHUNK, SC), 1) & (CHUNK - 1)
    mask_ref[0] = jnp.where(i_cat >= j_cat, 0.0, MASK_VALUE)
    mask_ref[1] = jnp.where(i_cat > j_cat, -1.0, 0.0)
    mask_ref[2] = jnp.where(i_cat == j_cat, 1.0, 0.0)
    r = lax.broadcasted_iota(jnp.int32, (SC, SC), 0)
    c = lax.broadcasted_iota(jnp.int32, (SC, SC), 1)
    bd_ref[...] = jnp.where((r & -CHUNK) == (c & -CHUNK), 1.0, 0.0).astype(BF16)
    lane_chunk = lax.broadcasted_iota(jnp.int32, (1, SC), 1) & -CHUNK

    sub_in_chunk = lax.broadcasted_iota(jnp.int32, (SC, LANES), 0) & (CHUNK - 1)
    lane_in_chunk = lax.broadcasted_iota(jnp.int32, (N_GATE, SC), 1) & (CHUNK - 1)
    neg_a_l = -jnp.exp(alog_l_ref[...])
    neg_a_s = -jnp.exp(alog_s_ref[...])
    scale = HEAD_DIM ** -0.5

    def cat(x):
        out = x[(n_chunks - 1) * CHUNK:, :]
        for ci in range(n_chunks - 2, -1, -1):
            out = jnp.where(lane_chunk == ci * CHUNK, x[ci * CHUNK:(ci + 1) * CHUNK, :], out)
        return out

    def block_diag(x_cat):
        return jnp.concatenate([x_cat.astype(BF16)] * n_chunks, axis=0) * bd_ref[...]

    def head_cols(col, h):
        return slice((col + h) * HEAD_DIM, (col + h + 1) * HEAD_DIM)

    def prepare(sc):
        rows = slice(sc * SC, (sc + 1) * SC)
        gate = gate_ref[rows, :]
        beta_l = _sigmoid(gate)
        logbeta_l = -_softplus(-gate)
        gc_l = neg_a_l * _softplus(gate + dtb_l_ref[...])
        gc_s = neg_a_s * _softplus(gate_t_ref[:, rows] + dtb_s_ref[...])
        step = 1
        while step < CHUNK:
            gc_l = gc_l + jnp.where(sub_in_chunk >= step, pltpu.roll(gc_l, step, axis=0), 0.0)
            gc_s = gc_s + jnp.where(lane_in_chunk >= step, pltpu.roll(gc_s, step, axis=1), 0.0)
            step *= 2
        causal, neg_strict, eye = mask_ref[0], mask_ref[1], mask_ref[2]

        n_pow, a_cat, rhs, q_dec, k_dec, g_last = [], [], [], [], [], []
        for h in heads:
            q = act_ref[rows, head_cols(COL_GQ, h)].astype(F32)
            k = act_ref[rows, head_cols(COL_GK, h)].astype(F32)
            v = act_ref[rows, head_cols(COL_GV, h)].astype(F32)
            q = q * (lax.rsqrt(jnp.sum(q * q, axis=-1, keepdims=True) + 1e-6) * scale)
            k = k * lax.rsqrt(jnp.sum(k * k, axis=-1, keepdims=True) + 1e-6)
            gcol = gc_l[:, GDN_HEADS + h:GDN_HEADS + h + 1]
            bcol = beta_l[:, h:h + 1]
            grow = gc_s[GDN_HEADS + h:GDN_HEADS + h + 1, :]
            kb = k.astype(BF16)
            gram = _dot_nt(jnp.concatenate([kb, q.astype(BF16)], axis=0), kb)
            e_cat = cat(gcol) - grow + causal
            n_pow.append(cat(gram[:SC]) * jnp.exp(e_cat + cat(logbeta_l[:, h:h + 1])) * neg_strict)
            a_cat.append(cat(gram[SC:]) * jnp.exp(e_cat))
            e_g = jnp.exp(gcol)
            rhs.append(jnp.concatenate([bcol * v, (bcol * e_g) * k], axis=1).astype(BF16))
            q_dec.append(q * e_g)
            g_last.append([gcol[(ci + 1) * CHUNK - 1:(ci + 1) * CHUNK, :] for ci in range(n_chunks)])
            k_dec.append([(k[ci * CHUNK:(ci + 1) * CHUNK, :]
                           * jnp.exp(g_last[h][ci] - gcol[ci * CHUNK:(ci + 1) * CHUNK, :])).astype(BF16)
                          for ci in range(n_chunks)])

        t_cat = [eye + n_pow[h] for h in heads]
        n_pow = [_dot(n_pow[h].astype(BF16), block_diag(n_pow[h])) for h in heads]
        span = 2
        while span < CHUNK // 2:
            both = [_dot(jnp.concatenate([t_cat[h], n_pow[h]], axis=0).astype(BF16), block_diag(n_pow[h]))
                    for h in heads]
            t_cat = [t_cat[h] + both[h][:CHUNK] for h in heads]
            n_pow = [both[h][CHUNK:] for h in heads]
            span *= 2
        t_cat = [t_cat[h] + _dot(t_cat[h].astype(BF16), block_diag(n_pow[h])) for h in heads]

        uw = [_dot(block_diag(t_cat[h]), rhs[h]) for h in heads]
        kt = [[_dot_tn(k_dec[h][ci], uw[h][ci * CHUNK:(ci + 1) * CHUNK, :].astype(BF16))
               for ci in range(n_chunks)] for h in heads]
        return uw, kt, q_dec, a_cat, g_last

    def finish(sc, prep, state):
        rows = slice(sc * SC, (sc + 1) * SC)
        uw, kt, q_dec, a_cat, g_last = prep
        state = list(state)
        state_in = [[None] * n_chunks for _ in heads]
        for ci in range(n_chunks):
            for h in heads:
                sb = state[h].astype(BF16)
                state_in[h][ci] = sb
                state[h] = (state[h] * jnp.exp(g_last[h][ci])
                            - _dot(kt[h][ci][:, HEAD_DIM:].astype(BF16), sb) + kt[h][ci][:, :HEAD_DIM])
        for h in heads:
            inter = []
            for ci in range(n_chunks):
                cr = slice(ci * CHUNK, (ci + 1) * CHUNK)
                lhs = jnp.concatenate([uw[h][cr, HEAD_DIM:], q_dec[h][cr, :]], axis=0).astype(BF16)
                inter.append(_dot(lhs, state_in[h][ci]))
            v_new = jnp.concatenate([uw[h][ci * CHUNK:(ci + 1) * CHUNK, :HEAD_DIM] - inter[ci][:CHUNK]
                                     for ci in range(n_chunks)], axis=0).astype(BF16)
            o = (jnp.concatenate([inter[ci][CHUNK:] for ci in range(n_chunks)], axis=0)
                 + _dot(block_diag(a_cat[h]), v_new))
            o = o * lax.rsqrt(jnp.mean(o * o, axis=-1, keepdims=True) + NORM_EPS)
            z = act_ref[rows, head_cols(COL_GZ, h)].astype(F32)
            o_ref[rows, head_cols(0, h)] = (o * (_silu(z) * gnw_ref[...])).astype(BF16)
        return state

    n_steps = seq_len // SC
    state = [jnp.zeros((HEAD_DIM, HEAD_DIM), F32) for _ in heads]
    prep = prepare(0)
    for sc in range(n_steps):
        nxt = prepare(sc + 1) if sc + 1 < n_steps else None
        state = finish(sc, prep, state)
        prep = nxt


def _gdn(act, gate, gate_t, alog_l, dtb_l, alog_s, dtb_s, gdn_norm_w, batch, seq_len):
    t = act.shape[0]
    small = lambda shape: pl.BlockSpec(shape, lambda b: (0, 0))
    return pl.pallas_call(
        _gdn_kernel,
        grid=(batch,),
        in_specs=[
            pl.BlockSpec((seq_len, 4 * GDN_WIDTH), lambda b: (b, 0)),
            pl.BlockSpec((seq_len, LANES), lambda b: (b, 0)),
            pl.BlockSpec((N_GATE, seq_len), lambda b: (0, b)),
            small((1, LANES)), small((1, LANES)), small((N_GATE, 1)), small((N_GATE, 1)),
            small((1, HEAD_DIM)),
        ],
        out_specs=pl.BlockSpec((seq_len, GDN_WIDTH), lambda b: (b, 0)),
        out_shape=jax.ShapeDtypeStruct((t, GDN_WIDTH), BF16),
        scratch_shapes=[
            pltpu.VMEM((3, CHUNK, SC), F32),
            pltpu.VMEM((SC, SC), BF16),
        ],
        compiler_params=pltpu.CompilerParams(
            dimension_semantics=("arbitrary",), vmem_limit_bytes=VMEM_LIMIT),
        name="gdn",
    )(act, gate, gate_t, alog_l, dtb_l, alog_s, dtb_s, gdn_norm_w)


def _split3(x):
    hi = x.astype(BF16).astype(F32)
    r1 = x - hi
    mid = r1.astype(BF16).astype(F32)
    return hi, mid, r1 - mid


def _attn_kernel(lam_init, q_ref, k_ref, vt_ref, z_ref, lq1_ref, lk1_ref, lq2_ref, lk2_ref,
                 nw_ref, o_ref, bias_ref, kext_ref, vtx_ref, s_ref, acc_ref, m_ref):
    h = pl.program_id(0)
    b = pl.program_id(1)
    seq_len = k_ref.shape[0]
    lane = lax.broadcasted_iota(jnp.int32, (1, HEAD_DIM), 1)
    n_bias = 6
    log2e = math.log2(math.e)
    assert BQ == 2 * BK

    @pl.when(b == 0)
    def _():
        slope = lax.shift_left(jnp.int32(1), 2 * (DIFF_HEADS - 1 - h)).astype(F32) * (log2e / 256.0)
        pos = lax.broadcasted_iota(jnp.int32, (seq_len, 1), 0)
        in_block = (pos & (BK - 1)).astype(F32) * slope
        block_off = (pos & -BK).astype(F32) * slope
        terms = _split3(in_block) + _split3(block_off)
        tile = jnp.zeros((seq_len, HEAD_DIM), F32)
        for j, term in enumerate(terms):
            tile = jnp.where((lane == j) | (lane == DIFF_QK_DIM + j), term, tile)
        bias_ref[...] = tile
        ones_row = lax.broadcasted_iota(jnp.int32, (V_PAD, seq_len), 0) == 0
        vtx_ref[HEAD_DIM:, :] = jnp.where(ones_row, 1.0, 0.0).astype(BF16)

    k = k_ref[...].astype(F32)
    kext_ref[0] = jnp.where(lane < DIFF_QK_DIM, k, bias_ref[...]).astype(BF16)
    kext_ref[1] = jnp.where(lane >= DIFF_QK_DIM, k, bias_ref[...]).astype(BF16)
    vtx_ref[0:HEAD_DIM, :] = vt_ref[...]

    lam = (jnp.exp(jnp.sum(lq1_ref[...] * lk1_ref[...], axis=-1, keepdims=True))
           - jnp.exp(jnp.sum(lq2_ref[...] * lk2_ref[...], axis=-1, keepdims=True)) + lam_init)

    def scores(q_ext, kb, slot, lo):
        rows = slice(kb * BK, (kb + 1) * BK)
        for c in range(2):
            s_ref[slot, c, :, lo:] = _dot_nt(kext_ref[c, rows, :], q_ext[c][lo:, :])

    def accumulate(par, kb, slot, masked, lo):
        vt = vtx_ref[:, kb * BK:(kb + 1) * BK]
        for c in range(2):
            s = s_ref[slot, c, :, lo:]
            if masked:
                krel = lax.broadcasted_iota(jnp.int32, (BK, BQ - lo), 0)
                qrel = lax.broadcasted_iota(jnp.int32, (BK, BQ - lo), 1)
                s = jnp.where(krel <= qrel, s, MASK_VALUE)
            m_old = m_ref[par, c, :, lo:]
            m_new = jnp.maximum(m_old, jnp.max(s, axis=0, keepdims=True))
            alpha = jnp.exp2(m_old - m_new)
            p = jnp.exp2(s - m_new)
            acc_ref[par, c, :, lo:] = alpha * acc_ref[par, c, :, lo:] + _dot(vt, p.astype(BF16))
            m_ref[par, c, :, lo:] = m_new

    slot = 0
    for qi in range(seq_len // BQ):
        par = qi % 2
        qrows = slice(qi * BQ, (qi + 1) * BQ)
        q = q_ref[qrows, :].astype(F32) * (DIFF_QK_DIM ** -0.5 * log2e)
        q_ext = (
            jnp.where(lane < DIFF_QK_DIM, q,
                      jnp.where(lane < DIFF_QK_DIM + n_bias, 1.0, 0.0)).astype(BF16),
            jnp.where(lane >= DIFF_QK_DIM, q, jnp.where(lane < n_bias, 1.0, 0.0)).astype(BF16),
        )
        m_ref[par] = jnp.full(m_ref.shape[1:], MASK_VALUE, F32)
        acc_ref[par] = jnp.zeros(acc_ref.shape[1:], F32)
        n_blocks = 2 * qi + 2
        lo_of = lambda kb: BK if kb == n_blocks - 1 else 0
        scores(q_ext, 0, slot, lo_of(0))
        for kb in range(n_blocks):
            if kb + 1 < n_blocks:
                scores(q_ext, kb + 1, 1 - slot, lo_of(kb + 1))
            accumulate(par, kb, slot, kb >= 2 * qi, lo_of(kb))
            slot = 1 - slot

        num = (acc_ref[par, 0, 0:HEAD_DIM, :], acc_ref[par, 1, 0:HEAD_DIM, :])
        den = (acc_ref[par, 0, HEAD_DIM:HEAD_DIM + 1, :], acc_ref[par, 1, HEAD_DIM:HEAD_DIM + 1, :])
        o_t = num[0] / den[0] - lam * (num[1] / den[1])
        o = o_t.T
        o = o * lax.rsqrt(jnp.mean(o * o, axis=-1, keepdims=True) + NORM_EPS)
        o = o * nw_ref[...] * (1.0 - lam_init)
        o_ref[qrows, :] = (o * _silu(z_ref[qrows, :].astype(F32))).astype(BF16)


def _attn(act, v_t, lq1, lk1, lq2, lk2, diff_norm_w, lam_init, batch, seq_len):
    t = act.shape[0]
    small = lambda shape: pl.BlockSpec(shape, lambda h, b: (0, 0))
    return pl.pallas_call(
        functools.partial(_attn_kernel, lam_init),
        grid=(DIFF_HEADS, batch),
        in_specs=[
            pl.BlockSpec((seq_len, HEAD_DIM), lambda h, b: (b, COL_DQ + h)),
            pl.BlockSpec((seq_len, HEAD_DIM), lambda h, b: (b, COL_DK + h)),
            pl.BlockSpec((HEAD_DIM, seq_len), lambda h, b: (h, b)),
            pl.BlockSpec((seq_len, HEAD_DIM), lambda h, b: (b, COL_DZ + h)),
            small((1, DIFF_QK_DIM)), small((1, DIFF_QK_DIM)),
            small((1, DIFF_QK_DIM)), small((1, DIFF_QK_DIM)),
            small((1, HEAD_DIM)),
        ],
        out_specs=pl.BlockSpec((seq_len, HEAD_DIM), lambda h, b: (b, h)),
        out_shape=jax.ShapeDtypeStruct((t, DIFF_WIDTH), BF16),
        scratch_shapes=[
            pltpu.VMEM((seq_len, HEAD_DIM), F32),
            pltpu.VMEM((2, seq_len, HEAD_DIM), BF16),
            pltpu.VMEM((HEAD_DIM + V_PAD, seq_len), BF16),
            pltpu.VMEM((2, 2, BK, BQ), F32),
            pltpu.VMEM((2, 2, HEAD_DIM + V_PAD, BQ), F32),
            pltpu.VMEM((2, 2, 1, BQ), F32),
        ],
        compiler_params=pltpu.CompilerParams(
            dimension_semantics=("arbitrary", "arbitrary"), vmem_limit_bytes=VMEM_LIMIT),
        name="diffattn",
    )(act, act, v_t, act, lq1, lk1, lq2, lk2, diff_norm_w)


def _outproj_kernel(oa_ref, ob_ref, x_ref, w_ref, fw_ref, out_ref):
    mix = _dot(oa_ref[...], w_ref[0:GDN_WIDTH, :]) + _dot(ob_ref[...], w_ref[GDN_WIDTH:, :])
    y = x_ref[...] + mix
    ms = jnp.mean(y * y, axis=-1, keepdims=True)
    out_ref[...] = y * lax.rsqrt(ms + NORM_EPS) * fw_ref[...]


def _outproj(o_a, o_b, xf, w_out, final_norm_w):
    t = xf.shape[0]
    return pl.pallas_call(
        _outproj_kernel,
        grid=(t // TM_OUT,),
        in_specs=[
            pl.BlockSpec((TM_OUT, GDN_WIDTH), lambda i: (i, 0)),
            pl.BlockSpec((TM_OUT, DIFF_WIDTH), lambda i: (i, 0)),
            pl.BlockSpec((TM_OUT, D_MODEL), lambda i: (i, 0)),
            pl.BlockSpec((D_MODEL, D_MODEL), lambda i: (0, 0)),
            pl.BlockSpec((1, D_MODEL), lambda i: (0, 0)),
        ],
        out_specs=pl.BlockSpec((TM_OUT, D_MODEL), lambda i: (i, 0)),
        out_shape=jax.ShapeDtypeStruct((t, D_MODEL), F32),
        compiler_params=pltpu.CompilerParams(
            dimension_semantics=("arbitrary",), vmem_limit_bytes=VMEM_LIMIT),
        name="outproj",
    )(o_a, o_b, xf, w_out, final_norm_w)


def kernel(x, norm_w, w_in, conv_w, a_log, dt_bias, gdn_norm_w, lambda_q1, lambda_k1,
           lambda_q2, lambda_k2, diff_norm_w, w_out, final_norm_w):
    batch, seq_len, d_model = x.shape
    depth = norm_w.shape[0]
    assert depth == 1 and d_model == D_MODEL
    assert seq_len % TM_IN == 0 and seq_len % SC == 0 and seq_len % BQ == 0 and BQ % BK == 0
    n_wide_a = 4 * GDN_WIDTH
    xf = x.reshape(batch * seq_len, d_model)

    w = w_in[0]
    c_dq = n_wide_a + N_GATE
    c_dv = c_dq + 2 * DIFF_WIDTH
    c_dz = c_dv + DIFF_WIDTH
    w_main = jnp.concatenate([w[:, :n_wide_a], w[:, c_dq:c_dv], w[:, c_dz:]], axis=1).astype(BF16)
    w_v_t = w[:, c_dv:c_dz].T.astype(BF16)
    w_gate_cols = w[:, n_wide_a:n_wide_a + N_GATE]
    w_gate = jnp.pad(w_gate_cols, ((0, 0), (0, LANES - N_GATE))).astype(BF16)
    w_gate_t = w_gate_cols.T.astype(BF16)

    act, gate, gate_t, v_t = _inproj(xf, norm_w[0][None, :], w_main, w_gate, w_gate_t, w_v_t,
                                     conv_w[0], seq_len)

    pad_l = lambda vec: jnp.pad(vec, (GDN_HEADS, LANES - N_GATE))[None, :]
    pad_s = lambda vec: jnp.pad(vec, (GDN_HEADS, 0))[:, None]
    o_a = _gdn(act, gate, gate_t, pad_l(a_log[0]), pad_l(dt_bias[0]), pad_s(a_log[0]),
               pad_s(dt_bias[0]), gdn_norm_w[0][None, :], batch, seq_len)

    lam_init = 0.8 - 0.6 * math.exp(-0.3 * 0)
    o_b = _attn(act, v_t, lambda_q1[0][None, :], lambda_k1[0][None, :], lambda_q2[0][None, :],
                lambda_k2[0][None, :], diff_norm_w[0][None, :], lam_init, batch, seq_len)

    out = _outproj(o_a, o_b, xf, w_out[0].astype(BF16), final_norm_w[None, :])
    return out.reshape(batch, seq_len, d_model)
```

```python
import functools
import math

import jax
import jax.numpy as jnp
from jax import lax
from jax.experimental import pallas as pl
from jax.experimental.pallas import tpu as pltpu

F32 = jnp.float32
BF16 = jnp.bfloat16

D_MODEL = 1024
GDN_HEADS = 4
HEAD_DIM = 128
GDN_WIDTH = GDN_HEADS * HEAD_DIM
DIFF_HEADS = 4
DIFF_QK_DIM = 64
DIFF_WIDTH = DIFF_HEADS * HEAD_DIM
CONV_K = 4
NORM_EPS = 1e-6
N_GATE = 2 * GDN_HEADS
N_MAIN = 4 * GDN_WIDTH + 3 * DIFF_WIDTH
N_CONV = 3 * GDN_WIDTH
LANES = 128
MASK_VALUE = -1e30

COL_GQ, COL_GK, COL_GV, COL_GZ = 0, 4, 8, 12
COL_DQ, COL_DK, COL_DZ = 16, 20, 24

TM_IN = 1024
NC_IN = 256
TAIL = 16
TM_OUT = 1024
SC = 256
CHUNK = 64
BQ = 512
BK = 256
V_PAD = 16
VMEM_LIMIT = 48 * 1024 * 1024


def _sigmoid(x):
    return 1.0 / (1.0 + jnp.exp(-x))


def _silu(x):
    half = 0.5 * x
    return half + half * jnp.tanh(half)


def _softplus(x):
    return jnp.maximum(x, 0.0) + jnp.log1p(jnp.exp(-jnp.abs(x)))


def _dot(a, b):
    return jnp.dot(a, b, preferred_element_type=F32)


def _dot_nt(a, b):
    return lax.dot_general(a, b, (((1,), (1,)), ((), ())), preferred_element_type=F32)


def _dot_tn(a, b):
    return lax.dot_general(a, b, (((0,), (0,)), ((), ())), preferred_element_type=F32)


def _inproj_kernel(tiles_per_seq, x_ref, nw_ref, w_ref, wgt_ref, wvt_ref, cw_ref,
                   main_ref, gate_t_ref, vt_ref, tail_ref):
    i = pl.program_id(0)
    x = x_ref[...]
    ms = jnp.mean(x * x, axis=-1, keepdims=True)
    h = (x * lax.rsqrt(ms + NORM_EPS) * nw_ref[...]).astype(BF16)

    gate_t_ref[...] = _dot_nt(wgt_ref[...], h)
    vt_ref[...] = _dot_nt(wvt_ref[...], h).astype(BF16)

    @pl.when(i % tiles_per_seq == 0)
    def _():
        tail_ref[...] = jnp.zeros_like(tail_ref)

    tm = x.shape[0]
    row = lax.broadcasted_iota(jnp.int32, (TAIL, LANES), 0)

    def conv_silu(acc, n0):
        cols = slice(n0, n0 + LANES)
        last = acc[tm - TAIL:, :]
        delta = tail_ref[:, cols] - last
        tail_ref[:, cols] = last
        y = acc * cw_ref[CONV_K - 1:CONV_K, cols]
        fix = jnp.zeros((TAIL, LANES), F32)
        for s in range(1, CONV_K):
            wk = cw_ref[CONV_K - 1 - s:CONV_K - s, cols]
            y = y + pltpu.roll(acc, s, axis=0) * wk
            fix = fix + jnp.where(row < s, pltpu.roll(delta, s, axis=0), 0.0) * wk
        main_ref[:, cols] = _silu(y).astype(BF16)
        main_ref[0:TAIL, cols] = _silu(y[:TAIL, :] + fix).astype(BF16)

    plain = [slice(n0, n0 + NC_IN) for n0 in range(N_CONV, N_MAIN, NC_IN)]
    pending = []
    for n0 in range(0, N_CONV, NC_IN):
        acc = _dot(h, w_ref[:, n0:n0 + NC_IN])
        if pending:
            conv_silu(*pending.pop(0))
        if plain:
            pc = plain.pop(0)
            main_ref[:, pc] = _dot(h, w_ref[:, pc]).astype(BF16)
        if pending:
            conv_silu(*pending.pop(0))
        pending = [(acc[:, j:j + LANES], n0 + j) for j in range(0, NC_IN, LANES)]
    for piece in pending:
        if plain:
            pc = plain.pop(0)
            main_ref[:, pc] = _dot(h, w_ref[:, pc]).astype(BF16)
        conv_silu(*piece)
    for pc in plain:
        main_ref[:, pc] = _dot(h, w_ref[:, pc]).astype(BF16)


def _inproj(xf, norm_w, w_main, w_gate_t, w_v_t, conv_w, seq_len):
    t = xf.shape[0]
    tiles_per_seq = seq_len // TM_IN
    return pl.pallas_call(
        functools.partial(_inproj_kernel, tiles_per_seq),
        grid=(t // TM_IN,),
        in_specs=[
            pl.BlockSpec((TM_IN, D_MODEL), lambda i: (i, 0)),
            pl.BlockSpec((1, D_MODEL), lambda i: (0, 0)),
            pl.BlockSpec((D_MODEL, N_MAIN), lambda i: (0, 0)),
            pl.BlockSpec((N_GATE, D_MODEL), lambda i: (0, 0)),
            pl.BlockSpec((DIFF_WIDTH, D_MODEL), lambda i: (0, 0)),
            pl.BlockSpec((CONV_K, N_CONV), lambda i: (0, 0)),
        ],
        out_specs=[
            pl.BlockSpec((TM_IN, N_MAIN), lambda i: (i, 0)),
            pl.BlockSpec((N_GATE, TM_IN), lambda i: (0, i)),
            pl.BlockSpec((DIFF_WIDTH, TM_IN), lambda i: (0, i)),
        ],
        out_shape=[
            jax.ShapeDtypeStruct((t, N_MAIN), BF16),
            jax.ShapeDtypeStruct((N_GATE, t), F32),
            jax.ShapeDtypeStruct((DIFF_WIDTH, t), BF16),
        ],
        scratch_shapes=[pltpu.VMEM((TAIL, N_CONV), F32)],
        compiler_params=pltpu.CompilerParams(
            dimension_semantics=("arbitrary",), vmem_limit_bytes=VMEM_LIMIT),
        name="inproj",
    )(xf, norm_w, w_main, w_gate_t, w_v_t, conv_w)


def _gdn_kernel(act_ref, gate_t_ref, alog_s_ref, dtb_s_ref, gnw_ref, o_ref, mask_ref, bd_ref):
    seq_len = act_ref.shape[0]
    n_chunks = SC // CHUNK
    heads = range(GDN_HEADS)
    i_cat = lax.broadcasted_iota(jnp.int32, (CHUNK, SC), 0)
    j_cat = lax.broadcasted_iota(jnp.int32, (CHUNK, SC), 1) & (CHUNK - 1)
    mask_ref[0] = jnp.where(i_cat >= j_cat, 0.0, MASK_VALUE)
    mask_ref[1] = jnp.where(i_cat > j_cat, -1.0, 0.0)
    mask_ref[2] = jnp.where(i_cat == j_cat, 1.0, 0.0)
    r = lax.broadcasted_iota(jnp.int32, (SC, SC), 0)
    c = lax.broadcasted_iota(jnp.int32, (SC, SC), 1)
    bd_ref[...] = jnp.where((r & -CHUNK) == (c & -CHUNK), 1.0, 0.0).astype(BF16)
    lane_chunk = lax.broadcasted_iota(jnp.int32, (1, SC), 1) & -CHUNK

    lane_in_chunk = lax.broadcasted_iota(jnp.int32, (N_GATE, SC), 1) & (CHUNK - 1)
    neg_a_s = -jnp.exp(alog_s_ref[...])
    scale = HEAD_DIM ** -0.5

    def cat(x):
        out = x[(n_chunks - 1) * CHUNK:, :]
        for ci in range(n_chunks - 2, -1, -1):
            out = jnp.where(lane_chunk == ci * CHUNK, x[ci * CHUNK:(ci + 1) * CHUNK, :], out)
        return out

    def block_diag(x_cat):
        xb = x_cat.astype(BF16)
        zero = jnp.zeros((CHUNK, LANES), BF16)
        row_blocks = []
        for ci in range(n_chunks):
            g = ci * CHUNK // LANES
            grp = slice(g * LANES, (g + 1) * LANES)
            part = xb[:, grp] * bd_ref[ci * CHUNK:(ci + 1) * CHUNK, grp]
            row_blocks.append(jnp.concatenate(
                [part if j == g else zero for j in range(SC // LANES)], axis=1))
        return jnp.concatenate(row_blocks, axis=0)

    def head_cols(col, h):
        return slice((col + h) * HEAD_DIM, (col + h + 1) * HEAD_DIM)

    def prepare(sc):
        rows = slice(sc * SC, (sc + 1) * SC)
        gate_t = gate_t_ref[:, rows]
        beta = _sigmoid(gate_t)
        log_beta = -_softplus(-gate_t)
        g_step = neg_a_s * _softplus(gate_t + dtb_s_ref[...])
        gc = g_step
        g_after = jnp.zeros_like(g_step)
        step = 1
        while step < CHUNK:
            gc = gc + jnp.where(lane_in_chunk >= step, pltpu.roll(gc, step, axis=1), 0.0)
            step *= 2
        step = 1
        tail = g_step
        while step < CHUNK:
            ahead = jnp.where(lane_in_chunk + step < CHUNK, pltpu.roll(tail, SC - step, axis=1), 0.0)
            g_after = g_after + ahead
            tail = tail + ahead
            step *= 2
        e_gc = jnp.exp(gc)
        is_beta_row = lax.broadcasted_iota(jnp.int32, (N_GATE, SC), 0) < GDN_HEADS
        swap = pltpu.roll(e_gc, GDN_HEADS, axis=0)
        cols_a = jnp.where(is_beta_row, beta, gc).T
        cols_b = jnp.where(is_beta_row, log_beta, e_gc).T
        cols_c = jnp.where(is_beta_row, beta * swap, jnp.exp(g_after)).T
        causal, neg_strict, eye = mask_ref[0], mask_ref[1], mask_ref[2]

        n_pow, a_cat, rhs, q_dec, k_dec, decay_last = [], [], [], [], [], []
        for h in heads:
            q = act_ref[rows, head_cols(COL_GQ, h)].astype(F32)
            k = act_ref[rows, head_cols(COL_GK, h)].astype(F32)
            v = act_ref[rows, head_cols(COL_GV, h)].astype(F32)
            q = q * (lax.rsqrt(jnp.sum(q * q, axis=-1, keepdims=True) + 1e-6) * scale)
            k = k * lax.rsqrt(jnp.sum(k * k, axis=-1, keepdims=True) + 1e-6)
            hb, hg = slice(h, h + 1), slice(GDN_HEADS + h, GDN_HEADS + h + 1)
            kb = k.astype(BF16)
            gram = _dot_nt(jnp.concatenate([kb, q.astype(BF16)], axis=0), kb)
            e_cat = cat(cols_a[:, hg]) - gc[hg, :] + causal
            n_pow.append(cat(gram[:SC]) * jnp.exp(e_cat + cat(cols_b[:, hb])) * neg_strict)
            a_cat.append(cat(gram[SC:]) * jnp.exp(e_cat))
            rhs.append(jnp.concatenate([cols_a[:, hb] * v, cols_c[:, hb] * k], axis=1).astype(BF16))
            q_dec.append(q * cols_b[:, hg])
            kd = (k * cols_c[:, hg]).astype(BF16)
            k_dec.append([kd[ci * CHUNK:(ci + 1) * CHUNK, :] for ci in range(n_chunks)])
            decay_last.append([e_gc[hg, (ci + 1) * CHUNK - 1:(ci + 1) * CHUNK] for ci in range(n_chunks)])

        t_cat = [eye + n_pow[h] for h in heads]
        n_pow = [_dot(n_pow[h].astype(BF16), block_diag(n_pow[h])) for h in heads]
        span = 2
        while span < CHUNK // 2:
            both = [_dot(jnp.concatenate([t_cat[h], n_pow[h]], axis=0).astype(BF16), block_diag(n_pow[h]))
                    for h in heads]
            t_cat = [t_cat[h] + both[h][:CHUNK] for h in heads]
            n_pow = [both[h][CHUNK:] for h in heads]
            span *= 2
        t_cat = [t_cat[h] + _dot(t_cat[h].astype(BF16), block_diag(n_pow[h])) for h in heads]

        uw = [_dot(block_diag(t_cat[h]), rhs[h]) for h in heads]
        kt = [[_dot_tn(k_dec[h][ci], uw[h][ci * CHUNK:(ci + 1) * CHUNK, :].astype(BF16))
               for ci in range(n_chunks)] for h in heads]
        return uw, kt, q_dec, a_cat, decay_last

    def finish(sc, prep, state):
        rows = slice(sc * SC, (sc + 1) * SC)
        uw, kt, q_dec, a_cat, decay_last = prep
        state = list(state)
        state_in = [[None] * n_chunks for _ in heads]
        for ci in range(n_chunks):
            for h in heads:
                sb = state[h].astype(BF16)
                state_in[h][ci] = sb
                state[h] = (state[h] * decay_last[h][ci]
                            - _dot(kt[h][ci][:, HEAD_DIM:].astype(BF16), sb) + kt[h][ci][:, :HEAD_DIM])
        for h in heads:
            inter = []
            for ci in range(n_chunks):
                cr = slice(ci * CHUNK, (ci + 1) * CHUNK)
                lhs = jnp.concatenate([uw[h][cr, HEAD_DIM:], q_dec[h][cr, :]], axis=0).astype(BF16)
                inter.append(_dot(lhs, state_in[h][ci]))
            v_new = jnp.concatenate([uw[h][ci * CHUNK:(ci + 1) * CHUNK, :HEAD_DIM] - inter[ci][:CHUNK]
                                     for ci in range(n_chunks)], axis=0).astype(BF16)
            o = (jnp.concatenate([inter[ci][CHUNK:] for ci in range(n_chunks)], axis=0)
                 + _dot(block_diag(a_cat[h]), v_new))
            o = o * lax.rsqrt(jnp.mean(o * o, axis=-1, keepdims=True) + NORM_EPS)
            z = act_ref[rows, head_cols(COL_GZ, h)].astype(F32)
            o_ref[rows, head_cols(0, h)] = (o * (_silu(z) * gnw_ref[...])).astype(BF16)
        return state

    n_steps = seq_len // SC
    state = [jnp.zeros((HEAD_DIM, HEAD_DIM), F32) for _ in heads]
    prep = prepare(0)
    for sc in range(n_steps):
        nxt = prepare(sc + 1) if sc + 1 < n_steps else None
        state = finish(sc, prep, state)
        prep = nxt


def _gdn(act, gate_t, alog_s, dtb_s, gdn_norm_w, batch, seq_len):
    t = act.shape[0]
    small = lambda shape: pl.BlockSpec(shape, lambda b: (0, 0))
    return pl.pallas_call(
        _gdn_kernel,
        grid=(batch,),
        in_specs=[
            pl.BlockSpec((seq_len, 4 * GDN_WIDTH), lambda b: (b, 0)),
            pl.BlockSpec((N_GATE, seq_len), lambda b: (0, b)),
            small((N_GATE, 1)), small((N_GATE, 1)),
            small((1, HEAD_DIM)),
        ],
        out_specs=pl.BlockSpec((seq_len, GDN_WIDTH), lambda b: (b, 0)),
        out_shape=jax.ShapeDtypeStruct((t, GDN_WIDTH), BF16),
        scratch_shapes=[
            pltpu.VMEM((3, CHUNK, SC), F32),
            pltpu.VMEM((SC, SC), BF16),
        ],
        compiler_params=pltpu.CompilerParams(
            dimension_semantics=("arbitrary",), vmem_limit_bytes=VMEM_LIMIT),
        name="gdn",
    )(act, gate_t, alog_s, dtb_s, gdn_norm_w)


def _split3(x):
    hi = x.astype(BF16).astype(F32)
    r1 = x - hi
    mid = r1.astype(BF16).astype(F32)
    return hi, mid, r1 - mid


def _attn_kernel(lam_init, q_ref, k_ref, vt_ref, z_ref, lq1_ref, lk1_ref, lq2_ref, lk2_ref,
                 nw_ref, o_ref, bias_ref, kext_ref, vtx_ref, acc_ref, m_ref):
    h = pl.program_id(0)
    b = pl.program_id(1)
    seq_len = k_ref.shape[0]
    lane = lax.broadcasted_iota(jnp.int32, (1, HEAD_DIM), 1)
    n_bias = 6
    log2e = math.log2(math.e)
    assert BQ == 2 * BK

    @pl.when(b == 0)
    def _():
        slope = lax.shift_left(jnp.int32(1), 2 * (DIFF_HEADS - 1 - h)).astype(F32) * (log2e / 256.0)
        pos = lax.broadcasted_iota(jnp.int32, (seq_len, 1), 0)
        in_block = (pos & (BK - 1)).astype(F32) * slope
        block_off = (pos & -BK).astype(F32) * slope
        terms = _split3(in_block) + _split3(block_off)
        tile = jnp.zeros((seq_len, HEAD_DIM), F32)
        for j, term in enumerate(terms):
            tile = jnp.where((lane == j) | (lane == DIFF_QK_DIM + j), term, tile)
        bias_ref[...] = tile
        ones_row = lax.broadcasted_iota(jnp.int32, (V_PAD, seq_len), 0) == 0
        vtx_ref[HEAD_DIM:, :] = jnp.where(ones_row, 1.0, 0.0).astype(BF16)

    k = k_ref[...].astype(F32)
    kext_ref[0] = jnp.where(lane < DIFF_QK_DIM, k, bias_ref[...]).astype(BF16)
    kext_ref[1] = jnp.where(lane >= DIFF_QK_DIM, k, bias_ref[...]).astype(BF16)
    vtx_ref[0:HEAD_DIM, :] = vt_ref[...]

    lam = (jnp.exp(jnp.sum(lq1_ref[...] * lk1_ref[...], axis=-1, keepdims=True))
           - jnp.exp(jnp.sum(lq2_ref[...] * lk2_ref[...], axis=-1, keepdims=True)) + lam_init)

    def scores(q_ext, kb, masked, lo):
        rows = slice(kb * BK, (kb + 1) * BK)
        out = []
        for c in range(2):
            s = _dot_nt(kext_ref[c, rows, :], q_ext[c][lo:, :])
            if masked:
                krel = lax.broadcasted_iota(jnp.int32, (BK, BQ - lo), 0)
                qrel = lax.broadcasted_iota(jnp.int32, (BK, BQ - lo), 1)
                s = jnp.where(krel <= qrel, s, MASK_VALUE)
            out.append((s, jnp.max(s, axis=0, keepdims=True)))
        return out

    def accumulate(par, kb, block, lo):
        vt = vtx_ref[:, kb * BK:(kb + 1) * BK]
        for c in range(2):
            s, s_max = block[c]
            m_old = m_ref[par, c, :, lo:]
            m_new = jnp.maximum(m_old, s_max)
            alpha = jnp.exp2(m_old - m_new)
            p = jnp.exp2(s - m_new)
            acc_ref[par, c, :, lo:] = alpha * acc_ref[par, c, :, lo:] + _dot(vt, p.astype(BF16))
            m_ref[par, c, :, lo:] = m_new

    for qi in range(seq_len // BQ):
        par = qi % 2
        qrows = slice(qi * BQ, (qi + 1) * BQ)
        q = q_ref[qrows, :].astype(F32) * (DIFF_QK_DIM ** -0.5 * log2e)
        q_ext = (
            jnp.where(lane < DIFF_QK_DIM, q,
                      jnp.where(lane < DIFF_QK_DIM + n_bias, 1.0, 0.0)).astype(BF16),
            jnp.where(lane >= DIFF_QK_DIM, q, jnp.where(lane < n_bias, 1.0, 0.0)).astype(BF16),
        )
        m_ref[par] = jnp.full(m_ref.shape[1:], MASK_VALUE, F32)
        acc_ref[par] = jnp.zeros(acc_ref.shape[1:], F32)
        n_blocks = 2 * qi + 2
        lo_of = lambda kb: BK if kb == n_blocks - 1 else 0
        cur = scores(q_ext, 0, 0 >= 2 * qi, lo_of(0))
        for kb in range(n_blocks):
            nxt = scores(q_ext, kb + 1, kb + 1 >= 2 * qi, lo_of(kb + 1)) if kb + 1 < n_blocks else None
            accumulate(par, kb, cur, lo_of(kb))
            cur = nxt

        num = (acc_ref[par, 0, 0:HEAD_DIM, :], acc_ref[par, 1, 0:HEAD_DIM, :])
        den = (acc_ref[par, 0, HEAD_DIM:HEAD_DIM + 1, :], acc_ref[par, 1, HEAD_DIM:HEAD_DIM + 1, :])
        o_t = num[0] / den[0] - lam * (num[1] / den[1])
        o = o_t.T
        o = o * lax.rsqrt(jnp.mean(o * o, axis=-1, keepdims=True) + NORM_EPS)
        o = o * nw_ref[...] * (1.0 - lam_init)
        o_ref[qrows, :] = (o * _silu(z_ref[qrows, :].astype(F32))).astype(BF16)


def _attn(act, v_t, lq1, lk1, lq2, lk2, diff_norm_w, lam_init, batch, seq_len):
    t = act.shape[0]
    small = lambda shape: pl.BlockSpec(shape, lambda h, b: (0, 0))
    return pl.pallas_call(
        functools.partial(_attn_kernel, lam_init),
        grid=(DIFF_HEADS, batch),
        in_specs=[
            pl.BlockSpec((seq_len, HEAD_DIM), lambda h, b: (b, COL_DQ + h)),
            pl.BlockSpec((seq_len, HEAD_DIM), lambda h, b: (b, COL_DK + h)),
            pl.BlockSpec((HEAD_DIM, seq_len), lambda h, b: (h, b)),
            pl.BlockSpec((seq_len, HEAD_DIM), lambda h, b: (b, COL_DZ + h)),
            small((1, DIFF_QK_DIM)), small((1, DIFF_QK_DIM)),
            small((1, DIFF_QK_DIM)), small((1, DIFF_QK_DIM)),
            small((1, HEAD_DIM)),
        ],
        out_specs=pl.BlockSpec((seq_len, HEAD_DIM), lambda h, b: (b, h)),
        out_shape=jax.ShapeDtypeStruct((t, DIFF_WIDTH), BF16),
        scratch_shapes=[
            pltpu.VMEM((seq_len, HEAD_DIM), F32),
            pltpu.VMEM((2, seq_len, HEAD_DIM), BF16),
            pltpu.VMEM((HEAD_DIM + V_PAD, seq_len), BF16),
            pltpu.VMEM((2, 2, HEAD_DIM + V_PAD, BQ), F32),
            pltpu.VMEM((2, 2, 1, BQ), F32),
        ],
        compiler_params=pltpu.CompilerParams(
            dimension_semantics=("arbitrary", "arbitrary"), vmem_limit_bytes=VMEM_LIMIT),
        name="diffattn",
    )(act, act, v_t, act, lq1, lk1, lq2, lk2, diff_norm_w)


def _outproj_kernel(oa_ref, ob_ref, x_ref, w_ref, fw_ref, out_ref):
    mix = _dot(oa_ref[...], w_ref[0:GDN_WIDTH, :]) + _dot(ob_ref[...], w_ref[GDN_WIDTH:, :])
    y = x_ref[...] + mix
    ms = jnp.mean(y * y, axis=-1, keepdims=True)
    out_ref[...] = y * lax.rsqrt(ms + NORM_EPS) * fw_ref[...]


def _outproj(o_a, o_b, xf, w_out, final_norm_w):
    t = xf.shape[0]
    return pl.pallas_call(
        _outproj_kernel,
        grid=(t // TM_OUT,),
        in_specs=[
            pl.BlockSpec((TM_OUT, GDN_WIDTH), lambda i: (i, 0)),
            pl.BlockSpec((TM_OUT, DIFF_WIDTH), lambda i: (i, 0)),
            pl.BlockSpec((TM_OUT, D_MODEL), lambda i: (i, 0)),
            pl.BlockSpec((D_MODEL, D_MODEL), lambda i: (0, 0)),
            pl.BlockSpec((1, D_MODEL), lambda i: (0, 0)),
        ],
        out_specs=pl.BlockSpec((TM_OUT, D_MODEL), lambda i: (i, 0)),
        out_shape=jax.ShapeDtypeStruct((t, D_MODEL), F32),
        compiler_params=pltpu.CompilerParams(
            dimension_semantics=("arbitrary",), vmem_limit_bytes=VMEM_LIMIT),
        name="outproj",
    )(o_a, o_b, xf, w_out, final_norm_w)


def kernel(x, norm_w, w_in, conv_w, a_log, dt_bias, gdn_norm_w, lambda_q1, lambda_k1,
           lambda_q2, lambda_k2, diff_norm_w, w_out, final_norm_w):
    batch, seq_len, d_model = x.shape
    depth = norm_w.shape[0]
    assert depth == 1 and d_model == D_MODEL
    assert seq_len % TM_IN == 0 and seq_len % SC == 0 and seq_len % BQ == 0 and BQ % BK == 0
    n_wide_a = 4 * GDN_WIDTH
    xf = x.reshape(batch * seq_len, d_model)

    w = w_in[0]
    c_dq = n_wide_a + N_GATE
    c_dv = c_dq + 2 * DIFF_WIDTH
    c_dz = c_dv + DIFF_WIDTH
    w_main = jnp.concatenate([w[:, :n_wide_a], w[:, c_dq:c_dv], w[:, c_dz:]], axis=1).astype(BF16)
    w_v_t = w[:, c_dv:c_dz].T.astype(BF16)
    w_gate_t = w[:, n_wide_a:n_wide_a + N_GATE].T.astype(BF16)

    act, gate_t, v_t = _inproj(xf, norm_w[0][None, :], w_main, w_gate_t, w_v_t, conv_w[0], seq_len)

    pad_s = lambda vec: jnp.pad(vec, (GDN_HEADS, 0))[:, None]
    o_a = _gdn(act, gate_t, pad_s(a_log[0]), pad_s(dt_bias[0]), gdn_norm_w[0][None, :],
               batch, seq_len)

    lam_init = 0.8 - 0.6 * math.exp(-0.3 * 0)
    o_b = _attn(act, v_t, lambda_q1[0][None, :], lambda_k1[0][None, :], lambda_q2[0][None, :],
                lambda_k2[0][None, :], diff_norm_w[0][None, :], lam_init, batch, seq_len)

    out = _outproj(o_a, o_b, xf, w_out[0].astype(BF16), final_norm_w[None, :])
    return out.reshape(batch, seq_len, d_model)
```

```python
import functools
import math

import jax
import jax.numpy as jnp
from jax import lax
from jax.experimental import pallas as pl
from jax.experimental.pallas import tpu as pltpu

F32 = jnp.float32
BF16 = jnp.bfloat16

D_MODEL = 1024
GDN_HEADS = 4
HEAD_DIM = 128
GDN_WIDTH = GDN_HEADS * HEAD_DIM
DIFF_HEADS = 4
DIFF_QK_DIM = 64
DIFF_WIDTH = DIFF_HEADS * HEAD_DIM
CONV_K = 4
NORM_EPS = 1e-6
N_GATE = 2 * GDN_HEADS
N_MAIN = 4 * GDN_WIDTH + 3 * DIFF_WIDTH
N_CONV = 3 * GDN_WIDTH
LANES = 128
MASK_VALUE = -1e30

COL_GQ, COL_GK, COL_GV, COL_GZ = 0, 4, 8, 12
COL_DQ, COL_DK, COL_DZ = 16, 20, 24

TM_IN = 1024
NC_IN = 256
TAIL = 16
TM_OUT = 1024
SC = 256
CHUNK = 64
BQ = 512
BK = 256
V_PAD = 16
VMEM_LIMIT = 48 * 1024 * 1024


def _sigmoid(x):
    return 1.0 / (1.0 + jnp.exp(-x))


def _silu(x):
    half = 0.5 * x
    return half + half * jnp.tanh(half)


def _softplus(x):
    return jnp.maximum(x, 0.0) + jnp.log1p(jnp.exp(-jnp.abs(x)))


def _dot(a, b):
    return jnp.dot(a, b, preferred_element_type=F32)


def _dot_nt(a, b):
    return lax.dot_general(a, b, (((1,), (1,)), ((), ())), preferred_element_type=F32)


def _dot_tn(a, b):
    return lax.dot_general(a, b, (((0,), (0,)), ((), ())), preferred_element_type=F32)


def _inproj_kernel(tiles_per_seq, x_ref, nw_ref, w_ref, wgt_ref, wvt_ref, cw_ref,
                   main_ref, gate_t_ref, vt_ref, tail_ref):
    i = pl.program_id(0)
    x = x_ref[...]
    ms = jnp.mean(x * x, axis=-1, keepdims=True)
    h = (x * lax.rsqrt(ms + NORM_EPS) * nw_ref[...]).astype(BF16)

    gate_t_ref[...] = _dot_nt(wgt_ref[...], h)
    vt_ref[...] = _dot_nt(wvt_ref[...], h).astype(BF16)

    @pl.when(i % tiles_per_seq == 0)
    def _():
        tail_ref[...] = jnp.zeros_like(tail_ref)

    tm = x.shape[0]
    row = lax.broadcasted_iota(jnp.int32, (TAIL, LANES), 0)

    def conv_silu(acc, n0):
        cols = slice(n0, n0 + LANES)
        last = acc[tm - TAIL:, :]
        delta = tail_ref[:, cols] - last
        tail_ref[:, cols] = last
        y = acc * cw_ref[CONV_K - 1:CONV_K, cols]
        fix = jnp.zeros((TAIL, LANES), F32)
        for s in range(1, CONV_K):
            wk = cw_ref[CONV_K - 1 - s:CONV_K - s, cols]
            y = y + pltpu.roll(acc, s, axis=0) * wk
            fix = fix + jnp.where(row < s, pltpu.roll(delta, s, axis=0), 0.0) * wk
        main_ref[:, cols] = _silu(y).astype(BF16)
        main_ref[0:TAIL, cols] = _silu(y[:TAIL, :] + fix).astype(BF16)

    plain = [slice(n0, n0 + NC_IN) for n0 in range(N_CONV, N_MAIN, NC_IN)]
    pending = []
    for n0 in range(0, N_CONV, NC_IN):
        acc = _dot(h, w_ref[:, n0:n0 + NC_IN])
        if pending:
            conv_silu(*pending.pop(0))
        if plain:
            pc = plain.pop(0)
            main_ref[:, pc] = _dot(h, w_ref[:, pc]).astype(BF16)
        if pending:
            conv_silu(*pending.pop(0))
        pending = [(acc[:, j:j + LANES], n0 + j) for j in range(0, NC_IN, LANES)]
    for piece in pending:
        if plain:
            pc = plain.pop(0)
            main_ref[:, pc] = _dot(h, w_ref[:, pc]).astype(BF16)
        conv_silu(*piece)
    for pc in plain:
        main_ref[:, pc] = _dot(h, w_ref[:, pc]).astype(BF16)


def _inproj(xf, norm_w, w_main, w_gate_t, w_v_t, conv_w, seq_len):
    t = xf.shape[0]
    tiles_per_seq = seq_len // TM_IN
    return pl.pallas_call(
        functools.partial(_inproj_kernel, tiles_per_seq),
        grid=(t // TM_IN,),
        in_specs=[
            pl.BlockSpec((TM_IN, D_MODEL), lambda i: (i, 0)),
            pl.BlockSpec((1, D_MODEL), lambda i: (0, 0)),
            pl.BlockSpec((D_MODEL, N_MAIN), lambda i: (0, 0)),
            pl.BlockSpec((N_GATE, D_MODEL), lambda i: (0, 0)),
            pl.BlockSpec((DIFF_WIDTH, D_MODEL), lambda i: (0, 0)),
            pl.BlockSpec((CONV_K, N_CONV), lambda i: (0, 0)),
        ],
        out_specs=[
            pl.BlockSpec((TM_IN, N_MAIN), lambda i: (i, 0)),
            pl.BlockSpec((N_GATE, TM_IN), lambda i: (0, i)),
            pl.BlockSpec((DIFF_WIDTH, TM_IN), lambda i: (0, i)),
        ],
        out_shape=[
            jax.ShapeDtypeStruct((t, N_MAIN), BF16),
            jax.ShapeDtypeStruct((N_GATE, t), F32),
            jax.ShapeDtypeStruct((DIFF_WIDTH, t), BF16),
        ],
        scratch_shapes=[pltpu.VMEM((TAIL, N_CONV), F32)],
        compiler_params=pltpu.CompilerParams(
            dimension_semantics=("arbitrary",), vmem_limit_bytes=VMEM_LIMIT),
        name="inproj",
    )(xf, norm_w, w_main, w_gate_t, w_v_t, conv_w)


def _gdn_kernel(act_ref, gate_t_ref, alog_s_ref, dtb_s_ref, gnw_ref, o_ref, mask_ref, bd_ref,
                row_ref, col_ref):
    seq_len = act_ref.shape[0]
    n_chunks = SC // CHUNK
    heads = range(GDN_HEADS)
    i_cat = lax.broadcasted_iota(jnp.int32, (CHUNK, SC), 0)
    j_cat = lax.broadcasted_iota(jnp.int32, (CHUNK, SC), 1) & (CHUNK - 1)
    mask_ref[0] = jnp.where(i_cat >= j_cat, 0.0, MASK_VALUE)
    mask_ref[1] = jnp.where(i_cat > j_cat, -1.0, 0.0)
    mask_ref[2] = jnp.where(i_cat == j_cat, 1.0, 0.0)
    r = lax.broadcasted_iota(jnp.int32, (SC, SC), 0)
    c = lax.broadcasted_iota(jnp.int32, (SC, SC), 1)
    bd_ref[...] = jnp.where((r & -CHUNK) == (c & -CHUNK), 1.0, 0.0).astype(BF16)
    lane_chunk = lax.broadcasted_iota(jnp.int32, (1, SC), 1) & -CHUNK

    scale = HEAD_DIM ** -0.5

    lane_in_chunk = lax.broadcasted_iota(jnp.int32, (N_GATE, seq_len), 1) & (CHUNK - 1)
    gate_t = gate_t_ref[...]
    beta = _sigmoid(gate_t)
    log_beta = -_softplus(-gate_t)
    g_step = -jnp.exp(alog_s_ref[...]) * _softplus(gate_t + dtb_s_ref[...])
    gc = g_step
    g_after = jnp.zeros_like(g_step)
    tail = g_step
    step = 1
    while step < CHUNK:
        gc = gc + jnp.where(lane_in_chunk >= step, pltpu.roll(gc, step, axis=1), 0.0)
        ahead = jnp.where(lane_in_chunk + step < CHUNK, pltpu.roll(tail, seq_len - step, axis=1), 0.0)
        g_after = g_after + ahead
        tail = tail + ahead
        step *= 2
    e_gc = jnp.exp(gc)
    is_beta_row = lax.broadcasted_iota(jnp.int32, (N_GATE, seq_len), 0) < GDN_HEADS
    swap = pltpu.roll(e_gc, GDN_HEADS, axis=0)
    row_ref[0] = gc
    row_ref[1] = e_gc
    col_ref[...] = jnp.concatenate([
        jnp.where(is_beta_row, beta, gc),
        jnp.where(is_beta_row, log_beta, e_gc),
        jnp.where(is_beta_row, beta * swap, jnp.exp(g_after))], axis=0).T

    def cat(x):
        out = x[(n_chunks - 1) * CHUNK:, :]
        for ci in range(n_chunks - 2, -1, -1):
            out = jnp.where(lane_chunk == ci * CHUNK, x[ci * CHUNK:(ci + 1) * CHUNK, :], out)
        return out

    def block_diag(x_cat):
        xb = x_cat.astype(BF16)
        zero = jnp.zeros((CHUNK, LANES), BF16)
        row_blocks = []
        for ci in range(n_chunks):
            g = ci * CHUNK // LANES
            grp = slice(g * LANES, (g + 1) * LANES)
            part = xb[:, grp] * bd_ref[ci * CHUNK:(ci + 1) * CHUNK, grp]
            row_blocks.append(jnp.concatenate(
                [part if j == g else zero for j in range(SC // LANES)], axis=1))
        return jnp.concatenate(row_blocks, axis=0)

    def head_cols(col, h):
        return slice((col + h) * HEAD_DIM, (col + h + 1) * HEAD_DIM)

    def prepare(sc):
        rows = slice(sc * SC, (sc + 1) * SC)
        col = lambda j: col_ref[rows, j:j + 1]
        causal, neg_strict, eye = mask_ref[0], mask_ref[1], mask_ref[2]

        n_pow, a_cat, rhs, q_dec, k_dec, decay_last = [], [], [], [], [], []
        for h in heads:
            q = act_ref[rows, head_cols(COL_GQ, h)].astype(F32)
            k = act_ref[rows, head_cols(COL_GK, h)].astype(F32)
            v = act_ref[rows, head_cols(COL_GV, h)].astype(F32)
            q = q * (lax.rsqrt(jnp.sum(q * q, axis=-1, keepdims=True) + 1e-6) * scale)
            k = k * lax.rsqrt(jnp.sum(k * k, axis=-1, keepdims=True) + 1e-6)
            hb, hg = h, GDN_HEADS + h
            kb = k.astype(BF16)
            gram = _dot_nt(jnp.concatenate([kb, q.astype(BF16)], axis=0), kb)
            e_cat = cat(col(hg)) - row_ref[0, hg:hg + 1, rows] + causal
            n_pow.append(cat(gram[:SC]) * jnp.exp(e_cat + cat(col(N_GATE + hb))) * neg_strict)
            a_cat.append(cat(gram[SC:]) * jnp.exp(e_cat))
            rhs.append(jnp.concatenate([col(hb) * v, col(2 * N_GATE + hb) * k], axis=1).astype(BF16))
            q_dec.append(q * col(N_GATE + hg))
            kd = (k * col(2 * N_GATE + hg)).astype(BF16)
            k_dec.append([kd[ci * CHUNK:(ci + 1) * CHUNK, :] for ci in range(n_chunks)])
            decay_last.append([row_ref[1, hg:hg + 1, sc * SC + (ci + 1) * CHUNK - 1:sc * SC + (ci + 1) * CHUNK]
                               for ci in range(n_chunks)])

        t_cat = [eye + n_pow[h] for h in heads]
        n_pow = [_dot(n_pow[h].astype(BF16), block_diag(n_pow[h])) for h in heads]
        span = 2
        while span < CHUNK // 2:
            both = [_dot(jnp.concatenate([t_cat[h], n_pow[h]], axis=0).astype(BF16), block_diag(n_pow[h]))
                    for h in heads]
            t_cat = [t_cat[h] + both[h][:CHUNK] for h in heads]
            n_pow = [both[h][CHUNK:] for h in heads]
            span *= 2
        t_cat = [t_cat[h] + _dot(t_cat[h].astype(BF16), block_diag(n_pow[h])) for h in heads]

        uw = [_dot(block_diag(t_cat[h]), rhs[h]) for h in heads]
        kt = [[_dot_tn(k_dec[h][ci], uw[h][ci * CHUNK:(ci + 1) * CHUNK, :].astype(BF16))
               for ci in range(n_chunks)] for h in heads]
        return uw, kt, q_dec, a_cat, decay_last

    def finish(sc, prep, state):
        rows = slice(sc * SC, (sc + 1) * SC)
        uw, kt, q_dec, a_cat, decay_last = prep
        state = list(state)
        state_in = [[None] * n_chunks for _ in heads]
        for ci in range(n_chunks):
            for h in heads:
                sb = state[h].astype(BF16)
                state_in[h][ci] = sb
                state[h] = (state[h] * decay_last[h][ci]
                            - _dot(kt[h][ci][:, HEAD_DIM:].astype(BF16), sb) + kt[h][ci][:, :HEAD_DIM])
        for h in heads:
            inter = []
            for ci in range(n_chunks):
                cr = slice(ci * CHUNK, (ci + 1) * CHUNK)
                lhs = jnp.concatenate([uw[h][cr, HEAD_DIM:], q_dec[h][cr, :]], axis=0).astype(BF16)
                inter.append(_dot(lhs, state_in[h][ci]))
            v_new = jnp.concatenate([uw[h][ci * CHUNK:(ci + 1) * CHUNK, :HEAD_DIM] - inter[ci][:CHUNK]
                                     for ci in range(n_chunks)], axis=0).astype(BF16)
            o = (jnp.concatenate([inter[ci][CHUNK:] for ci in range(n_chunks)], axis=0)
                 + _dot(block_diag(a_cat[h]), v_new))
            o = o * lax.rsqrt(jnp.mean(o * o, axis=-1, keepdims=True) + NORM_EPS)
            z = act_ref[rows, head_cols(COL_GZ, h)].astype(F32)
            o_ref[rows, head_cols(0, h)] = (o * (_silu(z) * gnw_ref[...])).astype(BF16)
        return state

    n_steps = seq_len // SC
    state = [jnp.zeros((HEAD_DIM, HEAD_DIM), F32) for _ in heads]
    prep = prepare(0)
    for sc in range(n_steps):
        nxt = prepare(sc + 1) if sc + 1 < n_steps else None
        state = finish(sc, prep, state)
        prep = nxt


def _gdn(act, gate_t, alog_s, dtb_s, gdn_norm_w, batch, seq_len):
    t = act.shape[0]
    small = lambda shape: pl.BlockSpec(shape, lambda b: (0, 0))
    return pl.pallas_call(
        _gdn_kernel,
        grid=(batch,),
        in_specs=[
            pl.BlockSpec((seq_len, 4 * GDN_WIDTH), lambda b: (b, 0)),
            pl.BlockSpec((N_GATE, seq_len), lambda b: (0, b)),
            small((N_GATE, 1)), small((N_GATE, 1)),
            small((1, HEAD_DIM)),
        ],
        out_specs=pl.BlockSpec((seq_len, GDN_WIDTH), lambda b: (b, 0)),
        out_shape=jax.ShapeDtypeStruct((t, GDN_WIDTH), BF16),
        scratch_shapes=[
            pltpu.VMEM((3, CHUNK, SC), F32),
            pltpu.VMEM((SC, SC), BF16),
            pltpu.VMEM((2, N_GATE, seq_len), F32),
            pltpu.VMEM((seq_len, 3 * N_GATE), F32),
        ],
        compiler_params=pltpu.CompilerParams(
            dimension_semantics=("arbitrary",), vmem_limit_bytes=VMEM_LIMIT),
        name="gdn",
    )(act, gate_t, alog_s, dtb_s, gdn_norm_w)


def _split3(x):
    hi = x.astype(BF16).astype(F32)
    r1 = x - hi
    mid = r1.astype(BF16).astype(F32)
    return hi, mid, r1 - mid


def _attn_kernel(lam_init, q_ref, k_ref, vt_ref, z_ref, lq1_ref, lk1_ref, lq2_ref, lk2_ref,
                 nw_ref, o_ref, bias_ref, kext_ref, vtx_ref, acc_ref, m_ref):
    h = pl.program_id(0)
    b = pl.program_id(1)
    seq_len = k_ref.shape[0]
    lane = lax.broadcasted_iota(jnp.int32, (1, HEAD_DIM), 1)
    n_bias = 6
    log2e = math.log2(math.e)
    assert BQ == 2 * BK

    @pl.when(b == 0)
    def _():
        slope = lax.shift_left(jnp.int32(1), 2 * (DIFF_HEADS - 1 - h)).astype(F32) * (log2e / 256.0)
        pos = lax.broadcasted_iota(jnp.int32, (seq_len, 1), 0)
        in_block = (pos & (BK - 1)).astype(F32) * slope
        block_off = (pos & -BK).astype(F32) * slope
        terms = _split3(in_block) + _split3(block_off)
        tile = jnp.zeros((seq_len, HEAD_DIM), F32)
        for j, term in enumerate(terms):
            tile = jnp.where((lane == j) | (lane == DIFF_QK_DIM + j), term, tile)
        bias_ref[...] = tile
        ones_row = lax.broadcasted_iota(jnp.int32, (V_PAD, seq_len), 0) == 0
        vtx_ref[HEAD_DIM:, :] = jnp.where(ones_row, 1.0, 0.0).astype(BF16)

    k = k_ref[...].astype(F32)
    kext_ref[0] = jnp.where(lane < DIFF_QK_DIM, k, bias_ref[...]).astype(BF16)
    kext_ref[1] = jnp.where(lane >= DIFF_QK_DIM, k, bias_ref[...]).astype(BF16)
    vtx_ref[0:HEAD_DIM, :] = vt_ref[...]

    lam = (jnp.exp(jnp.sum(lq1_ref[...] * lk1_ref[...], axis=-1, keepdims=True))
           - jnp.exp(jnp.sum(lq2_ref[...] * lk2_ref[...], axis=-1, keepdims=True)) + lam_init)

    def scores(q_ext, kb, masked, lo):
        rows = slice(kb * BK, (kb + 1) * BK)
        out = []
        for c in range(2):
            s = _dot_nt(kext_ref[c, rows, :], q_ext[c][lo:, :])
            if masked:
                krel = lax.broadcasted_iota(jnp.int32, (BK, BQ - lo), 0)
                qrel = lax.broadcasted_iota(jnp.int32, (BK, BQ - lo), 1)
                s = jnp.where(krel <= qrel, s, MASK_VALUE)
            out.append((s, jnp.max(s, axis=0, keepdims=True)))
        return out

    def accumulate(par, kb, block, lo):
        vt = vtx_ref[:, kb * BK:(kb + 1) * BK]
        for c in range(2):
            s, s_max = block[c]
            m_old = m_ref[par, c, :, lo:]
            m_new = jnp.maximum(m_old, s_max)
            alpha = jnp.exp2(m_old - m_new)
            p = jnp.exp2(s - m_new)
            acc_ref[par, c, :, lo:] = alpha * acc_ref[par, c, :, lo:] + _dot(vt, p.astype(BF16))
            m_ref[par, c, :, lo:] = m_new

    for qi in range(seq_len // BQ):
        par = qi % 2
        qrows = slice(qi * BQ, (qi + 1) * BQ)
        q = q_ref[qrows, :].astype(F32) * (DIFF_QK_DIM ** -0.5 * log2e)
        q_ext = (
            jnp.where(lane < DIFF_QK_DIM, q,
                      jnp.where(lane < DIFF_QK_DIM + n_bias, 1.0, 0.0)).astype(BF16),
            jnp.where(lane >= DIFF_QK_DIM, q, jnp.where(lane < n_bias, 1.0, 0.0)).astype(BF16),
        )
        m_ref[par] = jnp.full(m_ref.shape[1:], MASK_VALUE, F32)
        acc_ref[par] = jnp.zeros(acc_ref.shape[1:], F32)
        n_blocks = 2 * qi + 2
        lo_of = lambda kb: BK if kb == n_blocks - 1 else 0
        cur = scores(q_ext, 0, 0 >= 2 * qi, lo_of(0))
        for kb in range(n_blocks):
            nxt = scores(q_ext, kb + 1, kb + 1 >= 2 * qi, lo_of(kb + 1)) if kb + 1 < n_blocks else None
            accumulate(par, kb, cur, lo_of(kb))
            cur = nxt

        num = (acc_ref[par, 0, 0:HEAD_DIM, :], acc_ref[par, 1, 0:HEAD_DIM, :])
        den = (acc_ref[par, 0, HEAD_DIM:HEAD_DIM + 1, :], acc_ref[par, 1, HEAD_DIM:HEAD_DIM + 1, :])
        o_t = num[0] / den[0] - lam * (num[1] / den[1])
        o = o_t.T
        o = o * lax.rsqrt(jnp.mean(o * o, axis=-1, keepdims=True) + NORM_EPS)
        o = o * nw_ref[...] * (1.0 - lam_init)
        o_ref[qrows, :] = (o * _silu(z_ref[qrows, :].astype(F32))).astype(BF16)


def _attn(act, v_t, lq1, lk1, lq2, lk2, diff_norm_w, lam_init, batch, seq_len):
    t = act.shape[0]
    small = lambda shape: pl.BlockSpec(shape, lambda h, b: (0, 0))
    return pl.pallas_call(
        functools.partial(_attn_kernel, lam_init),
        grid=(DIFF_HEADS, batch),
        in_specs=[
            pl.BlockSpec((seq_len, HEAD_DIM), lambda h, b: (b, COL_DQ + h)),
            pl.BlockSpec((seq_len, HEAD_DIM), lambda h, b: (b, COL_DK + h)),
            pl.BlockSpec((HEAD_DIM, seq_len), lambda h, b: (h, b)),
            pl.BlockSpec((seq_len, HEAD_DIM), lambda h, b: (b, COL_DZ + h)),
            small((1, DIFF_QK_DIM)), small((1, DIFF_QK_DIM)),
            small((1, DIFF_QK_DIM)), small((1, DIFF_QK_DIM)),
            small((1, HEAD_DIM)),
        ],
        out_specs=pl.BlockSpec((seq_len, HEAD_DIM), lambda h, b: (b, h)),
        out_shape=jax.ShapeDtypeStruct((t, DIFF_WIDTH), BF16),
        scratch_shapes=[
            pltpu.VMEM((seq_len, HEAD_DIM), F32),
            pltpu.VMEM((2, seq_len, HEAD_DIM), BF16),
            pltpu.VMEM((HEAD_DIM + V_PAD, seq_len), BF16),
            pltpu.VMEM((2, 2, HEAD_DIM + V_PAD, BQ), F32),
            pltpu.VMEM((2, 2, 1, BQ), F32),
        ],
        compiler_params=pltpu.CompilerParams(
            dimension_semantics=("arbitrary", "arbitrary"), vmem_limit_bytes=VMEM_LIMIT),
        name="diffattn",
    )(act, act, v_t, act, lq1, lk1, lq2, lk2, diff_norm_w)


def _outproj_kernel(oa_ref, ob_ref, x_ref, w_ref, fw_ref, out_ref):
    mix = _dot(oa_ref[...], w_ref[0:GDN_WIDTH, :]) + _dot(ob_ref[...], w_ref[GDN_WIDTH:, :])
    y = x_ref[...] + mix
    ms = jnp.mean(y * y, axis=-1, keepdims=True)
    out_ref[...] = y * lax.rsqrt(ms + NORM_EPS) * fw_ref[...]


def _outproj(o_a, o_b, xf, w_out, final_norm_w):
    t = xf.shape[0]
    return pl.pallas_call(
        _outproj_kernel,
        grid=(t // TM_OUT,),
        in_specs=[
            pl.BlockSpec((TM_OUT, GDN_WIDTH), lambda i: (i, 0)),
            pl.BlockSpec((TM_OUT, DIFF_WIDTH), lambda i: (i, 0)),
            pl.BlockSpec((TM_OUT, D_MODEL), lambda i: (i, 0)),
            pl.BlockSpec((D_MODEL, D_MODEL), lambda i: (0, 0)),
            pl.BlockSpec((1, D_MODEL), lambda i: (0, 0)),
        ],
        out_specs=pl.BlockSpec((TM_OUT, D_MODEL), lambda i: (i, 0)),
        out_shape=jax.ShapeDtypeStruct((t, D_MODEL), F32),
        compiler_params=pltpu.CompilerParams(
            dimension_semantics=("arbitrary",), vmem_limit_bytes=VMEM_LIMIT),
        name="outproj",
    )(o_a, o_b, xf, w_out, final_norm_w)


def kernel(x, norm_w, w_in, conv_w, a_log, dt_bias, gdn_norm_w, lambda_q1, lambda_k1,
           lambda_q2, lambda_k2, diff_norm_w, w_out, final_norm_w):
    batch, seq_len, d_model = x.shape
    depth = norm_w.shape[0]
    assert depth == 1 and d_model == D_MODEL
    assert seq_len % TM_IN == 0 and seq_len % SC == 0 and seq_len % BQ == 0 and BQ % BK == 0
    n_wide_a = 4 * GDN_WIDTH
    xf = x.reshape(batch * seq_len, d_model)

    w = w_in[0]
    c_dq = n_wide_a + N_GATE
    c_dv = c_dq + 2 * DIFF_WIDTH
    c_dz = c_dv + DIFF_WIDTH
    w_main = jnp.concatenate([w[:, :n_wide_a], w[:, c_dq:c_dv], w[:, c_dz:]], axis=1).astype(BF16)
    w_v_t = w[:, c_dv:c_dz].T.astype(BF16)
    w_gate_t = w[:, n_wide_a:n_wide_a + N_GATE].T.astype(BF16)

    act, gate_t, v_t = _inproj(xf, norm_w[0][None, :], w_main, w_gate_t, w_v_t, conv_w[0], seq_len)

    pad_s = lambda vec: jnp.pad(vec, (GDN_HEADS, 0))[:, None]
    o_a = _gdn(act, gate_t, pad_s(a_log[0]), pad_s(dt_bias[0]), gdn_norm_w[0][None, :],
               batch, seq_len)

    lam_init = 0.8 - 0.6 * math.exp(-0.3 * 0)
    o_b = _attn(act, v_t, lambda_q1[0][None, :], lambda_k1[0][None, :], lambda_q2[0][None, :],
                lambda_k2[0][None, :], diff_norm_w[0][None, :], lam_init, batch, seq_len)

    out = _outproj(o_a, o_b, xf, w_out[0].astype(BF16), final_norm_w[None, :])
    return out.reshape(batch, seq_len, d_model)
```

```python
import functools
import math

import jax
import jax.numpy as jnp
from jax import lax
from jax.experimental import pallas as pl
from jax.experimental.pallas import tpu as pltpu

F32 = jnp.float32
BF16 = jnp.bfloat16

D_MODEL = 1024
GDN_HEADS = 4
HEAD_DIM = 128
GDN_WIDTH = GDN_HEADS * HEAD_DIM
DIFF_HEADS = 4
DIFF_QK_DIM = 64
DIFF_WIDTH = DIFF_HEADS * HEAD_DIM
CONV_K = 4
NORM_EPS = 1e-6
N_GATE = 2 * GDN_HEADS
N_MAIN = 4 * GDN_WIDTH + 3 * DIFF_WIDTH
N_CONV = 3 * GDN_WIDTH
LANES = 128
MASK_VALUE = -1e30

COL_GQ, COL_GK, COL_GV, COL_GZ = 0, 4, 8, 12
COL_DQ, COL_DK, COL_DZ = 16, 20, 24

TM_IN = 1024
NC_IN = 256
TAIL = 16
TM_OUT = 1024
SC = 256
CHUNK = 64
GDN_GROUP = 2
BQ = 512
BK = 256
V_PAD = 16
VMEM_LIMIT = 48 * 1024 * 1024


def _sigmoid(x):
    return 1.0 / (1.0 + jnp.exp(-x))


def _silu(x):
    half = 0.5 * x
    return half + half * jnp.tanh(half)


def _softplus(x):
    return jnp.maximum(x, 0.0) + jnp.log1p(jnp.exp(-jnp.abs(x)))


def _dot(a, b):
    return jnp.dot(a, b, preferred_element_type=F32)


def _dot_nt(a, b):
    return lax.dot_general(a, b, (((1,), (1,)), ((), ())), preferred_element_type=F32)


def _dot_tn(a, b):
    return lax.dot_general(a, b, (((0,), (0,)), ((), ())), preferred_element_type=F32)


def _inproj_kernel(tiles_per_seq, x_ref, nw_ref, w_ref, wgt_ref, wvt_ref, cw_ref,
                   main_ref, gate_t_ref, vt_ref, tail_ref):
    i = pl.program_id(0)
    x = x_ref[...]
    ms = jnp.mean(x * x, axis=-1, keepdims=True)
    h = (x * lax.rsqrt(ms + NORM_EPS) * nw_ref[...]).astype(BF16)

    gate_t_ref[...] = _dot_nt(wgt_ref[...], h)
    vt_ref[...] = _dot_nt(wvt_ref[...], h).astype(BF16)

    @pl.when(i % tiles_per_seq == 0)
    def _():
        tail_ref[...] = jnp.zeros_like(tail_ref)

    tm = x.shape[0]
    row = lax.broadcasted_iota(jnp.int32, (TAIL, LANES), 0)

    def conv_silu(acc, n0):
        cols = slice(n0, n0 + LANES)
        last = acc[tm - TAIL:, :]
        delta = tail_ref[:, cols] - last
        tail_ref[:, cols] = last
        y = acc * cw_ref[CONV_K - 1:CONV_K, cols]
        fix = jnp.zeros((TAIL, LANES), F32)
        for s in range(1, CONV_K):
            wk = cw_ref[CONV_K - 1 - s:CONV_K - s, cols]
            y = y + pltpu.roll(acc, s, axis=0) * wk
            fix = fix + jnp.where(row < s, pltpu.roll(delta, s, axis=0), 0.0) * wk
        main_ref[:, cols] = _silu(y).astype(BF16)
        main_ref[0:TAIL, cols] = _silu(y[:TAIL, :] + fix).astype(BF16)

    plain = [slice(n0, n0 + NC_IN) for n0 in range(N_CONV, N_MAIN, NC_IN)]
    pending = []
    for n0 in range(0, N_CONV, NC_IN):
        acc = _dot(h, w_ref[:, n0:n0 + NC_IN])
        if pending:
            conv_silu(*pending.pop(0))
        if plain:
            pc = plain.pop(0)
            main_ref[:, pc] = _dot(h, w_ref[:, pc]).astype(BF16)
        if pending:
            conv_silu(*pending.pop(0))
        pending = [(acc[:, j:j + LANES], n0 + j) for j in range(0, NC_IN, LANES)]
    for piece in pending:
        if plain:
            pc = plain.pop(0)
            main_ref[:, pc] = _dot(h, w_ref[:, pc]).astype(BF16)
        conv_silu(*piece)
    for pc in plain:
        main_ref[:, pc] = _dot(h, w_ref[:, pc]).astype(BF16)


def _inproj(xf, norm_w, w_main, w_gate_t, w_v_t, conv_w, seq_len):
    t = xf.shape[0]
    tiles_per_seq = seq_len // TM_IN
    return pl.pallas_call(
        functools.partial(_inproj_kernel, tiles_per_seq),
        grid=(t // TM_IN,),
        in_specs=[
            pl.BlockSpec((TM_IN, D_MODEL), lambda i: (i, 0)),
            pl.BlockSpec((1, D_MODEL), lambda i: (0, 0)),
            pl.BlockSpec((D_MODEL, N_MAIN), lambda i: (0, 0)),
            pl.BlockSpec((N_GATE, D_MODEL), lambda i: (0, 0)),
            pl.BlockSpec((DIFF_WIDTH, D_MODEL), lambda i: (0, 0)),
            pl.BlockSpec((CONV_K, N_CONV), lambda i: (0, 0)),
        ],
        out_specs=[
            pl.BlockSpec((TM_IN, N_MAIN), lambda i: (i, 0)),
            pl.BlockSpec((N_GATE, TM_IN), lambda i: (0, i)),
            pl.BlockSpec((DIFF_WIDTH, TM_IN), lambda i: (0, i)),
        ],
        out_shape=[
            jax.ShapeDtypeStruct((t, N_MAIN), BF16),
            jax.ShapeDtypeStruct((N_GATE, t), F32),
            jax.ShapeDtypeStruct((DIFF_WIDTH, t), BF16),
        ],
        scratch_shapes=[pltpu.VMEM((TAIL, N_CONV), F32)],
        compiler_params=pltpu.CompilerParams(
            dimension_semantics=("arbitrary",), vmem_limit_bytes=VMEM_LIMIT),
        name="inproj",
    )(xf, norm_w, w_main, w_gate_t, w_v_t, conv_w)


def _gdn_kernel(act_ref, gate_t_ref, alog_s_ref, dtb_s_ref, gnw_ref, o_ref, mask_ref, bd_ref,
                row_ref, col_ref):
    seq_len = act_ref.shape[0]
    n_chunks = SC // CHUNK
    heads = range(GDN_HEADS)
    i_cat = lax.broadcasted_iota(jnp.int32, (CHUNK, SC), 0)
    j_cat = lax.broadcasted_iota(jnp.int32, (CHUNK, SC), 1) & (CHUNK - 1)
    mask_ref[0] = jnp.where(i_cat >= j_cat, 0.0, MASK_VALUE)
    mask_ref[1] = jnp.where(i_cat > j_cat, -1.0, 0.0)
    mask_ref[2] = jnp.where(i_cat == j_cat, 1.0, 0.0)
    r = lax.broadcasted_iota(jnp.int32, (SC, SC), 0)
    c = lax.broadcasted_iota(jnp.int32, (SC, SC), 1)
    bd_ref[...] = jnp.where((r & -CHUNK) == (c & -CHUNK), 1.0, 0.0).astype(BF16)
    lane_chunk = lax.broadcasted_iota(jnp.int32, (1, SC), 1) & -CHUNK

    scale = HEAD_DIM ** -0.5

    lane_in_chunk = lax.broadcasted_iota(jnp.int32, (N_GATE, seq_len), 1) & (CHUNK - 1)
    gate_t = gate_t_ref[...]
    beta = _sigmoid(gate_t)
    log_beta = -_softplus(-gate_t)
    g_step = -jnp.exp(alog_s_ref[...]) * _softplus(gate_t + dtb_s_ref[...])
    gc = g_step
    g_after = jnp.zeros_like(g_step)
    tail = g_step
    step = 1
    while step < CHUNK:
        gc = gc + jnp.where(lane_in_chunk >= step, pltpu.roll(gc, step, axis=1), 0.0)
        ahead = jnp.where(lane_in_chunk + step < CHUNK, pltpu.roll(tail, seq_len - step, axis=1), 0.0)
        g_after = g_after + ahead
        tail = tail + ahead
        step *= 2
    e_gc = jnp.exp(gc)
    is_beta_row = lax.broadcasted_iota(jnp.int32, (N_GATE, seq_len), 0) < GDN_HEADS
    swap = pltpu.roll(e_gc, GDN_HEADS, axis=0)
    row_ref[0] = gc
    row_ref[1] = e_gc
    col_ref[...] = jnp.concatenate([
        jnp.where(is_beta_row, beta, gc),
        jnp.where(is_beta_row, log_beta, e_gc),
        jnp.where(is_beta_row, beta * swap, jnp.exp(g_after))], axis=0).T

    def cat(x):
        out = x[(n_chunks - 1) * CHUNK:, :]
        for ci in range(n_chunks - 2, -1, -1):
            out = jnp.where(lane_chunk == ci * CHUNK, x[ci * CHUNK:(ci + 1) * CHUNK, :], out)
        return out

    def block_diag(x_cat):
        xb = x_cat.astype(BF16)
        zero = jnp.zeros((CHUNK, LANES), BF16)
        row_blocks = []
        for ci in range(n_chunks):
            g = ci * CHUNK // LANES
            grp = slice(g * LANES, (g + 1) * LANES)
            part = xb[:, grp] * bd_ref[ci * CHUNK:(ci + 1) * CHUNK, grp]
            row_blocks.append(jnp.concatenate(
                [part if j == g else zero for j in range(SC // LANES)], axis=1))
        return jnp.concatenate(row_blocks, axis=0)

    def head_cols(col, h):
        return slice((col + h) * HEAD_DIM, (col + h + 1) * HEAD_DIM)

    def prepare(steps):
        causal, neg_strict, eye = mask_ref[0], mask_ref[1], mask_ref[2]
        units = [(sc, h) for sc in steps for h in heads]
        n_pow, a_cat, rhs, q_dec, k_dec, decay_last = [], [], [], [], [], []
        for sc, h in units:
            rows = slice(sc * SC, (sc + 1) * SC)
            col = lambda j: col_ref[rows, j:j + 1]
            q = act_ref[rows, head_cols(COL_GQ, h)].astype(F32)
            k = act_ref[rows, head_cols(COL_GK, h)].astype(F32)
            v = act_ref[rows, head_cols(COL_GV, h)].astype(F32)
            q = q * (lax.rsqrt(jnp.sum(q * q, axis=-1, keepdims=True) + 1e-6) * scale)
            k = k * lax.rsqrt(jnp.sum(k * k, axis=-1, keepdims=True) + 1e-6)
            hb, hg = h, GDN_HEADS + h
            kb = k.astype(BF16)
            gram = _dot_nt(jnp.concatenate([kb, q.astype(BF16)], axis=0), kb)
            e_cat = cat(col(hg)) - row_ref[0, hg:hg + 1, rows] + causal
            n_pow.append(cat(gram[:SC]) * jnp.exp(e_cat + cat(col(N_GATE + hb))) * neg_strict)
            a_cat.append(cat(gram[SC:]) * jnp.exp(e_cat))
            rhs.append(jnp.concatenate([col(hb) * v, col(2 * N_GATE + hb) * k], axis=1).astype(BF16))
            q_dec.append(q * col(N_GATE + hg))
            kd = (k * col(2 * N_GATE + hg)).astype(BF16)
            k_dec.append([kd[ci * CHUNK:(ci + 1) * CHUNK, :] for ci in range(n_chunks)])
            decay_last.append([row_ref[1, hg:hg + 1, sc * SC + (ci + 1) * CHUNK - 1:sc * SC + (ci + 1) * CHUNK]
                               for ci in range(n_chunks)])

        every = range(len(units))
        t_cat = [eye + n_pow[u] for u in every]
        n_pow = [_dot(n_pow[u].astype(BF16), block_diag(n_pow[u])) for u in every]
        span = 2
        while span < CHUNK // 2:
            both = [_dot(jnp.concatenate([t_cat[u], n_pow[u]], axis=0).astype(BF16), block_diag(n_pow[u]))
                    for u in every]
            t_cat = [t_cat[u] + both[u][:CHUNK] for u in every]
            n_pow = [both[u][CHUNK:] for u in every]
            span *= 2
        t_cat = [t_cat[u] + _dot(t_cat[u].astype(BF16), block_diag(n_pow[u])) for u in every]

        uw = [_dot(block_diag(t_cat[u]), rhs[u]) for u in every]
        kt = [[_dot_tn(k_dec[u][ci], uw[u][ci * CHUNK:(ci + 1) * CHUNK, :].astype(BF16))
               for ci in range(n_chunks)] for u in every]
        per_step = lambda xs, i: xs[i * GDN_HEADS:(i + 1) * GDN_HEADS]
        return [tuple(per_step(xs, i) for xs in (uw, kt, q_dec, a_cat, decay_last))
                for i in range(len(steps))]

    def finish(sc, prep, state):
        rows = slice(sc * SC, (sc + 1) * SC)
        uw, kt, q_dec, a_cat, decay_last = prep
        state = list(state)
        state_in = [[None] * n_chunks for _ in heads]
        for ci in range(n_chunks):
            for h in heads:
                sb = state[h].astype(BF16)
                state_in[h][ci] = sb
                state[h] = (state[h] * decay_last[h][ci]
                            - _dot(kt[h][ci][:, HEAD_DIM:].astype(BF16), sb) + kt[h][ci][:, :HEAD_DIM])
        for h in heads:
            inter = []
            for ci in range(n_chunks):
                cr = slice(ci * CHUNK, (ci + 1) * CHUNK)
                lhs = jnp.concatenate([uw[h][cr, HEAD_DIM:], q_dec[h][cr, :]], axis=0).astype(BF16)
                inter.append(_dot(lhs, state_in[h][ci]))
            v_new = jnp.concatenate([uw[h][ci * CHUNK:(ci + 1) * CHUNK, :HEAD_DIM] - inter[ci][:CHUNK]
                                     for ci in range(n_chunks)], axis=0).astype(BF16)
            o = (jnp.concatenate([inter[ci][CHUNK:] for ci in range(n_chunks)], axis=0)
                 + _dot(block_diag(a_cat[h]), v_new))
            o = o * lax.rsqrt(jnp.mean(o * o, axis=-1, keepdims=True) + NORM_EPS)
            z = act_ref[rows, head_cols(COL_GZ, h)].astype(F32)
            o_ref[rows, head_cols(0, h)] = (o * (_silu(z) * gnw_ref[...])).astype(BF16)
        return state

    n_groups = seq_len // (SC * GDN_GROUP)
    group = lambda g: list(range(g * GDN_GROUP, (g + 1) * GDN_GROUP))
    state = [jnp.zeros((HEAD_DIM, HEAD_DIM), F32) for _ in heads]
    preps = prepare(group(0))
    for g in range(n_groups):
        nxt = prepare(group(g + 1)) if g + 1 < n_groups else None
        for sc, prep in zip(group(g), preps):
            state = finish(sc, prep, state)
        preps = nxt


def _gdn(act, gate_t, alog_s, dtb_s, gdn_norm_w, batch, seq_len):
    t = act.shape[0]
    small = lambda shape: pl.BlockSpec(shape, lambda b: (0, 0))
    return pl.pallas_call(
        _gdn_kernel,
        grid=(batch,),
        in_specs=[
            pl.BlockSpec((seq_len, 4 * GDN_WIDTH), lambda b: (b, 0)),
            pl.BlockSpec((N_GATE, seq_len), lambda b: (0, b)),
            small((N_GATE, 1)), small((N_GATE, 1)),
            small((1, HEAD_DIM)),
        ],
        out_specs=pl.BlockSpec((seq_len, GDN_WIDTH), lambda b: (b, 0)),
        out_shape=jax.ShapeDtypeStruct((t, GDN_WIDTH), BF16),
        scratch_shapes=[
            pltpu.VMEM((3, CHUNK, SC), F32),
            pltpu.VMEM((SC, SC), BF16),
            pltpu.VMEM((2, N_GATE, seq_len), F32),
            pltpu.VMEM((seq_len, 3 * N_GATE), F32),
        ],
        compiler_params=pltpu.CompilerParams(
            dimension_semantics=("arbitrary",), vmem_limit_bytes=VMEM_LIMIT),
        name="gdn",
    )(act, gate_t, alog_s, dtb_s, gdn_norm_w)


def _split3(x):
    hi = x.astype(BF16).astype(F32)
    r1 = x - hi
    mid = r1.astype(BF16).astype(F32)
    return hi, mid, r1 - mid


def _attn_kernel(lam_init, q_ref, k_ref, vt_ref, z_ref, lq1_ref, lk1_ref, lq2_ref, lk2_ref,
                 nw_ref, o_ref, bias_ref, kext_ref, vtx_ref, acc_ref, m_ref):
    h = pl.program_id(0)
    b = pl.program_id(1)
    seq_len = k_ref.shape[0]
    lane = lax.broadcasted_iota(jnp.int32, (1, HEAD_DIM), 1)
    n_bias = 6
    log2e = math.log2(math.e)
    assert BQ == 2 * BK

    @pl.when(b == 0)
    def _():
        slope = lax.shift_left(jnp.int32(1), 2 * (DIFF_HEADS - 1 - h)).astype(F32) * (log2e / 256.0)
        pos = lax.broadcasted_iota(jnp.int32, (seq_len, 1), 0)
        in_block = (pos & (BK - 1)).astype(F32) * slope
        block_off = (pos & -BK).astype(F32) * slope
        terms = _split3(in_block) + _split3(block_off)
        tile = jnp.zeros((seq_len, HEAD_DIM), F32)
        for j, term in enumerate(terms):
            tile = jnp.where((lane == j) | (lane == DIFF_QK_DIM + j), term, tile)
        bias_ref[...] = tile
        ones_row = lax.broadcasted_iota(jnp.int32, (V_PAD, seq_len), 0) == 0
        vtx_ref[HEAD_DIM:, :] = jnp.where(ones_row, 1.0, 0.0).astype(BF16)

    k = k_ref[...].astype(F32)
    kext_ref[0] = jnp.where(lane < DIFF_QK_DIM, k, bias_ref[...]).astype(BF16)
    kext_ref[1] = jnp.where(lane >= DIFF_QK_DIM, k, bias_ref[...]).astype(BF16)
    vtx_ref[0:HEAD_DIM, :] = vt_ref[...]

    lam = (jnp.exp(jnp.sum(lq1_ref[...] * lk1_ref[...], axis=-1, keepdims=True))
           - jnp.exp(jnp.sum(lq2_ref[...] * lk2_ref[...], axis=-1, keepdims=True)) + lam_init)

    def scores(q_ext, kb, masked, lo):
        rows = slice(kb * BK, (kb + 1) * BK)
        out = []
        for c in range(2):
            s = _dot_nt(kext_ref[c, rows, :], q_ext[c][lo:, :])
            if masked:
                krel = lax.broadcasted_iota(jnp.int32, (BK, BQ - lo), 0)
                qrel = lax.broadcasted_iota(jnp.int32, (BK, BQ - lo), 1)
                s = jnp.where(krel <= qrel, s, MASK_VALUE)
            out.append((s, jnp.max(s, axis=0, keepdims=True)))
        return out

    def accumulate(par, kb, block, lo):
        vt = vtx_ref[:, kb * BK:(kb + 1) * BK]
        for c in range(2):
            s, s_max = block[c]
            m_old = m_ref[par, c, :, lo:]
            m_new = jnp.maximum(m_old, s_max)
            alpha = jnp.exp2(m_old - m_new)
            p = jnp.exp2(s - m_new)
            acc_ref[par, c, :, lo:] = alpha * acc_ref[par, c, :, lo:] + _dot(vt, p.astype(BF16))
            m_ref[par, c, :, lo:] = m_new

    for qi in range(seq_len // BQ):
        par = qi % 2
        qrows = slice(qi * BQ, (qi + 1) * BQ)
        q = q_ref[qrows, :].astype(F32) * (DIFF_QK_DIM ** -0.5 * log2e)
        q_ext = (
            jnp.where(lane < DIFF_QK_DIM, q,
                      jnp.where(lane < DIFF_QK_DIM + n_bias, 1.0, 0.0)).astype(BF16),
            jnp.where(lane >= DIFF_QK_DIM, q, jnp.where(lane < n_bias, 1.0, 0.0)).astype(BF16),
        )
        m_ref[par] = jnp.full(m_ref.shape[1:], MASK_VALUE, F32)
        acc_ref[par] = jnp.zeros(acc_ref.shape[1:], F32)
        n_blocks = 2 * qi + 2
        lo_of = lambda kb: BK if kb == n_blocks - 1 else 0
        cur = scores(q_ext, 0, 0 >= 2 * qi, lo_of(0))
        for kb in range(n_blocks):
            nxt = scores(q_ext, kb + 1, kb + 1 >= 2 * qi, lo_of(kb + 1)) if kb + 1 < n_blocks else None
            accumulate(par, kb, cur, lo_of(kb))
            cur = nxt

        num = (acc_ref[par, 0, 0:HEAD_DIM, :], acc_ref[par, 1, 0:HEAD_DIM, :])
        den = (acc_ref[par, 0, HEAD_DIM:HEAD_DIM + 1, :], acc_ref[par, 1, HEAD_DIM:HEAD_DIM + 1, :])
        o_t = num[0] / den[0] - lam * (num[1] / den[1])
        o = o_t.T
        o = o * lax.rsqrt(jnp.mean(o * o, axis=-1, keepdims=True) + NORM_EPS)
        o = o * nw_ref[...] * (1.0 - lam_init)
        o_ref[qrows, :] = (o * _silu(z_ref[qrows, :].astype(F32))).astype(BF16)


def _attn(act, v_t, lq1, lk1, lq2, lk2, diff_norm_w, lam_init, batch, seq_len):
    t = act.shape[0]
    small = lambda shape: pl.BlockSpec(shape, lambda h, b: (0, 0))
    return pl.pallas_call(
        functools.partial(_attn_kernel, lam_init),
        grid=(DIFF_HEADS, batch),
        in_specs=[
            pl.BlockSpec((seq_len, HEAD_DIM), lambda h, b: (b, COL_DQ + h)),
            pl.BlockSpec((seq_len, HEAD_DIM), lambda h, b: (b, COL_DK + h)),
            pl.BlockSpec((HEAD_DIM, seq_len), lambda h, b: (h, b)),
            pl.BlockSpec((seq_len, HEAD_DIM), lambda h, b: (b, COL_DZ + h)),
            small((1, DIFF_QK_DIM)), small((1, DIFF_QK_DIM)),
            small((1, DIFF_QK_DIM)), small((1, DIFF_QK_DIM)),
            small((1, HEAD_DIM)),
        ],
        out_specs=pl.BlockSpec((seq_len, HEAD_DIM), lambda h, b: (b, h)),
        out_shape=jax.ShapeDtypeStruct((t, DIFF_WIDTH), BF16),
        scratch_shapes=[
            pltpu.VMEM((seq_len, HEAD_DIM), F32),
            pltpu.VMEM((2, seq_len, HEAD_DIM), BF16),
            pltpu.VMEM((HEAD_DIM + V_PAD, seq_len), BF16),
            pltpu.VMEM((2, 2, HEAD_DIM + V_PAD, BQ), F32),
            pltpu.VMEM((2, 2, 1, BQ), F32),
        ],
        compiler_params=pltpu.CompilerParams(
            dimension_semantics=("arbitrary", "arbitrary"), vmem_limit_bytes=VMEM_LIMIT),
        name="diffattn",
    )(act, act, v_t, act, lq1, lk1, lq2, lk2, diff_norm_w)


def _outproj_kernel(oa_ref, ob_ref, x_ref, w_ref, fw_ref, out_ref):
    mix = _dot(oa_ref[...], w_ref[0:GDN_WIDTH, :]) + _dot(ob_ref[...], w_ref[GDN_WIDTH:, :])
    y = x_ref[...] + mix
    ms = jnp.mean(y * y, axis=-1, keepdims=True)
    out_ref[...] = y * lax.rsqrt(ms + NORM_EPS) * fw_ref[...]


def _outproj(o_a, o_b, xf, w_out, final_norm_w):
    t = xf.shape[0]
    return pl.pallas_call(
        _outproj_kernel,
        grid=(t // TM_OUT,),
        in_specs=[
            pl.BlockSpec((TM_OUT, GDN_WIDTH), lambda i: (i, 0)),
            pl.BlockSpec((TM_OUT, DIFF_WIDTH), lambda i: (i, 0)),
            pl.BlockSpec((TM_OUT, D_MODEL), lambda i: (i, 0)),
            pl.BlockSpec((D_MODEL, D_MODEL), lambda i: (0, 0)),
            pl.BlockSpec((1, D_MODEL), lambda i: (0, 0)),
        ],
        out_specs=pl.BlockSpec((TM_OUT, D_MODEL), lambda i: (i, 0)),
        out_shape=jax.ShapeDtypeStruct((t, D_MODEL), F32),
        compiler_params=pltpu.CompilerParams(
            dimension_semantics=("arbitrary",), vmem_limit_bytes=VMEM_LIMIT),
        name="outproj",
    )(o_a, o_b, xf, w_out, final_norm_w)


def kernel(x, norm_w, w_in, conv_w, a_log, dt_bias, gdn_norm_w, lambda_q1, lambda_k1,
           lambda_q2, lambda_k2, diff_norm_w, w_out, final_norm_w):
    batch, seq_len, d_model = x.shape
    depth = norm_w.shape[0]
    assert depth == 1 and d_model == D_MODEL
    assert seq_len % TM_IN == 0 and seq_len % SC == 0 and seq_len % BQ == 0 and BQ % BK == 0
    n_wide_a = 4 * GDN_WIDTH
    xf = x.reshape(batch * seq_len, d_model)

    w = w_in[0]
    c_dq = n_wide_a + N_GATE
    c_dv = c_dq + 2 * DIFF_WIDTH
    c_dz = c_dv + DIFF_WIDTH
    w_main = jnp.concatenate([w[:, :n_wide_a], w[:, c_dq:c_dv], w[:, c_dz:]], axis=1).astype(BF16)
    w_v_t = w[:, c_dv:c_dz].T.astype(BF16)
    w_gate_t = w[:, n_wide_a:n_wide_a + N_GATE].T.astype(BF16)

    act, gate_t, v_t = _inproj(xf, norm_w[0][None, :], w_main, w_gate_t, w_v_t, conv_w[0], seq_len)

    pad_s = lambda vec: jnp.pad(vec, (GDN_HEADS, 0))[:, None]
    o_a = _gdn(act, gate_t, pad_s(a_log[0]), pad_s(dt_bias[0]), gdn_norm_w[0][None, :],
               batch, seq_len)

    lam_init = 0.8 - 0.6 * math.exp(-0.3 * 0)
    o_b = _attn(act, v_t, lambda_q1[0][None, :], lambda_k1[0][None, :], lambda_q2[0][None, :],
                lambda_k2[0][None, :], diff_norm_w[0][None, :], lam_init, batch, seq_len)

    out = _outproj(o_a, o_b, xf, w_out[0].astype(BF16), final_norm_w[None, :])
    return out.reshape(batch, seq_len, d_model)
```

```python
import functools
import math

import jax
import jax.numpy as jnp
from jax import lax
from jax.experimental import pallas as pl
from jax.experimental.pallas import tpu as pltpu

F32 = jnp.float32
BF16 = jnp.bfloat16

D_MODEL = 1024
GDN_HEADS = 4
HEAD_DIM = 128
GDN_WIDTH = GDN_HEADS * HEAD_DIM
DIFF_HEADS = 4
DIFF_QK_DIM = 64
DIFF_WIDTH = DIFF_HEADS * HEAD_DIM
CONV_K = 4
NORM_EPS = 1e-6
N_GATE = 2 * GDN_HEADS
N_MAIN = 4 * GDN_WIDTH + 3 * DIFF_WIDTH
N_CONV = 3 * GDN_WIDTH
LANES = 128
MASK_VALUE = -1e30

COL_GQ, COL_GK, COL_GV, COL_GZ = 0, 4, 8, 12
COL_DQ, COL_DK, COL_DZ = 16, 20, 24

TM_IN = 1024
NC_IN = 256
TAIL = 16
TM_OUT = 1024
SC = 256
CHUNK = 64
GDN_GROUP = 2
BQ = 512
BK = 256
V_PAD = 16
VMEM_LIMIT = 48 * 1024 * 1024


def _sigmoid(x):
    return 1.0 / (1.0 + jnp.exp(-x))


def _silu(x):
    half = 0.5 * x
    return half + half * jnp.tanh(half)


def _softplus(x):
    return jnp.maximum(x, 0.0) + jnp.log1p(jnp.exp(-jnp.abs(x)))


def _dot(a, b):
    return jnp.dot(a, b, preferred_element_type=F32)


def _dot_nt(a, b):
    return lax.dot_general(a, b, (((1,), (1,)), ((), ())), preferred_element_type=F32)


def _dot_tn(a, b):
    return lax.dot_general(a, b, (((0,), (0,)), ((), ())), preferred_element_type=F32)


def _inproj_kernel(tiles_per_seq, x_ref, nw_ref, w_ref, wgt_ref, wvt_ref, cw_ref,
                   main_ref, gate_t_ref, vt_ref, tail_ref):
    i = pl.program_id(0)
    x = x_ref[...]
    ms = jnp.mean(x * x, axis=-1, keepdims=True)
    h = (x * lax.rsqrt(ms + NORM_EPS) * nw_ref[...]).astype(BF16)

    gate_t_ref[...] = _dot_nt(wgt_ref[...], h)
    vt_ref[...] = _dot_nt(wvt_ref[...], h).astype(BF16)

    @pl.when(i % tiles_per_seq == 0)
    def _():
        tail_ref[...] = jnp.zeros_like(tail_ref)

    tm = x.shape[0]
    row = lax.broadcasted_iota(jnp.int32, (TAIL, LANES), 0)

    def conv_silu(acc, n0):
        cols = slice(n0, n0 + LANES)
        last = acc[tm - TAIL:, :]
        delta = tail_ref[:, cols] - last
        tail_ref[:, cols] = last
        y = acc * cw_ref[CONV_K - 1:CONV_K, cols]
        fix = jnp.zeros((TAIL, LANES), F32)
        for s in range(1, CONV_K):
            wk = cw_ref[CONV_K - 1 - s:CONV_K - s, cols]
            y = y + pltpu.roll(acc, s, axis=0) * wk
            fix = fix + jnp.where(row < s, pltpu.roll(delta, s, axis=0), 0.0) * wk
        main_ref[:, cols] = _silu(y).astype(BF16)
        main_ref[0:TAIL, cols] = _silu(y[:TAIL, :] + fix).astype(BF16)

    plain = [slice(n0, n0 + NC_IN) for n0 in range(N_CONV, N_MAIN, NC_IN)]
    pending = []
    for n0 in range(0, N_CONV, NC_IN):
        acc = _dot(h, w_ref[:, n0:n0 + NC_IN])
        if pending:
            conv_silu(*pending.pop(0))
        if plain:
            pc = plain.pop(0)
            main_ref[:, pc] = _dot(h, w_ref[:, pc]).astype(BF16)
        if pending:
            conv_silu(*pending.pop(0))
        pending = [(acc[:, j:j + LANES], n0 + j) for j in range(0, NC_IN, LANES)]
    for piece in pending:
        if plain:
            pc = plain.pop(0)
            main_ref[:, pc] = _dot(h, w_ref[:, pc]).astype(BF16)
        conv_silu(*piece)
    for pc in plain:
        main_ref[:, pc] = _dot(h, w_ref[:, pc]).astype(BF16)


def _inproj(xf, norm_w, w_main, w_gate_t, w_v_t, conv_w, seq_len):
    t = xf.shape[0]
    tiles_per_seq = seq_len // TM_IN
    return pl.pallas_call(
        functools.partial(_inproj_kernel, tiles_per_seq),
        grid=(t // TM_IN,),
        in_specs=[
            pl.BlockSpec((TM_IN, D_MODEL), lambda i: (i, 0)),
            pl.BlockSpec((1, D_MODEL), lambda i: (0, 0)),
            pl.BlockSpec((D_MODEL, N_MAIN), lambda i: (0, 0)),
            pl.BlockSpec((N_GATE, D_MODEL), lambda i: (0, 0)),
            pl.BlockSpec((DIFF_WIDTH, D_MODEL), lambda i: (0, 0)),
            pl.BlockSpec((CONV_K, N_CONV), lambda i: (0, 0)),
        ],
        out_specs=[
            pl.BlockSpec((TM_IN, N_MAIN), lambda i: (i, 0)),
            pl.BlockSpec((N_GATE, TM_IN), lambda i: (0, i)),
            pl.BlockSpec((DIFF_WIDTH, TM_IN), lambda i: (0, i)),
        ],
        out_shape=[
            jax.ShapeDtypeStruct((t, N_MAIN), BF16),
            jax.ShapeDtypeStruct((N_GATE, t), F32),
            jax.ShapeDtypeStruct((DIFF_WIDTH, t), BF16),
        ],
        scratch_shapes=[pltpu.VMEM((TAIL, N_CONV), F32)],
        compiler_params=pltpu.CompilerParams(
            dimension_semantics=("arbitrary",), vmem_limit_bytes=VMEM_LIMIT),
        name="inproj",
    )(xf, norm_w, w_main, w_gate_t, w_v_t, conv_w)


def _gdn_kernel(act_ref, gate_t_ref, alog_s_ref, dtb_s_ref, gnw_ref, o_ref, mask_ref, bd_ref,
                row_ref, col_ref):
    seq_len = act_ref.shape[0]
    n_chunks = SC // CHUNK
    heads = range(GDN_HEADS)
    i_cat = lax.broadcasted_iota(jnp.int32, (CHUNK, SC), 0)
    j_cat = lax.broadcasted_iota(jnp.int32, (CHUNK, SC), 1) & (CHUNK - 1)
    mask_ref[0] = jnp.where(i_cat >= j_cat, 0.0, MASK_VALUE)
    mask_ref[1] = jnp.where(i_cat > j_cat, -1.0, 0.0)
    mask_ref[2] = jnp.where(i_cat == j_cat, 1.0, 0.0)
    r = lax.broadcasted_iota(jnp.int32, (SC, SC), 0)
    c = lax.broadcasted_iota(jnp.int32, (SC, SC), 1)
    bd_ref[...] = jnp.where((r & -CHUNK) == (c & -CHUNK), 1.0, 0.0).astype(BF16)
    lane_chunk = lax.broadcasted_iota(jnp.int32, (1, SC), 1) & -CHUNK

    scale = HEAD_DIM ** -0.5

    lane_in_chunk = lax.broadcasted_iota(jnp.int32, (N_GATE, seq_len), 1) & (CHUNK - 1)
    gate_t = gate_t_ref[...]
    beta = _sigmoid(gate_t)
    log_beta = -_softplus(-gate_t)
    g_step = -jnp.exp(alog_s_ref[...]) * _softplus(gate_t + dtb_s_ref[...])
    gc = g_step
    g_after = jnp.zeros_like(g_step)
    tail = g_step
    step = 1
    while step < CHUNK:
        gc = gc + jnp.where(lane_in_chunk >= step, pltpu.roll(gc, step, axis=1), 0.0)
        ahead = jnp.where(lane_in_chunk + step < CHUNK, pltpu.roll(tail, seq_len - step, axis=1), 0.0)
        g_after = g_after + ahead
        tail = tail + ahead
        step *= 2
    e_gc = jnp.exp(gc)
    is_beta_row = lax.broadcasted_iota(jnp.int32, (N_GATE, seq_len), 0) < GDN_HEADS
    swap = pltpu.roll(e_gc, GDN_HEADS, axis=0)
    row_ref[0] = gc
    row_ref[1] = e_gc
    col_ref[...] = jnp.concatenate([
        jnp.where(is_beta_row, beta, gc),
        jnp.where(is_beta_row, log_beta, e_gc),
        jnp.where(is_beta_row, beta * swap, jnp.exp(g_after))], axis=0).T

    def cat(x):
        out = x[(n_chunks - 1) * CHUNK:, :]
        for ci in range(n_chunks - 2, -1, -1):
            out = jnp.where(lane_chunk == ci * CHUNK, x[ci * CHUNK:(ci + 1) * CHUNK, :], out)
        return out

    def block_diag(x_cat):
        xb = x_cat.astype(BF16)
        zero = jnp.zeros((CHUNK, LANES), BF16)
        row_blocks = []
        for ci in range(n_chunks):
            g = ci * CHUNK // LANES
            grp = slice(g * LANES, (g + 1) * LANES)
            part = xb[:, grp] * bd_ref[ci * CHUNK:(ci + 1) * CHUNK, grp]
            row_blocks.append(jnp.concatenate(
                [part if j == g else zero for j in range(SC // LANES)], axis=1))
        return jnp.concatenate(row_blocks, axis=0)

    def head_cols(col, h):
        return slice((col + h) * HEAD_DIM, (col + h + 1) * HEAD_DIM)

    def prepare(steps):
        causal, neg_strict, eye = mask_ref[0], mask_ref[1], mask_ref[2]
        units = [(sc, h) for sc in steps for h in heads]
        n_pow, a_cat, rhs, q_dec, k_dec, decay_last = [], [], [], [], [], []
        for sc, h in units:
            rows = slice(sc * SC, (sc + 1) * SC)
            col = lambda j: col_ref[rows, j:j + 1]
            q = act_ref[rows, head_cols(COL_GQ, h)].astype(F32)
            k = act_ref[rows, head_cols(COL_GK, h)].astype(F32)
            v = act_ref[rows, head_cols(COL_GV, h)].astype(F32)
            q = q * (lax.rsqrt(jnp.sum(q * q, axis=-1, keepdims=True) + 1e-6) * scale)
            k = k * lax.rsqrt(jnp.sum(k * k, axis=-1, keepdims=True) + 1e-6)
            hb, hg = h, GDN_HEADS + h
            kb = k.astype(BF16)
            gram = _dot_nt(jnp.concatenate([kb, q.astype(BF16)], axis=0), kb)
            e_cat = cat(col(hg)) - row_ref[0, hg:hg + 1, rows] + causal
            n_pow.append(cat(gram[:SC]) * jnp.exp(e_cat + cat(col(N_GATE + hb))) * neg_strict)
            a_cat.append(cat(gram[SC:]) * jnp.exp(e_cat))
            rhs.append(jnp.concatenate([col(hb) * v, col(2 * N_GATE + hb) * k], axis=1).astype(BF16))
            q_dec.append(q * col(N_GATE + hg))
            kd = (k * col(2 * N_GATE + hg)).astype(BF16)
            k_dec.append([kd[ci * CHUNK:(ci + 1) * CHUNK, :] for ci in range(n_chunks)])
            decay_last.append([row_ref[1, hg:hg + 1, sc * SC + (ci + 1) * CHUNK - 1:sc * SC + (ci + 1) * CHUNK]
                               for ci in range(n_chunks)])

        every = range(len(units))
        t_cat = [eye + n_pow[u] for u in every]
        n_pow = [_dot(n_pow[u].astype(BF16), block_diag(n_pow[u])) for u in every]
        span = 2
        while span < CHUNK // 2:
            both = [_dot(jnp.concatenate([t_cat[u], n_pow[u]], axis=0).astype(BF16), block_diag(n_pow[u]))
                    for u in every]
            t_cat = [t_cat[u] + both[u][:CHUNK] for u in every]
            n_pow = [both[u][CHUNK:] for u in every]
            span *= 2
        t_cat = [t_cat[u] + _dot(t_cat[u].astype(BF16), block_diag(n_pow[u])) for u in every]

        uw = [_dot(block_diag(t_cat[u]), rhs[u]) for u in every]
        kt = [[_dot_tn(k_dec[u][ci], uw[u][ci * CHUNK:(ci + 1) * CHUNK, :].astype(BF16))
               for ci in range(n_chunks)] for u in every]
        per_step = lambda xs, i: xs[i * GDN_HEADS:(i + 1) * GDN_HEADS]
        return [tuple(per_step(xs, i) for xs in (uw, kt, q_dec, a_cat, decay_last))
                for i in range(len(steps))]

    def finish(sc, prep, state):
        rows = slice(sc * SC, (sc + 1) * SC)
        uw, kt, q_dec, a_cat, decay_last = prep
        state = list(state)
        state_in = [[None] * n_chunks for _ in heads]
        for ci in range(n_chunks):
            for h in heads:
                sb = state[h].astype(BF16)
                state_in[h][ci] = sb
                state[h] = (state[h] * decay_last[h][ci]
                            - _dot(kt[h][ci][:, HEAD_DIM:].astype(BF16), sb) + kt[h][ci][:, :HEAD_DIM])
        for h in heads:
            inter = []
            for ci in range(n_chunks):
                cr = slice(ci * CHUNK, (ci + 1) * CHUNK)
                lhs = jnp.concatenate([uw[h][cr, HEAD_DIM:], q_dec[h][cr, :]], axis=0).astype(BF16)
                inter.append(_dot(lhs, state_in[h][ci]))
            v_new = jnp.concatenate([uw[h][ci * CHUNK:(ci + 1) * CHUNK, :HEAD_DIM] - inter[ci][:CHUNK]
                                     for ci in range(n_chunks)], axis=0).astype(BF16)
            o = (jnp.concatenate([inter[ci][CHUNK:] for ci in range(n_chunks)], axis=0)
                 + _dot(block_diag(a_cat[h]), v_new))
            o = o * lax.rsqrt(jnp.mean(o * o, axis=-1, keepdims=True) + NORM_EPS)
            z = act_ref[rows, head_cols(COL_GZ, h)].astype(F32)
            o_ref[rows, head_cols(0, h)] = (o * (_silu(z) * gnw_ref[...])).astype(BF16)
        return state

    n_groups = seq_len // (SC * GDN_GROUP)
    group = lambda g: list(range(g * GDN_GROUP, (g + 1) * GDN_GROUP))
    state = [jnp.zeros((HEAD_DIM, HEAD_DIM), F32) for _ in heads]
    preps = prepare(group(0))
    for g in range(n_groups):
        nxt = prepare(group(g + 1)) if g + 1 < n_groups else None
        for sc, prep in zip(group(g), preps):
            state = finish(sc, prep, state)
        preps = nxt


def _gdn(act, gate_t, alog_s, dtb_s, gdn_norm_w, batch, seq_len):
    t = act.shape[0]
    small = lambda shape: pl.BlockSpec(shape, lambda b: (0, 0))
    return pl.pallas_call(
        _gdn_kernel,
        grid=(batch,),
        in_specs=[
            pl.BlockSpec((seq_len, 4 * GDN_WIDTH), lambda b: (b, 0)),
            pl.BlockSpec((N_GATE, seq_len), lambda b: (0, b)),
            small((N_GATE, 1)), small((N_GATE, 1)),
            small((1, HEAD_DIM)),
        ],
        out_specs=pl.BlockSpec((seq_len, GDN_WIDTH), lambda b: (b, 0)),
        out_shape=jax.ShapeDtypeStruct((t, GDN_WIDTH), BF16),
        scratch_shapes=[
            pltpu.VMEM((3, CHUNK, SC), F32),
            pltpu.VMEM((SC, SC), BF16),
            pltpu.VMEM((2, N_GATE, seq_len), F32),
            pltpu.VMEM((seq_len, 3 * N_GATE), F32),
        ],
        compiler_params=pltpu.CompilerParams(
            dimension_semantics=("arbitrary",), vmem_limit_bytes=VMEM_LIMIT),
        name="gdn",
    )(act, gate_t, alog_s, dtb_s, gdn_norm_w)


def _split3(x):
    hi = x.astype(BF16).astype(F32)
    r1 = x - hi
    mid = r1.astype(BF16).astype(F32)
    return hi, mid, r1 - mid


def _attn_kernel(lam_init, q_ref, k_ref, vt_ref, z_ref, lq1_ref, lk1_ref, lq2_ref, lk2_ref,
                 nw_ref, o_ref, bias_ref, kext_ref, vtx_ref, acc_ref, m_ref):
    h = pl.program_id(0)
    b = pl.program_id(1)
    seq_len = k_ref.shape[0]
    lane = lax.broadcasted_iota(jnp.int32, (1, HEAD_DIM), 1)
    n_bias = 6
    log2e = math.log2(math.e)
    assert BQ == 2 * BK

    @pl.when(b == 0)
    def _():
        slope = lax.shift_left(jnp.int32(1), 2 * (DIFF_HEADS - 1 - h)).astype(F32) * (log2e / 256.0)
        pos = lax.broadcasted_iota(jnp.int32, (seq_len, 1), 0)
        in_block = (pos & (BK - 1)).astype(F32) * slope
        block_off = (pos & -BK).astype(F32) * slope
        terms = _split3(in_block) + _split3(block_off)
        tile = jnp.zeros((seq_len, HEAD_DIM), F32)
        for j, term in enumerate(terms):
            tile = jnp.where((lane == j) | (lane == DIFF_QK_DIM + j), term, tile)
        bias_ref[...] = tile
        ones_row = lax.broadcasted_iota(jnp.int32, (V_PAD, seq_len), 0) == 0
        vtx_ref[HEAD_DIM:, :] = jnp.where(ones_row, 1.0, 0.0).astype(BF16)

    k = k_ref[...].astype(F32)
    kext_ref[0] = jnp.where(lane < DIFF_QK_DIM, k, bias_ref[...]).astype(BF16)
    kext_ref[1] = jnp.where(lane >= DIFF_QK_DIM, k, bias_ref[...]).astype(BF16)
    vtx_ref[0:HEAD_DIM, :] = vt_ref[...]

    lam = (jnp.exp(jnp.sum(lq1_ref[...] * lk1_ref[...], axis=-1, keepdims=True))
           - jnp.exp(jnp.sum(lq2_ref[...] * lk2_ref[...], axis=-1, keepdims=True)) + lam_init)

    def scores(q_ext, kb, masked, lo):
        rows = slice(kb * BK, (kb + 1) * BK)
        out = []
        for c in range(2):
            s = _dot_nt(kext_ref[c, rows, :], q_ext[c][lo:, :])
            if masked:
                krel = lax.broadcasted_iota(jnp.int32, (BK, BQ - lo), 0)
                qrel = lax.broadcasted_iota(jnp.int32, (BK, BQ - lo), 1)
                s = jnp.where(krel <= qrel, s, MASK_VALUE)
            out.append((s, jnp.max(s, axis=0, keepdims=True)))
        return out

    def accumulate(par, kb, first, second, lo):
        vt = vtx_ref[:, kb * BK:(kb + 2) * BK]
        for c in range(2):
            (s_a, max_a), (s_b, max_b) = first[c], second[c]
            m_old = m_ref[par, c]
            if lo:
                max_b = jnp.concatenate([jnp.full((1, lo), MASK_VALUE, F32), max_b], axis=1)
            m_new = jnp.maximum(m_old, jnp.maximum(max_a, max_b))
            alpha = jnp.exp2(m_old - m_new)
            p_a = jnp.exp2(s_a - m_new).astype(BF16)
            p_b = jnp.exp2(s_b - m_new[:, lo:]).astype(BF16)
            both = _dot(vt, jnp.concatenate([p_a[:, lo:], p_b], axis=0))
            if lo:
                both = jnp.concatenate([_dot(vt[:, :BK], p_a[:, :lo]), both], axis=1)
            acc_ref[par, c] = alpha * acc_ref[par, c] + both
            m_ref[par, c] = m_new

    for qi in range(seq_len // BQ):
        par = qi % 2
        qrows = slice(qi * BQ, (qi + 1) * BQ)
        q = q_ref[qrows, :].astype(F32) * (DIFF_QK_DIM ** -0.5 * log2e)
        q_ext = (
            jnp.where(lane < DIFF_QK_DIM, q,
                      jnp.where(lane < DIFF_QK_DIM + n_bias, 1.0, 0.0)).astype(BF16),
            jnp.where(lane >= DIFF_QK_DIM, q, jnp.where(lane < n_bias, 1.0, 0.0)).astype(BF16),
        )
        m_ref[par] = jnp.full(m_ref.shape[1:], MASK_VALUE, F32)
        acc_ref[par] = jnp.zeros(acc_ref.shape[1:], F32)
        n_pairs = qi + 1
        lo_of = lambda j: BK if j == n_pairs - 1 else 0

        def pair_scores(j):
            diag = j == n_pairs - 1
            return (scores(q_ext, 2 * j, diag, 0), scores(q_ext, 2 * j + 1, diag, lo_of(j)))

        cur = pair_scores(0)
        for j in range(n_pairs):
            nxt = pair_scores(j + 1) if j + 1 < n_pairs else None
            accumulate(par, 2 * j, cur[0], cur[1], lo_of(j))
            cur = nxt

        num = (acc_ref[par, 0, 0:HEAD_DIM, :], acc_ref[par, 1, 0:HEAD_DIM, :])
        den = (acc_ref[par, 0, HEAD_DIM:HEAD_DIM + 1, :], acc_ref[par, 1, HEAD_DIM:HEAD_DIM + 1, :])
        o_t = num[0] / den[0] - lam * (num[1] / den[1])
        o = o_t.T
        o = o * lax.rsqrt(jnp.mean(o * o, axis=-1, keepdims=True) + NORM_EPS)
        o = o * nw_ref[...] * (1.0 - lam_init)
        o_ref[qrows, :] = (o * _silu(z_ref[qrows, :].astype(F32))).astype(BF16)


def _attn(act, v_t, lq1, lk1, lq2, lk2, diff_norm_w, lam_init, batch, seq_len):
    t = act.shape[0]
    small = lambda shape: pl.BlockSpec(shape, lambda h, b: (0, 0))
    return pl.pallas_call(
        functools.partial(_attn_kernel, lam_init),
        grid=(DIFF_HEADS, batch),
        in_specs=[
            pl.BlockSpec((seq_len, HEAD_DIM), lambda h, b: (b, COL_DQ + h)),
            pl.BlockSpec((seq_len, HEAD_DIM), lambda h, b: (b, COL_DK + h)),
            pl.BlockSpec((HEAD_DIM, seq_len), lambda h, b: (h, b)),
            pl.BlockSpec((seq_len, HEAD_DIM), lambda h, b: (b, COL_DZ + h)),
            small((1, DIFF_QK_DIM)), small((1, DIFF_QK_DIM)),
            small((1, DIFF_QK_DIM)), small((1, DIFF_QK_DIM)),
            small((1, HEAD_DIM)),
        ],
        out_specs=pl.BlockSpec((seq_len, HEAD_DIM), lambda h, b: (b, h)),
        out_shape=jax.ShapeDtypeStruct((t, DIFF_WIDTH), BF16),
        scratch_shapes=[
            pltpu.VMEM((seq_len, HEAD_DIM), F32),
            pltpu.VMEM((2, seq_len, HEAD_DIM), BF16),
            pltpu.VMEM((HEAD_DIM + V_PAD, seq_len), BF16),
            pltpu.VMEM((2, 2, HEAD_DIM + V_PAD, BQ), F32),
            pltpu.VMEM((2, 2, 1, BQ), F32),
        ],
        compiler_params=pltpu.CompilerParams(
            dimension_semantics=("arbitrary", "arbitrary"), vmem_limit_bytes=VMEM_LIMIT),
        name="diffattn",
    )(act, act, v_t, act, lq1, lk1, lq2, lk2, diff_norm_w)


def _outproj_kernel(oa_ref, ob_ref, x_ref, w_ref, fw_ref, out_ref):
    mix = _dot(oa_ref[...], w_ref[0:GDN_WIDTH, :]) + _dot(ob_ref[...], w_ref[GDN_WIDTH:, :])
    y = x_ref[...] + mix
    ms = jnp.mean(y * y, axis=-1, keepdims=True)
    out_ref[...] = y * lax.rsqrt(ms + NORM_EPS) * fw_ref[...]


def _outproj(o_a, o_b, xf, w_out, final_norm_w):
    t = xf.shape[0]
    return pl.pallas_call(
        _outproj_kernel,
        grid=(t // TM_OUT,),
        in_specs=[
            pl.BlockSpec((TM_OUT, GDN_WIDTH), lambda i: (i, 0)),
            pl.BlockSpec((TM_OUT, DIFF_WIDTH), lambda i: (i, 0)),
            pl.BlockSpec((TM_OUT, D_MODEL), lambda i: (i, 0)),
            pl.BlockSpec((D_MODEL, D_MODEL), lambda i: (0, 0)),
            pl.BlockSpec((1, D_MODEL), lambda i: (0, 0)),
        ],
        out_specs=pl.BlockSpec((TM_OUT, D_MODEL), lambda i: (i, 0)),
        out_shape=jax.ShapeDtypeStruct((t, D_MODEL), F32),
        compiler_params=pltpu.CompilerParams(
            dimension_semantics=("arbitrary",), vmem_limit_bytes=VMEM_LIMIT),
        name="outproj",
    )(o_a, o_b, xf, w_out, final_norm_w)


def kernel(x, norm_w, w_in, conv_w, a_log, dt_bias, gdn_norm_w, lambda_q1, lambda_k1,
           lambda_q2, lambda_k2, diff_norm_w, w_out, final_norm_w):
    batch, seq_len, d_model = x.shape
    depth = norm_w.shape[0]
    assert depth == 1 and d_model == D_MODEL
    assert seq_len % TM_IN == 0 and seq_len % SC == 0 and seq_len % BQ == 0 and BQ % BK == 0
    n_wide_a = 4 * GDN_WIDTH
    xf = x.reshape(batch * seq_len, d_model)

    w = w_in[0]
    c_dq = n_wide_a + N_GATE
    c_dv = c_dq + 2 * DIFF_WIDTH
    c_dz = c_dv + DIFF_WIDTH
    w_main = jnp.concatenate([w[:, :n_wide_a], w[:, c_dq:c_dv], w[:, c_dz:]], axis=1).astype(BF16)
    w_v_t = w[:, c_dv:c_dz].T.astype(BF16)
    w_gate_t = w[:, n_wide_a:n_wide_a + N_GATE].T.astype(BF16)

    act, gate_t, v_t = _inproj(xf, norm_w[0][None, :], w_main, w_gate_t, w_v_t, conv_w[0], seq_len)

    pad_s = lambda vec: jnp.pad(vec, (GDN_HEADS, 0))[:, None]
    o_a = _gdn(act, gate_t, pad_s(a_log[0]), pad_s(dt_bias[0]), gdn_norm_w[0][None, :],
               batch, seq_len)

    lam_init = 0.8 - 0.6 * math.exp(-0.3 * 0)
    o_b = _attn(act, v_t, lambda_q1[0][None, :], lambda_k1[0][None, :], lambda_q2[0][None, :],
                lambda_k2[0][None, :], diff_norm_w[0][None, :], lam_init, batch, seq_len)

    out = _outproj(o_a, o_b, xf, w_out[0].astype(BF16), final_norm_w[None, :])
    return out.reshape(batch, seq_len, d_model)
```

```python
import functools
import math

import jax
import jax.numpy as jnp
from jax import lax
from jax.experimental import pallas as pl
from jax.experimental.pallas import tpu as pltpu

F32 = jnp.float32
BF16 = jnp.bfloat16

D_MODEL = 1024
GDN_HEADS = 4
HEAD_DIM = 128
GDN_WIDTH = GDN_HEADS * HEAD_DIM
DIFF_HEADS = 4
DIFF_QK_DIM = 64
DIFF_WIDTH = DIFF_HEADS * HEAD_DIM
CONV_K = 4
NORM_EPS = 1e-6
N_GATE = 2 * GDN_HEADS
N_MAIN = 4 * GDN_WIDTH + 3 * DIFF_WIDTH
N_CONV = 3 * GDN_WIDTH
LANES = 128
MASK_VALUE = -1e30

COL_GQ, COL_GK, COL_GV, COL_GZ = 0, 4, 8, 12
COL_DQ, COL_DK, COL_DZ = 16, 20, 24

TM_IN = 1024
NC_IN = 256
HALO = 16
TM_OUT = 1024
SC = 256
CHUNK = 64
GDN_GROUP = 2
BQ = 512
BK = 256
V_PAD = 16
VMEM_LIMIT = 48 * 1024 * 1024


def _sigmoid(x):
    return 1.0 / (1.0 + jnp.exp(-x))


def _silu(x):
    half = 0.5 * x
    return half + half * jnp.tanh(half)


def _softplus(x):
    return jnp.maximum(x, 0.0) + jnp.log1p(jnp.exp(-jnp.abs(x)))


def _dot(a, b):
    return jnp.dot(a, b, preferred_element_type=F32)


def _dot_nt(a, b):
    return lax.dot_general(a, b, (((1,), (1,)), ((), ())), preferred_element_type=F32)


def _dot_tn(a, b):
    return lax.dot_general(a, b, (((0,), (0,)), ((), ())), preferred_element_type=F32)


def _inproj_kernel(x_ref, nw_ref, w_ref, wgt_ref, wvt_ref, main_ref, gate_t_ref, vt_ref):
    x = x_ref[...]
    ms = jnp.mean(x * x, axis=-1, keepdims=True)
    h = (x * lax.rsqrt(ms + NORM_EPS) * nw_ref[...]).astype(BF16)

    gate_t_ref[...] = _dot_nt(wgt_ref[...], h)
    vt_ref[...] = _dot_nt(wvt_ref[...], h).astype(BF16)
    for n0 in range(0, N_MAIN, NC_IN):
        main_ref[:, n0:n0 + NC_IN] = _dot(h, w_ref[:, n0:n0 + NC_IN]).astype(BF16)


def _inproj(xf, norm_w, w_main, w_gate_t, w_v_t):
    t = xf.shape[0]
    return pl.pallas_call(
        _inproj_kernel,
        grid=(t // TM_IN,),
        in_specs=[
            pl.BlockSpec((TM_IN, D_MODEL), lambda i: (i, 0)),
            pl.BlockSpec((1, D_MODEL), lambda i: (0, 0)),
            pl.BlockSpec((D_MODEL, N_MAIN), lambda i: (0, 0)),
            pl.BlockSpec((N_GATE, D_MODEL), lambda i: (0, 0)),
            pl.BlockSpec((DIFF_WIDTH, D_MODEL), lambda i: (0, 0)),
        ],
        out_specs=[
            pl.BlockSpec((TM_IN, N_MAIN), lambda i: (i, 0)),
            pl.BlockSpec((N_GATE, TM_IN), lambda i: (0, i)),
            pl.BlockSpec((DIFF_WIDTH, TM_IN), lambda i: (0, i)),
        ],
        out_shape=[
            jax.ShapeDtypeStruct((t, N_MAIN), BF16),
            jax.ShapeDtypeStruct((N_GATE, t), F32),
            jax.ShapeDtypeStruct((DIFF_WIDTH, t), BF16),
        ],
        compiler_params=pltpu.CompilerParams(
            dimension_semantics=("arbitrary",), vmem_limit_bytes=VMEM_LIMIT),
        name="inproj",
    )(xf, norm_w, w_main, w_gate_t, w_v_t)


def _gdn_kernel(act_ref, gate_t_ref, alog_s_ref, dtb_s_ref, cw_ref, gnw_ref, o_ref, mask_ref, bd_ref,
                row_ref, col_ref):
    seq_len = act_ref.shape[0]
    n_chunks = SC // CHUNK
    heads = range(GDN_HEADS)
    i_cat = lax.broadcasted_iota(jnp.int32, (CHUNK, SC), 0)
    j_cat = lax.broadcasted_iota(jnp.int32, (CHUNK, SC), 1) & (CHUNK - 1)
    mask_ref[0] = jnp.where(i_cat >= j_cat, 0.0, MASK_VALUE)
    mask_ref[1] = jnp.where(i_cat > j_cat, -1.0, 0.0)
    mask_ref[2] = jnp.where(i_cat == j_cat, 1.0, 0.0)
    r = lax.broadcasted_iota(jnp.int32, (SC, SC), 0)
    c = lax.broadcasted_iota(jnp.int32, (SC, SC), 1)
    bd_ref[...] = jnp.where((r & -CHUNK) == (c & -CHUNK), 1.0, 0.0).astype(BF16)
    lane_chunk = lax.broadcasted_iota(jnp.int32, (1, SC), 1) & -CHUNK

    scale = HEAD_DIM ** -0.5

    lane_in_chunk = lax.broadcasted_iota(jnp.int32, (N_GATE, seq_len), 1) & (CHUNK - 1)
    gate_t = gate_t_ref[...]
    beta = _sigmoid(gate_t)
    log_beta = -_softplus(-gate_t)
    g_step = -jnp.exp(alog_s_ref[...]) * _softplus(gate_t + dtb_s_ref[...])
    gc = g_step
    g_after = jnp.zeros_like(g_step)
    tail = g_step
    step = 1
    while step < CHUNK:
        gc = gc + jnp.where(lane_in_chunk >= step, pltpu.roll(gc, step, axis=1), 0.0)
        ahead = jnp.where(lane_in_chunk + step < CHUNK, pltpu.roll(tail, seq_len - step, axis=1), 0.0)
        g_after = g_after + ahead
        tail = tail + ahead
        step *= 2
    e_gc = jnp.exp(gc)
    is_beta_row = lax.broadcasted_iota(jnp.int32, (N_GATE, seq_len), 0) < GDN_HEADS
    swap = pltpu.roll(e_gc, GDN_HEADS, axis=0)
    row_ref[0] = gc
    row_ref[1] = e_gc
    col_ref[...] = jnp.concatenate([
        jnp.where(is_beta_row, beta, gc),
        jnp.where(is_beta_row, log_beta, e_gc),
        jnp.where(is_beta_row, beta * swap, jnp.exp(g_after))], axis=0).T

    def cat(x):
        out = x[(n_chunks - 1) * CHUNK:, :]
        for ci in range(n_chunks - 2, -1, -1):
            out = jnp.where(lane_chunk == ci * CHUNK, x[ci * CHUNK:(ci + 1) * CHUNK, :], out)
        return out

    def block_diag(x_cat):
        xb = x_cat.astype(BF16)
        zero = jnp.zeros((CHUNK, LANES), BF16)
        row_blocks = []
        for ci in range(n_chunks):
            g = ci * CHUNK // LANES
            grp = slice(g * LANES, (g + 1) * LANES)
            part = xb[:, grp] * bd_ref[ci * CHUNK:(ci + 1) * CHUNK, grp]
            row_blocks.append(jnp.concatenate(
                [part if j == g else zero for j in range(SC // LANES)], axis=1))
        return jnp.concatenate(row_blocks, axis=0)

    def head_cols(col, h):
        return slice((col + h) * HEAD_DIM, (col + h + 1) * HEAD_DIM)

    def conv_silu(sc, cols):
        if sc == 0:
            x = jnp.concatenate([jnp.zeros((HALO, HEAD_DIM), F32),
                                 act_ref[0:SC, cols].astype(F32)], axis=0)
        else:
            x = act_ref[sc * SC - HALO:(sc + 1) * SC, cols].astype(F32)
        y = None
        for s in range(CONV_K):
            term = x[HALO - s:HALO - s + SC, :] * cw_ref[CONV_K - 1 - s:CONV_K - s, cols]
            y = term if y is None else y + term
        return _silu(y)

    def prepare(steps):
        causal, neg_strict, eye = mask_ref[0], mask_ref[1], mask_ref[2]
        units = [(sc, h) for sc in steps for h in heads]
        n_pow, a_cat, rhs, q_dec, k_dec, decay_last = [], [], [], [], [], []
        for sc, h in units:
            rows = slice(sc * SC, (sc + 1) * SC)
            col = lambda j: col_ref[rows, j:j + 1]
            q = conv_silu(sc, head_cols(COL_GQ, h))
            k = conv_silu(sc, head_cols(COL_GK, h))
            v = conv_silu(sc, head_cols(COL_GV, h))
            q = q * (lax.rsqrt(jnp.sum(q * q, axis=-1, keepdims=True) + 1e-6) * scale)
            k = k * lax.rsqrt(jnp.sum(k * k, axis=-1, keepdims=True) + 1e-6)
            hb, hg = h, GDN_HEADS + h
            kb = k.astype(BF16)
            gram = _dot_nt(jnp.concatenate([kb, q.astype(BF16)], axis=0), kb)
            e_cat = cat(col(hg)) - row_ref[0, hg:hg + 1, rows] + causal
            n_pow.append(cat(gram[:SC]) * jnp.exp(e_cat + cat(col(N_GATE + hb))) * neg_strict)
            a_cat.append(cat(gram[SC:]) * jnp.exp(e_cat))
            rhs.append(jnp.concatenate([col(hb) * v, col(2 * N_GATE + hb) * k], axis=1).astype(BF16))
            q_dec.append(q * col(N_GATE + hg))
            kd = (k * col(2 * N_GATE + hg)).astype(BF16)
            k_dec.append([kd[ci * CHUNK:(ci + 1) * CHUNK, :] for ci in range(n_chunks)])
            decay_last.append([row_ref[1, hg:hg + 1, sc * SC + (ci + 1) * CHUNK - 1:sc * SC + (ci + 1) * CHUNK]
                               for ci in range(n_chunks)])

        every = range(len(units))
        t_cat = [eye + n_pow[u] for u in every]
        n_pow = [_dot(n_pow[u].astype(BF16), block_diag(n_pow[u])) for u in every]
        span = 2
        while span < CHUNK // 2:
            both = [_dot(jnp.concatenate([t_cat[u], n_pow[u]], axis=0).astype(BF16), block_diag(n_pow[u]))
                    for u in every]
            t_cat = [t_cat[u] + both[u][:CHUNK] for u in every]
            n_pow = [both[u][CHUNK:] for u in every]
            span *= 2
        t_cat = [t_cat[u] + _dot(t_cat[u].astype(BF16), block_diag(n_pow[u])) for u in every]

        uw = [_dot(block_diag(t_cat[u]), rhs[u]) for u in every]
        kt = [[_dot_tn(k_dec[u][ci], uw[u][ci * CHUNK:(ci + 1) * CHUNK, :].astype(BF16))
               for ci in range(n_chunks)] for u in every]
        per_step = lambda xs, i: xs[i * GDN_HEADS:(i + 1) * GDN_HEADS]
        return [tuple(per_step(xs, i) for xs in (uw, kt, q_dec, a_cat, decay_last))
                for i in range(len(steps))]

    def finish(sc, prep, state):
        rows = slice(sc * SC, (sc + 1) * SC)
        uw, kt, q_dec, a_cat, decay_last = prep
        state = list(state)
        state_in = [[None] * n_chunks for _ in heads]
        for ci in range(n_chunks):
            for h in heads:
                sb = state[h].astype(BF16)
                state_in[h][ci] = sb
                state[h] = (state[h] * decay_last[h][ci]
                            - _dot(kt[h][ci][:, HEAD_DIM:].astype(BF16), sb) + kt[h][ci][:, :HEAD_DIM])
        for h in heads:
            inter = []
            for ci in range(n_chunks):
                cr = slice(ci * CHUNK, (ci + 1) * CHUNK)
                lhs = jnp.concatenate([uw[h][cr, HEAD_DIM:], q_dec[h][cr, :]], axis=0).astype(BF16)
                inter.append(_dot(lhs, state_in[h][ci]))
            v_new = jnp.concatenate([uw[h][ci * CHUNK:(ci + 1) * CHUNK, :HEAD_DIM] - inter[ci][:CHUNK]
                                     for ci in range(n_chunks)], axis=0).astype(BF16)
            o = (jnp.concatenate([inter[ci][CHUNK:] for ci in range(n_chunks)], axis=0)
                 + _dot(block_diag(a_cat[h]), v_new))
            o = o * lax.rsqrt(jnp.mean(o * o, axis=-1, keepdims=True) + NORM_EPS)
            z = act_ref[rows, head_cols(COL_GZ, h)].astype(F32)
            o_ref[rows, head_cols(0, h)] = (o * (_silu(z) * gnw_ref[...])).astype(BF16)
        return state

    n_groups = seq_len // (SC * GDN_GROUP)
    group = lambda g: list(range(g * GDN_GROUP, (g + 1) * GDN_GROUP))
    state = [jnp.zeros((HEAD_DIM, HEAD_DIM), F32) for _ in heads]
    preps = prepare(group(0))
    for g in range(n_groups):
        nxt = prepare(group(g + 1)) if g + 1 < n_groups else None
        for sc, prep in zip(group(g), preps):
            state = finish(sc, prep, state)
        preps = nxt


def _gdn(act, gate_t, alog_s, dtb_s, conv_w, gdn_norm_w, batch, seq_len):
    t = act.shape[0]
    small = lambda shape: pl.BlockSpec(shape, lambda b: (0, 0))
    return pl.pallas_call(
        _gdn_kernel,
        grid=(batch,),
        in_specs=[
            pl.BlockSpec((seq_len, 4 * GDN_WIDTH), lambda b: (b, 0)),
            pl.BlockSpec((N_GATE, seq_len), lambda b: (0, b)),
            small((N_GATE, 1)), small((N_GATE, 1)), small((CONV_K, N_CONV)),
            small((1, HEAD_DIM)),
        ],
        out_specs=pl.BlockSpec((seq_len, GDN_WIDTH), lambda b: (b, 0)),
        out_shape=jax.ShapeDtypeStruct((t, GDN_WIDTH), BF16),
        scratch_shapes=[
            pltpu.VMEM((3, CHUNK, SC), F32),
            pltpu.VMEM((SC, SC), BF16),
            pltpu.VMEM((2, N_GATE, seq_len), F32),
            pltpu.VMEM((seq_len, 3 * N_GATE), F32),
        ],
        compiler_params=pltpu.CompilerParams(
            dimension_semantics=("arbitrary",), vmem_limit_bytes=VMEM_LIMIT),
        name="gdn",
    )(act, gate_t, alog_s, dtb_s, conv_w, gdn_norm_w)


def _split3(x):
    hi = x.astype(BF16).astype(F32)
    r1 = x - hi
    mid = r1.astype(BF16).astype(F32)
    return hi, mid, r1 - mid


def _attn_kernel(lam_init, q_ref, k_ref, vt_ref, z_ref, lq1_ref, lk1_ref, lq2_ref, lk2_ref,
                 nw_ref, o_ref, bias_ref, kext_ref, vtx_ref, acc_ref, m_ref):
    h = pl.program_id(0)
    b = pl.program_id(1)
    seq_len = k_ref.shape[0]
    lane = lax.broadcasted_iota(jnp.int32, (1, HEAD_DIM), 1)
    n_bias = 6
    log2e = math.log2(math.e)
    assert BQ == 2 * BK

    @pl.when(b == 0)
    def _():
        slope = lax.shift_left(jnp.int32(1), 2 * (DIFF_HEADS - 1 - h)).astype(F32) * (log2e / 256.0)
        pos = lax.broadcasted_iota(jnp.int32, (seq_len, 1), 0)
        in_block = (pos & (BK - 1)).astype(F32) * slope
        block_off = (pos & -BK).astype(F32) * slope
        terms = _split3(in_block) + _split3(block_off)
        tile = jnp.zeros((seq_len, HEAD_DIM), F32)
        for j, term in enumerate(terms):
            tile = jnp.where((lane == j) | (lane == DIFF_QK_DIM + j), term, tile)
        bias_ref[...] = tile
        ones_row = lax.broadcasted_iota(jnp.int32, (V_PAD, seq_len), 0) == 0
        vtx_ref[HEAD_DIM:, :] = jnp.where(ones_row, 1.0, 0.0).astype(BF16)

    k = k_ref[...].astype(F32)
    kext_ref[0] = jnp.where(lane < DIFF_QK_DIM, k, bias_ref[...]).astype(BF16)
    kext_ref[1] = jnp.where(lane >= DIFF_QK_DIM, k, bias_ref[...]).astype(BF16)
    vtx_ref[0:HEAD_DIM, :] = vt_ref[...]

    lam = (jnp.exp(jnp.sum(lq1_ref[...] * lk1_ref[...], axis=-1, keepdims=True))
           - jnp.exp(jnp.sum(lq2_ref[...] * lk2_ref[...], axis=-1, keepdims=True)) + lam_init)

    def scores(q_ext, kb, masked, lo):
        rows = slice(kb * BK, (kb + 1) * BK)
        out = []
        for c in range(2):
            s = _dot_nt(kext_ref[c, rows, :], q_ext[c][lo:, :])
            if masked:
                krel = lax.broadcasted_iota(jnp.int32, (BK, BQ - lo), 0)
                qrel = lax.broadcasted_iota(jnp.int32, (BK, BQ - lo), 1)
                s = jnp.where(krel <= qrel, s, MASK_VALUE)
            out.append((s, jnp.max(s, axis=0, keepdims=True)))
        return out

    def accumulate(par, kb, first, second, lo):
        vt = vtx_ref[:, kb * BK:(kb + 2) * BK]
        for c in range(2):
            (s_a, max_a), (s_b, max_b) = first[c], second[c]
            m_old = m_ref[par, c]
            if lo:
                max_b = jnp.concatenate([jnp.full((1, lo), MASK_VALUE, F32), max_b], axis=1)
            m_new = jnp.maximum(m_old, jnp.maximum(max_a, max_b))
            alpha = jnp.exp2(m_old - m_new)
            p_a = jnp.exp2(s_a - m_new).astype(BF16)
            p_b = jnp.exp2(s_b - m_new[:, lo:]).astype(BF16)
            both = _dot(vt, jnp.concatenate([p_a[:, lo:], p_b], axis=0))
            if lo:
                both = jnp.concatenate([_dot(vt[:, :BK], p_a[:, :lo]), both], axis=1)
            acc_ref[par, c] = alpha * acc_ref[par, c] + both
            m_ref[par, c] = m_new

    for qi in range(seq_len // BQ):
        par = qi % 2
        qrows = slice(qi * BQ, (qi + 1) * BQ)
        q = q_ref[qrows, :].astype(F32) * (DIFF_QK_DIM ** -0.5 * log2e)
        q_ext = (
            jnp.where(lane < DIFF_QK_DIM, q,
                      jnp.where(lane < DIFF_QK_DIM + n_bias, 1.0, 0.0)).astype(BF16),
            jnp.where(lane >= DIFF_QK_DIM, q, jnp.where(lane < n_bias, 1.0, 0.0)).astype(BF16),
        )
        m_ref[par] = jnp.full(m_ref.shape[1:], MASK_VALUE, F32)
        acc_ref[par] = jnp.zeros(acc_ref.shape[1:], F32)
        n_pairs = qi + 1
        lo_of = lambda j: BK if j == n_pairs - 1 else 0

        def pair_scores(j):
            diag = j == n_pairs - 1
            return (scores(q_ext, 2 * j, diag, 0), scores(q_ext, 2 * j + 1, diag, lo_of(j)))

        cur = pair_scores(0)
        for j in range(n_pairs):
            nxt = pair_scores(j + 1) if j + 1 < n_pairs else None
            accumulate(par, 2 * j, cur[0], cur[1], lo_of(j))
            cur = nxt

        num = (acc_ref[par, 0, 0:HEAD_DIM, :], acc_ref[par, 1, 0:HEAD_DIM, :])
        den = (acc_ref[par, 0, HEAD_DIM:HEAD_DIM + 1, :], acc_ref[par, 1, HEAD_DIM:HEAD_DIM + 1, :])
        o_t = num[0] / den[0] - lam * (num[1] / den[1])
        o = o_t.T
        o = o * lax.rsqrt(jnp.mean(o * o, axis=-1, keepdims=True) + NORM_EPS)
        o = o * nw_ref[...] * (1.0 - lam_init)
        o_ref[qrows, :] = (o * _silu(z_ref[qrows, :].astype(F32))).astype(BF16)


def _attn(act, v_t, lq1, lk1, lq2, lk2, diff_norm_w, lam_init, batch, seq_len):
    t = act.shape[0]
    small = lambda shape: pl.BlockSpec(shape, lambda h, b: (0, 0))
    return pl.pallas_call(
        functools.partial(_attn_kernel, lam_init),
        grid=(DIFF_HEADS, batch),
        in_specs=[
            pl.BlockSpec((seq_len, HEAD_DIM), lambda h, b: (b, COL_DQ + h)),
            pl.BlockSpec((seq_len, HEAD_DIM), lambda h, b: (b, COL_DK + h)),
            pl.BlockSpec((HEAD_DIM, seq_len), lambda h, b: (h, b)),
            pl.BlockSpec((seq_len, HEAD_DIM), lambda h, b: (b, COL_DZ + h)),
            small((1, DIFF_QK_DIM)), small((1, DIFF_QK_DIM)),
            small((1, DIFF_QK_DIM)), small((1, DIFF_QK_DIM)),
            small((1, HEAD_DIM)),
        ],
        out_specs=pl.BlockSpec((seq_len, HEAD_DIM), lambda h, b: (b, h)),
        out_shape=jax.ShapeDtypeStruct((t, DIFF_WIDTH), BF16),
        scratch_shapes=[
            pltpu.VMEM((seq_len, HEAD_DIM), F32),
            pltpu.VMEM((2, seq_len, HEAD_DIM), BF16),
            pltpu.VMEM((HEAD_DIM + V_PAD, seq_len), BF16),
            pltpu.VMEM((2, 2, HEAD_DIM + V_PAD, BQ), F32),
            pltpu.VMEM((2, 2, 1, BQ), F32),
        ],
        compiler_params=pltpu.CompilerParams(
            dimension_semantics=("arbitrary", "arbitrary"), vmem_limit_bytes=VMEM_LIMIT),
        name="diffattn",
    )(act, act, v_t, act, lq1, lk1, lq2, lk2, diff_norm_w)


def _outproj_kernel(oa_ref, ob_ref, x_ref, w_ref, fw_ref, out_ref):
    mix = _dot(oa_ref[...], w_ref[0:GDN_WIDTH, :]) + _dot(ob_ref[...], w_ref[GDN_WIDTH:, :])
    y = x_ref[...] + mix
    ms = jnp.mean(y * y, axis=-1, keepdims=True)
    out_ref[...] = y * lax.rsqrt(ms + NORM_EPS) * fw_ref[...]


def _outproj(o_a, o_b, xf, w_out, final_norm_w):
    t = xf.shape[0]
    return pl.pallas_call(
        _outproj_kernel,
        grid=(t // TM_OUT,),
        in_specs=[
            pl.BlockSpec((TM_OUT, GDN_WIDTH), lambda i: (i, 0)),
            pl.BlockSpec((TM_OUT, DIFF_WIDTH), lambda i: (i, 0)),
            pl.BlockSpec((TM_OUT, D_MODEL), lambda i: (i, 0)),
            pl.BlockSpec((D_MODEL, D_MODEL), lambda i: (0, 0)),
            pl.BlockSpec((1, D_MODEL), lambda i: (0, 0)),
        ],
        out_specs=pl.BlockSpec((TM_OUT, D_MODEL), lambda i: (i, 0)),
        out_shape=jax.ShapeDtypeStruct((t, D_MODEL), F32),
        compiler_params=pltpu.CompilerParams(
            dimension_semantics=("arbitrary",), vmem_limit_bytes=VMEM_LIMIT),
        name="outproj",
    )(o_a, o_b, xf, w_out, final_norm_w)


def kernel(x, norm_w, w_in, conv_w, a_log, dt_bias, gdn_norm_w, lambda_q1, lambda_k1,
           lambda_q2, lambda_k2, diff_norm_w, w_out, final_norm_w):
    batch, seq_len, d_model = x.shape
    depth = norm_w.shape[0]
    assert depth == 1 and d_model == D_MODEL
    assert seq_len % TM_IN == 0 and seq_len % SC == 0 and seq_len % BQ == 0 and BQ % BK == 0
    n_wide_a = 4 * GDN_WIDTH
    xf = x.reshape(batch * seq_len, d_model)

    w = w_in[0]
    c_dq = n_wide_a + N_GATE
    c_dv = c_dq + 2 * DIFF_WIDTH
    c_dz = c_dv + DIFF_WIDTH
    w_main = jnp.concatenate([w[:, :n_wide_a], w[:, c_dq:c_dv], w[:, c_dz:]], axis=1).astype(BF16)
    w_v_t = w[:, c_dv:c_dz].T.astype(BF16)
    w_gate_t = w[:, n_wide_a:n_wide_a + N_GATE].T.astype(BF16)

    act, gate_t, v_t = _inproj(xf, norm_w[0][None, :], w_main, w_gate_t, w_v_t)

    pad_s = lambda vec: jnp.pad(vec, (GDN_HEADS, 0))[:, None]
    o_a = _gdn(act, gate_t, pad_s(a_log[0]), pad_s(dt_bias[0]), conv_w[0], gdn_norm_w[0][None, :],
               batch, seq_len)

    lam_init = 0.8 - 0.6 * math.exp(-0.3 * 0)
    o_b = _attn(act, v_t, lambda_q1[0][None, :], lambda_k1[0][None, :], lambda_q2[0][None, :],
                lambda_k2[0][None, :], diff_norm_w[0][None, :], lam_init, batch, seq_len)

    out = _outproj(o_a, o_b, xf, w_out[0].astype(BF16), final_norm_w[None, :])
    return out.reshape(batch, seq_len, d_model)
```

```python
import functools
import math

import jax
import jax.numpy as jnp
from jax import lax
from jax.experimental import pallas as pl
from jax.experimental.pallas import tpu as pltpu

F32 = jnp.float32
BF16 = jnp.bfloat16

D_MODEL = 1024
GDN_HEADS = 4
HEAD_DIM = 128
GDN_WIDTH = GDN_HEADS * HEAD_DIM
DIFF_HEADS = 4
DIFF_QK_DIM = 64
DIFF_WIDTH = DIFF_HEADS * HEAD_DIM
CONV_K = 4
NORM_EPS = 1e-6
N_GATE = 2 * GDN_HEADS
N_MAIN = 4 * GDN_WIDTH + 3 * DIFF_WIDTH
N_CONV = 3 * GDN_WIDTH
N_PLAIN = N_MAIN - N_CONV
LANES = 128
MASK_VALUE = -1e30

COL_GQ, COL_GK, COL_GV = 0, 4, 8
COL_GZ, COL_DQ, COL_DK, COL_DZ = 0, 4, 8, 12

TM_IN = 1024
NC_IN = 256
HALO = 16
TM_OUT = 1024
SC = 256
CHUNK = 64
GDN_GROUP = 2
BQ = 512
BK = 256
V_PAD = 16
VMEM_LIMIT = 48 * 1024 * 1024


def _sigmoid(x):
    return 1.0 / (1.0 + jnp.exp(-x))


def _silu(x):
    half = 0.5 * x
    return half + half * jnp.tanh(half)


def _softplus(x):
    return jnp.maximum(x, 0.0) + jnp.log1p(jnp.exp(-jnp.abs(x)))


def _dot(a, b):
    return jnp.dot(a, b, preferred_element_type=F32)


def _dot_nt(a, b):
    return lax.dot_general(a, b, (((1,), (1,)), ((), ())), preferred_element_type=F32)


def _dot_tn(a, b):
    return lax.dot_general(a, b, (((0,), (0,)), ((), ())), preferred_element_type=F32)


def _inproj_kernel(n_tiles, tiles_per_seq, x_ref, nw_ref, w_ref, wgt_ref, wvt_ref, cw_ref,
                   conv_ref, plain_ref, gate_t_ref, vt_ref, pre_ref):
    i = pl.program_id(0)
    slot = i % 2
    prev = 1 - slot
    tm = x_ref.shape[0]

    @pl.when(i == 0)
    def _():
        pre_ref[1] = jnp.zeros(pre_ref.shape[1:], pre_ref.dtype)

    def conv_silu(n0):
        cols = slice(n0, n0 + LANES)
        x = pre_ref[prev, :, cols].astype(F32)
        y = None
        for s in range(CONV_K):
            term = x[HALO - s:HALO - s + tm, :] * cw_ref[CONV_K - 1 - s:CONV_K - s, cols]
            y = term if y is None else y + term
        conv_ref[:, cols] = _silu(y).astype(BF16)

    pieces = list(range(0, N_CONV, LANES))

    @pl.when(i < n_tiles)
    def _():
        x = x_ref[...]
        ms = jnp.mean(x * x, axis=-1, keepdims=True)
        h = (x * lax.rsqrt(ms + NORM_EPS) * nw_ref[...]).astype(BF16)
        hist = pre_ref[prev, tm:tm + HALO, :]
        pre_ref[slot, 0:HALO, :] = jnp.where(i % tiles_per_seq == 0, jnp.zeros_like(hist), hist)

        gate_t_ref[...] = _dot_nt(wgt_ref[...], h)
        vt_ref[...] = _dot_nt(wvt_ref[...], h).astype(BF16)
        todo = list(pieces)
        for n0 in range(0, N_MAIN, NC_IN):
            acc = _dot(h, w_ref[:, n0:n0 + NC_IN]).astype(BF16)
            if n0 < N_CONV:
                pre_ref[slot, HALO:, n0:n0 + NC_IN] = acc
            else:
                plain_ref[:, n0 - N_CONV:n0 - N_CONV + NC_IN] = acc
            if todo:
                conv_silu(todo.pop(0))
        for n0 in todo:
            conv_silu(n0)

    @pl.when(i == n_tiles)
    def _():
        for n0 in pieces:
            conv_silu(n0)


def _inproj(xf, norm_w, w_main, w_gate_t, w_v_t, conv_w, seq_len):
    t = xf.shape[0]
    n_tiles = t // TM_IN
    cur = lambda i: jnp.minimum(i, n_tiles - 1)
    return pl.pallas_call(
        functools.partial(_inproj_kernel, n_tiles, seq_len // TM_IN),
        grid=(n_tiles + 1,),
        in_specs=[
            pl.BlockSpec((TM_IN, D_MODEL), lambda i: (cur(i), 0)),
            pl.BlockSpec((1, D_MODEL), lambda i: (0, 0)),
            pl.BlockSpec((D_MODEL, N_MAIN), lambda i: (0, 0)),
            pl.BlockSpec((N_GATE, D_MODEL), lambda i: (0, 0)),
            pl.BlockSpec((DIFF_WIDTH, D_MODEL), lambda i: (0, 0)),
            pl.BlockSpec((CONV_K, N_CONV), lambda i: (0, 0)),
        ],
        out_specs=[
            pl.BlockSpec((TM_IN, N_CONV), lambda i: (jnp.maximum(i - 1, 0), 0)),
            pl.BlockSpec((TM_IN, N_PLAIN), lambda i: (cur(i), 0)),
            pl.BlockSpec((N_GATE, TM_IN), lambda i: (0, cur(i))),
            pl.BlockSpec((DIFF_WIDTH, TM_IN), lambda i: (0, cur(i))),
        ],
        out_shape=[
            jax.ShapeDtypeStruct((t, N_CONV), BF16),
            jax.ShapeDtypeStruct((t, N_PLAIN), BF16),
            jax.ShapeDtypeStruct((N_GATE, t), F32),
            jax.ShapeDtypeStruct((DIFF_WIDTH, t), BF16),
        ],
        scratch_shapes=[pltpu.VMEM((2, HALO + TM_IN, N_CONV), BF16)],
        compiler_params=pltpu.CompilerParams(
            dimension_semantics=("arbitrary",), vmem_limit_bytes=VMEM_LIMIT),
        name="inproj",
    )(xf, norm_w, w_main, w_gate_t, w_v_t, conv_w)


def _gdn_kernel(act_ref, z_ref, gate_t_ref, alog_s_ref, dtb_s_ref, gnw_ref, o_ref, mask_ref, bd_ref,
                row_ref, col_ref):
    seq_len = act_ref.shape[0]
    n_chunks = SC // CHUNK
    heads = range(GDN_HEADS)
    i_cat = lax.broadcasted_iota(jnp.int32, (CHUNK, SC), 0)
    j_cat = lax.broadcasted_iota(jnp.int32, (CHUNK, SC), 1) & (CHUNK - 1)
    mask_ref[0] = jnp.where(i_cat >= j_cat, 0.0, MASK_VALUE)
    mask_ref[1] = jnp.where(i_cat > j_cat, -1.0, 0.0)
    mask_ref[2] = jnp.where(i_cat == j_cat, 1.0, 0.0)
    r = lax.broadcasted_iota(jnp.int32, (SC, SC), 0)
    c = lax.broadcasted_iota(jnp.int32, (SC, SC), 1)
    bd_ref[...] = jnp.where((r & -CHUNK) == (c & -CHUNK), 1.0, 0.0).astype(BF16)
    lane_chunk = lax.broadcasted_iota(jnp.int32, (1, SC), 1) & -CHUNK

    scale = HEAD_DIM ** -0.5

    lane_in_chunk = lax.broadcasted_iota(jnp.int32, (N_GATE, seq_len), 1) & (CHUNK - 1)
    gate_t = gate_t_ref[...]
    beta = _sigmoid(gate_t)
    log_beta = -_softplus(-gate_t)
    g_step = -jnp.exp(alog_s_ref[...]) * _softplus(gate_t + dtb_s_ref[...])
    gc = g_step
    g_after = jnp.zeros_like(g_step)
    tail = g_step
    step = 1
    while step < CHUNK:
        gc = gc + jnp.where(lane_in_chunk >= step, pltpu.roll(gc, step, axis=1), 0.0)
        ahead = jnp.where(lane_in_chunk + step < CHUNK, pltpu.roll(tail, seq_len - step, axis=1), 0.0)
        g_after = g_after + ahead
        tail = tail + ahead
        step *= 2
    e_gc = jnp.exp(gc)
    is_beta_row = lax.broadcasted_iota(jnp.int32, (N_GATE, seq_len), 0) < GDN_HEADS
    swap = pltpu.roll(e_gc, GDN_HEADS, axis=0)
    row_ref[0] = gc
    row_ref[1] = e_gc
    col_ref[...] = jnp.concatenate([
        jnp.where(is_beta_row, beta, gc),
        jnp.where(is_beta_row, log_beta, e_gc),
        jnp.where(is_beta_row, beta * swap, jnp.exp(g_after))], axis=0).T

    def cat(x):
        out = x[(n_chunks - 1) * CHUNK:, :]
        for ci in range(n_chunks - 2, -1, -1):
            out = jnp.where(lane_chunk == ci * CHUNK, x[ci * CHUNK:(ci + 1) * CHUNK, :], out)
        return out

    def block_diag(x_cat):
        xb = x_cat.astype(BF16)
        zero = jnp.zeros((CHUNK, LANES), BF16)
        row_blocks = []
        for ci in range(n_chunks):
            g = ci * CHUNK // LANES
            grp = slice(g * LANES, (g + 1) * LANES)
            part = xb[:, grp] * bd_ref[ci * CHUNK:(ci + 1) * CHUNK, grp]
            row_blocks.append(jnp.concatenate(
                [part if j == g else zero for j in range(SC // LANES)], axis=1))
        return jnp.concatenate(row_blocks, axis=0)

    def head_cols(col, h):
        return slice((col + h) * HEAD_DIM, (col + h + 1) * HEAD_DIM)

    def prepare(steps):
        causal, neg_strict, eye = mask_ref[0], mask_ref[1], mask_ref[2]
        units = [(sc, h) for sc in steps for h in heads]
        n_pow, a_cat, rhs, q_dec, k_dec, decay_last = [], [], [], [], [], []
        for sc, h in units:
            rows = slice(sc * SC, (sc + 1) * SC)
            col = lambda j: col_ref[rows, j:j + 1]
            q = act_ref[rows, head_cols(COL_GQ, h)].astype(F32)
            k = act_ref[rows, head_cols(COL_GK, h)].astype(F32)
            v = act_ref[rows, head_cols(COL_GV, h)].astype(F32)
            q = q * (lax.rsqrt(jnp.sum(q * q, axis=-1, keepdims=True) + 1e-6) * scale)
            k = k * lax.rsqrt(jnp.sum(k * k, axis=-1, keepdims=True) + 1e-6)
            hb, hg = h, GDN_HEADS + h
            kb = k.astype(BF16)
            gram = _dot_nt(jnp.concatenate([kb, q.astype(BF16)], axis=0), kb)
            e_cat = cat(col(hg)) - row_ref[0, hg:hg + 1, rows] + causal
            n_pow.append(cat(gram[:SC]) * jnp.exp(e_cat + cat(col(N_GATE + hb))) * neg_strict)
            a_cat.append(cat(gram[SC:]) * jnp.exp(e_cat))
            rhs.append(jnp.concatenate([col(hb) * v, col(2 * N_GATE + hb) * k], axis=1).astype(BF16))
            q_dec.append(q * col(N_GATE + hg))
            kd = (k * col(2 * N_GATE + hg)).astype(BF16)
            k_dec.append([kd[ci * CHUNK:(ci + 1) * CHUNK, :] for ci in range(n_chunks)])
            decay_last.append([row_ref[1, hg:hg + 1, sc * SC + (ci + 1) * CHUNK - 1:sc * SC + (ci + 1) * CHUNK]
                               for ci in range(n_chunks)])

        every = range(len(units))
        t_cat = [eye + n_pow[u] for u in every]
        n_pow = [_dot(n_pow[u].astype(BF16), block_diag(n_pow[u])) for u in every]
        span = 2
        while span < CHUNK // 2:
            both = [_dot(jnp.concatenate([t_cat[u], n_pow[u]], axis=0).astype(BF16), block_diag(n_pow[u]))
                    for u in every]
            t_cat = [t_cat[u] + both[u][:CHUNK] for u in every]
            n_pow = [both[u][CHUNK:] for u in every]
            span *= 2
        t_cat = [t_cat[u] + _dot(t_cat[u].astype(BF16), block_diag(n_pow[u])) for u in every]

        uw = [_dot(block_diag(t_cat[u]), rhs[u]) for u in every]
        kt = [[_dot_tn(k_dec[u][ci], uw[u][ci * CHUNK:(ci + 1) * CHUNK, :].astype(BF16))
               for ci in range(n_chunks)] for u in every]
        per_step = lambda xs, i: xs[i * GDN_HEADS:(i + 1) * GDN_HEADS]
        return [tuple(per_step(xs, i) for xs in (uw, kt, q_dec, a_cat, decay_last))
                for i in range(len(steps))]

    def finish(sc, prep, state):
        rows = slice(sc * SC, (sc + 1) * SC)
        uw, kt, q_dec, a_cat, decay_last = prep
        state = list(state)
        state_in = [[None] * n_chunks for _ in heads]
        for ci in range(n_chunks):
            for h in heads:
                sb = state[h].astype(BF16)
                state_in[h][ci] = sb
                state[h] = (state[h] * decay_last[h][ci]
                            - _dot(kt[h][ci][:, HEAD_DIM:].astype(BF16), sb) + kt[h][ci][:, :HEAD_DIM])
        for h in heads:
            inter = []
            for ci in range(n_chunks):
                cr = slice(ci * CHUNK, (ci + 1) * CHUNK)
                lhs = jnp.concatenate([uw[h][cr, HEAD_DIM:], q_dec[h][cr, :]], axis=0).astype(BF16)
                inter.append(_dot(lhs, state_in[h][ci]))
            v_new = jnp.concatenate([uw[h][ci * CHUNK:(ci + 1) * CHUNK, :HEAD_DIM] - inter[ci][:CHUNK]
                                     for ci in range(n_chunks)], axis=0).astype(BF16)
            o = (jnp.concatenate([inter[ci][CHUNK:] for ci in range(n_chunks)], axis=0)
                 + _dot(block_diag(a_cat[h]), v_new))
            o = o * lax.rsqrt(jnp.mean(o * o, axis=-1, keepdims=True) + NORM_EPS)
            z = z_ref[rows, head_cols(COL_GZ, h)].astype(F32)
            o_ref[rows, head_cols(0, h)] = (o * (_silu(z) * gnw_ref[...])).astype(BF16)
        return state

    n_groups = seq_len // (SC * GDN_GROUP)
    group = lambda g: list(range(g * GDN_GROUP, (g + 1) * GDN_GROUP))
    state = [jnp.zeros((HEAD_DIM, HEAD_DIM), F32) for _ in heads]
    preps = prepare(group(0))
    for g in range(n_groups):
        nxt = prepare(group(g + 1)) if g + 1 < n_groups else None
        for sc, prep in zip(group(g), preps):
            state = finish(sc, prep, state)
        preps = nxt


def _gdn(act, plain, gate_t, alog_s, dtb_s, gdn_norm_w, batch, seq_len):
    t = act.shape[0]
    small = lambda shape: pl.BlockSpec(shape, lambda b: (0, 0))
    return pl.pallas_call(
        _gdn_kernel,
        grid=(batch,),
        in_specs=[
            pl.BlockSpec((seq_len, N_CONV), lambda b: (b, 0)),
            pl.BlockSpec((seq_len, GDN_WIDTH), lambda b: (b, 0)),
            pl.BlockSpec((N_GATE, seq_len), lambda b: (0, b)),
            small((N_GATE, 1)), small((N_GATE, 1)),
            small((1, HEAD_DIM)),
        ],
        out_specs=pl.BlockSpec((seq_len, GDN_WIDTH), lambda b: (b, 0)),
        out_shape=jax.ShapeDtypeStruct((t, GDN_WIDTH), BF16),
        scratch_shapes=[
            pltpu.VMEM((3, CHUNK, SC), F32),
            pltpu.VMEM((SC, SC), BF16),
            pltpu.VMEM((2, N_GATE, seq_len), F32),
            pltpu.VMEM((seq_len, 3 * N_GATE), F32),
        ],
        compiler_params=pltpu.CompilerParams(
            dimension_semantics=("arbitrary",), vmem_limit_bytes=VMEM_LIMIT),
        name="gdn",
    )(act, plain, gate_t, alog_s, dtb_s, gdn_norm_w)


def _split3(x):
    hi = x.astype(BF16).astype(F32)
    r1 = x - hi
    mid = r1.astype(BF16).astype(F32)
    return hi, mid, r1 - mid


def _attn_kernel(lam_init, q_ref, k_ref, vt_ref, z_ref, lq1_ref, lk1_ref, lq2_ref, lk2_ref,
                 nw_ref, o_ref, bias_ref, kext_ref, vtx_ref, acc_ref, m_ref):
    h = pl.program_id(0)
    b = pl.program_id(1)
    seq_len = k_ref.shape[0]
    lane = lax.broadcasted_iota(jnp.int32, (1, HEAD_DIM), 1)
    n_bias = 6
    log2e = math.log2(math.e)
    assert BQ == 2 * BK

    @pl.when(b == 0)
    def _():
        slope = lax.shift_left(jnp.int32(1), 2 * (DIFF_HEADS - 1 - h)).astype(F32) * (log2e / 256.0)
        pos = lax.broadcasted_iota(jnp.int32, (seq_len, 1), 0)
        in_block = (pos & (BK - 1)).astype(F32) * slope
        block_off = (pos & -BK).astype(F32) * slope
        terms = _split3(in_block) + _split3(block_off)
        tile = jnp.zeros((seq_len, HEAD_DIM), F32)
        for j, term in enumerate(terms):
            tile = jnp.where((lane == j) | (lane == DIFF_QK_DIM + j), term, tile)
        bias_ref[...] = tile
        ones_row = lax.broadcasted_iota(jnp.int32, (V_PAD, seq_len), 0) == 0
        vtx_ref[HEAD_DIM:, :] = jnp.where(ones_row, 1.0, 0.0).astype(BF16)

    k = k_ref[...].astype(F32)
    kext_ref[0] = jnp.where(lane < DIFF_QK_DIM, k, bias_ref[...]).astype(BF16)
    kext_ref[1] = jnp.where(lane >= DIFF_QK_DIM, k, bias_ref[...]).astype(BF16)
    vtx_ref[0:HEAD_DIM, :] = vt_ref[...]

    lam = (jnp.exp(jnp.sum(lq1_ref[...] * lk1_ref[...], axis=-1, keepdims=True))
           - jnp.exp(jnp.sum(lq2_ref[...] * lk2_ref[...], axis=-1, keepdims=True)) + lam_init)

    def scores(q_ext, kb, masked, lo):
        rows = slice(kb * BK, (kb + 1) * BK)
        out = []
        for c in range(2):
            s = _dot_nt(kext_ref[c, rows, :], q_ext[c][lo:, :])
            if masked:
                krel = lax.broadcasted_iota(jnp.int32, (BK, BQ - lo), 0)
                qrel = lax.broadcasted_iota(jnp.int32, (BK, BQ - lo), 1)
                s = jnp.where(krel <= qrel, s, MASK_VALUE)
            out.append((s, jnp.max(s, axis=0, keepdims=True)))
        return out

    def accumulate(par, kb, first, second, lo):
        vt = vtx_ref[:, kb * BK:(kb + 2) * BK]
        for c in range(2):
            (s_a, max_a), (s_b, max_b) = first[c], second[c]
            m_old = m_ref[par, c]
            if lo:
                max_b = jnp.concatenate([jnp.full((1, lo), MASK_VALUE, F32), max_b], axis=1)
            m_new = jnp.maximum(m_old, jnp.maximum(max_a, max_b))
            alpha = jnp.exp2(m_old - m_new)
            p_a = jnp.exp2(s_a - m_new).astype(BF16)
            p_b = jnp.exp2(s_b - m_new[:, lo:]).astype(BF16)
            both = _dot(vt, jnp.concatenate([p_a[:, lo:], p_b], axis=0))
            if lo:
                both = jnp.concatenate([_dot(vt[:, :BK], p_a[:, :lo]), both], axis=1)
            acc_ref[par, c] = alpha * acc_ref[par, c] + both
            m_ref[par, c] = m_new

    for qi in range(seq_len // BQ):
        par = qi % 2
        qrows = slice(qi * BQ, (qi + 1) * BQ)
        q = q_ref[qrows, :].astype(F32) * (DIFF_QK_DIM ** -0.5 * log2e)
        q_ext = (
            jnp.where(lane < DIFF_QK_DIM, q,
                      jnp.where(lane < DIFF_QK_DIM + n_bias, 1.0, 0.0)).astype(BF16),
            jnp.where(lane >= DIFF_QK_DIM, q, jnp.where(lane < n_bias, 1.0, 0.0)).astype(BF16),
        )
        m_ref[par] = jnp.full(m_ref.shape[1:], MASK_VALUE, F32)
        acc_ref[par] = jnp.zeros(acc_ref.shape[1:], F32)
        n_pairs = qi + 1
        lo_of = lambda j: BK if j == n_pairs - 1 else 0

        def pair_scores(j):
            diag = j == n_pairs - 1
            return (scores(q_ext, 2 * j, diag, 0), scores(q_ext, 2 * j + 1, diag, lo_of(j)))

        cur = pair_scores(0)
        for j in range(n_pairs):
            nxt = pair_scores(j + 1) if j + 1 < n_pairs else None
            accumulate(par, 2 * j, cur[0], cur[1], lo_of(j))
            cur = nxt

        num = (acc_ref[par, 0, 0:HEAD_DIM, :], acc_ref[par, 1, 0:HEAD_DIM, :])
        den = (acc_ref[par, 0, HEAD_DIM:HEAD_DIM + 1, :], acc_ref[par, 1, HEAD_DIM:HEAD_DIM + 1, :])
        o_t = num[0] / den[0] - lam * (num[1] / den[1])
        o = o_t.T
        o = o * lax.rsqrt(jnp.mean(o * o, axis=-1, keepdims=True) + NORM_EPS)
        o = o * nw_ref[...] * (1.0 - lam_init)
        o_ref[qrows, :] = (o * _silu(z_ref[qrows, :].astype(F32))).astype(BF16)


def _attn(act, v_t, lq1, lk1, lq2, lk2, diff_norm_w, lam_init, batch, seq_len):
    t = act.shape[0]
    small = lambda shape: pl.BlockSpec(shape, lambda h, b: (0, 0))
    return pl.pallas_call(
        functools.partial(_attn_kernel, lam_init),
        grid=(DIFF_HEADS, batch),
        in_specs=[
            pl.BlockSpec((seq_len, HEAD_DIM), lambda h, b: (b, COL_DQ + h)),
            pl.BlockSpec((seq_len, HEAD_DIM), lambda h, b: (b, COL_DK + h)),
            pl.BlockSpec((HEAD_DIM, seq_len), lambda h, b: (h, b)),
            pl.BlockSpec((seq_len, HEAD_DIM), lambda h, b: (b, COL_DZ + h)),
            small((1, DIFF_QK_DIM)), small((1, DIFF_QK_DIM)),
            small((1, DIFF_QK_DIM)), small((1, DIFF_QK_DIM)),
            small((1, HEAD_DIM)),
        ],
        out_specs=pl.BlockSpec((seq_len, HEAD_DIM), lambda h, b: (b, h)),
        out_shape=jax.ShapeDtypeStruct((t, DIFF_WIDTH), BF16),
        scratch_shapes=[
            pltpu.VMEM((seq_len, HEAD_DIM), F32),
            pltpu.VMEM((2, seq_len, HEAD_DIM), BF16),
            pltpu.VMEM((HEAD_DIM + V_PAD, seq_len), BF16),
            pltpu.VMEM((2, 2, HEAD_DIM + V_PAD, BQ), F32),
            pltpu.VMEM((2, 2, 1, BQ), F32),
        ],
        compiler_params=pltpu.CompilerParams(
            dimension_semantics=("arbitrary", "arbitrary"), vmem_limit_bytes=VMEM_LIMIT),
        name="diffattn",
    )(act, act, v_t, act, lq1, lk1, lq2, lk2, diff_norm_w)


def _outproj_kernel(oa_ref, ob_ref, x_ref, w_ref, fw_ref, out_ref):
    mix = _dot(oa_ref[...], w_ref[0:GDN_WIDTH, :]) + _dot(ob_ref[...], w_ref[GDN_WIDTH:, :])
    y = x_ref[...] + mix
    ms = jnp.mean(y * y, axis=-1, keepdims=True)
    out_ref[...] = y * lax.rsqrt(ms + NORM_EPS) * fw_ref[...]


def _outproj(o_a, o_b, xf, w_out, final_norm_w):
    t = xf.shape[0]
    return pl.pallas_call(
        _outproj_kernel,
        grid=(t // TM_OUT,),
        in_specs=[
            pl.BlockSpec((TM_OUT, GDN_WIDTH), lambda i: (i, 0)),
            pl.BlockSpec((TM_OUT, DIFF_WIDTH), lambda i: (i, 0)),
            pl.BlockSpec((TM_OUT, D_MODEL), lambda i: (i, 0)),
            pl.BlockSpec((D_MODEL, D_MODEL), lambda i: (0, 0)),
            pl.BlockSpec((1, D_MODEL), lambda i: (0, 0)),
        ],
        out_specs=pl.BlockSpec((TM_OUT, D_MODEL), lambda i: (i, 0)),
        out_shape=jax.ShapeDtypeStruct((t, D_MODEL), F32),
        compiler_params=pltpu.CompilerParams(
            dimension_semantics=("arbitrary",), vmem_limit_bytes=VMEM_LIMIT),
        name="outproj",
    )(o_a, o_b, xf, w_out, final_norm_w)


def kernel(x, norm_w, w_in, conv_w, a_log, dt_bias, gdn_norm_w, lambda_q1, lambda_k1,
           lambda_q2, lambda_k2, diff_norm_w, w_out, final_norm_w):
    batch, seq_len, d_model = x.shape
    depth = norm_w.shape[0]
    assert depth == 1 and d_model == D_MODEL
    assert seq_len % TM_IN == 0 and seq_len % SC == 0 and seq_len % BQ == 0 and BQ % BK == 0
    n_wide_a = 4 * GDN_WIDTH
    xf = x.reshape(batch * seq_len, d_model)

    w = w_in[0]
    c_dq = n_wide_a + N_GATE
    c_dv = c_dq + 2 * DIFF_WIDTH
    c_dz = c_dv + DIFF_WIDTH
    w_main = jnp.concatenate([w[:, :n_wide_a], w[:, c_dq:c_dv], w[:, c_dz:]], axis=1).astype(BF16)
    w_v_t = w[:, c_dv:c_dz].T.astype(BF16)
    w_gate_t = w[:, n_wide_a:n_wide_a + N_GATE].T.astype(BF16)

    act, plain, gate_t, v_t = _inproj(xf, norm_w[0][None, :], w_main, w_gate_t, w_v_t, conv_w[0], seq_len)

    pad_s = lambda vec: jnp.pad(vec, (GDN_HEADS, 0))[:, None]
    o_a = _gdn(act, plain, gate_t, pad_s(a_log[0]), pad_s(dt_bias[0]), gdn_norm_w[0][None, :],
               batch, seq_len)

    lam_init = 0.8 - 0.6 * math.exp(-0.3 * 0)
    o_b = _attn(plain, v_t, lambda_q1[0][None, :], lambda_k1[0][None, :], lambda_q2[0][None, :],
                lambda_k2[0][None, :], diff_norm_w[0][None, :], lam_init, batch, seq_len)

    out = _outproj(o_a, o_b, xf, w_out[0].astype(BF16), final_norm_w[None, :])
    return out.reshape(batch, seq_len, d_model)
```

```python
import functools
import math

import jax
import jax.numpy as jnp
from jax import lax
from jax.experimental import pallas as pl
from jax.experimental.pallas import tpu as pltpu

F32 = jnp.float32
BF16 = jnp.bfloat16

D_MODEL = 1024
GDN_HEADS = 4
HEAD_DIM = 128
GDN_WIDTH = GDN_HEADS * HEAD_DIM
DIFF_HEADS = 4
DIFF_QK_DIM = 64
DIFF_WIDTH = DIFF_HEADS * HEAD_DIM
CONV_K = 4
NORM_EPS = 1e-6
N_GATE = 2 * GDN_HEADS
N_MAIN = 4 * GDN_WIDTH + 3 * DIFF_WIDTH
N_CONV = 3 * GDN_WIDTH
N_PLAIN = N_MAIN - N_CONV
LANES = 128
MASK_VALUE = -1e30

COL_GQ, COL_GK, COL_GV = 0, 4, 8
COL_GZ, COL_DQ, COL_DK, COL_DZ = 0, 4, 8, 12

TM_IN = 1024
NC_IN = 256
HALO = 16
TM_OUT = 1024
SC = 256
CHUNK = 64
GDN_GROUP = 2
BQ = 512
BK = 256
V_PAD = 16
VMEM_LIMIT = 48 * 1024 * 1024


def _sigmoid(x):
    return 1.0 / (1.0 + jnp.exp(-x))


def _silu(x):
    half = 0.5 * x
    return half + half * jnp.tanh(half)


def _softplus(x):
    return jnp.maximum(x, 0.0) + jnp.log1p(jnp.exp(-jnp.abs(x)))


def _dot(a, b):
    return jnp.dot(a, b, preferred_element_type=F32)


def _dot_nt(a, b):
    return lax.dot_general(a, b, (((1,), (1,)), ((), ())), preferred_element_type=F32)


def _dot_tn(a, b):
    return lax.dot_general(a, b, (((0,), (0,)), ((), ())), preferred_element_type=F32)


def _inproj_kernel(n_tiles, tiles_per_seq, x_ref, nw_ref, w_ref, wgt_ref, wvt_ref, cw_ref,
                   conv_ref, plain_ref, gate_t_ref, vt_ref, pre_even_ref, pre_odd_ref):
    i = pl.program_id(0)
    tm = x_ref.shape[0]
    pieces = list(range(0, N_CONV, LANES))

    @pl.when(i == 0)
    def _():
        pre_odd_ref[...] = jnp.zeros_like(pre_odd_ref)

    def conv_silu(prev_ref, n0):
        cols = slice(n0, n0 + LANES)
        x = prev_ref[:, cols].astype(F32)
        y = None
        for s in range(CONV_K):
            term = x[HALO - s:HALO - s + tm, :] * cw_ref[CONV_K - 1 - s:CONV_K - s, cols]
            y = term if y is None else y + term
        conv_ref[:, cols] = _silu(y).astype(BF16)

    def project(cur_ref, prev_ref):
        x = x_ref[...]
        ms = jnp.mean(x * x, axis=-1, keepdims=True)
        h = (x * lax.rsqrt(ms + NORM_EPS) * nw_ref[...]).astype(BF16)
        hist = prev_ref[tm:tm + HALO, :]
        cur_ref[0:HALO, :] = jnp.where(i % tiles_per_seq == 0, jnp.zeros_like(hist), hist)

        gate_t_ref[...] = _dot_nt(wgt_ref[...], h)
        vt_ref[...] = _dot_nt(wvt_ref[...], h).astype(BF16)
        todo = list(pieces)
        for n0 in range(0, N_MAIN, NC_IN):
            acc = _dot(h, w_ref[:, n0:n0 + NC_IN]).astype(BF16)
            if n0 < N_CONV:
                cur_ref[HALO:, n0:n0 + NC_IN] = acc
            else:
                plain_ref[:, n0 - N_CONV:n0 - N_CONV + NC_IN] = acc
            if todo:
                conv_silu(prev_ref, todo.pop(0))
        for n0 in todo:
            conv_silu(prev_ref, n0)

    @pl.when((i < n_tiles) & (i % 2 == 0))
    def _():
        project(pre_even_ref, pre_odd_ref)

    @pl.when((i < n_tiles) & (i % 2 == 1))
    def _():
        project(pre_odd_ref, pre_even_ref)

    @pl.when(i == n_tiles)
    def _():
        last_ref = pre_odd_ref if n_tiles % 2 == 0 else pre_even_ref
        for n0 in pieces:
            conv_silu(last_ref, n0)


def _inproj(xf, norm_w, w_main, w_gate_t, w_v_t, conv_w, seq_len):
    t = xf.shape[0]
    n_tiles = t // TM_IN
    cur = lambda i: jnp.minimum(i, n_tiles - 1)
    return pl.pallas_call(
        functools.partial(_inproj_kernel, n_tiles, seq_len // TM_IN),
        grid=(n_tiles + 1,),
        in_specs=[
            pl.BlockSpec((TM_IN, D_MODEL), lambda i: (cur(i), 0)),
            pl.BlockSpec((1, D_MODEL), lambda i: (0, 0)),
            pl.BlockSpec((D_MODEL, N_MAIN), lambda i: (0, 0)),
            pl.BlockSpec((N_GATE, D_MODEL), lambda i: (0, 0)),
            pl.BlockSpec((DIFF_WIDTH, D_MODEL), lambda i: (0, 0)),
            pl.BlockSpec((CONV_K, N_CONV), lambda i: (0, 0)),
        ],
        out_specs=[
            pl.BlockSpec((TM_IN, N_CONV), lambda i: (jnp.maximum(i - 1, 0), 0)),
            pl.BlockSpec((TM_IN, N_PLAIN), lambda i: (cur(i), 0)),
            pl.BlockSpec((N_GATE, TM_IN), lambda i: (0, cur(i))),
            pl.BlockSpec((DIFF_WIDTH, TM_IN), lambda i: (0, cur(i))),
        ],
        out_shape=[
            jax.ShapeDtypeStruct((t, N_CONV), BF16),
            jax.ShapeDtypeStruct((t, N_PLAIN), BF16),
            jax.ShapeDtypeStruct((N_GATE, t), F32),
            jax.ShapeDtypeStruct((DIFF_WIDTH, t), BF16),
        ],
        scratch_shapes=[pltpu.VMEM((HALO + TM_IN, N_CONV), BF16), pltpu.VMEM((HALO + TM_IN, N_CONV), BF16)],
        compiler_params=pltpu.CompilerParams(
            dimension_semantics=("arbitrary",), vmem_limit_bytes=VMEM_LIMIT),
        name="inproj",
    )(xf, norm_w, w_main, w_gate_t, w_v_t, conv_w)


def _gdn_kernel(act_ref, z_ref, gate_t_ref, alog_s_ref, dtb_s_ref, gnw_ref, o_ref, mask_ref, bd_ref,
                row_ref, col_ref):
    seq_len = act_ref.shape[0]
    n_chunks = SC // CHUNK
    heads = range(GDN_HEADS)
    i_cat = lax.broadcasted_iota(jnp.int32, (CHUNK, SC), 0)
    j_cat = lax.broadcasted_iota(jnp.int32, (CHUNK, SC), 1) & (CHUNK - 1)
    mask_ref[0] = jnp.where(i_cat >= j_cat, 0.0, MASK_VALUE)
    mask_ref[1] = jnp.where(i_cat > j_cat, -1.0, 0.0)
    mask_ref[2] = jnp.where(i_cat == j_cat, 1.0, 0.0)
    r = lax.broadcasted_iota(jnp.int32, (SC, SC), 0)
    c = lax.broadcasted_iota(jnp.int32, (SC, SC), 1)
    bd_ref[...] = jnp.where((r & -CHUNK) == (c & -CHUNK), 1.0, 0.0).astype(BF16)
    lane_chunk = lax.broadcasted_iota(jnp.int32, (1, SC), 1) & -CHUNK

    scale = HEAD_DIM ** -0.5

    lane_in_chunk = lax.broadcasted_iota(jnp.int32, (N_GATE, seq_len), 1) & (CHUNK - 1)
    gate_t = gate_t_ref[...]
    beta = _sigmoid(gate_t)
    log_beta = -_softplus(-gate_t)
    g_step = -jnp.exp(alog_s_ref[...]) * _softplus(gate_t + dtb_s_ref[...])
    gc = g_step
    g_after = jnp.zeros_like(g_step)
    tail = g_step
    step = 1
    while step < CHUNK:
        gc = gc + jnp.where(lane_in_chunk >= step, pltpu.roll(gc, step, axis=1), 0.0)
        ahead = jnp.where(lane_in_chunk + step < CHUNK, pltpu.roll(tail, seq_len - step, axis=1), 0.0)
        g_after = g_after + ahead
        tail = tail + ahead
        step *= 2
    e_gc = jnp.exp(gc)
    is_beta_row = lax.broadcasted_iota(jnp.int32, (N_GATE, seq_len), 0) < GDN_HEADS
    swap = pltpu.roll(e_gc, GDN_HEADS, axis=0)
    row_ref[0] = gc
    row_ref[1] = e_gc
    col_ref[...] = jnp.concatenate([
        jnp.where(is_beta_row, beta, gc),
        jnp.where(is_beta_row, log_beta, e_gc),
        jnp.where(is_beta_row, beta * swap, jnp.exp(g_after))], axis=0).T

    def cat(x):
        out = x[(n_chunks - 1) * CHUNK:, :]
        for ci in range(n_chunks - 2, -1, -1):
            out = jnp.where(lane_chunk == ci * CHUNK, x[ci * CHUNK:(ci + 1) * CHUNK, :], out)
        return out

    def block_diag(x_cat):
        xb = x_cat.astype(BF16)
        zero = jnp.zeros((CHUNK, LANES), BF16)
        row_blocks = []
        for ci in range(n_chunks):
            g = ci * CHUNK // LANES
            grp = slice(g * LANES, (g + 1) * LANES)
            part = xb[:, grp] * bd_ref[ci * CHUNK:(ci + 1) * CHUNK, grp]
            row_blocks.append(jnp.concatenate(
                [part if j == g else zero for j in range(SC // LANES)], axis=1))
        return jnp.concatenate(row_blocks, axis=0)

    def head_cols(col, h):
        return slice((col + h) * HEAD_DIM, (col + h + 1) * HEAD_DIM)

    def prepare(steps):
        causal, neg_strict, eye = mask_ref[0], mask_ref[1], mask_ref[2]
        units = [(sc, h) for sc in steps for h in heads]
        n_pow, a_cat, rhs, q_dec, k_dec, decay_last = [], [], [], [], [], []
        for sc, h in units:
            rows = slice(sc * SC, (sc + 1) * SC)
            col = lambda j: col_ref[rows, j:j + 1]
            q = act_ref[rows, head_cols(COL_GQ, h)].astype(F32)
            k = act_ref[rows, head_cols(COL_GK, h)].astype(F32)
            v = act_ref[rows, head_cols(COL_GV, h)].astype(F32)
            q = q * (lax.rsqrt(jnp.sum(q * q, axis=-1, keepdims=True) + 1e-6) * scale)
            k = k * lax.rsqrt(jnp.sum(k * k, axis=-1, keepdims=True) + 1e-6)
            hb, hg = h, GDN_HEADS + h
            kb = k.astype(BF16)
            gram = _dot_nt(jnp.concatenate([kb, q.astype(BF16)], axis=0), kb)
            e_cat = cat(col(hg)) - row_ref[0, hg:hg + 1, rows] + causal
            n_pow.append(cat(gram[:SC]) * jnp.exp(e_cat + cat(col(N_GATE + hb))) * neg_strict)
            a_cat.append(cat(gram[SC:]) * jnp.exp(e_cat))
            rhs.append(jnp.concatenate([col(hb) * v, col(2 * N_GATE + hb) * k], axis=1).astype(BF16))
            q_dec.append(q * col(N_GATE + hg))
            kd = (k * col(2 * N_GATE + hg)).astype(BF16)
            k_dec.append([kd[ci * CHUNK:(ci + 1) * CHUNK, :] for ci in range(n_chunks)])
            decay_last.append([row_ref[1, hg:hg + 1, sc * SC + (ci + 1) * CHUNK - 1:sc * SC + (ci + 1) * CHUNK]
                               for ci in range(n_chunks)])

        every = range(len(units))
        t_cat = [eye + n_pow[u] for u in every]
        n_pow = [_dot(n_pow[u].astype(BF16), block_diag(n_pow[u])) for u in every]
        span = 2
        while span < CHUNK // 2:
            both = [_dot(jnp.concatenate([t_cat[u], n_pow[u]], axis=0).astype(BF16), block_diag(n_pow[u]))
                    for u in every]
            t_cat = [t_cat[u] + both[u][:CHUNK] for u in every]
            n_pow = [both[u][CHUNK:] for u in every]
            span *= 2
        t_cat = [t_cat[u] + _dot(t_cat[u].astype(BF16), block_diag(n_pow[u])) for u in every]

        uw = [_dot(block_diag(t_cat[u]), rhs[u]) for u in every]
        kt = [[_dot_tn(k_dec[u][ci], uw[u][ci * CHUNK:(ci + 1) * CHUNK, :].astype(BF16))
               for ci in range(n_chunks)] for u in every]
        per_step = lambda xs, i: xs[i * GDN_HEADS:(i + 1) * GDN_HEADS]
        return [tuple(per_step(xs, i) for xs in (uw, kt, q_dec, a_cat, decay_last))
                for i in range(len(steps))]

    def finish(sc, prep, state):
        rows = slice(sc * SC, (sc + 1) * SC)
        uw, kt, q_dec, a_cat, decay_last = prep
        state = list(state)
        state_in = [[None] * n_chunks for _ in heads]
        for ci in range(n_chunks):
            for h in heads:
                sb = state[h].astype(BF16)
                state_in[h][ci] = sb
                state[h] = (state[h] * decay_last[h][ci]
                            - _dot(kt[h][ci][:, HEAD_DIM:].astype(BF16), sb) + kt[h][ci][:, :HEAD_DIM])
        for h in heads:
            inter = []
            for ci in range(n_chunks):
                cr = slice(ci * CHUNK, (ci + 1) * CHUNK)
                lhs = jnp.concatenate([uw[h][cr, HEAD_DIM:], q_dec[h][cr, :]], axis=0).astype(BF16)
                inter.append(_dot(lhs, state_in[h][ci]))
            v_new = jnp.concatenate([uw[h][ci * CHUNK:(ci + 1) * CHUNK, :HEAD_DIM] - inter[ci][:CHUNK]
                                     for ci in range(n_chunks)], axis=0).astype(BF16)
            o = (jnp.concatenate([inter[ci][CHUNK:] for ci in range(n_chunks)], axis=0)
                 + _dot(block_diag(a_cat[h]), v_new))
            o = o * lax.rsqrt(jnp.mean(o * o, axis=-1, keepdims=True) + NORM_EPS)
            z = z_ref[rows, head_cols(COL_GZ, h)].astype(F32)
            o_ref[rows, head_cols(0, h)] = (o * (_silu(z) * gnw_ref[...])).astype(BF16)
        return state

    n_groups = seq_len // (SC * GDN_GROUP)
    group = lambda g: list(range(g * GDN_GROUP, (g + 1) * GDN_GROUP))
    state = [jnp.zeros((HEAD_DIM, HEAD_DIM), F32) for _ in heads]
    preps = prepare(group(0))
    for g in range(n_groups):
        nxt = prepare(group(g + 1)) if g + 1 < n_groups else None
        for sc, prep in zip(group(g), preps):
            state = finish(sc, prep, state)
        preps = nxt


def _gdn(act, plain, gate_t, alog_s, dtb_s, gdn_norm_w, batch, seq_len):
    t = act.shape[0]
    small = lambda shape: pl.BlockSpec(shape, lambda b: (0, 0))
    return pl.pallas_call(
        _gdn_kernel,
        grid=(batch,),
        in_specs=[
            pl.BlockSpec((seq_len, N_CONV), lambda b: (b, 0)),
            pl.BlockSpec((seq_len, GDN_WIDTH), lambda b: (b, 0)),
            pl.BlockSpec((N_GATE, seq_len), lambda b: (0, b)),
            small((N_GATE, 1)), small((N_GATE, 1)),
            small((1, HEAD_DIM)),
        ],
        out_specs=pl.BlockSpec((seq_len, GDN_WIDTH), lambda b: (b, 0)),
        out_shape=jax.ShapeDtypeStruct((t, GDN_WIDTH), BF16),
        scratch_shapes=[
            pltpu.VMEM((3, CHUNK, SC), F32),
            pltpu.VMEM((SC, SC), BF16),
            pltpu.VMEM((2, N_GATE, seq_len), F32),
            pltpu.VMEM((seq_len, 3 * N_GATE), F32),
        ],
        compiler_params=pltpu.CompilerParams(
            dimension_semantics=("arbitrary",), vmem_limit_bytes=VMEM_LIMIT),
        name="gdn",
    )(act, plain, gate_t, alog_s, dtb_s, gdn_norm_w)


def _split3(x):
    hi = x.astype(BF16).astype(F32)
    r1 = x - hi
    mid = r1.astype(BF16).astype(F32)
    return hi, mid, r1 - mid


def _attn_kernel(lam_init, q_ref, k_ref, vt_ref, z_ref, lq1_ref, lk1_ref, lq2_ref, lk2_ref,
                 nw_ref, o_ref, bias_ref, kext_ref, vtx_ref, acc_ref, m_ref):
    h = pl.program_id(0)
    b = pl.program_id(1)
    seq_len = k_ref.shape[0]
    lane = lax.broadcasted_iota(jnp.int32, (1, HEAD_DIM), 1)
    n_bias = 6
    log2e = math.log2(math.e)
    assert BQ == 2 * BK

    @pl.when(b == 0)
    def _():
        slope = lax.shift_left(jnp.int32(1), 2 * (DIFF_HEADS - 1 - h)).astype(F32) * (log2e / 256.0)
        pos = lax.broadcasted_iota(jnp.int32, (seq_len, 1), 0)
        in_block = (pos & (BK - 1)).astype(F32) * slope
        block_off = (pos & -BK).astype(F32) * slope
        terms = _split3(in_block) + _split3(block_off)
        tile = jnp.zeros((seq_len, HEAD_DIM), F32)
        for j, term in enumerate(terms):
            tile = jnp.where((lane == j) | (lane == DIFF_QK_DIM + j), term, tile)
        bias_ref[...] = tile
        ones_row = lax.broadcasted_iota(jnp.int32, (V_PAD, seq_len), 0) == 0
        vtx_ref[HEAD_DIM:, :] = jnp.where(ones_row, 1.0, 0.0).astype(BF16)

    k = k_ref[...].astype(F32)
    kext_ref[0] = jnp.where(lane < DIFF_QK_DIM, k, bias_ref[...]).astype(BF16)
    kext_ref[1] = jnp.where(lane >= DIFF_QK_DIM, k, bias_ref[...]).astype(BF16)
    vtx_ref[0:HEAD_DIM, :] = vt_ref[...]

    lam = (jnp.exp(jnp.sum(lq1_ref[...] * lk1_ref[...], axis=-1, keepdims=True))
           - jnp.exp(jnp.sum(lq2_ref[...] * lk2_ref[...], axis=-1, keepdims=True)) + lam_init)

    def scores(q_ext, kb, masked, lo):
        rows = slice(kb * BK, (kb + 1) * BK)
        out = []
        for c in range(2):
            s = _dot_nt(kext_ref[c, rows, :], q_ext[c][lo:, :])
            if masked:
                krel = lax.broadcasted_iota(jnp.int32, (BK, BQ - lo), 0)
                qrel = lax.broadcasted_iota(jnp.int32, (BK, BQ - lo), 1)
                s = jnp.where(krel <= qrel, s, MASK_VALUE)
            out.append((s, jnp.max(s, axis=0, keepdims=True)))
        return out

    def accumulate(par, kb, first, second, lo):
        vt = vtx_ref[:, kb * BK:(kb + 2) * BK]
        for c in range(2):
            (s_a, max_a), (s_b, max_b) = first[c], second[c]
            m_old = m_ref[par, c]
            if lo:
                max_b = jnp.concatenate([jnp.full((1, lo), MASK_VALUE, F32), max_b], axis=1)
            m_new = jnp.maximum(m_old, jnp.maximum(max_a, max_b))
            alpha = jnp.exp2(m_old - m_new)
            p_a = jnp.exp2(s_a - m_new).astype(BF16)
            p_b = jnp.exp2(s_b - m_new[:, lo:]).astype(BF16)
            both = _dot(vt, jnp.concatenate([p_a[:, lo:], p_b], axis=0))
            if lo:
                both = jnp.concatenate([_dot(vt[:, :BK], p_a[:, :lo]), both], axis=1)
            acc_ref[par, c] = alpha * acc_ref[par, c] + both
            m_ref[par, c] = m_new

    for qi in range(seq_len // BQ):
        par = qi % 2
        qrows = slice(qi * BQ, (qi + 1) * BQ)
        q = q_ref[qrows, :].astype(F32) * (DIFF_QK_DIM ** -0.5 * log2e)
        q_ext = (
            jnp.where(lane < DIFF_QK_DIM, q,
                      jnp.where(lane < DIFF_QK_DIM + n_bias, 1.0, 0.0)).astype(BF16),
            jnp.where(lane >= DIFF_QK_DIM, q, jnp.where(lane < n_bias, 1.0, 0.0)).astype(BF16),
        )
        m_ref[par] = jnp.full(m_ref.shape[1:], MASK_VALUE, F32)
        acc_ref[par] = jnp.zeros(acc_ref.shape[1:], F32)
        n_pairs = qi + 1
        lo_of = lambda j: BK if j == n_pairs - 1 else 0

        def pair_scores(j):
            diag = j == n_pairs - 1
            return (scores(q_ext, 2 * j, diag, 0), scores(q_ext, 2 * j + 1, diag, lo_of(j)))

        cur = pair_scores(0)
        for j in range(n_pairs):
            nxt = pair_scores(j + 1) if j + 1 < n_pairs else None
            accumulate(par, 2 * j, cur[0], cur[1], lo_of(j))
            cur = nxt

        num = (acc_ref[par, 0, 0:HEAD_DIM, :], acc_ref[par, 1, 0:HEAD_DIM, :])
        den = (acc_ref[par, 0, HEAD_DIM:HEAD_DIM + 1, :], acc_ref[par, 1, HEAD_DIM:HEAD_DIM + 1, :])
        o_t = num[0] / den[0] - lam * (num[1] / den[1])
        o = o_t.T
        o = o * lax.rsqrt(jnp.mean(o * o, axis=-1, keepdims=True) + NORM_EPS)
        o = o * nw_ref[...] * (1.0 - lam_init)
        o_ref[qrows, :] = (o * _silu(z_ref[qrows, :].astype(F32))).astype(BF16)


def _attn(act, v_t, lq1, lk1, lq2, lk2, diff_norm_w, lam_init, batch, seq_len):
    t = act.shape[0]
    small = lambda shape: pl.BlockSpec(shape, lambda h, b: (0, 0))
    return pl.pallas_call(
        functools.partial(_attn_kernel, lam_init),
        grid=(DIFF_HEADS, batch),
        in_specs=[
            pl.BlockSpec((seq_len, HEAD_DIM), lambda h, b: (b, COL_DQ + h)),
            pl.BlockSpec((seq_len, HEAD_DIM), lambda h, b: (b, COL_DK + h)),
            pl.BlockSpec((HEAD_DIM, seq_len), lambda h, b: (h, b)),
            pl.BlockSpec((seq_len, HEAD_DIM), lambda h, b: (b, COL_DZ + h)),
            small((1, DIFF_QK_DIM)), small((1, DIFF_QK_DIM)),
            small((1, DIFF_QK_DIM)), small((1, DIFF_QK_DIM)),
            small((1, HEAD_DIM)),
        ],
        out_specs=pl.BlockSpec((seq_len, HEAD_DIM), lambda h, b: (b, h)),
        out_shape=jax.ShapeDtypeStruct((t, DIFF_WIDTH), BF16),
        scratch_shapes=[
            pltpu.VMEM((seq_len, HEAD_DIM), F32),
            pltpu.VMEM((2, seq_len, HEAD_DIM), BF16),
            pltpu.VMEM((HEAD_DIM + V_PAD, seq_len), BF16),
            pltpu.VMEM((2, 2, HEAD_DIM + V_PAD, BQ), F32),
            pltpu.VMEM((2, 2, 1, BQ), F32),
        ],
        compiler_params=pltpu.CompilerParams(
            dimension_semantics=("arbitrary", "arbitrary"), vmem_limit_bytes=VMEM_LIMIT),
        name="diffattn",
    )(act, act, v_t, act, lq1, lk1, lq2, lk2, diff_norm_w)


def _outproj_kernel(oa_ref, ob_ref, x_ref, w_ref, fw_ref, out_ref):
    mix = _dot(oa_ref[...], w_ref[0:GDN_WIDTH, :]) + _dot(ob_ref[...], w_ref[GDN_WIDTH:, :])
    y = x_ref[...] + mix
    ms = jnp.mean(y * y, axis=-1, keepdims=True)
    out_ref[...] = y * lax.rsqrt(ms + NORM_EPS) * fw_ref[...]


def _outproj(o_a, o_b, xf, w_out, final_norm_w):
    t = xf.shape[0]
    return pl.pallas_call(
        _outproj_kernel,
        grid=(t // TM_OUT,),
        in_specs=[
            pl.BlockSpec((TM_OUT, GDN_WIDTH), lambda i: (i, 0)),
            pl.BlockSpec((TM_OUT, DIFF_WIDTH), lambda i: (i, 0)),
            pl.BlockSpec((TM_OUT, D_MODEL), lambda i: (i, 0)),
            pl.BlockSpec((D_MODEL, D_MODEL), lambda i: (0, 0)),
            pl.BlockSpec((1, D_MODEL), lambda i: (0, 0)),
        ],
        out_specs=pl.BlockSpec((TM_OUT, D_MODEL), lambda i: (i, 0)),
        out_shape=jax.ShapeDtypeStruct((t, D_MODEL), F32),
        compiler_params=pltpu.CompilerParams(
            dimension_semantics=("arbitrary",), vmem_limit_bytes=VMEM_LIMIT),
        name="outproj",
    )(o_a, o_b, xf, w_out, final_norm_w)


def kernel(x, norm_w, w_in, conv_w, a_log, dt_bias, gdn_norm_w, lambda_q1, lambda_k1,
           lambda_q2, lambda_k2, diff_norm_w, w_out, final_norm_w):
    batch, seq_len, d_model = x.shape
    depth = norm_w.shape[0]
    assert depth == 1 and d_model == D_MODEL
    assert seq_len % TM_IN == 0 and seq_len % SC == 0 and seq_len % BQ == 0 and BQ % BK == 0
    n_wide_a = 4 * GDN_WIDTH
    xf = x.reshape(batch * seq_len, d_model)

    w = w_in[0]
    c_dq = n_wide_a + N_GATE
    c_dv = c_dq + 2 * DIFF_WIDTH
    c_dz = c_dv + DIFF_WIDTH
    w_main = jnp.concatenate([w[:, :n_wide_a], w[:, c_dq:c_dv], w[:, c_dz:]], axis=1).astype(BF16)
    w_v_t = w[:, c_dv:c_dz].T.astype(BF16)
    w_gate_t = w[:, n_wide_a:n_wide_a + N_GATE].T.astype(BF16)

    act, plain, gate_t, v_t = _inproj(xf, norm_w[0][None, :], w_main, w_gate_t, w_v_t, conv_w[0], seq_len)

    pad_s = lambda vec: jnp.pad(vec, (GDN_HEADS, 0))[:, None]
    o_a = _gdn(act, plain, gate_t, pad_s(a_log[0]), pad_s(dt_bias[0]), gdn_norm_w[0][None, :],
               batch, seq_len)

    lam_init = 0.8 - 0.6 * math.exp(-0.3 * 0)
    o_b = _attn(plain, v_t, lambda_q1[0][None, :], lambda_k1[0][None, :], lambda_q2[0][None, :],
                lambda_k2[0][None, :], diff_norm_w[0][None, :], lam_init, batch, seq_len)

    out = _outproj(o_a, o_b, xf, w_out[0].astype(BF16), final_norm_w[None, :])
    return out.reshape(batch, seq_len, d_model)
```

```python
import functools
import math

import jax
import jax.numpy as jnp
from jax import lax
from jax.experimental import pallas as pl
from jax.experimental.pallas import tpu as pltpu

F32 = jnp.float32
BF16 = jnp.bfloat16

D_MODEL = 1024
GDN_HEADS = 4
HEAD_DIM = 128
GDN_WIDTH = GDN_HEADS * HEAD_DIM
DIFF_HEADS = 4
DIFF_QK_DIM = 64
DIFF_WIDTH = DIFF_HEADS * HEAD_DIM
CONV_K = 4
NORM_EPS = 1e-6
N_GATE = 2 * GDN_HEADS
N_MAIN = 4 * GDN_WIDTH + 3 * DIFF_WIDTH
N_CONV = 3 * GDN_WIDTH
LANES = 128
MASK_VALUE = -1e30

COL_GQ, COL_GK, COL_GV, COL_GZ = 0, 4, 8, 12
COL_DQ, COL_DK, COL_DZ = 16, 20, 24

TM_IN = 1024
NC_IN = 256
HALO = 16
TM_OUT = 1024
SC = 256
CHUNK = 64
GDN_GROUP = 2
BQ = 512
BK = 256
V_PAD = 16
VMEM_LIMIT = 48 * 1024 * 1024


def _sigmoid(x):
    return 1.0 / (1.0 + jnp.exp(-x))


def _silu(x):
    half = 0.5 * x
    return half + half * jnp.tanh(half)


def _softplus(x):
    return jnp.maximum(x, 0.0) + jnp.log1p(jnp.exp(-jnp.abs(x)))


def _dot(a, b):
    return jnp.dot(a, b, preferred_element_type=F32)


def _dot_nt(a, b):
    return lax.dot_general(a, b, (((1,), (1,)), ((), ())), preferred_element_type=F32)


def _dot_tn(a, b):
    return lax.dot_general(a, b, (((0,), (0,)), ((), ())), preferred_element_type=F32)


def _inproj_kernel(x_ref, nw_ref, w_ref, wgt_ref, wvt_ref, main_ref, gate_t_ref, vt_ref):
    x = x_ref[...]
    ms = jnp.mean(x * x, axis=-1, keepdims=True)
    h = (x * lax.rsqrt(ms + NORM_EPS) * nw_ref[...]).astype(BF16)

    gate_t_ref[...] = _dot_nt(wgt_ref[...], h)
    vt_ref[...] = _dot_nt(wvt_ref[...], h).astype(BF16)
    for n0 in range(0, N_MAIN, NC_IN):
        main_ref[:, n0:n0 + NC_IN] = _dot(h, w_ref[:, n0:n0 + NC_IN]).astype(BF16)


def _inproj(xf, norm_w, w_main, w_gate_t, w_v_t):
    t = xf.shape[0]
    return pl.pallas_call(
        _inproj_kernel,
        grid=(t // TM_IN,),
        in_specs=[
            pl.BlockSpec((TM_IN, D_MODEL), lambda i: (i, 0)),
            pl.BlockSpec((1, D_MODEL), lambda i: (0, 0)),
            pl.BlockSpec((D_MODEL, N_MAIN), lambda i: (0, 0)),
            pl.BlockSpec((N_GATE, D_MODEL), lambda i: (0, 0)),
            pl.BlockSpec((DIFF_WIDTH, D_MODEL), lambda i: (0, 0)),
        ],
        out_specs=[
            pl.BlockSpec((TM_IN, N_MAIN), lambda i: (i, 0)),
            pl.BlockSpec((N_GATE, TM_IN), lambda i: (0, i)),
            pl.BlockSpec((DIFF_WIDTH, TM_IN), lambda i: (0, i)),
        ],
        out_shape=[
            jax.ShapeDtypeStruct((t, N_MAIN), BF16),
            jax.ShapeDtypeStruct((N_GATE, t), F32),
            jax.ShapeDtypeStruct((DIFF_WIDTH, t), BF16),
        ],
        compiler_params=pltpu.CompilerParams(
            dimension_semantics=("arbitrary",), vmem_limit_bytes=VMEM_LIMIT),
        name="inproj",
    )(xf, norm_w, w_main, w_gate_t, w_v_t)


def _gdn_kernel(act_ref, gate_t_ref, alog_s_ref, dtb_s_ref, cw_ref, gnw_ref, o_ref, mask_ref, bd_ref,
                row_ref, col_ref):
    seq_len = act_ref.shape[0]
    n_chunks = SC // CHUNK
    heads = range(GDN_HEADS)
    i_cat = lax.broadcasted_iota(jnp.int32, (CHUNK, SC), 0)
    j_cat = lax.broadcasted_iota(jnp.int32, (CHUNK, SC), 1) & (CHUNK - 1)
    mask_ref[0] = jnp.where(i_cat >= j_cat, 0.0, MASK_VALUE)
    mask_ref[1] = jnp.where(i_cat > j_cat, -1.0, 0.0)
    mask_ref[2] = jnp.where(i_cat == j_cat, 1.0, 0.0)
    r = lax.broadcasted_iota(jnp.int32, (SC, SC), 0)
    c = lax.broadcasted_iota(jnp.int32, (SC, SC), 1)
    bd_ref[...] = jnp.where((r & -CHUNK) == (c & -CHUNK), 1.0, 0.0).astype(BF16)
    lane_chunk = lax.broadcasted_iota(jnp.int32, (1, SC), 1) & -CHUNK

    scale = HEAD_DIM ** -0.5

    lane_in_chunk = lax.broadcasted_iota(jnp.int32, (N_GATE, seq_len), 1) & (CHUNK - 1)
    gate_t = gate_t_ref[...]
    beta = _sigmoid(gate_t)
    log_beta = -_softplus(-gate_t)
    g_step = -jnp.exp(alog_s_ref[...]) * _softplus(gate_t + dtb_s_ref[...])
    gc = g_step
    g_after = jnp.zeros_like(g_step)
    tail = g_step
    step = 1
    while step < CHUNK:
        gc = gc + jnp.where(lane_in_chunk >= step, pltpu.roll(gc, step, axis=1), 0.0)
        ahead = jnp.where(lane_in_chunk + step < CHUNK, pltpu.roll(tail, seq_len - step, axis=1), 0.0)
        g_after = g_after + ahead
        tail = tail + ahead
        step *= 2
    e_gc = jnp.exp(gc)
    is_beta_row = lax.broadcasted_iota(jnp.int32, (N_GATE, seq_len), 0) < GDN_HEADS
    swap = pltpu.roll(e_gc, GDN_HEADS, axis=0)
    row_ref[0] = gc
    row_ref[1] = e_gc
    col_ref[...] = jnp.concatenate([
        jnp.where(is_beta_row, beta, gc),
        jnp.where(is_beta_row, log_beta, e_gc),
        jnp.where(is_beta_row, beta * swap, jnp.exp(g_after))], axis=0).T

    def cat(x):
        out = x[(n_chunks - 1) * CHUNK:, :]
        for ci in range(n_chunks - 2, -1, -1):
            out = jnp.where(lane_chunk == ci * CHUNK, x[ci * CHUNK:(ci + 1) * CHUNK, :], out)
        return out

    def block_diag(x_cat):
        xb = x_cat.astype(BF16)
        zero = jnp.zeros((CHUNK, LANES), BF16)
        row_blocks = []
        for ci in range(n_chunks):
            g = ci * CHUNK // LANES
            grp = slice(g * LANES, (g + 1) * LANES)
            part = xb[:, grp] * bd_ref[ci * CHUNK:(ci + 1) * CHUNK, grp]
            row_blocks.append(jnp.concatenate(
                [part if j == g else zero for j in range(SC // LANES)], axis=1))
        return jnp.concatenate(row_blocks, axis=0)

    def head_cols(col, h):
        return slice((col + h) * HEAD_DIM, (col + h + 1) * HEAD_DIM)

    def conv_silu(sc, cols):
        if sc == 0:
            x = jnp.concatenate([jnp.zeros((HALO, HEAD_DIM), F32),
                                 act_ref[0:SC, cols].astype(F32)], axis=0)
        else:
            x = act_ref[sc * SC - HALO:(sc + 1) * SC, cols].astype(F32)
        y = None
        for s in range(CONV_K):
            term = x[HALO - s:HALO - s + SC, :] * cw_ref[CONV_K - 1 - s:CONV_K - s, cols]
            y = term if y is None else y + term
        return _silu(y)

    def conv_inputs(steps):
        return [tuple(conv_silu(sc, head_cols(c, h)) for c in (COL_GQ, COL_GK, COL_GV))
                for sc in steps for h in heads]

    def prepare(steps, qkv):
        causal, neg_strict, eye = mask_ref[0], mask_ref[1], mask_ref[2]
        units = [(sc, h) for sc in steps for h in heads]
        n_pow, a_cat, rhs, q_dec, k_dec, decay_last = [], [], [], [], [], []
        for (sc, h), (q, k, v) in zip(units, qkv):
            rows = slice(sc * SC, (sc + 1) * SC)
            col = lambda j: col_ref[rows, j:j + 1]
            q = q * (lax.rsqrt(jnp.sum(q * q, axis=-1, keepdims=True) + 1e-6) * scale)
            k = k * lax.rsqrt(jnp.sum(k * k, axis=-1, keepdims=True) + 1e-6)
            hb, hg = h, GDN_HEADS + h
            kb = k.astype(BF16)
            gram = _dot_nt(jnp.concatenate([kb, q.astype(BF16)], axis=0), kb)
            e_cat = cat(col(hg)) - row_ref[0, hg:hg + 1, rows] + causal
            n_pow.append(cat(gram[:SC]) * jnp.exp(e_cat + cat(col(N_GATE + hb))) * neg_strict)
            a_cat.append(cat(gram[SC:]) * jnp.exp(e_cat))
            rhs.append(jnp.concatenate([col(hb) * v, col(2 * N_GATE + hb) * k], axis=1).astype(BF16))
            q_dec.append(q * col(N_GATE + hg))
            kd = (k * col(2 * N_GATE + hg)).astype(BF16)
            k_dec.append([kd[ci * CHUNK:(ci + 1) * CHUNK, :] for ci in range(n_chunks)])
            decay_last.append([row_ref[1, hg:hg + 1, sc * SC + (ci + 1) * CHUNK - 1:sc * SC + (ci + 1) * CHUNK]
                               for ci in range(n_chunks)])

        every = range(len(units))
        t_cat = [eye + n_pow[u] for u in every]
        n_pow = [_dot(n_pow[u].astype(BF16), block_diag(n_pow[u])) for u in every]
        span = 2
        while span < CHUNK // 2:
            both = [_dot(jnp.concatenate([t_cat[u], n_pow[u]], axis=0).astype(BF16), block_diag(n_pow[u]))
                    for u in every]
            t_cat = [t_cat[u] + both[u][:CHUNK] for u in every]
            n_pow = [both[u][CHUNK:] for u in every]
            span *= 2
        t_cat = [t_cat[u] + _dot(t_cat[u].astype(BF16), block_diag(n_pow[u])) for u in every]

        uw = [_dot(block_diag(t_cat[u]), rhs[u]) for u in every]
        kt = [[_dot_tn(k_dec[u][ci], uw[u][ci * CHUNK:(ci + 1) * CHUNK, :].astype(BF16))
               for ci in range(n_chunks)] for u in every]
        per_step = lambda xs, i: xs[i * GDN_HEADS:(i + 1) * GDN_HEADS]
        return [tuple(per_step(xs, i) for xs in (uw, kt, q_dec, a_cat, decay_last))
                for i in range(len(steps))]

    def finish(sc, prep, state):
        rows = slice(sc * SC, (sc + 1) * SC)
        uw, kt, q_dec, a_cat, decay_last = prep
        state = list(state)
        state_in = [[None] * n_chunks for _ in heads]
        for ci in range(n_chunks):
            for h in heads:
                sb = state[h].astype(BF16)
                state_in[h][ci] = sb
                state[h] = (state[h] * decay_last[h][ci]
                            - _dot(kt[h][ci][:, HEAD_DIM:].astype(BF16), sb) + kt[h][ci][:, :HEAD_DIM])
        for h in heads:
            inter = []
            for ci in range(n_chunks):
                cr = slice(ci * CHUNK, (ci + 1) * CHUNK)
                lhs = jnp.concatenate([uw[h][cr, HEAD_DIM:], q_dec[h][cr, :]], axis=0).astype(BF16)
                inter.append(_dot(lhs, state_in[h][ci]))
            v_new = jnp.concatenate([uw[h][ci * CHUNK:(ci + 1) * CHUNK, :HEAD_DIM] - inter[ci][:CHUNK]
                                     for ci in range(n_chunks)], axis=0).astype(BF16)
            o = (jnp.concatenate([inter[ci][CHUNK:] for ci in range(n_chunks)], axis=0)
                 + _dot(block_diag(a_cat[h]), v_new))
            o = o * lax.rsqrt(jnp.mean(o * o, axis=-1, keepdims=True) + NORM_EPS)
            z = act_ref[rows, head_cols(COL_GZ, h)].astype(F32)
            o_ref[rows, head_cols(0, h)] = (o * (_silu(z) * gnw_ref[...])).astype(BF16)
        return state

    n_groups = seq_len // (SC * GDN_GROUP)
    group = lambda g: list(range(g * GDN_GROUP, (g + 1) * GDN_GROUP))
    state = [jnp.zeros((HEAD_DIM, HEAD_DIM), F32) for _ in heads]
    preps = prepare(group(0), conv_inputs(group(0)))
    qkv_next = conv_inputs(group(1)) if n_groups > 1 else None
    for g in range(n_groups):
        nxt = prepare(group(g + 1), qkv_next) if g + 1 < n_groups else None
        qkv_next = conv_inputs(group(g + 2)) if g + 2 < n_groups else None
        for sc, prep in zip(group(g), preps):
            state = finish(sc, prep, state)
        preps = nxt


def _gdn(act, gate_t, alog_s, dtb_s, conv_w, gdn_norm_w, batch, seq_len):
    t = act.shape[0]
    small = lambda shape: pl.BlockSpec(shape, lambda b: (0, 0))
    return pl.pallas_call(
        _gdn_kernel,
        grid=(batch,),
        in_specs=[
            pl.BlockSpec((seq_len, 4 * GDN_WIDTH), lambda b: (b, 0)),
            pl.BlockSpec((N_GATE, seq_len), lambda b: (0, b)),
            small((N_GATE, 1)), small((N_GATE, 1)), small((CONV_K, N_CONV)),
            small((1, HEAD_DIM)),
        ],
        out_specs=pl.BlockSpec((seq_len, GDN_WIDTH), lambda b: (b, 0)),
        out_shape=jax.ShapeDtypeStruct((t, GDN_WIDTH), BF16),
        scratch_shapes=[
            pltpu.VMEM((3, CHUNK, SC), F32),
            pltpu.VMEM((SC, SC), BF16),
            pltpu.VMEM((2, N_GATE, seq_len), F32),
            pltpu.VMEM((seq_len, 3 * N_GATE), F32),
        ],
        compiler_params=pltpu.CompilerParams(
            dimension_semantics=("arbitrary",), vmem_limit_bytes=VMEM_LIMIT),
        name="gdn",
    )(act, gate_t, alog_s, dtb_s, conv_w, gdn_norm_w)


def _split3(x):
    hi = x.astype(BF16).astype(F32)
    r1 = x - hi
    mid = r1.astype(BF16).astype(F32)
    return hi, mid, r1 - mid


def _attn_kernel(lam_init, q_ref, k_ref, vt_ref, z_ref, lq1_ref, lk1_ref, lq2_ref, lk2_ref,
                 nw_ref, o_ref, bias_ref, kext_ref, vtx_ref, acc_ref, m_ref):
    h = pl.program_id(0)
    b = pl.program_id(1)
    seq_len = k_ref.shape[0]
    lane = lax.broadcasted_iota(jnp.int32, (1, HEAD_DIM), 1)
    n_bias = 6
    log2e = math.log2(math.e)
    assert BQ == 2 * BK

    @pl.when(b == 0)
    def _():
        slope = lax.shift_left(jnp.int32(1), 2 * (DIFF_HEADS - 1 - h)).astype(F32) * (log2e / 256.0)
        pos = lax.broadcasted_iota(jnp.int32, (seq_len, 1), 0)
        in_block = (pos & (BK - 1)).astype(F32) * slope
        block_off = (pos & -BK).astype(F32) * slope
        terms = _split3(in_block) + _split3(block_off)
        tile = jnp.zeros((seq_len, HEAD_DIM), F32)
        for j, term in enumerate(terms):
            tile = jnp.where((lane == j) | (lane == DIFF_QK_DIM + j), term, tile)
        bias_ref[...] = tile
        ones_row = lax.broadcasted_iota(jnp.int32, (V_PAD, seq_len), 0) == 0
        vtx_ref[HEAD_DIM:, :] = jnp.where(ones_row, 1.0, 0.0).astype(BF16)

    k = k_ref[...].astype(F32)
    kext_ref[0] = jnp.where(lane < DIFF_QK_DIM, k, bias_ref[...]).astype(BF16)
    kext_ref[1] = jnp.where(lane >= DIFF_QK_DIM, k, bias_ref[...]).astype(BF16)
    vtx_ref[0:HEAD_DIM, :] = vt_ref[...]

    lam = (jnp.exp(jnp.sum(lq1_ref[...] * lk1_ref[...], axis=-1, keepdims=True))
           - jnp.exp(jnp.sum(lq2_ref[...] * lk2_ref[...], axis=-1, keepdims=True)) + lam_init)

    def scores(q_ext, kb, masked, lo):
        rows = slice(kb * BK, (kb + 1) * BK)
        out = []
        for c in range(2):
            s = _dot_nt(kext_ref[c, rows, :], q_ext[c][lo:, :])
            if masked:
                krel = lax.broadcasted_iota(jnp.int32, (BK, BQ - lo), 0)
                qrel = lax.broadcasted_iota(jnp.int32, (BK, BQ - lo), 1)
                s = jnp.where(krel <= qrel, s, MASK_VALUE)
            out.append((s, jnp.max(s, axis=0, keepdims=True)))
        return out

    def accumulate(par, kb, first, second, lo):
        vt = vtx_ref[:, kb * BK:(kb + 2) * BK]
        for c in range(2):
            (s_a, max_a), (s_b, max_b) = first[c], second[c]
            m_old = m_ref[par, c]
            if lo:
                max_b = jnp.concatenate([jnp.full((1, lo), MASK_VALUE, F32), max_b], axis=1)
            m_new = jnp.maximum(m_old, jnp.maximum(max_a, max_b))
            alpha = jnp.exp2(m_old - m_new)
            p_a = jnp.exp2(s_a - m_new).astype(BF16)
            p_b = jnp.exp2(s_b - m_new[:, lo:]).astype(BF16)
            both = _dot(vt, jnp.concatenate([p_a[:, lo:], p_b], axis=0))
            if lo:
                both = jnp.concatenate([_dot(vt[:, :BK], p_a[:, :lo]), both], axis=1)
            acc_ref[par, c] = alpha * acc_ref[par, c] + both
            m_ref[par, c] = m_new

    for qi in range(seq_len // BQ):
        par = qi % 2
        qrows = slice(qi * BQ, (qi + 1) * BQ)
        q = q_ref[qrows, :].astype(F32) * (DIFF_QK_DIM ** -0.5 * log2e)
        q_ext = (
            jnp.where(lane < DIFF_QK_DIM, q,
                      jnp.where(lane < DIFF_QK_DIM + n_bias, 1.0, 0.0)).astype(BF16),
            jnp.where(lane >= DIFF_QK_DIM, q, jnp.where(lane < n_bias, 1.0, 0.0)).astype(BF16),
        )
        m_ref[par] = jnp.full(m_ref.shape[1:], MASK_VALUE, F32)
        acc_ref[par] = jnp.zeros(acc_ref.shape[1:], F32)
        n_pairs = qi + 1
        lo_of = lambda j: BK if j == n_pairs - 1 else 0

        def pair_scores(j):
            diag = j == n_pairs - 1
            return (scores(q_ext, 2 * j, diag, 0), scores(q_ext, 2 * j + 1, diag, lo_of(j)))

        cur = pair_scores(0)
        for j in range(n_pairs):
            nxt = pair_scores(j + 1) if j + 1 < n_pairs else None
            accumulate(par, 2 * j, cur[0], cur[1], lo_of(j))
            cur = nxt

        num = (acc_ref[par, 0, 0:HEAD_DIM, :], acc_ref[par, 1, 0:HEAD_DIM, :])
        den = (acc_ref[par, 0, HEAD_DIM:HEAD_DIM + 1, :], acc_ref[par, 1, HEAD_DIM:HEAD_DIM + 1, :])
        o_t = num[0] / den[0] - lam * (num[1] / den[1])
        o = o_t.T
        o = o * lax.rsqrt(jnp.mean(o * o, axis=-1, keepdims=True) + NORM_EPS)
        o = o * nw_ref[...] * (1.0 - lam_init)
        o_ref[qrows, :] = (o * _silu(z_ref[qrows, :].astype(F32))).astype(BF16)


def _attn(act, v_t, lq1, lk1, lq2, lk2, diff_norm_w, lam_init, batch, seq_len):
    t = act.shape[0]
    small = lambda shape: pl.BlockSpec(shape, lambda h, b: (0, 0))
    return pl.pallas_call(
        functools.partial(_attn_kernel, lam_init),
        grid=(DIFF_HEADS, batch),
        in_specs=[
            pl.BlockSpec((seq_len, HEAD_DIM), lambda h, b: (b, COL_DQ + h)),
            pl.BlockSpec((seq_len, HEAD_DIM), lambda h, b: (b, COL_DK + h)),
            pl.BlockSpec((HEAD_DIM, seq_len), lambda h, b: (h, b)),
            pl.BlockSpec((seq_len, HEAD_DIM), lambda h, b: (b, COL_DZ + h)),
            small((1, DIFF_QK_DIM)), small((1, DIFF_QK_DIM)),
            small((1, DIFF_QK_DIM)), small((1, DIFF_QK_DIM)),
            small((1, HEAD_DIM)),
        ],
        out_specs=pl.BlockSpec((seq_len, HEAD_DIM), lambda h, b: (b, h)),
        out_shape=jax.ShapeDtypeStruct((t, DIFF_WIDTH), BF16),
        scratch_shapes=[
            pltpu.VMEM((seq_len, HEAD_DIM), F32),
            pltpu.VMEM((2, seq_len, HEAD_DIM), BF16),
            pltpu.VMEM((HEAD_DIM + V_PAD, seq_len), BF16),
            pltpu.VMEM((2, 2, HEAD_DIM + V_PAD, BQ), F32),
            pltpu.VMEM((2, 2, 1, BQ), F32),
        ],
        compiler_params=pltpu.CompilerParams(
            dimension_semantics=("arbitrary", "arbitrary"), vmem_limit_bytes=VMEM_LIMIT),
        name="diffattn",
    )(act, act, v_t, act, lq1, lk1, lq2, lk2, diff_norm_w)


def _outproj_kernel(oa_ref, ob_ref, x_ref, w_ref, fw_ref, out_ref):
    mix = _dot(oa_ref[...], w_ref[0:GDN_WIDTH, :]) + _dot(ob_ref[...], w_ref[GDN_WIDTH:, :])
    y = x_ref[...] + mix
    ms = jnp.mean(y * y, axis=-1, keepdims=True)
    out_ref[...] = y * lax.rsqrt(ms + NORM_EPS) * fw_ref[...]


def _outproj(o_a, o_b, xf, w_out, final_norm_w):
    t = xf.shape[0]
    return pl.pallas_call(
        _outproj_kernel,
        grid=(t // TM_OUT,),
        in_specs=[
            pl.BlockSpec((TM_OUT, GDN_WIDTH), lambda i: (i, 0)),
            pl.BlockSpec((TM_OUT, DIFF_WIDTH), lambda i: (i, 0)),
            pl.BlockSpec((TM_OUT, D_MODEL), lambda i: (i, 0)),
            pl.BlockSpec((D_MODEL, D_MODEL), lambda i: (0, 0)),
            pl.BlockSpec((1, D_MODEL), lambda i: (0, 0)),
        ],
        out_specs=pl.BlockSpec((TM_OUT, D_MODEL), lambda i: (i, 0)),
        out_shape=jax.ShapeDtypeStruct((t, D_MODEL), F32),
        compiler_params=pltpu.CompilerParams(
            dimension_semantics=("arbitrary",), vmem_limit_bytes=VMEM_LIMIT),
        name="outproj",
    )(o_a, o_b, xf, w_out, final_norm_w)


def kernel(x, norm_w, w_in, conv_w, a_log, dt_bias, gdn_norm_w, lambda_q1, lambda_k1,
           lambda_q2, lambda_k2, diff_norm_w, w_out, final_norm_w):
    batch, seq_len, d_model = x.shape
    depth = norm_w.shape[0]
    assert depth == 1 and d_model == D_MODEL
    assert seq_len % TM_IN == 0 and seq_len % SC == 0 and seq_len % BQ == 0 and BQ % BK == 0
    n_wide_a = 4 * GDN_WIDTH
    xf = x.reshape(batch * seq_len, d_model)

    w = w_in[0]
    c_dq = n_wide_a + N_GATE
    c_dv = c_dq + 2 * DIFF_WIDTH
    c_dz = c_dv + DIFF_WIDTH
    w_main = jnp.concatenate([w[:, :n_wide_a], w[:, c_dq:c_dv], w[:, c_dz:]], axis=1).astype(BF16)
    w_v_t = w[:, c_dv:c_dz].T.astype(BF16)
    w_gate_t = w[:, n_wide_a:n_wide_a + N_GATE].T.astype(BF16)

    act, gate_t, v_t = _inproj(xf, norm_w[0][None, :], w_main, w_gate_t, w_v_t)

    pad_s = lambda vec: jnp.pad(vec, (GDN_HEADS, 0))[:, None]
    o_a = _gdn(act, gate_t, pad_s(a_log[0]), pad_s(dt_bias[0]), conv_w[0], gdn_norm_w[0][None, :],
               batch, seq_len)

    lam_init = 0.8 - 0.6 * math.exp(-0.3 * 0)
    o_b = _attn(act, v_t, lambda_q1[0][None, :], lambda_k1[0][None, :], lambda_q2[0][None, :],
                lambda_k2[0][None, :], diff_norm_w[0][None, :], lam_init, batch, seq_len)

    out = _outproj(o_a, o_b, xf, w_out[0].astype(BF16), final_norm_w[None, :])
    return out.reshape(batch, seq_len, d_model)
```

```python
import functools
import math

import jax
import jax.numpy as jnp
from jax import lax
from jax.experimental import pallas as pl
from jax.experimental.pallas import tpu as pltpu

F32 = jnp.float32
BF16 = jnp.bfloat16

D_MODEL = 1024
GDN_HEADS = 4
HEAD_DIM = 128
GDN_WIDTH = GDN_HEADS * HEAD_DIM
DIFF_HEADS = 4
DIFF_QK_DIM = 64
DIFF_WIDTH = DIFF_HEADS * HEAD_DIM
CONV_K = 4
NORM_EPS = 1e-6
N_GATE = 2 * GDN_HEADS
N_MAIN = 4 * GDN_WIDTH + 3 * DIFF_WIDTH
N_CONV = 3 * GDN_WIDTH
LANES = 128
MASK_VALUE = -1e30

COL_GQ, COL_GK, COL_GV, COL_GZ = 0, 4, 8, 12
COL_DQ, COL_DK, COL_DZ = 16, 20, 24

TM_IN = 1024
NC_IN = 256
HALO = 16
TM_OUT = 1024
SC = 256
CHUNK = 64
GDN_GROUP = 2
BQ = 512
BK = 256
V_PAD = 16
VMEM_LIMIT = 48 * 1024 * 1024


def _sigmoid(x):
    return 1.0 / (1.0 + jnp.exp(-x))


def _silu(x):
    half = 0.5 * x
    return half + half * jnp.tanh(half)


def _softplus(x):
    return jnp.maximum(x, 0.0) + jnp.log1p(jnp.exp(-jnp.abs(x)))


def _dot(a, b):
    return jnp.dot(a, b, preferred_element_type=F32)


def _dot_nt(a, b):
    return lax.dot_general(a, b, (((1,), (1,)), ((), ())), preferred_element_type=F32)


def _dot_tn(a, b):
    return lax.dot_general(a, b, (((0,), (0,)), ((), ())), preferred_element_type=F32)


def _inproj_kernel(x_ref, nw_ref, w_ref, wgt_ref, wvt_ref, main_ref, gate_t_ref, vt_ref):
    x = x_ref[...]
    ms = jnp.mean(x * x, axis=-1, keepdims=True)
    h = (x * lax.rsqrt(ms + NORM_EPS) * nw_ref[...]).astype(BF16)

    gate_t_ref[...] = _dot_nt(wgt_ref[...], h)
    vt_ref[...] = _dot_nt(wvt_ref[...], h).astype(BF16)
    for n0 in range(0, N_MAIN, NC_IN):
        main_ref[:, n0:n0 + NC_IN] = _dot(h, w_ref[:, n0:n0 + NC_IN]).astype(BF16)


def _inproj(xf, norm_w, w_main, w_gate_t, w_v_t):
    t = xf.shape[0]
    return pl.pallas_call(
        _inproj_kernel,
        grid=(t // TM_IN,),
        in_specs=[
            pl.BlockSpec((TM_IN, D_MODEL), lambda i: (i, 0)),
            pl.BlockSpec((1, D_MODEL), lambda i: (0, 0)),
            pl.BlockSpec((D_MODEL, N_MAIN), lambda i: (0, 0)),
            pl.BlockSpec((N_GATE, D_MODEL), lambda i: (0, 0)),
            pl.BlockSpec((DIFF_WIDTH, D_MODEL), lambda i: (0, 0)),
        ],
        out_specs=[
            pl.BlockSpec((TM_IN, N_MAIN), lambda i: (i, 0)),
            pl.BlockSpec((N_GATE, TM_IN), lambda i: (0, i)),
            pl.BlockSpec((DIFF_WIDTH, TM_IN), lambda i: (0, i)),
        ],
        out_shape=[
            jax.ShapeDtypeStruct((t, N_MAIN), BF16),
            jax.ShapeDtypeStruct((N_GATE, t), F32),
            jax.ShapeDtypeStruct((DIFF_WIDTH, t), BF16),
        ],
        compiler_params=pltpu.CompilerParams(
            dimension_semantics=("arbitrary",), vmem_limit_bytes=VMEM_LIMIT),
        name="inproj",
    )(xf, norm_w, w_main, w_gate_t, w_v_t)


def _gdn_kernel(act_ref, gate_t_ref, alog_s_ref, dtb_s_ref, cw_ref, gnw_ref, o_ref, mask_ref, bd_ref,
                row_ref, col_ref):
    seq_len = act_ref.shape[0]
    n_chunks = SC // CHUNK
    heads = range(GDN_HEADS)
    i_cat = lax.broadcasted_iota(jnp.int32, (CHUNK, SC), 0)
    j_cat = lax.broadcasted_iota(jnp.int32, (CHUNK, SC), 1) & (CHUNK - 1)
    mask_ref[0] = jnp.where(i_cat >= j_cat, 0.0, MASK_VALUE)
    mask_ref[1] = jnp.where(i_cat > j_cat, -1.0, 0.0)
    mask_ref[2] = jnp.where(i_cat == j_cat, 1.0, 0.0)
    r = lax.broadcasted_iota(jnp.int32, (SC, SC), 0)
    c = lax.broadcasted_iota(jnp.int32, (SC, SC), 1)
    bd_ref[...] = jnp.where((r & -CHUNK) == (c & -CHUNK), 1.0, 0.0).astype(BF16)
    lane_chunk = lax.broadcasted_iota(jnp.int32, (1, SC), 1) & -CHUNK

    scale = HEAD_DIM ** -0.5

    lane_in_chunk = lax.broadcasted_iota(jnp.int32, (N_GATE, seq_len), 1) & (CHUNK - 1)
    gate_t = gate_t_ref[...]
    beta = _sigmoid(gate_t)
    log_beta = -_softplus(-gate_t)
    g_step = -jnp.exp(alog_s_ref[...]) * _softplus(gate_t + dtb_s_ref[...])
    gc = g_step
    g_after = jnp.zeros_like(g_step)
    tail = g_step
    step = 1
    while step < CHUNK:
        gc = gc + jnp.where(lane_in_chunk >= step, pltpu.roll(gc, step, axis=1), 0.0)
        ahead = jnp.where(lane_in_chunk + step < CHUNK, pltpu.roll(tail, seq_len - step, axis=1), 0.0)
        g_after = g_after + ahead
        tail = tail + ahead
        step *= 2
    e_gc = jnp.exp(gc)
    is_beta_row = lax.broadcasted_iota(jnp.int32, (N_GATE, seq_len), 0) < GDN_HEADS
    swap = pltpu.roll(e_gc, GDN_HEADS, axis=0)
    row_ref[0] = gc
    row_ref[1] = e_gc
    col_ref[...] = jnp.concatenate([
        jnp.where(is_beta_row, beta, gc),
        jnp.where(is_beta_row, log_beta, e_gc),
        jnp.where(is_beta_row, beta * swap, jnp.exp(g_after))], axis=0).T

    def cat(x):
        out = x[(n_chunks - 1) * CHUNK:, :]
        for ci in range(n_chunks - 2, -1, -1):
            out = jnp.where(lane_chunk == ci * CHUNK, x[ci * CHUNK:(ci + 1) * CHUNK, :], out)
        return out

    def block_diag(x_cat):
        xb = x_cat.astype(BF16)
        zero = jnp.zeros((CHUNK, LANES), BF16)
        row_blocks = []
        for ci in range(n_chunks):
            g = ci * CHUNK // LANES
            grp = slice(g * LANES, (g + 1) * LANES)
            part = xb[:, grp] * bd_ref[ci * CHUNK:(ci + 1) * CHUNK, grp]
            row_blocks.append(jnp.concatenate(
                [part if j == g else zero for j in range(SC // LANES)], axis=1))
        return jnp.concatenate(row_blocks, axis=0)

    def head_cols(col, h):
        return slice((col + h) * HEAD_DIM, (col + h + 1) * HEAD_DIM)

    def conv_silu(sc, cols):
        x = act_ref[sc * SC:(sc + 1) * SC, cols].astype(F32)
        if sc == 0:
            hist = jnp.zeros((HALO, HEAD_DIM), F32)
        else:
            hist = act_ref[sc * SC - HALO:sc * SC, cols].astype(F32)
        delta = hist - x[SC - HALO:, :]
        row = lax.broadcasted_iota(jnp.int32, (HALO, HEAD_DIM), 0)
        y = x * cw_ref[CONV_K - 1:CONV_K, cols]
        fix = jnp.zeros((HALO, HEAD_DIM), F32)
        for s in range(1, CONV_K):
            wk = cw_ref[CONV_K - 1 - s:CONV_K - s, cols]
            y = y + pltpu.roll(x, s, axis=0) * wk
            fix = fix + jnp.where(row < s, pltpu.roll(delta, s, axis=0), 0.0) * wk
        return _silu(jnp.concatenate([y[:HALO, :] + fix, y[HALO:, :]], axis=0))

    def conv_inputs(steps):
        return [tuple(conv_silu(sc, head_cols(c, h)) for c in (COL_GQ, COL_GK, COL_GV))
                for sc in steps for h in heads]

    def prepare(steps, qkv):
        causal, neg_strict, eye = mask_ref[0], mask_ref[1], mask_ref[2]
        units = [(sc, h) for sc in steps for h in heads]
        n_pow, a_cat, rhs, q_dec, k_dec, decay_last = [], [], [], [], [], []
        for (sc, h), (q, k, v) in zip(units, qkv):
            rows = slice(sc * SC, (sc + 1) * SC)
            col = lambda j: col_ref[rows, j:j + 1]
            q = q * (lax.rsqrt(jnp.sum(q * q, axis=-1, keepdims=True) + 1e-6) * scale)
            k = k * lax.rsqrt(jnp.sum(k * k, axis=-1, keepdims=True) + 1e-6)
            hb, hg = h, GDN_HEADS + h
            kb = k.astype(BF16)
            gram = _dot_nt(jnp.concatenate([kb, q.astype(BF16)], axis=0), kb)
            e_cat = cat(col(hg)) - row_ref[0, hg:hg + 1, rows] + causal
            n_pow.append(cat(gram[:SC]) * jnp.exp(e_cat + cat(col(N_GATE + hb))) * neg_strict)
            a_cat.append(cat(gram[SC:]) * jnp.exp(e_cat))
            rhs.append(jnp.concatenate([col(hb) * v, col(2 * N_GATE + hb) * k], axis=1).astype(BF16))
            q_dec.append(q * col(N_GATE + hg))
            kd = (k * col(2 * N_GATE + hg)).astype(BF16)
            k_dec.append([kd[ci * CHUNK:(ci + 1) * CHUNK, :] for ci in range(n_chunks)])
            decay_last.append([row_ref[1, hg:hg + 1, sc * SC + (ci + 1) * CHUNK - 1:sc * SC + (ci + 1) * CHUNK]
                               for ci in range(n_chunks)])

        every = range(len(units))
        t_cat = [eye + n_pow[u] for u in every]
        n_pow = [_dot(n_pow[u].astype(BF16), block_diag(n_pow[u])) for u in every]
        span = 2
        while span < CHUNK // 2:
            both = [_dot(jnp.concatenate([t_cat[u], n_pow[u]], axis=0).astype(BF16), block_diag(n_pow[u]))
                    for u in every]
            t_cat = [t_cat[u] + both[u][:CHUNK] for u in every]
            n_pow = [both[u][CHUNK:] for u in every]
            span *= 2
        t_cat = [t_cat[u] + _dot(t_cat[u].astype(BF16), block_diag(n_pow[u])) for u in every]

        uw = [_dot(block_diag(t_cat[u]), rhs[u]) for u in every]
        kt = [[_dot_tn(k_dec[u][ci], uw[u][ci * CHUNK:(ci + 1) * CHUNK, :].astype(BF16))
               for ci in range(n_chunks)] for u in every]
        per_step = lambda xs, i: xs[i * GDN_HEADS:(i + 1) * GDN_HEADS]
        return [tuple(per_step(xs, i) for xs in (uw, kt, q_dec, a_cat, decay_last))
                for i in range(len(steps))]

    def finish(sc, prep, state):
        rows = slice(sc * SC, (sc + 1) * SC)
        uw, kt, q_dec, a_cat, decay_last = prep
        state = list(state)
        state_in = [[None] * n_chunks for _ in heads]
        for ci in range(n_chunks):
            for h in heads:
                sb = state[h].astype(BF16)
                state_in[h][ci] = sb
                state[h] = (state[h] * decay_last[h][ci]
                            - _dot(kt[h][ci][:, HEAD_DIM:].astype(BF16), sb) + kt[h][ci][:, :HEAD_DIM])
        for h in heads:
            inter = []
            for ci in range(n_chunks):
                cr = slice(ci * CHUNK, (ci + 1) * CHUNK)
                lhs = jnp.concatenate([uw[h][cr, HEAD_DIM:], q_dec[h][cr, :]], axis=0).astype(BF16)
                inter.append(_dot(lhs, state_in[h][ci]))
            v_new = jnp.concatenate([uw[h][ci * CHUNK:(ci + 1) * CHUNK, :HEAD_DIM] - inter[ci][:CHUNK]
                                     for ci in range(n_chunks)], axis=0).astype(BF16)
            o = (jnp.concatenate([inter[ci][CHUNK:] for ci in range(n_chunks)], axis=0)
                 + _dot(block_diag(a_cat[h]), v_new))
            o = o * lax.rsqrt(jnp.mean(o * o, axis=-1, keepdims=True) + NORM_EPS)
            z = act_ref[rows, head_cols(COL_GZ, h)].astype(F32)
            o_ref[rows, head_cols(0, h)] = (o * (_silu(z) * gnw_ref[...])).astype(BF16)
        return state

    n_groups = seq_len // (SC * GDN_GROUP)
    group = lambda g: list(range(g * GDN_GROUP, (g + 1) * GDN_GROUP))
    state = [jnp.zeros((HEAD_DIM, HEAD_DIM), F32) for _ in heads]
    preps = prepare(group(0), conv_inputs(group(0)))
    qkv_next = conv_inputs(group(1)) if n_groups > 1 else None
    for g in range(n_groups):
        nxt = prepare(group(g + 1), qkv_next) if g + 1 < n_groups else None
        qkv_next = conv_inputs(group(g + 2)) if g + 2 < n_groups else None
        for sc, prep in zip(group(g), preps):
            state = finish(sc, prep, state)
        preps = nxt


def _gdn(act, gate_t, alog_s, dtb_s, conv_w, gdn_norm_w, batch, seq_len):
    t = act.shape[0]
    small = lambda shape: pl.BlockSpec(shape, lambda b: (0, 0))
    return pl.pallas_call(
        _gdn_kernel,
        grid=(batch,),
        in_specs=[
            pl.BlockSpec((seq_len, 4 * GDN_WIDTH), lambda b: (b, 0)),
            pl.BlockSpec((N_GATE, seq_len), lambda b: (0, b)),
            small((N_GATE, 1)), small((N_GATE, 1)), small((CONV_K, N_CONV)),
            small((1, HEAD_DIM)),
        ],
        out_specs=pl.BlockSpec((seq_len, GDN_WIDTH), lambda b: (b, 0)),
        out_shape=jax.ShapeDtypeStruct((t, GDN_WIDTH), BF16),
        scratch_shapes=[
            pltpu.VMEM((3, CHUNK, SC), F32),
            pltpu.VMEM((SC, SC), BF16),
            pltpu.VMEM((2, N_GATE, seq_len), F32),
            pltpu.VMEM((seq_len, 3 * N_GATE), F32),
        ],
        compiler_params=pltpu.CompilerParams(
            dimension_semantics=("arbitrary",), vmem_limit_bytes=VMEM_LIMIT),
        name="gdn",
    )(act, gate_t, alog_s, dtb_s, conv_w, gdn_norm_w)


def _split3(x):
    hi = x.astype(BF16).astype(F32)
    r1 = x - hi
    mid = r1.astype(BF16).astype(F32)
    return hi, mid, r1 - mid


def _attn_kernel(lam_init, q_ref, k_ref, vt_ref, z_ref, lq1_ref, lk1_ref, lq2_ref, lk2_ref,
                 nw_ref, o_ref, bias_ref, kext_ref, vtx_ref, acc_ref, m_ref):
    h = pl.program_id(0)
    b = pl.program_id(1)
    seq_len = k_ref.shape[0]
    lane = lax.broadcasted_iota(jnp.int32, (1, HEAD_DIM), 1)
    n_bias = 6
    log2e = math.log2(math.e)
    assert BQ == 2 * BK

    @pl.when(b == 0)
    def _():
        slope = lax.shift_left(jnp.int32(1), 2 * (DIFF_HEADS - 1 - h)).astype(F32) * (log2e / 256.0)
        pos = lax.broadcasted_iota(jnp.int32, (seq_len, 1), 0)
        in_block = (pos & (BK - 1)).astype(F32) * slope
        block_off = (pos & -BK).astype(F32) * slope
        terms = _split3(in_block) + _split3(block_off)
        tile = jnp.zeros((seq_len, HEAD_DIM), F32)
        for j, term in enumerate(terms):
            tile = jnp.where((lane == j) | (lane == DIFF_QK_DIM + j), term, tile)
        bias_ref[...] = tile
        ones_row = lax.broadcasted_iota(jnp.int32, (V_PAD, seq_len), 0) == 0
        vtx_ref[HEAD_DIM:, :] = jnp.where(ones_row, 1.0, 0.0).astype(BF16)

    k = k_ref[...].astype(F32)
    kext_ref[0] = jnp.where(lane < DIFF_QK_DIM, k, bias_ref[...]).astype(BF16)
    kext_ref[1] = jnp.where(lane >= DIFF_QK_DIM, k, bias_ref[...]).astype(BF16)
    vtx_ref[0:HEAD_DIM, :] = vt_ref[...]

    lam = (jnp.exp(jnp.sum(lq1_ref[...] * lk1_ref[...], axis=-1, keepdims=True))
           - jnp.exp(jnp.sum(lq2_ref[...] * lk2_ref[...], axis=-1, keepdims=True)) + lam_init)

    def scores(q_ext, kb, masked, lo):
        rows = slice(kb * BK, (kb + 1) * BK)
        out = []
        for c in range(2):
            s = _dot_nt(kext_ref[c, rows, :], q_ext[c][lo:, :])
            if masked:
                krel = lax.broadcasted_iota(jnp.int32, (BK, BQ - lo), 0)
                qrel = lax.broadcasted_iota(jnp.int32, (BK, BQ - lo), 1)
                s = jnp.where(krel <= qrel, s, MASK_VALUE)
            out.append((s, jnp.max(s, axis=0, keepdims=True)))
        return out

    def accumulate(par, kb, first, second, lo):
        vt = vtx_ref[:, kb * BK:(kb + 2) * BK]
        for c in range(2):
            (s_a, max_a), (s_b, max_b) = first[c], second[c]
            m_old = m_ref[par, c]
            if lo:
                max_b = jnp.concatenate([jnp.full((1, lo), MASK_VALUE, F32), max_b], axis=1)
            m_new = jnp.maximum(m_old, jnp.maximum(max_a, max_b))
            alpha = jnp.exp2(m_old - m_new)
            p_a = jnp.exp2(s_a - m_new).astype(BF16)
            p_b = jnp.exp2(s_b - m_new[:, lo:]).astype(BF16)
            both = _dot(vt, jnp.concatenate([p_a[:, lo:], p_b], axis=0))
            if lo:
                both = jnp.concatenate([_dot(vt[:, :BK], p_a[:, :lo]), both], axis=1)
            acc_ref[par, c] = alpha * acc_ref[par, c] + both
            m_ref[par, c] = m_new

    for qi in range(seq_len // BQ):
        par = qi % 2
        qrows = slice(qi * BQ, (qi + 1) * BQ)
        q = q_ref[qrows, :].astype(F32) * (DIFF_QK_DIM ** -0.5 * log2e)
        q_ext = (
            jnp.where(lane < DIFF_QK_DIM, q,
                      jnp.where(lane < DIFF_QK_DIM + n_bias, 1.0, 0.0)).astype(BF16),
            jnp.where(lane >= DIFF_QK_DIM, q, jnp.where(lane < n_bias, 1.0, 0.0)).astype(BF16),
        )
        m_ref[par] = jnp.full(m_ref.shape[1:], MASK_VALUE, F32)
        acc_ref[par] = jnp.zeros(acc_ref.shape[1:], F32)
        n_pairs = qi + 1
        lo_of = lambda j: BK if j == n_pairs - 1 else 0

        def pair_scores(j):
            diag = j == n_pairs - 1
            return (scores(q_ext, 2 * j, diag, 0), scores(q_ext, 2 * j + 1, diag, lo_of(j)))

        cur = pair_scores(0)
        for j in range(n_pairs):
            nxt = pair_scores(j + 1) if j + 1 < n_pairs else None
            accumulate(par, 2 * j, cur[0], cur[1], lo_of(j))
            cur = nxt

        num = (acc_ref[par, 0, 0:HEAD_DIM, :], acc_ref[par, 1, 0:HEAD_DIM, :])
        den = (acc_ref[par, 0, HEAD_DIM:HEAD_DIM + 1, :], acc_ref[par, 1, HEAD_DIM:HEAD_DIM + 1, :])
        o_t = num[0] / den[0] - lam * (num[1] / den[1])
        o = o_t.T
        o = o * lax.rsqrt(jnp.mean(o * o, axis=-1, keepdims=True) + NORM_EPS)
        o = o * nw_ref[...] * (1.0 - lam_init)
        o_ref[qrows, :] = (o * _silu(z_ref[qrows, :].astype(F32))).astype(BF16)


def _attn(act, v_t, lq1, lk1, lq2, lk2, diff_norm_w, lam_init, batch, seq_len):
    t = act.shape[0]
    small = lambda shape: pl.BlockSpec(shape, lambda h, b: (0, 0))
    return pl.pallas_call(
        functools.partial(_attn_kernel, lam_init),
        grid=(DIFF_HEADS, batch),
        in_specs=[
            pl.BlockSpec((seq_len, HEAD_DIM), lambda h, b: (b, COL_DQ + h)),
            pl.BlockSpec((seq_len, HEAD_DIM), lambda h, b: (b, COL_DK + h)),
            pl.BlockSpec((HEAD_DIM, seq_len), lambda h, b: (h, b)),
            pl.BlockSpec((seq_len, HEAD_DIM), lambda h, b: (b, COL_DZ + h)),
            small((1, DIFF_QK_DIM)), small((1, DIFF_QK_DIM)),
            small((1, DIFF_QK_DIM)), small((1, DIFF_QK_DIM)),
            small((1, HEAD_DIM)),
        ],
        out_specs=pl.BlockSpec((seq_len, HEAD_DIM), lambda h, b: (b, h)),
        out_shape=jax.ShapeDtypeStruct((t, DIFF_WIDTH), BF16),
        scratch_shapes=[
            pltpu.VMEM((seq_len, HEAD_DIM), F32),
            pltpu.VMEM((2, seq_len, HEAD_DIM), BF16),
            pltpu.VMEM((HEAD_DIM + V_PAD, seq_len), BF16),
            pltpu.VMEM((2, 2, HEAD_DIM + V_PAD, BQ), F32),
            pltpu.VMEM((2, 2, 1, BQ), F32),
        ],
        compiler_params=pltpu.CompilerParams(
            dimension_semantics=("arbitrary", "arbitrary"), vmem_limit_bytes=VMEM_LIMIT),
        name="diffattn",
    )(act, act, v_t, act, lq1, lk1, lq2, lk2, diff_norm_w)


def _outproj_kernel(oa_ref, ob_ref, x_ref, w_ref, fw_ref, out_ref):
    mix = _dot(oa_ref[...], w_ref[0:GDN_WIDTH, :]) + _dot(ob_ref[...], w_ref[GDN_WIDTH:, :])
    y = x_ref[...] + mix
    ms = jnp.mean(y * y, axis=-1, keepdims=True)
    out_ref[...] = y * lax.rsqrt(ms + NORM_EPS) * fw_ref[...]


def _outproj(o_a, o_b, xf, w_out, final_norm_w):
    t = xf.shape[0]
    return pl.pallas_call(
        _outproj_kernel,
        grid=(t // TM_OUT,),
        in_specs=[
            pl.BlockSpec((TM_OUT, GDN_WIDTH), lambda i: (i, 0)),
            pl.BlockSpec((TM_OUT, DIFF_WIDTH), lambda i: (i, 0)),
            pl.BlockSpec((TM_OUT, D_MODEL), lambda i: (i, 0)),
            pl.BlockSpec((D_MODEL, D_MODEL), lambda i: (0, 0)),
            pl.BlockSpec((1, D_MODEL), lambda i: (0, 0)),
        ],
        out_specs=pl.BlockSpec((TM_OUT, D_MODEL), lambda i: (i, 0)),
        out_shape=jax.ShapeDtypeStruct((t, D_MODEL), F32),
        compiler_params=pltpu.CompilerParams(
            dimension_semantics=("arbitrary",), vmem_limit_bytes=VMEM_LIMIT),
        name="outproj",
    )(o_a, o_b, xf, w_out, final_norm_w)


def kernel(x, norm_w, w_in, conv_w, a_log, dt_bias, gdn_norm_w, lambda_q1, lambda_k1,
           lambda_q2, lambda_k2, diff_norm_w, w_out, final_norm_w):
    batch, seq_len, d_model = x.shape
    depth = norm_w.shape[0]
    assert depth == 1 and d_model == D_MODEL
    assert seq_len % TM_IN == 0 and seq_len % SC == 0 and seq_len % BQ == 0 and BQ % BK == 0
    n_wide_a = 4 * GDN_WIDTH
    xf = x.reshape(batch * seq_len, d_model)

    w = w_in[0]
    c_dq = n_wide_a + N_GATE
    c_dv = c_dq + 2 * DIFF_WIDTH
    c_dz = c_dv + DIFF_WIDTH
    w_main = jnp.concatenate([w[:, :n_wide_a], w[:, c_dq:c_dv], w[:, c_dz:]], axis=1).astype(BF16)
    w_v_t = w[:, c_dv:c_dz].T.astype(BF16)
    w_gate_t = w[:, n_wide_a:n_wide_a + N_GATE].T.astype(BF16)

    act, gate_t, v_t = _inproj(xf, norm_w[0][None, :], w_main, w_gate_t, w_v_t)

    pad_s = lambda vec: jnp.pad(vec, (GDN_HEADS, 0))[:, None]
    o_a = _gdn(act, gate_t, pad_s(a_log[0]), pad_s(dt_bias[0]), conv_w[0], gdn_norm_w[0][None, :],
               batch, seq_len)

    lam_init = 0.8 - 0.6 * math.exp(-0.3 * 0)
    o_b = _attn(act, v_t, lambda_q1[0][None, :], lambda_k1[0][None, :], lambda_q2[0][None, :],
                lambda_k2[0][None, :], diff_norm_w[0][None, :], lam_init, batch, seq_len)

    out = _outproj(o_a, o_b, xf, w_out[0].astype(BF16), final_norm_w[None, :])
    return out.reshape(batch, seq_len, d_model)
```

```python
import functools
import math

import jax
import jax.numpy as jnp
from jax import lax
from jax.experimental import pallas as pl
from jax.experimental.pallas import tpu as pltpu

F32 = jnp.float32
BF16 = jnp.bfloat16

D_MODEL = 1024
GDN_HEADS = 4
HEAD_DIM = 128
GDN_WIDTH = GDN_HEADS * HEAD_DIM
DIFF_HEADS = 4
DIFF_QK_DIM = 64
DIFF_WIDTH = DIFF_HEADS * HEAD_DIM
CONV_K = 4
NORM_EPS = 1e-6
N_GATE = 2 * GDN_HEADS
N_MAIN = 4 * GDN_WIDTH + 3 * DIFF_WIDTH
N_CONV = 3 * GDN_WIDTH
LANES = 128
MASK_VALUE = -1e30

COL_GQ, COL_GK, COL_GV, COL_GZ = 0, 4, 8, 12
COL_DQ, COL_DK, COL_DZ = 16, 20, 24

TM_IN = 1024
NC_IN = 256
TAIL = 16
TM_OUT = 1024
SC = 256
CHUNK = 64
GDN_GROUP = 2
BQ = 512
BK = 256
V_PAD = 16
VMEM_LIMIT = 48 * 1024 * 1024


def _sigmoid(x):
    return 1.0 / (1.0 + jnp.exp(-x))


def _silu(x):
    half = 0.5 * x
    return half + half * jnp.tanh(half)


def _softplus(x):
    return jnp.maximum(x, 0.0) + jnp.log1p(jnp.exp(-jnp.abs(x)))


def _dot(a, b):
    return jnp.dot(a, b, preferred_element_type=F32)


def _dot_nt(a, b):
    return lax.dot_general(a, b, (((1,), (1,)), ((), ())), preferred_element_type=F32)


def _dot_tn(a, b):
    return lax.dot_general(a, b, (((0,), (0,)), ((), ())), preferred_element_type=F32)


def _inproj_kernel(tiles_per_seq, x_ref, nw_ref, w_ref, wgt_ref, wvt_ref, cw_ref,
                   main_ref, gate_t_ref, vt_ref, tail_ref):
    i = pl.program_id(0)
    x = x_ref[...]
    ms = jnp.mean(x * x, axis=-1, keepdims=True)
    h = (x * lax.rsqrt(ms + NORM_EPS) * nw_ref[...]).astype(BF16)

    gate_t_ref[...] = _dot_nt(wgt_ref[...], h)
    vt_ref[...] = _dot_nt(wvt_ref[...], h).astype(BF16)

    @pl.when(i % tiles_per_seq == 0)
    def _():
        tail_ref[...] = jnp.zeros_like(tail_ref)

    tm = x.shape[0]
    row = lax.broadcasted_iota(jnp.int32, (TAIL, LANES), 0)

    def conv_silu(acc, n0):
        cols = slice(n0, n0 + LANES)
        last = acc[tm - TAIL:, :]
        delta = tail_ref[:, cols] - last
        tail_ref[:, cols] = last
        y = acc * cw_ref[CONV_K - 1:CONV_K, cols]
        fix = jnp.zeros((TAIL, LANES), F32)
        for s in range(1, CONV_K):
            wk = cw_ref[CONV_K - 1 - s:CONV_K - s, cols]
            y = y + pltpu.roll(acc, s, axis=0) * wk
            fix = fix + jnp.where(row < s, pltpu.roll(delta, s, axis=0), 0.0) * wk
        main_ref[:, cols] = _silu(y).astype(BF16)
        main_ref[0:TAIL, cols] = _silu(y[:TAIL, :] + fix).astype(BF16)

    plain = [slice(n0, n0 + NC_IN) for n0 in range(N_CONV, N_MAIN, NC_IN)]
    pending = []
    for n0 in range(0, N_CONV, NC_IN):
        acc = _dot(h, w_ref[:, n0:n0 + NC_IN])
        if pending:
            conv_silu(*pending.pop(0))
        if plain:
            pc = plain.pop(0)
            main_ref[:, pc] = _dot(h, w_ref[:, pc]).astype(BF16)
        if pending:
            conv_silu(*pending.pop(0))
        pending = [(acc[:, j:j + LANES], n0 + j) for j in range(0, NC_IN, LANES)]
    for piece in pending:
        if plain:
            pc = plain.pop(0)
            main_ref[:, pc] = _dot(h, w_ref[:, pc]).astype(BF16)
        conv_silu(*piece)
    for pc in plain:
        main_ref[:, pc] = _dot(h, w_ref[:, pc]).astype(BF16)


def _inproj(xf, norm_w, w_main, w_gate_t, w_v_t, conv_w, seq_len):
    t = xf.shape[0]
    tiles_per_seq = seq_len // TM_IN
    return pl.pallas_call(
        functools.partial(_inproj_kernel, tiles_per_seq),
        grid=(t // TM_IN,),
        in_specs=[
            pl.BlockSpec((TM_IN, D_MODEL), lambda i: (i, 0)),
            pl.BlockSpec((1, D_MODEL), lambda i: (0, 0)),
            pl.BlockSpec((D_MODEL, N_MAIN), lambda i: (0, 0)),
            pl.BlockSpec((N_GATE, D_MODEL), lambda i: (0, 0)),
            pl.BlockSpec((DIFF_WIDTH, D_MODEL), lambda i: (0, 0)),
            pl.BlockSpec((CONV_K, N_CONV), lambda i: (0, 0)),
        ],
        out_specs=[
            pl.BlockSpec((TM_IN, N_MAIN), lambda i: (i, 0)),
            pl.BlockSpec((N_GATE, TM_IN), lambda i: (0, i)),
            pl.BlockSpec((DIFF_WIDTH, TM_IN), lambda i: (0, i)),
        ],
        out_shape=[
            jax.ShapeDtypeStruct((t, N_MAIN), BF16),
            jax.ShapeDtypeStruct((N_GATE, t), F32),
            jax.ShapeDtypeStruct((DIFF_WIDTH, t), BF16),
        ],
        scratch_shapes=[pltpu.VMEM((TAIL, N_CONV), F32)],
        compiler_params=pltpu.CompilerParams(
            dimension_semantics=("arbitrary",), vmem_limit_bytes=VMEM_LIMIT),
        name="inproj",
    )(xf, norm_w, w_main, w_gate_t, w_v_t, conv_w)


def _gdn_kernel(act_ref, gate_t_ref, alog_s_ref, dtb_s_ref, gnw_ref, o_ref, mask_ref, bd_ref,
                row_ref, col_ref):
    seq_len = act_ref.shape[0]
    n_chunks = SC // CHUNK
    heads = range(GDN_HEADS)
    i_cat = lax.broadcasted_iota(jnp.int32, (CHUNK, SC), 0)
    j_cat = lax.broadcasted_iota(jnp.int32, (CHUNK, SC), 1) & (CHUNK - 1)
    mask_ref[0] = jnp.where(i_cat >= j_cat, 0.0, MASK_VALUE)
    mask_ref[1] = jnp.where(i_cat > j_cat, -1.0, 0.0)
    mask_ref[2] = jnp.where(i_cat == j_cat, 1.0, 0.0)
    r = lax.broadcasted_iota(jnp.int32, (SC, SC), 0)
    c = lax.broadcasted_iota(jnp.int32, (SC, SC), 1)
    bd_ref[...] = jnp.where((r & -CHUNK) == (c & -CHUNK), 1.0, 0.0).astype(BF16)
    lane_chunk = lax.broadcasted_iota(jnp.int32, (1, SC), 1) & -CHUNK

    scale = HEAD_DIM ** -0.5

    lane_in_chunk = lax.broadcasted_iota(jnp.int32, (N_GATE, seq_len), 1) & (CHUNK - 1)
    gate_t = gate_t_ref[...]
    beta = _sigmoid(gate_t)
    log_beta = -_softplus(-gate_t)
    g_step = -jnp.exp(alog_s_ref[...]) * _softplus(gate_t + dtb_s_ref[...])
    gc = g_step
    g_after = jnp.zeros_like(g_step)
    tail = g_step
    step = 1
    while step < CHUNK:
        gc = gc + jnp.where(lane_in_chunk >= step, pltpu.roll(gc, step, axis=1), 0.0)
        ahead = jnp.where(lane_in_chunk + step < CHUNK, pltpu.roll(tail, seq_len - step, axis=1), 0.0)
        g_after = g_after + ahead
        tail = tail + ahead
        step *= 2
    e_gc = jnp.exp(gc)
    is_beta_row = lax.broadcasted_iota(jnp.int32, (N_GATE, seq_len), 0) < GDN_HEADS
    swap = pltpu.roll(e_gc, GDN_HEADS, axis=0)
    row_ref[0] = gc
    row_ref[1] = e_gc
    col_ref[...] = jnp.concatenate([
        jnp.where(is_beta_row, beta, gc),
        jnp.where(is_beta_row, log_beta, e_gc),
        jnp.where(is_beta_row, beta * swap, jnp.exp(g_after))], axis=0).T

    def cat(x):
        out = x[(n_chunks - 1) * CHUNK:, :]
        for ci in range(n_chunks - 2, -1, -1):
            out = jnp.where(lane_chunk == ci * CHUNK, x[ci * CHUNK:(ci + 1) * CHUNK, :], out)
        return out

    def block_diag(x_cat):
        xb = x_cat.astype(BF16)
        zero = jnp.zeros((CHUNK, LANES), BF16)
        row_blocks = []
        for ci in range(n_chunks):
            g = ci * CHUNK // LANES
            grp = slice(g * LANES, (g + 1) * LANES)
            part = xb[:, grp] * bd_ref[ci * CHUNK:(ci + 1) * CHUNK, grp]
            row_blocks.append(jnp.concatenate(
                [part if j == g else zero for j in range(SC // LANES)], axis=1))
        return jnp.concatenate(row_blocks, axis=0)

    def head_cols(col, h):
        return slice((col + h) * HEAD_DIM, (col + h + 1) * HEAD_DIM)

    def prepare(steps):
        causal, neg_strict, eye = mask_ref[0], mask_ref[1], mask_ref[2]
        units = [(sc, h) for sc in steps for h in heads]
        n_pow, a_cat, rhs, q_dec, k_dec, decay_last = [], [], [], [], [], []
        for sc, h in units:
            rows = slice(sc * SC, (sc + 1) * SC)
            col = lambda j: col_ref[rows, j:j + 1]
            q = act_ref[rows, head_cols(COL_GQ, h)].astype(F32)
            k = act_ref[rows, head_cols(COL_GK, h)].astype(F32)
            v = act_ref[rows, head_cols(COL_GV, h)].astype(F32)
            q = q * (lax.rsqrt(jnp.sum(q * q, axis=-1, keepdims=True) + 1e-6) * scale)
            k = k * lax.rsqrt(jnp.sum(k * k, axis=-1, keepdims=True) + 1e-6)
            hb, hg = h, GDN_HEADS + h
            kb = k.astype(BF16)
            gram = _dot_nt(jnp.concatenate([kb, q.astype(BF16)], axis=0), kb)
            e_cat = cat(col(hg)) - row_ref[0, hg:hg + 1, rows] + causal
            n_pow.append(cat(gram[:SC]) * jnp.exp(e_cat + cat(col(N_GATE + hb))) * neg_strict)
            a_cat.append(cat(gram[SC:]) * jnp.exp(e_cat))
            rhs.append(jnp.concatenate([col(hb) * v, col(2 * N_GATE + hb) * k], axis=1).astype(BF16))
            q_dec.append(q * col(N_GATE + hg))
            kd = (k * col(2 * N_GATE + hg)).astype(BF16)
            k_dec.append([kd[ci * CHUNK:(ci + 1) * CHUNK, :] for ci in range(n_chunks)])
            decay_last.append([row_ref[1, hg:hg + 1, sc * SC + (ci + 1) * CHUNK - 1:sc * SC + (ci + 1) * CHUNK]
                               for ci in range(n_chunks)])

        every = range(len(units))
        t_cat = [eye + n_pow[u] for u in every]
        n_pow = [_dot(n_pow[u].astype(BF16), block_diag(n_pow[u])) for u in every]
        span = 2
        while span < CHUNK // 2:
            both = [_dot(jnp.concatenate([t_cat[u], n_pow[u]], axis=0).astype(BF16), block_diag(n_pow[u]))
                    for u in every]
            t_cat = [t_cat[u] + both[u][:CHUNK] for u in every]
            n_pow = [both[u][CHUNK:] for u in every]
            span *= 2
        t_cat = [t_cat[u] + _dot(t_cat[u].astype(BF16), block_diag(n_pow[u])) for u in every]

        uw = [_dot(block_diag(t_cat[u]), rhs[u]) for u in every]
        kt = [[_dot_tn(k_dec[u][ci], uw[u][ci * CHUNK:(ci + 1) * CHUNK, :].astype(BF16))
               for ci in range(n_chunks)] for u in every]
        per_step = lambda xs, i: xs[i * GDN_HEADS:(i + 1) * GDN_HEADS]
        return [tuple(per_step(xs, i) for xs in (uw, kt, q_dec, a_cat, decay_last))
                for i in range(len(steps))]

    def finish(sc, prep, state):
        rows = slice(sc * SC, (sc + 1) * SC)
        uw, kt, q_dec, a_cat, decay_last = prep
        state = list(state)
        state_in = [[None] * n_chunks for _ in heads]
        for ci in range(n_chunks):
            for h in heads:
                sb = state[h].astype(BF16)
                state_in[h][ci] = sb
                state[h] = (state[h] * decay_last[h][ci]
                            - _dot(kt[h][ci][:, HEAD_DIM:].astype(BF16), sb) + kt[h][ci][:, :HEAD_DIM])
        for h in heads:
            inter = []
            for ci in range(n_chunks):
                cr = slice(ci * CHUNK, (ci + 1) * CHUNK)
                lhs = jnp.concatenate([uw[h][cr, HEAD_DIM:], q_dec[h][cr, :]], axis=0).astype(BF16)
                inter.append(_dot(lhs, state_in[h][ci]))
            v_new = jnp.concatenate([uw[h][ci * CHUNK:(ci + 1) * CHUNK, :HEAD_DIM] - inter[ci][:CHUNK]
                                     for ci in range(n_chunks)], axis=0).astype(BF16)
            o = (jnp.concatenate([inter[ci][CHUNK:] for ci in range(n_chunks)], axis=0)
                 + _dot(block_diag(a_cat[h]), v_new))
            o = o * lax.rsqrt(jnp.mean(o * o, axis=-1, keepdims=True) + NORM_EPS)
            z = act_ref[rows, head_cols(COL_GZ, h)].astype(F32)
            o_ref[rows, head_cols(0, h)] = (o * (_silu(z) * gnw_ref[...])).astype(BF16)
        return state

    n_groups = seq_len // (SC * GDN_GROUP)
    group = lambda g: list(range(g * GDN_GROUP, (g + 1) * GDN_GROUP))
    state = [jnp.zeros((HEAD_DIM, HEAD_DIM), F32) for _ in heads]
    preps = prepare(group(0))
    for g in range(n_groups):
        nxt = prepare(group(g + 1)) if g + 1 < n_groups else None
        for sc, prep in zip(group(g), preps):
            state = finish(sc, prep, state)
        preps = nxt


def _gdn(act, gate_t, alog_s, dtb_s, gdn_norm_w, batch, seq_len):
    t = act.shape[0]
    small = lambda shape: pl.BlockSpec(shape, lambda b: (0, 0))
    return pl.pallas_call(
        _gdn_kernel,
        grid=(batch,),
        in_specs=[
            pl.BlockSpec((seq_len, 4 * GDN_WIDTH), lambda b: (b, 0)),
            pl.BlockSpec((N_GATE, seq_len), lambda b: (0, b)),
            small((N_GATE, 1)), small((N_GATE, 1)),
            small((1, HEAD_DIM)),
        ],
        out_specs=pl.BlockSpec((seq_len, GDN_WIDTH), lambda b: (b, 0)),
        out_shape=jax.ShapeDtypeStruct((t, GDN_WIDTH), BF16),
        scratch_shapes=[
            pltpu.VMEM((3, CHUNK, SC), F32),
            pltpu.VMEM((SC, SC), BF16),
            pltpu.VMEM((2, N_GATE, seq_len), F32),
            pltpu.VMEM((seq_len, 3 * N_GATE), F32),
        ],
        compiler_params=pltpu.CompilerParams(
            dimension_semantics=("arbitrary",), vmem_limit_bytes=VMEM_LIMIT),
        name="gdn",
    )(act, gate_t, alog_s, dtb_s, gdn_norm_w)


def _split3(x):
    hi = x.astype(BF16).astype(F32)
    r1 = x - hi
    mid = r1.astype(BF16).astype(F32)
    return hi, mid, r1 - mid


def _attn_kernel(lam_init, q_ref, k_ref, vt_ref, z_ref, lq1_ref, lk1_ref, lq2_ref, lk2_ref,
                 nw_ref, o_ref, bias_ref, kext_ref, vtx_ref, acc_ref, m_ref):
    h = pl.program_id(0)
    b = pl.program_id(1)
    seq_len = k_ref.shape[0]
    lane = lax.broadcasted_iota(jnp.int32, (1, HEAD_DIM), 1)
    n_bias = 6
    log2e = math.log2(math.e)
    assert BQ == 2 * BK

    @pl.when(b == 0)
    def _():
        slope = lax.shift_left(jnp.int32(1), 2 * (DIFF_HEADS - 1 - h)).astype(F32) * (log2e / 256.0)
        pos = lax.broadcasted_iota(jnp.int32, (seq_len, 1), 0)
        in_block = (pos & (BK - 1)).astype(F32) * slope
        block_off = (pos & -BK).astype(F32) * slope
        terms = _split3(in_block) + _split3(block_off)
        tile = jnp.zeros((seq_len, HEAD_DIM), F32)
        for j, term in enumerate(terms):
            tile = jnp.where((lane == j) | (lane == DIFF_QK_DIM + j), term, tile)
        bias_ref[...] = tile
        ones_row = lax.broadcasted_iota(jnp.int32, (V_PAD, seq_len), 0) == 0
        vtx_ref[HEAD_DIM:, :] = jnp.where(ones_row, 1.0, 0.0).astype(BF16)

    k = k_ref[...].astype(F32)
    kext_ref[0] = jnp.where(lane < DIFF_QK_DIM, k, bias_ref[...]).astype(BF16)
    kext_ref[1] = jnp.where(lane >= DIFF_QK_DIM, k, bias_ref[...]).astype(BF16)
    vtx_ref[0:HEAD_DIM, :] = vt_ref[...]

    lam = (jnp.exp(jnp.sum(lq1_ref[...] * lk1_ref[...], axis=-1, keepdims=True))
           - jnp.exp(jnp.sum(lq2_ref[...] * lk2_ref[...], axis=-1, keepdims=True)) + lam_init)

    def scores(q_ext, kb, masked, lo, n_blocks=1):
        rows = slice(kb * BK, (kb + n_blocks) * BK)
        out = []
        for c in range(2):
            s = _dot_nt(kext_ref[c, rows, :], q_ext[c][lo:, :])
            if masked:
                krel = lax.broadcasted_iota(jnp.int32, (BK, BQ - lo), 0)
                qrel = lax.broadcasted_iota(jnp.int32, (BK, BQ - lo), 1)
                s = jnp.where(krel <= qrel, s, MASK_VALUE)
            out.append((s, jnp.max(s, axis=0, keepdims=True)))
        return out

    def accumulate(par, kb, first, second, lo):
        vt = vtx_ref[:, kb * BK:(kb + 2) * BK]
        for c in range(2):
            (s_a, max_a), (s_b, max_b) = first[c], second[c]
            m_old = m_ref[par, c]
            if lo:
                max_b = jnp.concatenate([jnp.full((1, lo), MASK_VALUE, F32), max_b], axis=1)
            m_new = jnp.maximum(m_old, jnp.maximum(max_a, max_b))
            alpha = jnp.exp2(m_old - m_new)
            p_a = jnp.exp2(s_a - m_new).astype(BF16)
            p_b = jnp.exp2(s_b - m_new[:, lo:]).astype(BF16)
            both = _dot(vt, jnp.concatenate([p_a[:, lo:], p_b], axis=0))
            if lo:
                both = jnp.concatenate([_dot(vt[:, :BK], p_a[:, :lo]), both], axis=1)
            acc_ref[par, c] = alpha * acc_ref[par, c] + both
            m_ref[par, c] = m_new

    def accumulate_full(par, kb, block):
        vt = vtx_ref[:, kb * BK:(kb + 2) * BK]
        for c in range(2):
            s, s_max = block[c]
            m_old = m_ref[par, c]
            m_new = jnp.maximum(m_old, s_max)
            alpha = jnp.exp2(m_old - m_new)
            acc_ref[par, c] = alpha * acc_ref[par, c] + _dot(vt, jnp.exp2(s - m_new).astype(BF16))
            m_ref[par, c] = m_new

    for qi in range(seq_len // BQ):
        par = qi % 2
        qrows = slice(qi * BQ, (qi + 1) * BQ)
        q = q_ref[qrows, :].astype(F32) * (DIFF_QK_DIM ** -0.5 * log2e)
        q_ext = (
            jnp.where(lane < DIFF_QK_DIM, q,
                      jnp.where(lane < DIFF_QK_DIM + n_bias, 1.0, 0.0)).astype(BF16),
            jnp.where(lane >= DIFF_QK_DIM, q, jnp.where(lane < n_bias, 1.0, 0.0)).astype(BF16),
        )
        m_ref[par] = jnp.full(m_ref.shape[1:], MASK_VALUE, F32)
        acc_ref[par] = jnp.zeros(acc_ref.shape[1:], F32)
        n_pairs = qi + 1

        def pair_scores(j):
            if j < n_pairs - 1:
                return scores(q_ext, 2 * j, False, 0, 2)
            return (scores(q_ext, 2 * j, True, 0), scores(q_ext, 2 * j + 1, True, BK))

        cur = pair_scores(0)
        for j in range(n_pairs):
            nxt = pair_scores(j + 1) if j + 1 < n_pairs else None
            if j < n_pairs - 1:
                accumulate_full(par, 2 * j, cur)
            else:
                accumulate(par, 2 * j, cur[0], cur[1], BK)
            cur = nxt

        num = (acc_ref[par, 0, 0:HEAD_DIM, :], acc_ref[par, 1, 0:HEAD_DIM, :])
        den = (acc_ref[par, 0, HEAD_DIM:HEAD_DIM + 1, :], acc_ref[par, 1, HEAD_DIM:HEAD_DIM + 1, :])
        o_t = num[0] / den[0] - lam * (num[1] / den[1])
        o = o_t.T
        o = o * lax.rsqrt(jnp.mean(o * o, axis=-1, keepdims=True) + NORM_EPS)
        o = o * nw_ref[...] * (1.0 - lam_init)
        o_ref[qrows, :] = (o * _silu(z_ref[qrows, :].astype(F32))).astype(BF16)


def _attn(act, v_t, lq1, lk1, lq2, lk2, diff_norm_w, lam_init, batch, seq_len):
    t = act.shape[0]
    small = lambda shape: pl.BlockSpec(shape, lambda h, b: (0, 0))
    return pl.pallas_call(
        functools.partial(_attn_kernel, lam_init),
        grid=(DIFF_HEADS, batch),
        in_specs=[
            pl.BlockSpec((seq_len, HEAD_DIM), lambda h, b: (b, COL_DQ + h)),
            pl.BlockSpec((seq_len, HEAD_DIM), lambda h, b: (b, COL_DK + h)),
            pl.BlockSpec((HEAD_DIM, seq_len), lambda h, b: (h, b)),
            pl.BlockSpec((seq_len, HEAD_DIM), lambda h, b: (b, COL_DZ + h)),
            small((1, DIFF_QK_DIM)), small((1, DIFF_QK_DIM)),
            small((1, DIFF_QK_DIM)), small((1, DIFF_QK_DIM)),
            small((1, HEAD_DIM)),
        ],
        out_specs=pl.BlockSpec((seq_len, HEAD_DIM), lambda h, b: (b, h)),
        out_shape=jax.ShapeDtypeStruct((t, DIFF_WIDTH), BF16),
        scratch_shapes=[
            pltpu.VMEM((seq_len, HEAD_DIM), F32),
            pltpu.VMEM((2, seq_len, HEAD_DIM), BF16),
            pltpu.VMEM((HEAD_DIM + V_PAD, seq_len), BF16),
            pltpu.VMEM((2, 2, HEAD_DIM + V_PAD, BQ), F32),
            pltpu.VMEM((2, 2, 1, BQ), F32),
        ],
        compiler_params=pltpu.CompilerParams(
            dimension_semantics=("arbitrary", "arbitrary"), vmem_limit_bytes=VMEM_LIMIT),
        name="diffattn",
    )(act, act, v_t, act, lq1, lk1, lq2, lk2, diff_norm_w)


def _outproj_kernel(oa_ref, ob_ref, x_ref, w_ref, fw_ref, out_ref):
    mix = _dot(oa_ref[...], w_ref[0:GDN_WIDTH, :]) + _dot(ob_ref[...], w_ref[GDN_WIDTH:, :])
    y = x_ref[...] + mix
    ms = jnp.mean(y * y, axis=-1, keepdims=True)
    out_ref[...] = y * lax.rsqrt(ms + NORM_EPS) * fw_ref[...]


def _outproj(o_a, o_b, xf, w_out, final_norm_w):
    t = xf.shape[0]
    return pl.pallas_call(
        _outproj_kernel,
        grid=(t // TM_OUT,),
        in_specs=[
            pl.BlockSpec((TM_OUT, GDN_WIDTH), lambda i: (i, 0)),
            pl.BlockSpec((TM_OUT, DIFF_WIDTH), lambda i: (i, 0)),
            pl.BlockSpec((TM_OUT, D_MODEL), lambda i: (i, 0)),
            pl.BlockSpec((D_MODEL, D_MODEL), lambda i: (0, 0)),
            pl.BlockSpec((1, D_MODEL), lambda i: (0, 0)),
        ],
        out_specs=pl.BlockSpec((TM_OUT, D_MODEL), lambda i: (i, 0)),
        out_shape=jax.ShapeDtypeStruct((t, D_MODEL), F32),
        compiler_params=pltpu.CompilerParams(
            dimension_semantics=("arbitrary",), vmem_limit_bytes=VMEM_LIMIT),
        name="outproj",
    )(o_a, o_b, xf, w_out, final_norm_w)


def kernel(x, norm_w, w_in, conv_w, a_log, dt_bias, gdn_norm_w, lambda_q1, lambda_k1,
           lambda_q2, lambda_k2, diff_norm_w, w_out, final_norm_w):
    batch, seq_len, d_model = x.shape
    depth = norm_w.shape[0]
    assert depth == 1 and d_model == D_MODEL
    assert seq_len % TM_IN == 0 and seq_len % SC == 0 and seq_len % BQ == 0 and BQ % BK == 0
    n_wide_a = 4 * GDN_WIDTH
    xf = x.reshape(batch * seq_len, d_model)

    w = w_in[0]
    c_dq = n_wide_a + N_GATE
    c_dv = c_dq + 2 * DIFF_WIDTH
    c_dz = c_dv + DIFF_WIDTH
    w_main = jnp.concatenate([w[:, :n_wide_a], w[:, c_dq:c_dv], w[:, c_dz:]], axis=1).astype(BF16)
    w_v_t = w[:, c_dv:c_dz].T.astype(BF16)
    w_gate_t = w[:, n_wide_a:n_wide_a + N_GATE].T.astype(BF16)

    act, gate_t, v_t = _inproj(xf, norm_w[0][None, :], w_main, w_gate_t, w_v_t, conv_w[0], seq_len)

    pad_s = lambda vec: jnp.pad(vec, (GDN_HEADS, 0))[:, None]
    o_a = _gdn(act, gate_t, pad_s(a_log[0]), pad_s(dt_bias[0]), gdn_norm_w[0][None, :],
               batch, seq_len)

    lam_init = 0.8 - 0.6 * math.exp(-0.3 * 0)
    o_b = _attn(act, v_t, lambda_q1[0][None, :], lambda_k1[0][None, :], lambda_q2[0][None, :],
                lambda_k2[0][None, :], diff_norm_w[0][None, :], lam_init, batch, seq_len)

    out = _outproj(o_a, o_b, xf, w_out[0].astype(BF16), final_norm_w[None, :])
    return out.reshape(batch, seq_len, d_model)
```

```python
import functools
import math

import jax
import jax.numpy as jnp
from jax import lax
from jax.experimental import pallas as pl
from jax.experimental.pallas import tpu as pltpu

F32 = jnp.float32
BF16 = jnp.bfloat16

D_MODEL = 1024
GDN_HEADS = 4
HEAD_DIM = 128
GDN_WIDTH = GDN_HEADS * HEAD_DIM
DIFF_HEADS = 4
DIFF_QK_DIM = 64
DIFF_WIDTH = DIFF_HEADS * HEAD_DIM
CONV_K = 4
NORM_EPS = 1e-6
N_GATE = 2 * GDN_HEADS
N_MAIN = 4 * GDN_WIDTH + 3 * DIFF_WIDTH
N_CONV = 3 * GDN_WIDTH
LANES = 128
MASK_VALUE = -1e30

COL_GQ, COL_GK, COL_GV, COL_GZ = 0, 4, 8, 12
COL_DQ, COL_DK, COL_DZ = 16, 20, 24

TM_IN = 1024
NC_IN = 256
TAIL = 16
TM_OUT = 1024
SC = 256
CHUNK = 64
GDN_GROUP = 2
BQ = 512
BK = 256
SPAN_BLOCKS = 4
V_PAD = 16
VMEM_LIMIT = 48 * 1024 * 1024


def _sigmoid(x):
    return 1.0 / (1.0 + jnp.exp(-x))


def _silu(x):
    half = 0.5 * x
    return half + half * jnp.tanh(half)


def _softplus(x):
    return jnp.maximum(x, 0.0) + jnp.log1p(jnp.exp(-jnp.abs(x)))


def _dot(a, b):
    return jnp.dot(a, b, preferred_element_type=F32)


def _dot_nt(a, b):
    return lax.dot_general(a, b, (((1,), (1,)), ((), ())), preferred_element_type=F32)


def _dot_tn(a, b):
    return lax.dot_general(a, b, (((0,), (0,)), ((), ())), preferred_element_type=F32)


def _inproj_kernel(tiles_per_seq, x_ref, nw_ref, w_ref, wgt_ref, wvt_ref, cw_ref,
                   main_ref, gate_t_ref, vt_ref, tail_ref):
    i = pl.program_id(0)
    x = x_ref[...]
    ms = jnp.mean(x * x, axis=-1, keepdims=True)
    h = (x * lax.rsqrt(ms + NORM_EPS) * nw_ref[...]).astype(BF16)

    gate_t_ref[...] = _dot_nt(wgt_ref[...], h)
    vt_ref[...] = _dot_nt(wvt_ref[...], h).astype(BF16)

    @pl.when(i % tiles_per_seq == 0)
    def _():
        tail_ref[...] = jnp.zeros_like(tail_ref)

    tm = x.shape[0]
    row = lax.broadcasted_iota(jnp.int32, (TAIL, LANES), 0)

    def conv_silu(acc, n0):
        cols = slice(n0, n0 + LANES)
        last = acc[tm - TAIL:, :]
        delta = tail_ref[:, cols] - last
        tail_ref[:, cols] = last
        y = acc * cw_ref[CONV_K - 1:CONV_K, cols]
        fix = jnp.zeros((TAIL, LANES), F32)
        for s in range(1, CONV_K):
            wk = cw_ref[CONV_K - 1 - s:CONV_K - s, cols]
            y = y + pltpu.roll(acc, s, axis=0) * wk
            fix = fix + jnp.where(row < s, pltpu.roll(delta, s, axis=0), 0.0) * wk
        main_ref[:, cols] = _silu(y).astype(BF16)
        main_ref[0:TAIL, cols] = _silu(y[:TAIL, :] + fix).astype(BF16)

    plain = [slice(n0, n0 + NC_IN) for n0 in range(N_CONV, N_MAIN, NC_IN)]
    pending = []
    for n0 in range(0, N_CONV, NC_IN):
        acc = _dot(h, w_ref[:, n0:n0 + NC_IN])
        if pending:
            conv_silu(*pending.pop(0))
        if plain:
            pc = plain.pop(0)
            main_ref[:, pc] = _dot(h, w_ref[:, pc]).astype(BF16)
        if pending:
            conv_silu(*pending.pop(0))
        pending = [(acc[:, j:j + LANES], n0 + j) for j in range(0, NC_IN, LANES)]
    for piece in pending:
        if plain:
            pc = plain.pop(0)
            main_ref[:, pc] = _dot(h, w_ref[:, pc]).astype(BF16)
        conv_silu(*piece)
    for pc in plain:
        main_ref[:, pc] = _dot(h, w_ref[:, pc]).astype(BF16)


def _inproj(xf, norm_w, w_main, w_gate_t, w_v_t, conv_w, seq_len):
    t = xf.shape[0]
    tiles_per_seq = seq_len // TM_IN
    return pl.pallas_call(
        functools.partial(_inproj_kernel, tiles_per_seq),
        grid=(t // TM_IN,),
        in_specs=[
            pl.BlockSpec((TM_IN, D_MODEL), lambda i: (i, 0)),
            pl.BlockSpec((1, D_MODEL), lambda i: (0, 0)),
            pl.BlockSpec((D_MODEL, N_MAIN), lambda i: (0, 0)),
            pl.BlockSpec((N_GATE, D_MODEL), lambda i: (0, 0)),
            pl.BlockSpec((DIFF_WIDTH, D_MODEL), lambda i: (0, 0)),
            pl.BlockSpec((CONV_K, N_CONV), lambda i: (0, 0)),
        ],
        out_specs=[
            pl.BlockSpec((TM_IN, N_MAIN), lambda i: (i, 0)),
            pl.BlockSpec((N_GATE, TM_IN), lambda i: (0, i)),
            pl.BlockSpec((DIFF_WIDTH, TM_IN), lambda i: (0, i)),
        ],
        out_shape=[
            jax.ShapeDtypeStruct((t, N_MAIN), BF16),
            jax.ShapeDtypeStruct((N_GATE, t), F32),
            jax.ShapeDtypeStruct((DIFF_WIDTH, t), BF16),
        ],
        scratch_shapes=[pltpu.VMEM((TAIL, N_CONV), F32)],
        compiler_params=pltpu.CompilerParams(
            dimension_semantics=("arbitrary",), vmem_limit_bytes=VMEM_LIMIT),
        name="inproj",
    )(xf, norm_w, w_main, w_gate_t, w_v_t, conv_w)


def _gdn_kernel(act_ref, gate_t_ref, alog_s_ref, dtb_s_ref, gnw_ref, o_ref, mask_ref, bd_ref,
                row_ref, col_ref):
    seq_len = act_ref.shape[0]
    n_chunks = SC // CHUNK
    heads = range(GDN_HEADS)
    i_cat = lax.broadcasted_iota(jnp.int32, (CHUNK, SC), 0)
    j_cat = lax.broadcasted_iota(jnp.int32, (CHUNK, SC), 1) & (CHUNK - 1)
    mask_ref[0] = jnp.where(i_cat >= j_cat, 0.0, MASK_VALUE)
    mask_ref[1] = jnp.where(i_cat > j_cat, -1.0, 0.0)
    mask_ref[2] = jnp.where(i_cat == j_cat, 1.0, 0.0)
    r = lax.broadcasted_iota(jnp.int32, (SC, SC), 0)
    c = lax.broadcasted_iota(jnp.int32, (SC, SC), 1)
    bd_ref[...] = jnp.where((r & -CHUNK) == (c & -CHUNK), 1.0, 0.0).astype(BF16)
    lane_chunk = lax.broadcasted_iota(jnp.int32, (1, SC), 1) & -CHUNK

    scale = HEAD_DIM ** -0.5

    lane_in_chunk = lax.broadcasted_iota(jnp.int32, (N_GATE, seq_len), 1) & (CHUNK - 1)
    gate_t = gate_t_ref[...]
    beta = _sigmoid(gate_t)
    log_beta = -_softplus(-gate_t)
    g_step = -jnp.exp(alog_s_ref[...]) * _softplus(gate_t + dtb_s_ref[...])
    gc = g_step
    g_after = jnp.zeros_like(g_step)
    tail = g_step
    step = 1
    while step < CHUNK:
        gc = gc + jnp.where(lane_in_chunk >= step, pltpu.roll(gc, step, axis=1), 0.0)
        ahead = jnp.where(lane_in_chunk + step < CHUNK, pltpu.roll(tail, seq_len - step, axis=1), 0.0)
        g_after = g_after + ahead
        tail = tail + ahead
        step *= 2
    e_gc = jnp.exp(gc)
    is_beta_row = lax.broadcasted_iota(jnp.int32, (N_GATE, seq_len), 0) < GDN_HEADS
    swap = pltpu.roll(e_gc, GDN_HEADS, axis=0)
    row_ref[0] = gc
    row_ref[1] = e_gc
    col_ref[...] = jnp.concatenate([
        jnp.where(is_beta_row, beta, gc),
        jnp.where(is_beta_row, log_beta, e_gc),
        jnp.where(is_beta_row, beta * swap, jnp.exp(g_after))], axis=0).T

    def cat(x):
        out = x[(n_chunks - 1) * CHUNK:, :]
        for ci in range(n_chunks - 2, -1, -1):
            out = jnp.where(lane_chunk == ci * CHUNK, x[ci * CHUNK:(ci + 1) * CHUNK, :], out)
        return out

    def block_diag(x_cat):
        xb = x_cat.astype(BF16)
        zero = jnp.zeros((CHUNK, LANES), BF16)
        row_blocks = []
        for ci in range(n_chunks):
            g = ci * CHUNK // LANES
            grp = slice(g * LANES, (g + 1) * LANES)
            part = xb[:, grp] * bd_ref[ci * CHUNK:(ci + 1) * CHUNK, grp]
            row_blocks.append(jnp.concatenate(
                [part if j == g else zero for j in range(SC // LANES)], axis=1))
        return jnp.concatenate(row_blocks, axis=0)

    def head_cols(col, h):
        return slice((col + h) * HEAD_DIM, (col + h + 1) * HEAD_DIM)

    def prepare(steps):
        causal, neg_strict, eye = mask_ref[0], mask_ref[1], mask_ref[2]
        units = [(sc, h) for sc in steps for h in heads]
        n_pow, a_cat, rhs, q_dec, k_dec, decay_last = [], [], [], [], [], []
        for sc, h in units:
            rows = slice(sc * SC, (sc + 1) * SC)
            col = lambda j: col_ref[rows, j:j + 1]
            q = act_ref[rows, head_cols(COL_GQ, h)].astype(F32)
            k = act_ref[rows, head_cols(COL_GK, h)].astype(F32)
            v = act_ref[rows, head_cols(COL_GV, h)].astype(F32)
            q = q * (lax.rsqrt(jnp.sum(q * q, axis=-1, keepdims=True) + 1e-6) * scale)
            k = k * lax.rsqrt(jnp.sum(k * k, axis=-1, keepdims=True) + 1e-6)
            hb, hg = h, GDN_HEADS + h
            kb = k.astype(BF16)
            gram = _dot_nt(jnp.concatenate([kb, q.astype(BF16)], axis=0), kb)
            e_cat = cat(col(hg)) - row_ref[0, hg:hg + 1, rows] + causal
            n_pow.append(cat(gram[:SC]) * jnp.exp(e_cat + cat(col(N_GATE + hb))) * neg_strict)
            a_cat.append(cat(gram[SC:]) * jnp.exp(e_cat))
            rhs.append(jnp.concatenate([col(hb) * v, col(2 * N_GATE + hb) * k], axis=1).astype(BF16))
            q_dec.append(q * col(N_GATE + hg))
            kd = (k * col(2 * N_GATE + hg)).astype(BF16)
            k_dec.append([kd[ci * CHUNK:(ci + 1) * CHUNK, :] for ci in range(n_chunks)])
            decay_last.append([row_ref[1, hg:hg + 1, sc * SC + (ci + 1) * CHUNK - 1:sc * SC + (ci + 1) * CHUNK]
                               for ci in range(n_chunks)])

        every = range(len(units))
        t_cat = [eye + n_pow[u] for u in every]
        n_pow = [_dot(n_pow[u].astype(BF16), block_diag(n_pow[u])) for u in every]
        span = 2
        while span < CHUNK // 2:
            both = [_dot(jnp.concatenate([t_cat[u], n_pow[u]], axis=0).astype(BF16), block_diag(n_pow[u]))
                    for u in every]
            t_cat = [t_cat[u] + both[u][:CHUNK] for u in every]
            n_pow = [both[u][CHUNK:] for u in every]
            span *= 2
        t_cat = [t_cat[u] + _dot(t_cat[u].astype(BF16), block_diag(n_pow[u])) for u in every]

        uw = [_dot(block_diag(t_cat[u]), rhs[u]) for u in every]
        kt = [[_dot_tn(k_dec[u][ci], uw[u][ci * CHUNK:(ci + 1) * CHUNK, :].astype(BF16))
               for ci in range(n_chunks)] for u in every]
        per_step = lambda xs, i: xs[i * GDN_HEADS:(i + 1) * GDN_HEADS]
        return [tuple(per_step(xs, i) for xs in (uw, kt, q_dec, a_cat, decay_last))
                for i in range(len(steps))]

    def finish(sc, prep, state):
        rows = slice(sc * SC, (sc + 1) * SC)
        uw, kt, q_dec, a_cat, decay_last = prep
        state = list(state)
        state_in = [[None] * n_chunks for _ in heads]
        for ci in range(n_chunks):
            for h in heads:
                sb = state[h].astype(BF16)
                state_in[h][ci] = sb
                state[h] = (state[h] * decay_last[h][ci]
                            - _dot(kt[h][ci][:, HEAD_DIM:].astype(BF16), sb) + kt[h][ci][:, :HEAD_DIM])
        for h in heads:
            inter = []
            for ci in range(n_chunks):
                cr = slice(ci * CHUNK, (ci + 1) * CHUNK)
                lhs = jnp.concatenate([uw[h][cr, HEAD_DIM:], q_dec[h][cr, :]], axis=0).astype(BF16)
                inter.append(_dot(lhs, state_in[h][ci]))
            v_new = jnp.concatenate([uw[h][ci * CHUNK:(ci + 1) * CHUNK, :HEAD_DIM] - inter[ci][:CHUNK]
                                     for ci in range(n_chunks)], axis=0).astype(BF16)
            o = (jnp.concatenate([inter[ci][CHUNK:] for ci in range(n_chunks)], axis=0)
                 + _dot(block_diag(a_cat[h]), v_new))
            o = o * lax.rsqrt(jnp.mean(o * o, axis=-1, keepdims=True) + NORM_EPS)
            z = act_ref[rows, head_cols(COL_GZ, h)].astype(F32)
            o_ref[rows, head_cols(0, h)] = (o * (_silu(z) * gnw_ref[...])).astype(BF16)
        return state

    n_groups = seq_len // (SC * GDN_GROUP)
    group = lambda g: list(range(g * GDN_GROUP, (g + 1) * GDN_GROUP))
    state = [jnp.zeros((HEAD_DIM, HEAD_DIM), F32) for _ in heads]
    preps = prepare(group(0))
    for g in range(n_groups):
        nxt = prepare(group(g + 1)) if g + 1 < n_groups else None
        for sc, prep in zip(group(g), preps):
            state = finish(sc, prep, state)
        preps = nxt


def _gdn(act, gate_t, alog_s, dtb_s, gdn_norm_w, batch, seq_len):
    t = act.shape[0]
    small = lambda shape: pl.BlockSpec(shape, lambda b: (0, 0))
    return pl.pallas_call(
        _gdn_kernel,
        grid=(batch,),
        in_specs=[
            pl.BlockSpec((seq_len, 4 * GDN_WIDTH), lambda b: (b, 0)),
            pl.BlockSpec((N_GATE, seq_len), lambda b: (0, b)),
            small((N_GATE, 1)), small((N_GATE, 1)),
            small((1, HEAD_DIM)),
        ],
        out_specs=pl.BlockSpec((seq_len, GDN_WIDTH), lambda b: (b, 0)),
        out_shape=jax.ShapeDtypeStruct((t, GDN_WIDTH), BF16),
        scratch_shapes=[
            pltpu.VMEM((3, CHUNK, SC), F32),
            pltpu.VMEM((SC, SC), BF16),
            pltpu.VMEM((2, N_GATE, seq_len), F32),
            pltpu.VMEM((seq_len, 3 * N_GATE), F32),
        ],
        compiler_params=pltpu.CompilerParams(
            dimension_semantics=("arbitrary",), vmem_limit_bytes=VMEM_LIMIT),
        name="gdn",
    )(act, gate_t, alog_s, dtb_s, gdn_norm_w)


def _split3(x):
    hi = x.astype(BF16).astype(F32)
    r1 = x - hi
    mid = r1.astype(BF16).astype(F32)
    return hi, mid, r1 - mid


def _attn_kernel(lam_init, q_ref, k_ref, vt_ref, z_ref, lq1_ref, lk1_ref, lq2_ref, lk2_ref,
                 nw_ref, o_ref, bias_ref, kext_ref, vtx_ref, acc_ref, m_ref):
    h = pl.program_id(0)
    b = pl.program_id(1)
    seq_len = k_ref.shape[0]
    lane = lax.broadcasted_iota(jnp.int32, (1, HEAD_DIM), 1)
    n_bias = 6
    log2e = math.log2(math.e)
    assert BQ == 2 * BK

    @pl.when(b == 0)
    def _():
        slope = lax.shift_left(jnp.int32(1), 2 * (DIFF_HEADS - 1 - h)).astype(F32) * (log2e / 256.0)
        pos = lax.broadcasted_iota(jnp.int32, (seq_len, 1), 0)
        in_block = (pos & (BK - 1)).astype(F32) * slope
        block_off = (pos & -BK).astype(F32) * slope
        terms = _split3(in_block) + _split3(block_off)
        tile = jnp.zeros((seq_len, HEAD_DIM), F32)
        for j, term in enumerate(terms):
            tile = jnp.where((lane == j) | (lane == DIFF_QK_DIM + j), term, tile)
        bias_ref[...] = tile
        ones_row = lax.broadcasted_iota(jnp.int32, (V_PAD, seq_len), 0) == 0
        vtx_ref[HEAD_DIM:, :] = jnp.where(ones_row, 1.0, 0.0).astype(BF16)

    k = k_ref[...].astype(F32)
    kext_ref[0] = jnp.where(lane < DIFF_QK_DIM, k, bias_ref[...]).astype(BF16)
    kext_ref[1] = jnp.where(lane >= DIFF_QK_DIM, k, bias_ref[...]).astype(BF16)
    vtx_ref[0:HEAD_DIM, :] = vt_ref[...]

    lam = (jnp.exp(jnp.sum(lq1_ref[...] * lk1_ref[...], axis=-1, keepdims=True))
           - jnp.exp(jnp.sum(lq2_ref[...] * lk2_ref[...], axis=-1, keepdims=True)) + lam_init)

    def scores(q_ext, kb, masked, lo, n_blocks=1):
        rows = slice(kb * BK, (kb + n_blocks) * BK)
        out = []
        for c in range(2):
            s = _dot_nt(kext_ref[c, rows, :], q_ext[c][lo:, :])
            if masked:
                krel = lax.broadcasted_iota(jnp.int32, (BK, BQ - lo), 0)
                qrel = lax.broadcasted_iota(jnp.int32, (BK, BQ - lo), 1)
                s = jnp.where(krel <= qrel, s, MASK_VALUE)
            out.append((s, jnp.max(s, axis=0, keepdims=True)))
        return out

    def accumulate(par, kb, first, second, lo):
        vt = vtx_ref[:, kb * BK:(kb + 2) * BK]
        for c in range(2):
            (s_a, max_a), (s_b, max_b) = first[c], second[c]
            m_old = m_ref[par, c]
            if lo:
                max_b = jnp.concatenate([jnp.full((1, lo), MASK_VALUE, F32), max_b], axis=1)
            m_new = jnp.maximum(m_old, jnp.maximum(max_a, max_b))
            alpha = jnp.exp2(m_old - m_new)
            p_a = jnp.exp2(s_a - m_new).astype(BF16)
            p_b = jnp.exp2(s_b - m_new[:, lo:]).astype(BF16)
            both = _dot(vt, jnp.concatenate([p_a[:, lo:], p_b], axis=0))
            if lo:
                both = jnp.concatenate([_dot(vt[:, :BK], p_a[:, :lo]), both], axis=1)
            acc_ref[par, c] = alpha * acc_ref[par, c] + both
            m_ref[par, c] = m_new

    def accumulate_full(par, kb, n_blocks, block):
        vt = vtx_ref[:, kb * BK:(kb + n_blocks) * BK]
        for c in range(2):
            s, s_max = block[c]
            m_old = m_ref[par, c]
            m_new = jnp.maximum(m_old, s_max)
            alpha = jnp.exp2(m_old - m_new)
            acc_ref[par, c] = alpha * acc_ref[par, c] + _dot(vt, jnp.exp2(s - m_new).astype(BF16))
            m_ref[par, c] = m_new

    for qi in range(seq_len // BQ):
        par = qi % 2
        qrows = slice(qi * BQ, (qi + 1) * BQ)
        q = q_ref[qrows, :].astype(F32) * (DIFF_QK_DIM ** -0.5 * log2e)
        q_ext = (
            jnp.where(lane < DIFF_QK_DIM, q,
                      jnp.where(lane < DIFF_QK_DIM + n_bias, 1.0, 0.0)).astype(BF16),
            jnp.where(lane >= DIFF_QK_DIM, q, jnp.where(lane < n_bias, 1.0, 0.0)).astype(BF16),
        )
        m_ref[par] = jnp.full(m_ref.shape[1:], MASK_VALUE, F32)
        acc_ref[par] = jnp.zeros(acc_ref.shape[1:], F32)
        spans = [(kb, min(SPAN_BLOCKS, 2 * qi - kb)) for kb in range(0, 2 * qi, SPAN_BLOCKS)]
        spans.append((2 * qi, 0))

        def span_scores(kb, n):
            if n:
                return scores(q_ext, kb, False, 0, n)
            return (scores(q_ext, kb, True, 0), scores(q_ext, kb + 1, True, BK))

        cur = span_scores(*spans[0])
        for idx, (kb, n) in enumerate(spans):
            nxt = span_scores(*spans[idx + 1]) if idx + 1 < len(spans) else None
            if n:
                accumulate_full(par, kb, n, cur)
            else:
                accumulate(par, kb, cur[0], cur[1], BK)
            cur = nxt

        num = (acc_ref[par, 0, 0:HEAD_DIM, :], acc_ref[par, 1, 0:HEAD_DIM, :])
        den = (acc_ref[par, 0, HEAD_DIM:HEAD_DIM + 1, :], acc_ref[par, 1, HEAD_DIM:HEAD_DIM + 1, :])
        o_t = num[0] / den[0] - lam * (num[1] / den[1])
        o = o_t.T
        o = o * lax.rsqrt(jnp.mean(o * o, axis=-1, keepdims=True) + NORM_EPS)
        o = o * nw_ref[...] * (1.0 - lam_init)
        o_ref[qrows, :] = (o * _silu(z_ref[qrows, :].astype(F32))).astype(BF16)


def _attn(act, v_t, lq1, lk1, lq2, lk2, diff_norm_w, lam_init, batch, seq_len):
    t = act.shape[0]
    small = lambda shape: pl.BlockSpec(shape, lambda h, b: (0, 0))
    return pl.pallas_call(
        functools.partial(_attn_kernel, lam_init),
        grid=(DIFF_HEADS, batch),
        in_specs=[
            pl.BlockSpec((seq_len, HEAD_DIM), lambda h, b: (b, COL_DQ + h)),
            pl.BlockSpec((seq_len, HEAD_DIM), lambda h, b: (b, COL_DK + h)),
            pl.BlockSpec((HEAD_DIM, seq_len), lambda h, b: (h, b)),
            pl.BlockSpec((seq_len, HEAD_DIM), lambda h, b: (b, COL_DZ + h)),
            small((1, DIFF_QK_DIM)), small((1, DIFF_QK_DIM)),
            small((1, DIFF_QK_DIM)), small((1, DIFF_QK_DIM)),
            small((1, HEAD_DIM)),
        ],
        out_specs=pl.BlockSpec((seq_len, HEAD_DIM), lambda h, b: (b, h)),
        out_shape=jax.ShapeDtypeStruct((t, DIFF_WIDTH), BF16),
        scratch_shapes=[
            pltpu.VMEM((seq_len, HEAD_DIM), F32),
            pltpu.VMEM((2, seq_len, HEAD_DIM), BF16),
            pltpu.VMEM((HEAD_DIM + V_PAD, seq_len), BF16),
            pltpu.VMEM((2, 2, HEAD_DIM + V_PAD, BQ), F32),
            pltpu.VMEM((2, 2, 1, BQ), F32),
        ],
        compiler_params=pltpu.CompilerParams(
            dimension_semantics=("arbitrary", "arbitrary"), vmem_limit_bytes=VMEM_LIMIT),
        name="diffattn",
    )(act, act, v_t, act, lq1, lk1, lq2, lk2, diff_norm_w)


def _outproj_kernel(oa_ref, ob_ref, x_ref, w_ref, fw_ref, out_ref):
    mix = _dot(oa_ref[...], w_ref[0:GDN_WIDTH, :]) + _dot(ob_ref[...], w_ref[GDN_WIDTH:, :])
    y = x_ref[...] + mix
    ms = jnp.mean(y * y, axis=-1, keepdims=True)
    out_ref[...] = y * lax.rsqrt(ms + NORM_EPS) * fw_ref[...]


def _outproj(o_a, o_b, xf, w_out, final_norm_w):
    t = xf.shape[0]
    return pl.pallas_call(
        _outproj_kernel,
        grid=(t // TM_OUT,),
        in_specs=[
            pl.BlockSpec((TM_OUT, GDN_WIDTH), lambda i: (i, 0)),
            pl.BlockSpec((TM_OUT, DIFF_WIDTH), lambda i: (i, 0)),
            pl.BlockSpec((TM_OUT, D_MODEL), lambda i: (i, 0)),
            pl.BlockSpec((D_MODEL, D_MODEL), lambda i: (0, 0)),
            pl.BlockSpec((1, D_MODEL), lambda i: (0, 0)),
        ],
        out_specs=pl.BlockSpec((TM_OUT, D_MODEL), lambda i: (i, 0)),
        out_shape=jax.ShapeDtypeStruct((t, D_MODEL), F32),
        compiler_params=pltpu.CompilerParams(
            dimension_semantics=("arbitrary",), vmem_limit_bytes=VMEM_LIMIT),
        name="outproj",
    )(o_a, o_b, xf, w_out, final_norm_w)


def kernel(x, norm_w, w_in, conv_w, a_log, dt_bias, gdn_norm_w, lambda_q1, lambda_k1,
           lambda_q2, lambda_k2, diff_norm_w, w_out, final_norm_w):
    batch, seq_len, d_model = x.shape
    depth = norm_w.shape[0]
    assert depth == 1 and d_model == D_MODEL
    assert seq_len % TM_IN == 0 and seq_len % SC == 0 and seq_len % BQ == 0 and BQ % BK == 0
    n_wide_a = 4 * GDN_WIDTH
    xf = x.reshape(batch * seq_len, d_model)

    w = w_in[0]
    c_dq = n_wide_a + N_GATE
    c_dv = c_dq + 2 * DIFF_WIDTH
    c_dz = c_dv + DIFF_WIDTH
    w_main = jnp.concatenate([w[:, :n_wide_a], w[:, c_dq:c_dv], w[:, c_dz:]], axis=1).astype(BF16)
    w_v_t = w[:, c_dv:c_dz].T.astype(BF16)
    w_gate_t = w[:, n_wide_a:n_wide_a + N_GATE].T.astype(BF16)

    act, gate_t, v_t = _inproj(xf, norm_w[0][None, :], w_main, w_gate_t, w_v_t, conv_w[0], seq_len)

    pad_s = lambda vec: jnp.pad(vec, (GDN_HEADS, 0))[:, None]
    o_a = _gdn(act, gate_t, pad_s(a_log[0]), pad_s(dt_bias[0]), gdn_norm_w[0][None, :],
               batch, seq_len)

    lam_init = 0.8 - 0.6 * math.exp(-0.3 * 0)
    o_b = _attn(act, v_t, lambda_q1[0][None, :], lambda_k1[0][None, :], lambda_q2[0][None, :],
                lambda_k2[0][None, :], diff_norm_w[0][None, :], lam_init, batch, seq_len)

    out = _outproj(o_a, o_b, xf, w_out[0].astype(BF16), final_norm_w[None, :])
    return out.reshape(batch, seq_len, d_model)
```

```python
import functools
import math

import jax
import jax.numpy as jnp
from jax import lax
from jax.experimental import pallas as pl
from jax.experimental.pallas import tpu as pltpu

F32 = jnp.float32
BF16 = jnp.bfloat16

D_MODEL = 1024
GDN_HEADS = 4
HEAD_DIM = 128
GDN_WIDTH = GDN_HEADS * HEAD_DIM
DIFF_HEADS = 4
DIFF_QK_DIM = 64
DIFF_WIDTH = DIFF_HEADS * HEAD_DIM
CONV_K = 4
NORM_EPS = 1e-6
N_GATE = 2 * GDN_HEADS
N_MAIN = 4 * GDN_WIDTH + 3 * DIFF_WIDTH
N_CONV = 3 * GDN_WIDTH
LANES = 128
MASK_VALUE = -1e30

COL_GQ, COL_GK, COL_GV, COL_GZ = 0, 4, 8, 12
COL_DQ, COL_DK, COL_DZ = 16, 20, 24

TM_IN = 1024
NC_IN = 256
TAIL = 16
TM_OUT = 1024
SC = 256
CHUNK = 64
GDN_GROUP = 2
BQ = 512
BK = 256
SPAN_BLOCKS = 6
V_PAD = 16
VMEM_LIMIT = 48 * 1024 * 1024


def _sigmoid(x):
    return 1.0 / (1.0 + jnp.exp(-x))


def _silu(x):
    half = 0.5 * x
    return half + half * jnp.tanh(half)


def _softplus(x):
    return jnp.maximum(x, 0.0) + jnp.log1p(jnp.exp(-jnp.abs(x)))


def _dot(a, b):
    return jnp.dot(a, b, preferred_element_type=F32)


def _dot_nt(a, b):
    return lax.dot_general(a, b, (((1,), (1,)), ((), ())), preferred_element_type=F32)


def _dot_tn(a, b):
    return lax.dot_general(a, b, (((0,), (0,)), ((), ())), preferred_element_type=F32)


def _inproj_kernel(tiles_per_seq, x_ref, nw_ref, w_ref, wgt_ref, wvt_ref, cw_ref,
                   main_ref, gate_t_ref, vt_ref, tail_ref):
    i = pl.program_id(0)
    x = x_ref[...]
    ms = jnp.mean(x * x, axis=-1, keepdims=True)
    h = (x * lax.rsqrt(ms + NORM_EPS) * nw_ref[...]).astype(BF16)

    gate_t_ref[...] = _dot_nt(wgt_ref[...], h)
    vt_ref[...] = _dot_nt(wvt_ref[...], h).astype(BF16)

    @pl.when(i % tiles_per_seq == 0)
    def _():
        tail_ref[...] = jnp.zeros_like(tail_ref)

    tm = x.shape[0]
    row = lax.broadcasted_iota(jnp.int32, (TAIL, LANES), 0)

    def conv_silu(acc, n0):
        cols = slice(n0, n0 + LANES)
        last = acc[tm - TAIL:, :]
        delta = tail_ref[:, cols] - last
        tail_ref[:, cols] = last
        y = acc * cw_ref[CONV_K - 1:CONV_K, cols]
        fix = jnp.zeros((TAIL, LANES), F32)
        for s in range(1, CONV_K):
            wk = cw_ref[CONV_K - 1 - s:CONV_K - s, cols]
            y = y + pltpu.roll(acc, s, axis=0) * wk
            fix = fix + jnp.where(row < s, pltpu.roll(delta, s, axis=0), 0.0) * wk
        main_ref[:, cols] = _silu(y).astype(BF16)
        main_ref[0:TAIL, cols] = _silu(y[:TAIL, :] + fix).astype(BF16)

    plain = [slice(n0, n0 + NC_IN) for n0 in range(N_CONV, N_MAIN, NC_IN)]
    pending = []
    for n0 in range(0, N_CONV, NC_IN):
        acc = _dot(h, w_ref[:, n0:n0 + NC_IN])
        if pending:
            conv_silu(*pending.pop(0))
        if plain:
            pc = plain.pop(0)
            main_ref[:, pc] = _dot(h, w_ref[:, pc]).astype(BF16)
        if pending:
            conv_silu(*pending.pop(0))
        pending = [(acc[:, j:j + LANES], n0 + j) for j in range(0, NC_IN, LANES)]
    for piece in pending:
        if plain:
            pc = plain.pop(0)
            main_ref[:, pc] = _dot(h, w_ref[:, pc]).astype(BF16)
        conv_silu(*piece)
    for pc in plain:
        main_ref[:, pc] = _dot(h, w_ref[:, pc]).astype(BF16)


def _inproj(xf, norm_w, w_main, w_gate_t, w_v_t, conv_w, seq_len):
    t = xf.shape[0]
    tiles_per_seq = seq_len // TM_IN
    return pl.pallas_call(
        functools.partial(_inproj_kernel, tiles_per_seq),
        grid=(t // TM_IN,),
        in_specs=[
            pl.BlockSpec((TM_IN, D_MODEL), lambda i: (i, 0)),
            pl.BlockSpec((1, D_MODEL), lambda i: (0, 0)),
            pl.BlockSpec((D_MODEL, N_MAIN), lambda i: (0, 0)),
            pl.BlockSpec((N_GATE, D_MODEL), lambda i: (0, 0)),
            pl.BlockSpec((DIFF_WIDTH, D_MODEL), lambda i: (0, 0)),
            pl.BlockSpec((CONV_K, N_CONV), lambda i: (0, 0)),
        ],
        out_specs=[
            pl.BlockSpec((TM_IN, N_MAIN), lambda i: (i, 0)),
            pl.BlockSpec((N_GATE, TM_IN), lambda i: (0, i)),
            pl.BlockSpec((DIFF_WIDTH, TM_IN), lambda i: (0, i)),
        ],
        out_shape=[
            jax.ShapeDtypeStruct((t, N_MAIN), BF16),
            jax.ShapeDtypeStruct((N_GATE, t), F32),
            jax.ShapeDtypeStruct((DIFF_WIDTH, t), BF16),
        ],
        scratch_shapes=[pltpu.VMEM((TAIL, N_CONV), F32)],
        compiler_params=pltpu.CompilerParams(
            dimension_semantics=("arbitrary",), vmem_limit_bytes=VMEM_LIMIT),
        name="inproj",
    )(xf, norm_w, w_main, w_gate_t, w_v_t, conv_w)


def _gdn_kernel(act_ref, gate_t_ref, alog_s_ref, dtb_s_ref, gnw_ref, o_ref, mask_ref, bd_ref,
                row_ref, col_ref):
    seq_len = act_ref.shape[0]
    n_chunks = SC // CHUNK
    heads = range(GDN_HEADS)
    i_cat = lax.broadcasted_iota(jnp.int32, (CHUNK, SC), 0)
    j_cat = lax.broadcasted_iota(jnp.int32, (CHUNK, SC), 1) & (CHUNK - 1)
    mask_ref[0] = jnp.where(i_cat >= j_cat, 0.0, MASK_VALUE)
    mask_ref[1] = jnp.where(i_cat > j_cat, -1.0, 0.0)
    mask_ref[2] = jnp.where(i_cat == j_cat, 1.0, 0.0)
    r = lax.broadcasted_iota(jnp.int32, (SC, SC), 0)
    c = lax.broadcasted_iota(jnp.int32, (SC, SC), 1)
    bd_ref[...] = jnp.where((r & -CHUNK) == (c & -CHUNK), 1.0, 0.0).astype(BF16)
    lane_chunk = lax.broadcasted_iota(jnp.int32, (1, SC), 1) & -CHUNK

    scale = HEAD_DIM ** -0.5

    lane_in_chunk = lax.broadcasted_iota(jnp.int32, (N_GATE, seq_len), 1) & (CHUNK - 1)
    gate_t = gate_t_ref[...]
    beta = _sigmoid(gate_t)
    log_beta = -_softplus(-gate_t)
    g_step = -jnp.exp(alog_s_ref[...]) * _softplus(gate_t + dtb_s_ref[...])
    gc = g_step
    g_after = jnp.zeros_like(g_step)
    tail = g_step
    step = 1
    while step < CHUNK:
        gc = gc + jnp.where(lane_in_chunk >= step, pltpu.roll(gc, step, axis=1), 0.0)
        ahead = jnp.where(lane_in_chunk + step < CHUNK, pltpu.roll(tail, seq_len - step, axis=1), 0.0)
        g_after = g_after + ahead
        tail = tail + ahead
        step *= 2
    e_gc = jnp.exp(gc)
    is_beta_row = lax.broadcasted_iota(jnp.int32, (N_GATE, seq_len), 0) < GDN_HEADS
    swap = pltpu.roll(e_gc, GDN_HEADS, axis=0)
    row_ref[0] = gc
    row_ref[1] = e_gc
    col_ref[...] = jnp.concatenate([
        jnp.where(is_beta_row, beta, gc),
        jnp.where(is_beta_row, log_beta, e_gc),
        jnp.where(is_beta_row, beta * swap, jnp.exp(g_after))], axis=0).T

    def cat(x):
        out = x[(n_chunks - 1) * CHUNK:, :]
        for ci in range(n_chunks - 2, -1, -1):
            out = jnp.where(lane_chunk == ci * CHUNK, x[ci * CHUNK:(ci + 1) * CHUNK, :], out)
        return out

    def block_diag(x_cat):
        xb = x_cat.astype(BF16)
        zero = jnp.zeros((CHUNK, LANES), BF16)
        row_blocks = []
        for ci in range(n_chunks):
            g = ci * CHUNK // LANES
            grp = slice(g * LANES, (g + 1) * LANES)
            part = xb[:, grp] * bd_ref[ci * CHUNK:(ci + 1) * CHUNK, grp]
            row_blocks.append(jnp.concatenate(
                [part if j == g else zero for j in range(SC // LANES)], axis=1))
        return jnp.concatenate(row_blocks, axis=0)

    def head_cols(col, h):
        return slice((col + h) * HEAD_DIM, (col + h + 1) * HEAD_DIM)

    def prepare(steps):
        causal, neg_strict, eye = mask_ref[0], mask_ref[1], mask_ref[2]
        units = [(sc, h) for sc in steps for h in heads]
        n_pow, a_cat, rhs, q_dec, k_dec, decay_last = [], [], [], [], [], []
        for sc, h in units:
            rows = slice(sc * SC, (sc + 1) * SC)
            col = lambda j: col_ref[rows, j:j + 1]
            q = act_ref[rows, head_cols(COL_GQ, h)].astype(F32)
            k = act_ref[rows, head_cols(COL_GK, h)].astype(F32)
            v = act_ref[rows, head_cols(COL_GV, h)].astype(F32)
            q = q * (lax.rsqrt(jnp.sum(q * q, axis=-1, keepdims=True) + 1e-6) * scale)
            k = k * lax.rsqrt(jnp.sum(k * k, axis=-1, keepdims=True) + 1e-6)
            hb, hg = h, GDN_HEADS + h
            kb = k.astype(BF16)
            gram = _dot_nt(jnp.concatenate([kb, q.astype(BF16)], axis=0), kb)
            e_cat = cat(col(hg)) - row_ref[0, hg:hg + 1, rows] + causal
            n_pow.append(cat(gram[:SC]) * jnp.exp(e_cat + cat(col(N_GATE + hb))) * neg_strict)
            a_cat.append(cat(gram[SC:]) * jnp.exp(e_cat))
            rhs.append(jnp.concatenate([col(hb) * v, col(2 * N_GATE + hb) * k], axis=1).astype(BF16))
            q_dec.append(q * col(N_GATE + hg))
            kd = (k * col(2 * N_GATE + hg)).astype(BF16)
            k_dec.append([kd[ci * CHUNK:(ci + 1) * CHUNK, :] for ci in range(n_chunks)])
            decay_last.append([row_ref[1, hg:hg + 1, sc * SC + (ci + 1) * CHUNK - 1:sc * SC + (ci + 1) * CHUNK]
                               for ci in range(n_chunks)])

        every = range(len(units))
        t_cat = [eye + n_pow[u] for u in every]
        n_pow = [_dot(n_pow[u].astype(BF16), block_diag(n_pow[u])) for u in every]
        span = 2
        while span < CHUNK // 2:
            both = [_dot(jnp.concatenate([t_cat[u], n_pow[u]], axis=0).astype(BF16), block_diag(n_pow[u]))
                    for u in every]
            t_cat = [t_cat[u] + both[u][:CHUNK] for u in every]
            n_pow = [both[u][CHUNK:] for u in every]
            span *= 2
        t_cat = [t_cat[u] + _dot(t_cat[u].astype(BF16), block_diag(n_pow[u])) for u in every]

        uw = [_dot(block_diag(t_cat[u]), rhs[u]) for u in every]
        kt = [[_dot_tn(k_dec[u][ci], uw[u][ci * CHUNK:(ci + 1) * CHUNK, :].astype(BF16))
               for ci in range(n_chunks)] for u in every]
        per_step = lambda xs, i: xs[i * GDN_HEADS:(i + 1) * GDN_HEADS]
        return [tuple(per_step(xs, i) for xs in (uw, kt, q_dec, a_cat, decay_last))
                for i in range(len(steps))]

    def finish(sc, prep, state):
        rows = slice(sc * SC, (sc + 1) * SC)
        uw, kt, q_dec, a_cat, decay_last = prep
        state = list(state)
        state_in = [[None] * n_chunks for _ in heads]
        for ci in range(n_chunks):
            for h in heads:
                sb = state[h].astype(BF16)
                state_in[h][ci] = sb
                state[h] = (state[h] * decay_last[h][ci]
                            - _dot(kt[h][ci][:, HEAD_DIM:].astype(BF16), sb) + kt[h][ci][:, :HEAD_DIM])
        for h in heads:
            inter = []
            for ci in range(n_chunks):
                cr = slice(ci * CHUNK, (ci + 1) * CHUNK)
                lhs = jnp.concatenate([uw[h][cr, HEAD_DIM:], q_dec[h][cr, :]], axis=0).astype(BF16)
                inter.append(_dot(lhs, state_in[h][ci]))
            v_new = jnp.concatenate([uw[h][ci * CHUNK:(ci + 1) * CHUNK, :HEAD_DIM] - inter[ci][:CHUNK]
                                     for ci in range(n_chunks)], axis=0).astype(BF16)
            o = (jnp.concatenate([inter[ci][CHUNK:] for ci in range(n_chunks)], axis=0)
                 + _dot(block_diag(a_cat[h]), v_new))
            o = o * lax.rsqrt(jnp.mean(o * o, axis=-1, keepdims=True) + NORM_EPS)
            z = act_ref[rows, head_cols(COL_GZ, h)].astype(F32)
            o_ref[rows, head_cols(0, h)] = (o * (_silu(z) * gnw_ref[...])).astype(BF16)
        return state

    n_groups = seq_len // (SC * GDN_GROUP)
    group = lambda g: list(range(g * GDN_GROUP, (g + 1) * GDN_GROUP))
    state = [jnp.zeros((HEAD_DIM, HEAD_DIM), F32) for _ in heads]
    preps = prepare(group(0))
    for g in range(n_groups):
        nxt = prepare(group(g + 1)) if g + 1 < n_groups else None
        for sc, prep in zip(group(g), preps):
            state = finish(sc, prep, state)
        preps = nxt


def _gdn(act, gate_t, alog_s, dtb_s, gdn_norm_w, batch, seq_len):
    t = act.shape[0]
    small = lambda shape: pl.BlockSpec(shape, lambda b: (0, 0))
    return pl.pallas_call(
        _gdn_kernel,
        grid=(batch,),
        in_specs=[
            pl.BlockSpec((seq_len, 4 * GDN_WIDTH), lambda b: (b, 0)),
            pl.BlockSpec((N_GATE, seq_len), lambda b: (0, b)),
            small((N_GATE, 1)), small((N_GATE, 1)),
            small((1, HEAD_DIM)),
        ],
        out_specs=pl.BlockSpec((seq_len, GDN_WIDTH), lambda b: (b, 0)),
        out_shape=jax.ShapeDtypeStruct((t, GDN_WIDTH), BF16),
        scratch_shapes=[
            pltpu.VMEM((3, CHUNK, SC), F32),
            pltpu.VMEM((SC, SC), BF16),
            pltpu.VMEM((2, N_GATE, seq_len), F32),
            pltpu.VMEM((seq_len, 3 * N_GATE), F32),
        ],
        compiler_params=pltpu.CompilerParams(
            dimension_semantics=("arbitrary",), vmem_limit_bytes=VMEM_LIMIT),
        name="gdn",
    )(act, gate_t, alog_s, dtb_s, gdn_norm_w)


def _split3(x):
    hi = x.astype(BF16).astype(F32)
    r1 = x - hi
    mid = r1.astype(BF16).astype(F32)
    return hi, mid, r1 - mid


def _attn_kernel(lam_init, q_ref, k_ref, vt_ref, z_ref, lq1_ref, lk1_ref, lq2_ref, lk2_ref,
                 nw_ref, o_ref, bias_ref, kext_ref, vtx_ref, acc_ref, m_ref):
    h = pl.program_id(0)
    b = pl.program_id(1)
    seq_len = k_ref.shape[0]
    lane = lax.broadcasted_iota(jnp.int32, (1, HEAD_DIM), 1)
    n_bias = 6
    log2e = math.log2(math.e)
    assert BQ == 2 * BK

    @pl.when(b == 0)
    def _():
        slope = lax.shift_left(jnp.int32(1), 2 * (DIFF_HEADS - 1 - h)).astype(F32) * (log2e / 256.0)
        pos = lax.broadcasted_iota(jnp.int32, (seq_len, 1), 0)
        in_block = (pos & (BK - 1)).astype(F32) * slope
        block_off = (pos & -BK).astype(F32) * slope
        terms = _split3(in_block) + _split3(block_off)
        tile = jnp.zeros((seq_len, HEAD_DIM), F32)
        for j, term in enumerate(terms):
            tile = jnp.where((lane == j) | (lane == DIFF_QK_DIM + j), term, tile)
        bias_ref[...] = tile
        ones_row = lax.broadcasted_iota(jnp.int32, (V_PAD, seq_len), 0) == 0
        vtx_ref[HEAD_DIM:, :] = jnp.where(ones_row, 1.0, 0.0).astype(BF16)

    k = k_ref[...].astype(F32)
    kext_ref[0] = jnp.where(lane < DIFF_QK_DIM, k, bias_ref[...]).astype(BF16)
    kext_ref[1] = jnp.where(lane >= DIFF_QK_DIM, k, bias_ref[...]).astype(BF16)
    vtx_ref[0:HEAD_DIM, :] = vt_ref[...]

    lam = (jnp.exp(jnp.sum(lq1_ref[...] * lk1_ref[...], axis=-1, keepdims=True))
           - jnp.exp(jnp.sum(lq2_ref[...] * lk2_ref[...], axis=-1, keepdims=True)) + lam_init)

    def scores(q_ext, kb, masked, lo, n_blocks=1):
        rows = slice(kb * BK, (kb + n_blocks) * BK)
        out = []
        for c in range(2):
            s = _dot_nt(kext_ref[c, rows, :], q_ext[c][lo:, :])
            if masked:
                krel = lax.broadcasted_iota(jnp.int32, (BK, BQ - lo), 0)
                qrel = lax.broadcasted_iota(jnp.int32, (BK, BQ - lo), 1)
                s = jnp.where(krel <= qrel, s, MASK_VALUE)
            out.append((s, jnp.max(s, axis=0, keepdims=True)))
        return out

    def accumulate(par, kb, first, second, lo):
        vt = vtx_ref[:, kb * BK:(kb + 2) * BK]
        for c in range(2):
            (s_a, max_a), (s_b, max_b) = first[c], second[c]
            m_old = m_ref[par, c]
            if lo:
                max_b = jnp.concatenate([jnp.full((1, lo), MASK_VALUE, F32), max_b], axis=1)
            m_new = jnp.maximum(m_old, jnp.maximum(max_a, max_b))
            alpha = jnp.exp2(m_old - m_new)
            p_a = jnp.exp2(s_a - m_new).astype(BF16)
            p_b = jnp.exp2(s_b - m_new[:, lo:]).astype(BF16)
            both = _dot(vt, jnp.concatenate([p_a[:, lo:], p_b], axis=0))
            if lo:
                both = jnp.concatenate([_dot(vt[:, :BK], p_a[:, :lo]), both], axis=1)
            acc_ref[par, c] = alpha * acc_ref[par, c] + both
            m_ref[par, c] = m_new

    def accumulate_full(par, kb, n_blocks, block):
        vt = vtx_ref[:, kb * BK:(kb + n_blocks) * BK]
        for c in range(2):
            s, s_max = block[c]
            m_old = m_ref[par, c]
            m_new = jnp.maximum(m_old, s_max)
            alpha = jnp.exp2(m_old - m_new)
            acc_ref[par, c] = alpha * acc_ref[par, c] + _dot(vt, jnp.exp2(s - m_new).astype(BF16))
            m_ref[par, c] = m_new

    for qi in range(seq_len // BQ):
        par = qi % 2
        qrows = slice(qi * BQ, (qi + 1) * BQ)
        q = q_ref[qrows, :].astype(F32) * (DIFF_QK_DIM ** -0.5 * log2e)
        q_ext = (
            jnp.where(lane < DIFF_QK_DIM, q,
                      jnp.where(lane < DIFF_QK_DIM + n_bias, 1.0, 0.0)).astype(BF16),
            jnp.where(lane >= DIFF_QK_DIM, q, jnp.where(lane < n_bias, 1.0, 0.0)).astype(BF16),
        )
        m_ref[par] = jnp.full(m_ref.shape[1:], MASK_VALUE, F32)
        acc_ref[par] = jnp.zeros(acc_ref.shape[1:], F32)
        spans = [(kb, min(SPAN_BLOCKS, 2 * qi - kb)) for kb in range(0, 2 * qi, SPAN_BLOCKS)]
        spans.append((2 * qi, 0))

        def span_scores(kb, n):
            if n:
                return scores(q_ext, kb, False, 0, n)
            return (scores(q_ext, kb, True, 0), scores(q_ext, kb + 1, True, BK))

        cur = span_scores(*spans[0])
        for idx, (kb, n) in enumerate(spans):
            nxt = span_scores(*spans[idx + 1]) if idx + 1 < len(spans) else None
            if n:
                accumulate_full(par, kb, n, cur)
            else:
                accumulate(par, kb, cur[0], cur[1], BK)
            cur = nxt

        num = (acc_ref[par, 0, 0:HEAD_DIM, :], acc_ref[par, 1, 0:HEAD_DIM, :])
        den = (acc_ref[par, 0, HEAD_DIM:HEAD_DIM + 1, :], acc_ref[par, 1, HEAD_DIM:HEAD_DIM + 1, :])
        o_t = num[0] / den[0] - lam * (num[1] / den[1])
        o = o_t.T
        o = o * lax.rsqrt(jnp.mean(o * o, axis=-1, keepdims=True) + NORM_EPS)
        o = o * nw_ref[...] * (1.0 - lam_init)
        o_ref[qrows, :] = (o * _silu(z_ref[qrows, :].astype(F32))).astype(BF16)


def _attn(act, v_t, lq1, lk1, lq2, lk2, diff_norm_w, lam_init, batch, seq_len):
    t = act.shape[0]
    small = lambda shape: pl.BlockSpec(shape, lambda h, b: (0, 0))
    return pl.pallas_call(
        functools.partial(_attn_kernel, lam_init),
        grid=(DIFF_HEADS, batch),
        in_specs=[
            pl.BlockSpec((seq_len, HEAD_DIM), lambda h, b: (b, COL_DQ + h)),
            pl.BlockSpec((seq_len, HEAD_DIM), lambda h, b: (b, COL_DK + h)),
            pl.BlockSpec((HEAD_DIM, seq_len), lambda h, b: (h, b)),
            pl.BlockSpec((seq_len, HEAD_DIM), lambda h, b: (b, COL_DZ + h)),
            small((1, DIFF_QK_DIM)), small((1, DIFF_QK_DIM)),
            small((1, DIFF_QK_DIM)), small((1, DIFF_QK_DIM)),
            small((1, HEAD_DIM)),
        ],
        out_specs=pl.BlockSpec((seq_len, HEAD_DIM), lambda h, b: (b, h)),
        out_shape=jax.ShapeDtypeStruct((t, DIFF_WIDTH), BF16),
        scratch_shapes=[
            pltpu.VMEM((seq_len, HEAD_DIM), F32),
            pltpu.VMEM((2, seq_len, HEAD_DIM), BF16),
            pltpu.VMEM((HEAD_DIM + V_PAD, seq_len), BF16),
            pltpu.VMEM((2, 2, HEAD_DIM + V_PAD, BQ), F32),
            pltpu.VMEM((2, 2, 1, BQ), F32),
        ],
        compiler_params=pltpu.CompilerParams(
            dimension_semantics=("arbitrary", "arbitrary"), vmem_limit_bytes=VMEM_LIMIT),
        name="diffattn",
    )(act, act, v_t, act, lq1, lk1, lq2, lk2, diff_norm_w)


def _outproj_kernel(oa_ref, ob_ref, x_ref, w_ref, fw_ref, out_ref):
    mix = _dot(oa_ref[...], w_ref[0:GDN_WIDTH, :]) + _dot(ob_ref[...], w_ref[GDN_WIDTH:, :])
    y = x_ref[...] + mix
    ms = jnp.mean(y * y, axis=-1, keepdims=True)
    out_ref[...] = y * lax.rsqrt(ms + NORM_EPS) * fw_ref[...]


def _outproj(o_a, o_b, xf, w_out, final_norm_w):
    t = xf.shape[0]
    return pl.pallas_call(
        _outproj_kernel,
        grid=(t // TM_OUT,),
        in_specs=[
            pl.BlockSpec((TM_OUT, GDN_WIDTH), lambda i: (i, 0)),
            pl.BlockSpec((TM_OUT, DIFF_WIDTH), lambda i: (i, 0)),
            pl.BlockSpec((TM_OUT, D_MODEL), lambda i: (i, 0)),
            pl.BlockSpec((D_MODEL, D_MODEL), lambda i: (0, 0)),
            pl.BlockSpec((1, D_MODEL), lambda i: (0, 0)),
        ],
        out_specs=pl.BlockSpec((TM_OUT, D_MODEL), lambda i: (i, 0)),
        out_shape=jax.ShapeDtypeStruct((t, D_MODEL), F32),
        compiler_params=pltpu.CompilerParams(
            dimension_semantics=("arbitrary",), vmem_limit_bytes=VMEM_LIMIT),
        name="outproj",
    )(o_a, o_b, xf, w_out, final_norm_w)


def kernel(x, norm_w, w_in, conv_w, a_log, dt_bias, gdn_norm_w, lambda_q1, lambda_k1,
           lambda_q2, lambda_k2, diff_norm_w, w_out, final_norm_w):
    batch, seq_len, d_model = x.shape
    depth = norm_w.shape[0]
    assert depth == 1 and d_model == D_MODEL
    assert seq_len % TM_IN == 0 and seq_len % SC == 0 and seq_len % BQ == 0 and BQ % BK == 0
    n_wide_a = 4 * GDN_WIDTH
    xf = x.reshape(batch * seq_len, d_model)

    w = w_in[0]
    c_dq = n_wide_a + N_GATE
    c_dv = c_dq + 2 * DIFF_WIDTH
    c_dz = c_dv + DIFF_WIDTH
    w_main = jnp.concatenate([w[:, :n_wide_a], w[:, c_dq:c_dv], w[:, c_dz:]], axis=1).astype(BF16)
    w_v_t = w[:, c_dv:c_dz].T.astype(BF16)
    w_gate_t = w[:, n_wide_a:n_wide_a + N_GATE].T.astype(BF16)

    act, gate_t, v_t = _inproj(xf, norm_w[0][None, :], w_main, w_gate_t, w_v_t, conv_w[0], seq_len)

    pad_s = lambda vec: jnp.pad(vec, (GDN_HEADS, 0))[:, None]
    o_a = _gdn(act, gate_t, pad_s(a_log[0]), pad_s(dt_bias[0]), gdn_norm_w[0][None, :],
               batch, seq_len)

    lam_init = 0.8 - 0.6 * math.exp(-0.3 * 0)
    o_b = _attn(act, v_t, lambda_q1[0][None, :], lambda_k1[0][None, :], lambda_q2[0][None, :],
                lambda_k2[0][None, :], diff_norm_w[0][None, :], lam_init, batch, seq_len)

    out = _outproj(o_a, o_b, xf, w_out[0].astype(BF16), final_norm_w[None, :])
    return out.reshape(batch, seq_len, d_model)
```

```python
import functools
import math

import jax
import jax.numpy as jnp
from jax import lax
from jax.experimental import pallas as pl
from jax.experimental.pallas import tpu as pltpu

F32 = jnp.float32
BF16 = jnp.bfloat16

D_MODEL = 1024
GDN_HEADS = 4
HEAD_DIM = 128
GDN_WIDTH = GDN_HEADS * HEAD_DIM
DIFF_HEADS = 4
DIFF_QK_DIM = 64
DIFF_WIDTH = DIFF_HEADS * HEAD_DIM
CONV_K = 4
NORM_EPS = 1e-6
N_GATE = 2 * GDN_HEADS
N_MAIN = 4 * GDN_WIDTH + 3 * DIFF_WIDTH
N_CONV = 3 * GDN_WIDTH
LANES = 128
MASK_VALUE = -1e30

COL_GQ, COL_GK, COL_GV, COL_GZ = 0, 4, 8, 12
COL_DQ, COL_DK, COL_DZ = 16, 20, 24

TM_IN = 1024
NC_IN = 256
TAIL = 16
TM_OUT = 1024
SC = 256
CHUNK = 64
GDN_GROUP = 2
BQ = 512
BK = 256
V_PAD = 16
VMEM_LIMIT = 48 * 1024 * 1024


def _sigmoid(x):
    return 1.0 / (1.0 + jnp.exp(-x))


def _silu(x):
    half = 0.5 * x
    return half + half * jnp.tanh(half)


def _softplus(x):
    return jnp.maximum(x, 0.0) + jnp.log1p(jnp.exp(-jnp.abs(x)))


def _dot(a, b):
    return jnp.dot(a, b, preferred_element_type=F32)


def _dot_nt(a, b):
    return lax.dot_general(a, b, (((1,), (1,)), ((), ())), preferred_element_type=F32)


def _dot_tn(a, b):
    return lax.dot_general(a, b, (((0,), (0,)), ((), ())), preferred_element_type=F32)


def _inproj_kernel(tiles_per_seq, x_ref, nw_ref, w_ref, wgt_ref, wvt_ref, cw_ref,
                   main_ref, gate_t_ref, vt_ref, tail_ref):
    i = pl.program_id(0)
    x = x_ref[...]
    ms = jnp.mean(x * x, axis=-1, keepdims=True)
    h = (x * lax.rsqrt(ms + NORM_EPS) * nw_ref[...]).astype(BF16)

    gate_t_ref[...] = _dot_nt(wgt_ref[...], h)
    vt_ref[...] = _dot_nt(wvt_ref[...], h).astype(BF16)

    @pl.when(i % tiles_per_seq == 0)
    def _():
        tail_ref[...] = jnp.zeros_like(tail_ref)

    tm = x.shape[0]
    row = lax.broadcasted_iota(jnp.int32, (TAIL, LANES), 0)

    def conv_silu(acc, n0):
        cols = slice(n0, n0 + LANES)
        last = acc[tm - TAIL:, :]
        delta = tail_ref[:, cols] - last
        tail_ref[:, cols] = last
        y = acc * cw_ref[CONV_K - 1:CONV_K, cols]
        fix = jnp.zeros((TAIL, LANES), F32)
        for s in range(1, CONV_K):
            wk = cw_ref[CONV_K - 1 - s:CONV_K - s, cols]
            y = y + pltpu.roll(acc, s, axis=0) * wk
            fix = fix + jnp.where(row < s, pltpu.roll(delta, s, axis=0), 0.0) * wk
        main_ref[:, cols] = _silu(y).astype(BF16)
        main_ref[0:TAIL, cols] = _silu(y[:TAIL, :] + fix).astype(BF16)

    plain = [slice(n0, n0 + NC_IN) for n0 in range(N_CONV, N_MAIN, NC_IN)]
    pending = []
    for n0 in range(0, N_CONV, NC_IN):
        acc = _dot(h, w_ref[:, n0:n0 + NC_IN])
        if pending:
            conv_silu(*pending.pop(0))
        if plain:
            pc = plain.pop(0)
            main_ref[:, pc] = _dot(h, w_ref[:, pc]).astype(BF16)
        if pending:
            conv_silu(*pending.pop(0))
        pending = [(acc[:, j:j + LANES], n0 + j) for j in range(0, NC_IN, LANES)]
    for piece in pending:
        if plain:
            pc = plain.pop(0)
            main_ref[:, pc] = _dot(h, w_ref[:, pc]).astype(BF16)
        conv_silu(*piece)
    for pc in plain:
        main_ref[:, pc] = _dot(h, w_ref[:, pc]).astype(BF16)


def _inproj(xf, norm_w, w_main, w_gate_t, w_v_t, conv_w, seq_len):
    t = xf.shape[0]
    tiles_per_seq = seq_len // TM_IN
    return pl.pallas_call(
        functools.partial(_inproj_kernel, tiles_per_seq),
        grid=(t // TM_IN,),
        in_specs=[
            pl.BlockSpec((TM_IN, D_MODEL), lambda i: (i, 0)),
            pl.BlockSpec((1, D_MODEL), lambda i: (0, 0)),
            pl.BlockSpec((D_MODEL, N_MAIN), lambda i: (0, 0)),
            pl.BlockSpec((N_GATE, D_MODEL), lambda i: (0, 0)),
            pl.BlockSpec((DIFF_WIDTH, D_MODEL), lambda i: (0, 0)),
            pl.BlockSpec((CONV_K, N_CONV), lambda i: (0, 0)),
        ],
        out_specs=[
            pl.BlockSpec((TM_IN, N_MAIN), lambda i: (i, 0)),
            pl.BlockSpec((N_GATE, TM_IN), lambda i: (0, i)),
            pl.BlockSpec((DIFF_WIDTH, TM_IN), lambda i: (0, i)),
        ],
        out_shape=[
            jax.ShapeDtypeStruct((t, N_MAIN), BF16),
            jax.ShapeDtypeStruct((N_GATE, t), F32),
            jax.ShapeDtypeStruct((DIFF_WIDTH, t), BF16),
        ],
        scratch_shapes=[pltpu.VMEM((TAIL, N_CONV), F32)],
        compiler_params=pltpu.CompilerParams(
            dimension_semantics=("arbitrary",), vmem_limit_bytes=VMEM_LIMIT),
        name="inproj",
    )(xf, norm_w, w_main, w_gate_t, w_v_t, conv_w)


def _gdn_kernel(act_ref, gate_t_ref, alog_s_ref, dtb_s_ref, gnw_ref, o_ref, mask_ref, bd_ref,
                row_ref, col_ref):
    seq_len = act_ref.shape[0]
    n_chunks = SC // CHUNK
    heads = range(GDN_HEADS)
    i_cat = lax.broadcasted_iota(jnp.int32, (CHUNK, SC), 0)
    j_cat = lax.broadcasted_iota(jnp.int32, (CHUNK, SC), 1) & (CHUNK - 1)
    mask_ref[0] = jnp.where(i_cat >= j_cat, 0.0, MASK_VALUE)
    mask_ref[1] = jnp.where(i_cat > j_cat, -1.0, 0.0)
    mask_ref[2] = jnp.where(i_cat == j_cat, 1.0, 0.0)
    r = lax.broadcasted_iota(jnp.int32, (SC, SC), 0)
    c = lax.broadcasted_iota(jnp.int32, (SC, SC), 1)
    bd_ref[...] = jnp.where((r & -CHUNK) == (c & -CHUNK), 1.0, 0.0).astype(BF16)
    lane_chunk = lax.broadcasted_iota(jnp.int32, (1, SC), 1) & -CHUNK

    scale = HEAD_DIM ** -0.5

    lane_in_chunk = lax.broadcasted_iota(jnp.int32, (N_GATE, seq_len), 1) & (CHUNK - 1)
    gate_t = gate_t_ref[...]
    beta = _sigmoid(gate_t)
    log_beta = -_softplus(-gate_t)
    g_step = -jnp.exp(alog_s_ref[...]) * _softplus(gate_t + dtb_s_ref[...])
    gc = g_step
    g_after = jnp.zeros_like(g_step)
    tail = g_step
    step = 1
    while step < CHUNK:
        gc = gc + jnp.where(lane_in_chunk >= step, pltpu.roll(gc, step, axis=1), 0.0)
        ahead = jnp.where(lane_in_chunk + step < CHUNK, pltpu.roll(tail, seq_len - step, axis=1), 0.0)
        g_after = g_after + ahead
        tail = tail + ahead
        step *= 2
    e_gc = jnp.exp(gc)
    is_beta_row = lax.broadcasted_iota(jnp.int32, (N_GATE, seq_len), 0) < GDN_HEADS
    swap = pltpu.roll(e_gc, GDN_HEADS, axis=0)
    row_ref[0] = gc
    row_ref[1] = e_gc
    col_ref[...] = jnp.concatenate([
        jnp.where(is_beta_row, beta, gc),
        jnp.where(is_beta_row, log_beta, e_gc),
        jnp.where(is_beta_row, beta * swap, jnp.exp(g_after))], axis=0).T

    def cat(x):
        out = x[(n_chunks - 1) * CHUNK:, :]
        for ci in range(n_chunks - 2, -1, -1):
            out = jnp.where(lane_chunk == ci * CHUNK, x[ci * CHUNK:(ci + 1) * CHUNK, :], out)
        return out

    def block_diag(x_cat):
        xb = x_cat.astype(BF16)
        zero = jnp.zeros((CHUNK, LANES), BF16)
        row_blocks = []
        for ci in range(n_chunks):
            g = ci * CHUNK // LANES
            grp = slice(g * LANES, (g + 1) * LANES)
            part = xb[:, grp] * bd_ref[ci * CHUNK:(ci + 1) * CHUNK, grp]
            row_blocks.append(jnp.concatenate(
                [part if j == g else zero for j in range(SC // LANES)], axis=1))
        return jnp.concatenate(row_blocks, axis=0)

    def head_cols(col, h):
        return slice((col + h) * HEAD_DIM, (col + h + 1) * HEAD_DIM)

    def prepare(steps):
        causal, neg_strict, eye = mask_ref[0], mask_ref[1], mask_ref[2]
        units = [(sc, h) for sc in steps for h in heads]
        n_pow, a_cat, rhs, q_dec, k_dec, decay_last = [], [], [], [], [], []
        for sc, h in units:
            rows = slice(sc * SC, (sc + 1) * SC)
            col = lambda j: col_ref[rows, j:j + 1]
            q = act_ref[rows, head_cols(COL_GQ, h)].astype(F32)
            k = act_ref[rows, head_cols(COL_GK, h)].astype(F32)
            v = act_ref[rows, head_cols(COL_GV, h)].astype(F32)
            q = q * (lax.rsqrt(jnp.sum(q * q, axis=-1, keepdims=True) + 1e-6) * scale)
            k = k * lax.rsqrt(jnp.sum(k * k, axis=-1, keepdims=True) + 1e-6)
            hb, hg = h, GDN_HEADS + h
            kb = k.astype(BF16)
            gram = _dot_nt(jnp.concatenate([kb, q.astype(BF16)], axis=0), kb)
            e_cat = cat(col(hg)) - row_ref[0, hg:hg + 1, rows] + causal
            n_pow.append(cat(gram[:SC]) * jnp.exp(e_cat + cat(col(N_GATE + hb))) * neg_strict)
            a_cat.append(cat(gram[SC:]) * jnp.exp(e_cat))
            rhs.append(jnp.concatenate([col(hb) * v, col(2 * N_GATE + hb) * k], axis=1).astype(BF16))
            q_dec.append(q * col(N_GATE + hg))
            kd = (k * col(2 * N_GATE + hg)).astype(BF16)
            k_dec.append([kd[ci * CHUNK:(ci + 1) * CHUNK, :] for ci in range(n_chunks)])
            decay_last.append([row_ref[1, hg:hg + 1, sc * SC + (ci + 1) * CHUNK - 1:sc * SC + (ci + 1) * CHUNK]
                               for ci in range(n_chunks)])

        every = range(len(units))
        t_cat = [eye + n_pow[u] for u in every]
        n_pow = [_dot(n_pow[u].astype(BF16), block_diag(n_pow[u])) for u in every]
        span = 2
        while span < CHUNK // 2:
            both = [_dot(jnp.concatenate([t_cat[u], n_pow[u]], axis=0).astype(BF16), block_diag(n_pow[u]))
                    for u in every]
            t_cat = [t_cat[u] + both[u][:CHUNK] for u in every]
            n_pow = [both[u][CHUNK:] for u in every]
            span *= 2
        t_cat = [t_cat[u] + _dot(t_cat[u].astype(BF16), block_diag(n_pow[u])) for u in every]

        uw = [_dot(block_diag(t_cat[u]), rhs[u]) for u in every]
        kt = [[_dot_tn(k_dec[u][ci], uw[u][ci * CHUNK:(ci + 1) * CHUNK, :].astype(BF16))
               for ci in range(n_chunks)] for u in every]
        per_step = lambda xs, i: xs[i * GDN_HEADS:(i + 1) * GDN_HEADS]
        return [tuple(per_step(xs, i) for xs in (uw, kt, q_dec, a_cat, decay_last))
                for i in range(len(steps))]

    def finish(sc, prep, state):
        rows = slice(sc * SC, (sc + 1) * SC)
        uw, kt, q_dec, a_cat, decay_last = prep
        state = list(state)
        state_in = [[None] * n_chunks for _ in heads]
        for ci in range(n_chunks):
            for h in heads:
                sb = state[h].astype(BF16)
                state_in[h][ci] = sb
                state[h] = (state[h] * decay_last[h][ci]
                            - _dot(kt[h][ci][:, HEAD_DIM:].astype(BF16), sb) + kt[h][ci][:, :HEAD_DIM])
        for h in heads:
            inter = []
            for ci in range(n_chunks):
                cr = slice(ci * CHUNK, (ci + 1) * CHUNK)
                lhs = jnp.concatenate([uw[h][cr, HEAD_DIM:], q_dec[h][cr, :]], axis=0).astype(BF16)
                inter.append(_dot(lhs, state_in[h][ci]))
            v_new = jnp.concatenate([uw[h][ci * CHUNK:(ci + 1) * CHUNK, :HEAD_DIM] - inter[ci][:CHUNK]
                                     for ci in range(n_chunks)], axis=0).astype(BF16)
            o = (jnp.concatenate([inter[ci][CHUNK:] for ci in range(n_chunks)], axis=0)
                 + _dot(block_diag(a_cat[h]), v_new))
            o = o * lax.rsqrt(jnp.mean(o * o, axis=-1, keepdims=True) + NORM_EPS)
            z = act_ref[rows, head_cols(COL_GZ, h)].astype(F32)
            o_ref[rows, head_cols(0, h)] = (o * (_silu(z) * gnw_ref[...])).astype(BF16)
        return state

    n_groups = seq_len // (SC * GDN_GROUP)
    group = lambda g: list(range(g * GDN_GROUP, (g + 1) * GDN_GROUP))
    state = [jnp.zeros((HEAD_DIM, HEAD_DIM), F32) for _ in heads]
    preps = prepare(group(0))
    for g in range(n_groups):
        nxt = prepare(group(g + 1)) if g + 1 < n_groups else None
        for sc, prep in zip(group(g), preps):
            state = finish(sc, prep, state)
        preps = nxt


def _gdn(act, gate_t, alog_s, dtb_s, gdn_norm_w, batch, seq_len):
    t = act.shape[0]
    small = lambda shape: pl.BlockSpec(shape, lambda b: (0, 0))
    return pl.pallas_call(
        _gdn_kernel,
        grid=(batch,),
        in_specs=[
            pl.BlockSpec((seq_len, 4 * GDN_WIDTH), lambda b: (b, 0)),
            pl.BlockSpec((N_GATE, seq_len), lambda b: (0, b)),
            small((N_GATE, 1)), small((N_GATE, 1)),
            small((1, HEAD_DIM)),
        ],
        out_specs=pl.BlockSpec((seq_len, GDN_WIDTH), lambda b: (b, 0)),
        out_shape=jax.ShapeDtypeStruct((t, GDN_WIDTH), BF16),
        scratch_shapes=[
            pltpu.VMEM((3, CHUNK, SC), F32),
            pltpu.VMEM((SC, SC), BF16),
            pltpu.VMEM((2, N_GATE, seq_len), F32),
            pltpu.VMEM((seq_len, 3 * N_GATE), F32),
        ],
        compiler_params=pltpu.CompilerParams(
            dimension_semantics=("arbitrary",), vmem_limit_bytes=VMEM_LIMIT),
        name="gdn",
    )(act, gate_t, alog_s, dtb_s, gdn_norm_w)


def _split3(x):
    hi = x.astype(BF16).astype(F32)
    r1 = x - hi
    mid = r1.astype(BF16).astype(F32)
    return hi, mid, r1 - mid


def _attn_kernel(lam_init, q_ref, k_ref, vt_ref, z_ref, lq1_ref, lk1_ref, lq2_ref, lk2_ref,
                 nw_ref, o_ref, bias_ref, kext_ref, vtx_ref):
    h = pl.program_id(0)
    b = pl.program_id(1)
    seq_len = k_ref.shape[0]
    lane = lax.broadcasted_iota(jnp.int32, (1, HEAD_DIM), 1)
    n_bias = 6
    log2e = math.log2(math.e)

    @pl.when(b == 0)
    def _():
        slope = lax.shift_left(jnp.int32(1), 2 * (DIFF_HEADS - 1 - h)).astype(F32) * (log2e / 256.0)
        pos = lax.broadcasted_iota(jnp.int32, (seq_len, 1), 0)
        in_block = (pos & (BK - 1)).astype(F32) * slope
        block_off = (pos & -BK).astype(F32) * slope
        terms = _split3(in_block) + _split3(block_off)
        tile = jnp.zeros((seq_len, HEAD_DIM), F32)
        for j, term in enumerate(terms):
            tile = jnp.where((lane == j) | (lane == DIFF_QK_DIM + j), term, tile)
        bias_ref[...] = tile
        ones_row = lax.broadcasted_iota(jnp.int32, (V_PAD, seq_len), 0) == 0
        vtx_ref[HEAD_DIM:, :] = jnp.where(ones_row, 1.0, 0.0).astype(BF16)

    k = k_ref[...].astype(F32)
    kext_ref[0] = jnp.where(lane < DIFF_QK_DIM, k, bias_ref[...]).astype(BF16)
    kext_ref[1] = jnp.where(lane >= DIFF_QK_DIM, k, bias_ref[...]).astype(BF16)
    vtx_ref[0:HEAD_DIM, :] = vt_ref[...]

    lam = (jnp.exp(jnp.sum(lq1_ref[...] * lk1_ref[...], axis=-1, keepdims=True))
           - jnp.exp(jnp.sum(lq2_ref[...] * lk2_ref[...], axis=-1, keepdims=True)) + lam_init)

    def block_scores(qi):
        q = q_ref[qi * BQ:(qi + 1) * BQ, :].astype(F32) * (DIFF_QK_DIM ** -0.5 * log2e)
        q_ext = (
            jnp.where(lane < DIFF_QK_DIM, q,
                      jnp.where(lane < DIFF_QK_DIM + n_bias, 1.0, 0.0)).astype(BF16),
            jnp.where(lane >= DIFF_QK_DIM, q, jnp.where(lane < n_bias, 1.0, 0.0)).astype(BF16),
        )
        n_keys = (qi + 1) * BQ
        krel = lax.broadcasted_iota(jnp.int32, (BQ, BQ), 0)
        qrel = lax.broadcasted_iota(jnp.int32, (BQ, BQ), 1)
        out = []
        for c in range(2):
            s = _dot_nt(kext_ref[c, 0:n_keys, :], q_ext[c])
            diag = jnp.where(krel <= qrel, s[n_keys - BQ:, :], MASK_VALUE)
            s = jnp.concatenate([s[:n_keys - BQ, :], diag], axis=0) if qi else diag
            out.append((s, jnp.max(s, axis=0, keepdims=True)))
        return out

    n_q = seq_len // BQ
    cur = block_scores(0)
    for qi in range(n_q):
        nxt = block_scores(qi + 1) if qi + 1 < n_q else None
        vt = vtx_ref[:, 0:(qi + 1) * BQ]
        maps = []
        for c in range(2):
            s, s_max = cur[c]
            acc = _dot(vt, jnp.exp2(s - s_max).astype(BF16))
            maps.append(acc[0:HEAD_DIM, :] / acc[HEAD_DIM:HEAD_DIM + 1, :])
        o = (maps[0] - lam * maps[1]).T
        o = o * lax.rsqrt(jnp.mean(o * o, axis=-1, keepdims=True) + NORM_EPS)
        o = o * nw_ref[...] * (1.0 - lam_init)
        qrows = slice(qi * BQ, (qi + 1) * BQ)
        o_ref[qrows, :] = (o * _silu(z_ref[qrows, :].astype(F32))).astype(BF16)
        cur = nxt


def _attn(act, v_t, lq1, lk1, lq2, lk2, diff_norm_w, lam_init, batch, seq_len):
    t = act.shape[0]
    small = lambda shape: pl.BlockSpec(shape, lambda h, b: (0, 0))
    return pl.pallas_call(
        functools.partial(_attn_kernel, lam_init),
        grid=(DIFF_HEADS, batch),
        in_specs=[
            pl.BlockSpec((seq_len, HEAD_DIM), lambda h, b: (b, COL_DQ + h)),
            pl.BlockSpec((seq_len, HEAD_DIM), lambda h, b: (b, COL_DK + h)),
            pl.BlockSpec((HEAD_DIM, seq_len), lambda h, b: (h, b)),
            pl.BlockSpec((seq_len, HEAD_DIM), lambda h, b: (b, COL_DZ + h)),
            small((1, DIFF_QK_DIM)), small((1, DIFF_QK_DIM)),
            small((1, DIFF_QK_DIM)), small((1, DIFF_QK_DIM)),
            small((1, HEAD_DIM)),
        ],
        out_specs=pl.BlockSpec((seq_len, HEAD_DIM), lambda h, b: (b, h)),
        out_shape=jax.ShapeDtypeStruct((t, DIFF_WIDTH), BF16),
        scratch_shapes=[
            pltpu.VMEM((seq_len, HEAD_DIM), F32),
            pltpu.VMEM((2, seq_len, HEAD_DIM), BF16),
            pltpu.VMEM((HEAD_DIM + V_PAD, seq_len), BF16),
        ],
        compiler_params=pltpu.CompilerParams(
            dimension_semantics=("arbitrary", "arbitrary"), vmem_limit_bytes=VMEM_LIMIT),
        name="diffattn",
    )(act, act, v_t, act, lq1, lk1, lq2, lk2, diff_norm_w)


def _outproj_kernel(oa_ref, ob_ref, x_ref, w_ref, fw_ref, out_ref):
    mix = _dot(oa_ref[...], w_ref[0:GDN_WIDTH, :]) + _dot(ob_ref[...], w_ref[GDN_WIDTH:, :])
    y = x_ref[...] + mix
    ms = jnp.mean(y * y, axis=-1, keepdims=True)
    out_ref[...] = y * lax.rsqrt(ms + NORM_EPS) * fw_ref[...]


def _outproj(o_a, o_b, xf, w_out, final_norm_w):
    t = xf.shape[0]
    return pl.pallas_call(
        _outproj_kernel,
        grid=(t // TM_OUT,),
        in_specs=[
            pl.BlockSpec((TM_OUT, GDN_WIDTH), lambda i: (i, 0)),
            pl.BlockSpec((TM_OUT, DIFF_WIDTH), lambda i: (i, 0)),
            pl.BlockSpec((TM_OUT, D_MODEL), lambda i: (i, 0)),
            pl.BlockSpec((D_MODEL, D_MODEL), lambda i: (0, 0)),
            pl.BlockSpec((1, D_MODEL), lambda i: (0, 0)),
        ],
        out_specs=pl.BlockSpec((TM_OUT, D_MODEL), lambda i: (i, 0)),
        out_shape=jax.ShapeDtypeStruct((t, D_MODEL), F32),
        compiler_params=pltpu.CompilerParams(
            dimension_semantics=("arbitrary",), vmem_limit_bytes=VMEM_LIMIT),
        name="outproj",
    )(o_a, o_b, xf, w_out, final_norm_w)


def kernel(x, norm_w, w_in, conv_w, a_log, dt_bias, gdn_norm_w, lambda_q1, lambda_k1,
           lambda_q2, lambda_k2, diff_norm_w, w_out, final_norm_w):
    batch, seq_len, d_model = x.shape
    depth = norm_w.shape[0]
    assert depth == 1 and d_model == D_MODEL
    assert seq_len % TM_IN == 0 and seq_len % SC == 0 and seq_len % BQ == 0 and BQ % BK == 0
    n_wide_a = 4 * GDN_WIDTH
    xf = x.reshape(batch * seq_len, d_model)

    w = w_in[0]
    c_dq = n_wide_a + N_GATE
    c_dv = c_dq + 2 * DIFF_WIDTH
    c_dz = c_dv + DIFF_WIDTH
    w_main = jnp.concatenate([w[:, :n_wide_a], w[:, c_dq:c_dv], w[:, c_dz:]], axis=1).astype(BF16)
    w_v_t = w[:, c_dv:c_dz].T.astype(BF16)
    w_gate_t = w[:, n_wide_a:n_wide_a + N_GATE].T.astype(BF16)

    act, gate_t, v_t = _inproj(xf, norm_w[0][None, :], w_main, w_gate_t, w_v_t, conv_w[0], seq_len)

    pad_s = lambda vec: jnp.pad(vec, (GDN_HEADS, 0))[:, None]
    o_a = _gdn(act, gate_t, pad_s(a_log[0]), pad_s(dt_bias[0]), gdn_norm_w[0][None, :],
               batch, seq_len)

    lam_init = 0.8 - 0.6 * math.exp(-0.3 * 0)
    o_b = _attn(act, v_t, lambda_q1[0][None, :], lambda_k1[0][None, :], lambda_q2[0][None, :],
                lambda_k2[0][None, :], diff_norm_w[0][None, :], lam_init, batch, seq_len)

    out = _outproj(o_a, o_b, xf, w_out[0].astype(BF16), final_norm_w[None, :])
    return out.reshape(batch, seq_len, d_model)
```

```python
import functools
import math

import jax
import jax.numpy as jnp
from jax import lax
from jax.experimental import pallas as pl
from jax.experimental.pallas import tpu as pltpu

F32 = jnp.float32
BF16 = jnp.bfloat16

D_MODEL = 1024
GDN_HEADS = 4
HEAD_DIM = 128
GDN_WIDTH = GDN_HEADS * HEAD_DIM
DIFF_HEADS = 4
DIFF_QK_DIM = 64
DIFF_WIDTH = DIFF_HEADS * HEAD_DIM
CONV_K = 4
NORM_EPS = 1e-6
N_GATE = 2 * GDN_HEADS
N_MAIN = 4 * GDN_WIDTH + 3 * DIFF_WIDTH
N_CONV = 3 * GDN_WIDTH
LANES = 128
MASK_VALUE = -1e30

COL_GQ, COL_GK, COL_GV, COL_GZ = 0, 4, 8, 12
COL_DQ, COL_DK, COL_DZ = 16, 20, 24

TM_IN = 1024
NC_IN = 256
TAIL = 16
TM_OUT = 1024
SC = 256
CHUNK = 64
GDN_GROUP = 2
BQ = 256
BK = 256
V_PAD = 16
VMEM_LIMIT = 48 * 1024 * 1024


def _sigmoid(x):
    return 1.0 / (1.0 + jnp.exp(-x))


def _silu(x):
    half = 0.5 * x
    return half + half * jnp.tanh(half)


def _softplus(x):
    return jnp.maximum(x, 0.0) + jnp.log1p(jnp.exp(-jnp.abs(x)))


def _dot(a, b):
    return jnp.dot(a, b, preferred_element_type=F32)


def _dot_nt(a, b):
    return lax.dot_general(a, b, (((1,), (1,)), ((), ())), preferred_element_type=F32)


def _dot_tn(a, b):
    return lax.dot_general(a, b, (((0,), (0,)), ((), ())), preferred_element_type=F32)


def _inproj_kernel(tiles_per_seq, x_ref, nw_ref, w_ref, wgt_ref, wvt_ref, cw_ref,
                   main_ref, gate_t_ref, vt_ref, tail_ref):
    i = pl.program_id(0)
    x = x_ref[...]
    ms = jnp.mean(x * x, axis=-1, keepdims=True)
    h = (x * lax.rsqrt(ms + NORM_EPS) * nw_ref[...]).astype(BF16)

    gate_t_ref[...] = _dot_nt(wgt_ref[...], h)
    vt_ref[...] = _dot_nt(wvt_ref[...], h).astype(BF16)

    @pl.when(i % tiles_per_seq == 0)
    def _():
        tail_ref[...] = jnp.zeros_like(tail_ref)

    tm = x.shape[0]
    row = lax.broadcasted_iota(jnp.int32, (TAIL, LANES), 0)

    def conv_silu(acc, n0):
        cols = slice(n0, n0 + LANES)
        last = acc[tm - TAIL:, :]
        delta = tail_ref[:, cols] - last
        tail_ref[:, cols] = last
        y = acc * cw_ref[CONV_K - 1:CONV_K, cols]
        fix = jnp.zeros((TAIL, LANES), F32)
        for s in range(1, CONV_K):
            wk = cw_ref[CONV_K - 1 - s:CONV_K - s, cols]
            y = y + pltpu.roll(acc, s, axis=0) * wk
            fix = fix + jnp.where(row < s, pltpu.roll(delta, s, axis=0), 0.0) * wk
        main_ref[:, cols] = _silu(y).astype(BF16)
        main_ref[0:TAIL, cols] = _silu(y[:TAIL, :] + fix).astype(BF16)

    plain = [slice(n0, n0 + NC_IN) for n0 in range(N_CONV, N_MAIN, NC_IN)]
    pending = []
    for n0 in range(0, N_CONV, NC_IN):
        acc = _dot(h, w_ref[:, n0:n0 + NC_IN])
        if pending:
            conv_silu(*pending.pop(0))
        if plain:
            pc = plain.pop(0)
            main_ref[:, pc] = _dot(h, w_ref[:, pc]).astype(BF16)
        if pending:
            conv_silu(*pending.pop(0))
        pending = [(acc[:, j:j + LANES], n0 + j) for j in range(0, NC_IN, LANES)]
    for piece in pending:
        if plain:
            pc = plain.pop(0)
            main_ref[:, pc] = _dot(h, w_ref[:, pc]).astype(BF16)
        conv_silu(*piece)
    for pc in plain:
        main_ref[:, pc] = _dot(h, w_ref[:, pc]).astype(BF16)


def _inproj(xf, norm_w, w_main, w_gate_t, w_v_t, conv_w, seq_len):
    t = xf.shape[0]
    tiles_per_seq = seq_len // TM_IN
    return pl.pallas_call(
        functools.partial(_inproj_kernel, tiles_per_seq),
        grid=(t // TM_IN,),
        in_specs=[
            pl.BlockSpec((TM_IN, D_MODEL), lambda i: (i, 0)),
            pl.BlockSpec((1, D_MODEL), lambda i: (0, 0)),
            pl.BlockSpec((D_MODEL, N_MAIN), lambda i: (0, 0)),
            pl.BlockSpec((N_GATE, D_MODEL), lambda i: (0, 0)),
            pl.BlockSpec((DIFF_WIDTH, D_MODEL), lambda i: (0, 0)),
            pl.BlockSpec((CONV_K, N_CONV), lambda i: (0, 0)),
        ],
        out_specs=[
            pl.BlockSpec((TM_IN, N_MAIN), lambda i: (i, 0)),
            pl.BlockSpec((N_GATE, TM_IN), lambda i: (0, i)),
            pl.BlockSpec((DIFF_WIDTH, TM_IN), lambda i: (0, i)),
        ],
        out_shape=[
            jax.ShapeDtypeStruct((t, N_MAIN), BF16),
            jax.ShapeDtypeStruct((N_GATE, t), F32),
            jax.ShapeDtypeStruct((DIFF_WIDTH, t), BF16),
        ],
        scratch_shapes=[pltpu.VMEM((TAIL, N_CONV), F32)],
        compiler_params=pltpu.CompilerParams(
            dimension_semantics=("arbitrary",), vmem_limit_bytes=VMEM_LIMIT),
        name="inproj",
    )(xf, norm_w, w_main, w_gate_t, w_v_t, conv_w)


def _gdn_kernel(act_ref, gate_t_ref, alog_s_ref, dtb_s_ref, gnw_ref, o_ref, mask_ref, bd_ref,
                row_ref, col_ref):
    seq_len = act_ref.shape[0]
    n_chunks = SC // CHUNK
    heads = range(GDN_HEADS)
    i_cat = lax.broadcasted_iota(jnp.int32, (CHUNK, SC), 0)
    j_cat = lax.broadcasted_iota(jnp.int32, (CHUNK, SC), 1) & (CHUNK - 1)
    mask_ref[0] = jnp.where(i_cat >= j_cat, 0.0, MASK_VALUE)
    mask_ref[1] = jnp.where(i_cat > j_cat, -1.0, 0.0)
    mask_ref[2] = jnp.where(i_cat == j_cat, 1.0, 0.0)
    r = lax.broadcasted_iota(jnp.int32, (SC, SC), 0)
    c = lax.broadcasted_iota(jnp.int32, (SC, SC), 1)
    bd_ref[...] = jnp.where((r & -CHUNK) == (c & -CHUNK), 1.0, 0.0).astype(BF16)
    lane_chunk = lax.broadcasted_iota(jnp.int32, (1, SC), 1) & -CHUNK

    scale = HEAD_DIM ** -0.5

    lane_in_chunk = lax.broadcasted_iota(jnp.int32, (N_GATE, seq_len), 1) & (CHUNK - 1)
    gate_t = gate_t_ref[...]
    beta = _sigmoid(gate_t)
    log_beta = -_softplus(-gate_t)
    g_step = -jnp.exp(alog_s_ref[...]) * _softplus(gate_t + dtb_s_ref[...])
    gc = g_step
    g_after = jnp.zeros_like(g_step)
    tail = g_step
    step = 1
    while step < CHUNK:
        gc = gc + jnp.where(lane_in_chunk >= step, pltpu.roll(gc, step, axis=1), 0.0)
        ahead = jnp.where(lane_in_chunk + step < CHUNK, pltpu.roll(tail, seq_len - step, axis=1), 0.0)
        g_after = g_after + ahead
        tail = tail + ahead
        step *= 2
    e_gc = jnp.exp(gc)
    is_beta_row = lax.broadcasted_iota(jnp.int32, (N_GATE, seq_len), 0) < GDN_HEADS
    swap = pltpu.roll(e_gc, GDN_HEADS, axis=0)
    row_ref[0] = gc
    row_ref[1] = e_gc
    col_ref[...] = jnp.concatenate([
        jnp.where(is_beta_row, beta, gc),
        jnp.where(is_beta_row, log_beta, e_gc),
        jnp.where(is_beta_row, beta * swap, jnp.exp(g_after))], axis=0).T

    def cat(x):
        out = x[(n_chunks - 1) * CHUNK:, :]
        for ci in range(n_chunks - 2, -1, -1):
            out = jnp.where(lane_chunk == ci * CHUNK, x[ci * CHUNK:(ci + 1) * CHUNK, :], out)
        return out

    def block_diag(x_cat):
        xb = x_cat.astype(BF16)
        zero = jnp.zeros((CHUNK, LANES), BF16)
        row_blocks = []
        for ci in range(n_chunks):
            g = ci * CHUNK // LANES
            grp = slice(g * LANES, (g + 1) * LANES)
            part = xb[:, grp] * bd_ref[ci * CHUNK:(ci + 1) * CHUNK, grp]
            row_blocks.append(jnp.concatenate(
                [part if j == g else zero for j in range(SC // LANES)], axis=1))
        return jnp.concatenate(row_blocks, axis=0)

    def head_cols(col, h):
        return slice((col + h) * HEAD_DIM, (col + h + 1) * HEAD_DIM)

    def prepare(steps):
        causal, neg_strict, eye = mask_ref[0], mask_ref[1], mask_ref[2]
        units = [(sc, h) for sc in steps for h in heads]
        n_pow, a_cat, rhs, q_dec, k_dec, decay_last = [], [], [], [], [], []
        for sc, h in units:
            rows = slice(sc * SC, (sc + 1) * SC)
            col = lambda j: col_ref[rows, j:j + 1]
            q = act_ref[rows, head_cols(COL_GQ, h)].astype(F32)
            k = act_ref[rows, head_cols(COL_GK, h)].astype(F32)
            v = act_ref[rows, head_cols(COL_GV, h)].astype(F32)
            q = q * (lax.rsqrt(jnp.sum(q * q, axis=-1, keepdims=True) + 1e-6) * scale)
            k = k * lax.rsqrt(jnp.sum(k * k, axis=-1, keepdims=True) + 1e-6)
            hb, hg = h, GDN_HEADS + h
            kb = k.astype(BF16)
            gram = _dot_nt(jnp.concatenate([kb, q.astype(BF16)], axis=0), kb)
            e_cat = cat(col(hg)) - row_ref[0, hg:hg + 1, rows] + causal
            n_pow.append(cat(gram[:SC]) * jnp.exp(e_cat + cat(col(N_GATE + hb))) * neg_strict)
            a_cat.append(cat(gram[SC:]) * jnp.exp(e_cat))
            rhs.append(jnp.concatenate([col(hb) * v, col(2 * N_GATE + hb) * k], axis=1).astype(BF16))
            q_dec.append(q * col(N_GATE + hg))
            kd = (k * col(2 * N_GATE + hg)).astype(BF16)
            k_dec.append([kd[ci * CHUNK:(ci + 1) * CHUNK, :] for ci in range(n_chunks)])
            decay_last.append([row_ref[1, hg:hg + 1, sc * SC + (ci + 1) * CHUNK - 1:sc * SC + (ci + 1) * CHUNK]
                               for ci in range(n_chunks)])

        every = range(len(units))
        t_cat = [eye + n_pow[u] for u in every]
        n_pow = [_dot(n_pow[u].astype(BF16), block_diag(n_pow[u])) for u in every]
        span = 2
        while span < CHUNK // 2:
            both = [_dot(jnp.concatenate([t_cat[u], n_pow[u]], axis=0).astype(BF16), block_diag(n_pow[u]))
                    for u in every]
            t_cat = [t_cat[u] + both[u][:CHUNK] for u in every]
            n_pow = [both[u][CHUNK:] for u in every]
            span *= 2
        t_cat = [t_cat[u] + _dot(t_cat[u].astype(BF16), block_diag(n_pow[u])) for u in every]

        uw = [_dot(block_diag(t_cat[u]), rhs[u]) for u in every]
        kt = [[_dot_tn(k_dec[u][ci], uw[u][ci * CHUNK:(ci + 1) * CHUNK, :].astype(BF16))
               for ci in range(n_chunks)] for u in every]
        per_step = lambda xs, i: xs[i * GDN_HEADS:(i + 1) * GDN_HEADS]
        return [tuple(per_step(xs, i) for xs in (uw, kt, q_dec, a_cat, decay_last))
                for i in range(len(steps))]

    def finish(sc, prep, state):
        rows = slice(sc * SC, (sc + 1) * SC)
        uw, kt, q_dec, a_cat, decay_last = prep
        state = list(state)
        state_in = [[None] * n_chunks for _ in heads]
        for ci in range(n_chunks):
            for h in heads:
                sb = state[h].astype(BF16)
                state_in[h][ci] = sb
                state[h] = (state[h] * decay_last[h][ci]
                            - _dot(kt[h][ci][:, HEAD_DIM:].astype(BF16), sb) + kt[h][ci][:, :HEAD_DIM])
        for h in heads:
            inter = []
            for ci in range(n_chunks):
                cr = slice(ci * CHUNK, (ci + 1) * CHUNK)
                lhs = jnp.concatenate([uw[h][cr, HEAD_DIM:], q_dec[h][cr, :]], axis=0).astype(BF16)
                inter.append(_dot(lhs, state_in[h][ci]))
            v_new = jnp.concatenate([uw[h][ci * CHUNK:(ci + 1) * CHUNK, :HEAD_DIM] - inter[ci][:CHUNK]
                                     for ci in range(n_chunks)], axis=0).astype(BF16)
            o = (jnp.concatenate([inter[ci][CHUNK:] for ci in range(n_chunks)], axis=0)
                 + _dot(block_diag(a_cat[h]), v_new))
            o = o * lax.rsqrt(jnp.mean(o * o, axis=-1, keepdims=True) + NORM_EPS)
            z = act_ref[rows, head_cols(COL_GZ, h)].astype(F32)
            o_ref[rows, head_cols(0, h)] = (o * (_silu(z) * gnw_ref[...])).astype(BF16)
        return state

    n_groups = seq_len // (SC * GDN_GROUP)
    group = lambda g: list(range(g * GDN_GROUP, (g + 1) * GDN_GROUP))
    state = [jnp.zeros((HEAD_DIM, HEAD_DIM), F32) for _ in heads]
    preps = prepare(group(0))
    for g in range(n_groups):
        nxt = prepare(group(g + 1)) if g + 1 < n_groups else None
        for sc, prep in zip(group(g), preps):
            state = finish(sc, prep, state)
        preps = nxt


def _gdn(act, gate_t, alog_s, dtb_s, gdn_norm_w, batch, seq_len):
    t = act.shape[0]
    small = lambda shape: pl.BlockSpec(shape, lambda b: (0, 0))
    return pl.pallas_call(
        _gdn_kernel,
        grid=(batch,),
        in_specs=[
            pl.BlockSpec((seq_len, 4 * GDN_WIDTH), lambda b: (b, 0)),
            pl.BlockSpec((N_GATE, seq_len), lambda b: (0, b)),
            small((N_GATE, 1)), small((N_GATE, 1)),
            small((1, HEAD_DIM)),
        ],
        out_specs=pl.BlockSpec((seq_len, GDN_WIDTH), lambda b: (b, 0)),
        out_shape=jax.ShapeDtypeStruct((t, GDN_WIDTH), BF16),
        scratch_shapes=[
            pltpu.VMEM((3, CHUNK, SC), F32),
            pltpu.VMEM((SC, SC), BF16),
            pltpu.VMEM((2, N_GATE, seq_len), F32),
            pltpu.VMEM((seq_len, 3 * N_GATE), F32),
        ],
        compiler_params=pltpu.CompilerParams(
            dimension_semantics=("arbitrary",), vmem_limit_bytes=VMEM_LIMIT),
        name="gdn",
    )(act, gate_t, alog_s, dtb_s, gdn_norm_w)


def _split3(x):
    hi = x.astype(BF16).astype(F32)
    r1 = x - hi
    mid = r1.astype(BF16).astype(F32)
    return hi, mid, r1 - mid


def _attn_kernel(lam_init, q_ref, k_ref, vt_ref, z_ref, lq1_ref, lk1_ref, lq2_ref, lk2_ref,
                 nw_ref, o_ref, bias_ref, kext_ref, vtx_ref):
    h = pl.program_id(0)
    b = pl.program_id(1)
    seq_len = k_ref.shape[0]
    lane = lax.broadcasted_iota(jnp.int32, (1, HEAD_DIM), 1)
    n_bias = 6
    log2e = math.log2(math.e)

    @pl.when(b == 0)
    def _():
        slope = lax.shift_left(jnp.int32(1), 2 * (DIFF_HEADS - 1 - h)).astype(F32) * (log2e / 256.0)
        pos = lax.broadcasted_iota(jnp.int32, (seq_len, 1), 0)
        in_block = (pos & (BK - 1)).astype(F32) * slope
        block_off = (pos & -BK).astype(F32) * slope
        terms = _split3(in_block) + _split3(block_off)
        tile = jnp.zeros((seq_len, HEAD_DIM), F32)
        for j, term in enumerate(terms):
            tile = jnp.where((lane == j) | (lane == DIFF_QK_DIM + j), term, tile)
        bias_ref[...] = tile
        ones_row = lax.broadcasted_iota(jnp.int32, (V_PAD, seq_len), 0) == 0
        vtx_ref[HEAD_DIM:, :] = jnp.where(ones_row, 1.0, 0.0).astype(BF16)

    k = k_ref[...].astype(F32)
    kext_ref[0] = jnp.where(lane < DIFF_QK_DIM, k, bias_ref[...]).astype(BF16)
    kext_ref[1] = jnp.where(lane >= DIFF_QK_DIM, k, bias_ref[...]).astype(BF16)
    vtx_ref[0:HEAD_DIM, :] = vt_ref[...]

    lam = (jnp.exp(jnp.sum(lq1_ref[...] * lk1_ref[...], axis=-1, keepdims=True))
           - jnp.exp(jnp.sum(lq2_ref[...] * lk2_ref[...], axis=-1, keepdims=True)) + lam_init)

    def block_scores(qi):
        q = q_ref[qi * BQ:(qi + 1) * BQ, :].astype(F32) * (DIFF_QK_DIM ** -0.5 * log2e)
        q_ext = (
            jnp.where(lane < DIFF_QK_DIM, q,
                      jnp.where(lane < DIFF_QK_DIM + n_bias, 1.0, 0.0)).astype(BF16),
            jnp.where(lane >= DIFF_QK_DIM, q, jnp.where(lane < n_bias, 1.0, 0.0)).astype(BF16),
        )
        n_keys = (qi + 1) * BQ
        krel = lax.broadcasted_iota(jnp.int32, (BQ, BQ), 0)
        qrel = lax.broadcasted_iota(jnp.int32, (BQ, BQ), 1)
        out = []
        for c in range(2):
            s = _dot_nt(kext_ref[c, 0:n_keys, :], q_ext[c])
            diag = jnp.where(krel <= qrel, s[n_keys - BQ:, :], MASK_VALUE)
            s = jnp.concatenate([s[:n_keys - BQ, :], diag], axis=0) if qi else diag
            out.append((s, jnp.max(s, axis=0, keepdims=True)))
        return out

    n_q = seq_len // BQ
    cur = block_scores(0)
    for qi in range(n_q):
        nxt = block_scores(qi + 1) if qi + 1 < n_q else None
        vt = vtx_ref[:, 0:(qi + 1) * BQ]
        maps = []
        for c in range(2):
            s, s_max = cur[c]
            acc = _dot(vt, jnp.exp2(s - s_max).astype(BF16))
            maps.append(acc[0:HEAD_DIM, :] / acc[HEAD_DIM:HEAD_DIM + 1, :])
        o = (maps[0] - lam * maps[1]).T
        o = o * lax.rsqrt(jnp.mean(o * o, axis=-1, keepdims=True) + NORM_EPS)
        o = o * nw_ref[...] * (1.0 - lam_init)
        qrows = slice(qi * BQ, (qi + 1) * BQ)
        o_ref[qrows, :] = (o * _silu(z_ref[qrows, :].astype(F32))).astype(BF16)
        cur = nxt


def _attn(act, v_t, lq1, lk1, lq2, lk2, diff_norm_w, lam_init, batch, seq_len):
    t = act.shape[0]
    small = lambda shape: pl.BlockSpec(shape, lambda h, b: (0, 0))
    return pl.pallas_call(
        functools.partial(_attn_kernel, lam_init),
        grid=(DIFF_HEADS, batch),
        in_specs=[
            pl.BlockSpec((seq_len, HEAD_DIM), lambda h, b: (b, COL_DQ + h)),
            pl.BlockSpec((seq_len, HEAD_DIM), lambda h, b: (b, COL_DK + h)),
            pl.BlockSpec((HEAD_DIM, seq_len), lambda h, b: (h, b)),
            pl.BlockSpec((seq_len, HEAD_DIM), lambda h, b: (b, COL_DZ + h)),
            small((1, DIFF_QK_DIM)), small((1, DIFF_QK_DIM)),
            small((1, DIFF_QK_DIM)), small((1, DIFF_QK_DIM)),
            small((1, HEAD_DIM)),
        ],
        out_specs=pl.BlockSpec((seq_len, HEAD_DIM), lambda h, b: (b, h)),
        out_shape=jax.ShapeDtypeStruct((t, DIFF_WIDTH), BF16),
        scratch_shapes=[
            pltpu.VMEM((seq_len, HEAD_DIM), F32),
            pltpu.VMEM((2, seq_len, HEAD_DIM), BF16),
            pltpu.VMEM((HEAD_DIM + V_PAD, seq_len), BF16),
        ],
        compiler_params=pltpu.CompilerParams(
            dimension_semantics=("arbitrary", "arbitrary"), vmem_limit_bytes=VMEM_LIMIT),
        name="diffattn",
    )(act, act, v_t, act, lq1, lk1, lq2, lk2, diff_norm_w)


def _outproj_kernel(oa_ref, ob_ref, x_ref, w_ref, fw_ref, out_ref):
    mix = _dot(oa_ref[...], w_ref[0:GDN_WIDTH, :]) + _dot(ob_ref[...], w_ref[GDN_WIDTH:, :])
    y = x_ref[...] + mix
    ms = jnp.mean(y * y, axis=-1, keepdims=True)
    out_ref[...] = y * lax.rsqrt(ms + NORM_EPS) * fw_ref[...]


def _outproj(o_a, o_b, xf, w_out, final_norm_w):
    t = xf.shape[0]
    return pl.pallas_call(
        _outproj_kernel,
        grid=(t // TM_OUT,),
        in_specs=[
            pl.BlockSpec((TM_OUT, GDN_WIDTH), lambda i: (i, 0)),
            pl.BlockSpec((TM_OUT, DIFF_WIDTH), lambda i: (i, 0)),
            pl.BlockSpec((TM_OUT, D_MODEL), lambda i: (i, 0)),
            pl.BlockSpec((D_MODEL, D_MODEL), lambda i: (0, 0)),
            pl.BlockSpec((1, D_MODEL), lambda i: (0, 0)),
        ],
        out_specs=pl.BlockSpec((TM_OUT, D_MODEL), lambda i: (i, 0)),
        out_shape=jax.ShapeDtypeStruct((t, D_MODEL), F32),
        compiler_params=pltpu.CompilerParams(
            dimension_semantics=("arbitrary",), vmem_limit_bytes=VMEM_LIMIT),
        name="outproj",
    )(o_a, o_b, xf, w_out, final_norm_w)


def kernel(x, norm_w, w_in, conv_w, a_log, dt_bias, gdn_norm_w, lambda_q1, lambda_k1,
           lambda_q2, lambda_k2, diff_norm_w, w_out, final_norm_w):
    batch, seq_len, d_model = x.shape
    depth = norm_w.shape[0]
    assert depth == 1 and d_model == D_MODEL
    assert seq_len % TM_IN == 0 and seq_len % SC == 0 and seq_len % BQ == 0 and BQ % BK == 0
    n_wide_a = 4 * GDN_WIDTH
    xf = x.reshape(batch * seq_len, d_model)

    w = w_in[0]
    c_dq = n_wide_a + N_GATE
    c_dv = c_dq + 2 * DIFF_WIDTH
    c_dz = c_dv + DIFF_WIDTH
    w_main = jnp.concatenate([w[:, :n_wide_a], w[:, c_dq:c_dv], w[:, c_dz:]], axis=1).astype(BF16)
    w_v_t = w[:, c_dv:c_dz].T.astype(BF16)
    w_gate_t = w[:, n_wide_a:n_wide_a + N_GATE].T.astype(BF16)

    act, gate_t, v_t = _inproj(xf, norm_w[0][None, :], w_main, w_gate_t, w_v_t, conv_w[0], seq_len)

    pad_s = lambda vec: jnp.pad(vec, (GDN_HEADS, 0))[:, None]
    o_a = _gdn(act, gate_t, pad_s(a_log[0]), pad_s(dt_bias[0]), gdn_norm_w[0][None, :],
               batch, seq_len)

    lam_init = 0.8 - 0.6 * math.exp(-0.3 * 0)
    o_b = _attn(act, v_t, lambda_q1[0][None, :], lambda_k1[0][None, :], lambda_q2[0][None, :],
                lambda_k2[0][None, :], diff_norm_w[0][None, :], lam_init, batch, seq_len)

    out = _outproj(o_a, o_b, xf, w_out[0].astype(BF16), final_norm_w[None, :])
    return out.reshape(batch, seq_len, d_model)
```

```python
import functools
import math

import jax
import jax.numpy as jnp
from jax import lax
from jax.experimental import pallas as pl
from jax.experimental.pallas import tpu as pltpu

F32 = jnp.float32
BF16 = jnp.bfloat16

D_MODEL = 1024
GDN_HEADS = 4
HEAD_DIM = 128
GDN_WIDTH = GDN_HEADS * HEAD_DIM
DIFF_HEADS = 4
DIFF_QK_DIM = 64
DIFF_WIDTH = DIFF_HEADS * HEAD_DIM
CONV_K = 4
NORM_EPS = 1e-6
N_GATE = 2 * GDN_HEADS
N_MAIN = 4 * GDN_WIDTH + 3 * DIFF_WIDTH
N_CONV = 3 * GDN_WIDTH
LANES = 128
MASK_VALUE = -1e30

COL_GQ, COL_GK, COL_GV, COL_GZ = 0, 4, 8, 12
COL_DQ, COL_DK, COL_DZ = 16, 20, 24

TM_IN = 1024
NC_IN = 256
TAIL = 16
TM_OUT = 1024
SC = 256
CHUNK = 64
GDN_GROUP = 2
BQ = 512
BK = 256
V_PAD = 16
VMEM_LIMIT = 48 * 1024 * 1024


def _sigmoid(x):
    return 1.0 / (1.0 + jnp.exp(-x))


def _silu(x):
    half = 0.5 * x
    return half + half * jnp.tanh(half)


def _softplus(x):
    return jnp.maximum(x, 0.0) + jnp.log1p(jnp.exp(-jnp.abs(x)))


def _dot(a, b):
    return jnp.dot(a, b, preferred_element_type=F32)


def _dot_nt(a, b):
    return lax.dot_general(a, b, (((1,), (1,)), ((), ())), preferred_element_type=F32)


def _dot_tn(a, b):
    return lax.dot_general(a, b, (((0,), (0,)), ((), ())), preferred_element_type=F32)


def _inproj_kernel(tiles_per_seq, x_ref, nw_ref, w_ref, wgt_ref, wvt_ref, cw_ref,
                   main_ref, gate_t_ref, vt_ref, tail_ref):
    i = pl.program_id(0)
    x = x_ref[...]
    ms = jnp.mean(x * x, axis=-1, keepdims=True)
    h = (x * lax.rsqrt(ms + NORM_EPS) * nw_ref[...]).astype(BF16)

    gate_t_ref[...] = _dot_nt(wgt_ref[...], h)
    vt_ref[...] = _dot_nt(wvt_ref[...], h).astype(BF16)

    @pl.when(i % tiles_per_seq == 0)
    def _():
        tail_ref[...] = jnp.zeros_like(tail_ref)

    tm = x.shape[0]
    row = lax.broadcasted_iota(jnp.int32, (TAIL, LANES), 0)

    def conv_silu(acc, n0):
        cols = slice(n0, n0 + LANES)
        last = acc[tm - TAIL:, :]
        delta = tail_ref[:, cols] - last
        tail_ref[:, cols] = last
        y = acc * cw_ref[CONV_K - 1:CONV_K, cols]
        fix = jnp.zeros((TAIL, LANES), F32)
        for s in range(1, CONV_K):
            wk = cw_ref[CONV_K - 1 - s:CONV_K - s, cols]
            y = y + pltpu.roll(acc, s, axis=0) * wk
            fix = fix + jnp.where(row < s, pltpu.roll(delta, s, axis=0), 0.0) * wk
        main_ref[:, cols] = _silu(y).astype(BF16)
        main_ref[0:TAIL, cols] = _silu(y[:TAIL, :] + fix).astype(BF16)

    plain = [slice(n0, n0 + NC_IN) for n0 in range(N_CONV, N_MAIN, NC_IN)]
    pending = []
    for n0 in range(0, N_CONV, NC_IN):
        acc = _dot(h, w_ref[:, n0:n0 + NC_IN])
        if pending:
            conv_silu(*pending.pop(0))
        if plain:
            pc = plain.pop(0)
            main_ref[:, pc] = _dot(h, w_ref[:, pc]).astype(BF16)
        if pending:
            conv_silu(*pending.pop(0))
        pending = [(acc[:, j:j + LANES], n0 + j) for j in range(0, NC_IN, LANES)]
    for piece in pending:
        if plain:
            pc = plain.pop(0)
            main_ref[:, pc] = _dot(h, w_ref[:, pc]).astype(BF16)
        conv_silu(*piece)
    for pc in plain:
        main_ref[:, pc] = _dot(h, w_ref[:, pc]).astype(BF16)


def _inproj(xf, norm_w, w_main, w_gate_t, w_v_t, conv_w, seq_len):
    t = xf.shape[0]
    tiles_per_seq = seq_len // TM_IN
    return pl.pallas_call(
        functools.partial(_inproj_kernel, tiles_per_seq),
        grid=(t // TM_IN,),
        in_specs=[
            pl.BlockSpec((TM_IN, D_MODEL), lambda i: (i, 0)),
            pl.BlockSpec((1, D_MODEL), lambda i: (0, 0)),
            pl.BlockSpec((D_MODEL, N_MAIN), lambda i: (0, 0)),
            pl.BlockSpec((N_GATE, D_MODEL), lambda i: (0, 0)),
            pl.BlockSpec((DIFF_WIDTH, D_MODEL), lambda i: (0, 0)),
            pl.BlockSpec((CONV_K, N_CONV), lambda i: (0, 0)),
        ],
        out_specs=[
            pl.BlockSpec((TM_IN, N_MAIN), lambda i: (i, 0)),
            pl.BlockSpec((N_GATE, TM_IN), lambda i: (0, i)),
            pl.BlockSpec((DIFF_WIDTH, TM_IN), lambda i: (0, i)),
        ],
        out_shape=[
            jax.ShapeDtypeStruct((t, N_MAIN), BF16),
            jax.ShapeDtypeStruct((N_GATE, t), F32),
            jax.ShapeDtypeStruct((DIFF_WIDTH, t), BF16),
        ],
        scratch_shapes=[pltpu.VMEM((TAIL, N_CONV), F32)],
        compiler_params=pltpu.CompilerParams(
            dimension_semantics=("arbitrary",), vmem_limit_bytes=VMEM_LIMIT),
        name="inproj",
    )(xf, norm_w, w_main, w_gate_t, w_v_t, conv_w)


def _gdn_kernel(act_ref, gate_t_ref, alog_s_ref, dtb_s_ref, gnw_ref, o_ref, mask_ref, bd_ref,
                row_ref, col_ref):
    seq_len = act_ref.shape[0]
    n_chunks = SC // CHUNK
    heads = range(GDN_HEADS)
    i_cat = lax.broadcasted_iota(jnp.int32, (CHUNK, SC), 0)
    j_cat = lax.broadcasted_iota(jnp.int32, (CHUNK, SC), 1) & (CHUNK - 1)
    mask_ref[0] = jnp.where(i_cat >= j_cat, 0.0, MASK_VALUE)
    mask_ref[1] = jnp.where(i_cat > j_cat, -1.0, 0.0)
    mask_ref[2] = jnp.where(i_cat == j_cat, 1.0, 0.0)
    r = lax.broadcasted_iota(jnp.int32, (SC, SC), 0)
    c = lax.broadcasted_iota(jnp.int32, (SC, SC), 1)
    bd_ref[...] = jnp.where((r & -CHUNK) == (c & -CHUNK), 1.0, 0.0).astype(BF16)
    lane_chunk = lax.broadcasted_iota(jnp.int32, (1, SC), 1) & -CHUNK

    scale = HEAD_DIM ** -0.5

    lane_in_chunk = lax.broadcasted_iota(jnp.int32, (N_GATE, seq_len), 1) & (CHUNK - 1)
    gate_t = gate_t_ref[...]
    beta = _sigmoid(gate_t)
    log_beta = -_softplus(-gate_t)
    g_step = -jnp.exp(alog_s_ref[...]) * _softplus(gate_t + dtb_s_ref[...])
    gc = g_step
    g_after = jnp.zeros_like(g_step)
    tail = g_step
    step = 1
    while step < CHUNK:
        gc = gc + jnp.where(lane_in_chunk >= step, pltpu.roll(gc, step, axis=1), 0.0)
        ahead = jnp.where(lane_in_chunk + step < CHUNK, pltpu.roll(tail, seq_len - step, axis=1), 0.0)
        g_after = g_after + ahead
        tail = tail + ahead
        step *= 2
    e_gc = jnp.exp(gc)
    is_beta_row = lax.broadcasted_iota(jnp.int32, (N_GATE, seq_len), 0) < GDN_HEADS
    swap = pltpu.roll(e_gc, GDN_HEADS, axis=0)
    row_ref[0] = gc
    row_ref[1] = e_gc
    col_ref[...] = jnp.concatenate([
        jnp.where(is_beta_row, beta, gc),
        jnp.where(is_beta_row, log_beta, e_gc),
        jnp.where(is_beta_row, beta * swap, jnp.exp(g_after))], axis=0).T

    def cat(x):
        out = x[(n_chunks - 1) * CHUNK:, :]
        for ci in range(n_chunks - 2, -1, -1):
            out = jnp.where(lane_chunk == ci * CHUNK, x[ci * CHUNK:(ci + 1) * CHUNK, :], out)
        return out

    def block_diag(x_cat):
        xb = x_cat.astype(BF16)
        zero = jnp.zeros((CHUNK, LANES), BF16)
        row_blocks = []
        for ci in range(n_chunks):
            g = ci * CHUNK // LANES
            grp = slice(g * LANES, (g + 1) * LANES)
            part = xb[:, grp] * bd_ref[ci * CHUNK:(ci + 1) * CHUNK, grp]
            row_blocks.append(jnp.concatenate(
                [part if j == g else zero for j in range(SC // LANES)], axis=1))
        return jnp.concatenate(row_blocks, axis=0)

    def head_cols(col, h):
        return slice((col + h) * HEAD_DIM, (col + h + 1) * HEAD_DIM)

    def prepare(steps):
        causal, neg_strict, eye = mask_ref[0], mask_ref[1], mask_ref[2]
        units = [(sc, h) for sc in steps for h in heads]
        n_pow, a_cat, rhs, q_dec, k_dec, decay_last = [], [], [], [], [], []
        for sc, h in units:
            rows = slice(sc * SC, (sc + 1) * SC)
            col = lambda j: col_ref[rows, j:j + 1]
            q = act_ref[rows, head_cols(COL_GQ, h)].astype(F32)
            k = act_ref[rows, head_cols(COL_GK, h)].astype(F32)
            v = act_ref[rows, head_cols(COL_GV, h)].astype(F32)
            q = q * (lax.rsqrt(jnp.sum(q * q, axis=-1, keepdims=True) + 1e-6) * scale)
            k = k * lax.rsqrt(jnp.sum(k * k, axis=-1, keepdims=True) + 1e-6)
            hb, hg = h, GDN_HEADS + h
            kb = k.astype(BF16)
            gram = _dot_nt(jnp.concatenate([kb, q.astype(BF16)], axis=0), kb)
            e_cat = cat(col(hg)) - row_ref[0, hg:hg + 1, rows] + causal
            n_pow.append(cat(gram[:SC]) * jnp.exp(e_cat + cat(col(N_GATE + hb))) * neg_strict)
            a_cat.append(cat(gram[SC:]) * jnp.exp(e_cat))
            rhs.append(jnp.concatenate([col(hb) * v, col(2 * N_GATE + hb) * k], axis=1).astype(BF16))
            q_dec.append(q * col(N_GATE + hg))
            kd = (k * col(2 * N_GATE + hg)).astype(BF16)
            k_dec.append([kd[ci * CHUNK:(ci + 1) * CHUNK, :] for ci in range(n_chunks)])
            decay_last.append([row_ref[1, hg:hg + 1, sc * SC + (ci + 1) * CHUNK - 1:sc * SC + (ci + 1) * CHUNK]
                               for ci in range(n_chunks)])

        every = range(len(units))
        t_cat = [eye + n_pow[u] for u in every]
        n_pow = [_dot(n_pow[u].astype(BF16), block_diag(n_pow[u])) for u in every]
        span = 2
        while span < CHUNK // 2:
            both = [_dot(jnp.concatenate([t_cat[u], n_pow[u]], axis=0).astype(BF16), block_diag(n_pow[u]))
                    for u in every]
            t_cat = [t_cat[u] + both[u][:CHUNK] for u in every]
            n_pow = [both[u][CHUNK:] for u in every]
            span *= 2
        t_cat = [t_cat[u] + _dot(t_cat[u].astype(BF16), block_diag(n_pow[u])) for u in every]

        uw = [_dot(block_diag(t_cat[u]), rhs[u]) for u in every]
        kt = [[_dot_tn(k_dec[u][ci], uw[u][ci * CHUNK:(ci + 1) * CHUNK, :].astype(BF16))
               for ci in range(n_chunks)] for u in every]
        per_step = lambda xs, i: xs[i * GDN_HEADS:(i + 1) * GDN_HEADS]
        return [tuple(per_step(xs, i) for xs in (uw, kt, q_dec, a_cat, decay_last))
                for i in range(len(steps))]

    def finish(sc, prep, state):
        rows = slice(sc * SC, (sc + 1) * SC)
        uw, kt, q_dec, a_cat, decay_last = prep
        state = list(state)
        state_in = [[None] * n_chunks for _ in heads]
        for ci in range(n_chunks):
            for h in heads:
                sb = state[h].astype(BF16)
                state_in[h][ci] = sb
                state[h] = (state[h] * decay_last[h][ci]
                            - _dot(kt[h][ci][:, HEAD_DIM:].astype(BF16), sb) + kt[h][ci][:, :HEAD_DIM])
        for h in heads:
            inter = []
            for ci in range(n_chunks):
                cr = slice(ci * CHUNK, (ci + 1) * CHUNK)
                lhs = jnp.concatenate([uw[h][cr, HEAD_DIM:], q_dec[h][cr, :]], axis=0).astype(BF16)
                inter.append(_dot(lhs, state_in[h][ci]))
            v_new = jnp.concatenate([uw[h][ci * CHUNK:(ci + 1) * CHUNK, :HEAD_DIM] - inter[ci][:CHUNK]
                                     for ci in range(n_chunks)], axis=0).astype(BF16)
            o = (jnp.concatenate([inter[ci][CHUNK:] for ci in range(n_chunks)], axis=0)
                 + _dot(block_diag(a_cat[h]), v_new))
            o = o * lax.rsqrt(jnp.mean(o * o, axis=-1, keepdims=True) + NORM_EPS)
            z = act_ref[rows, head_cols(COL_GZ, h)].astype(F32)
            o_ref[rows, head_cols(0, h)] = (o * (_silu(z) * gnw_ref[...])).astype(BF16)
        return state

    n_groups = seq_len // (SC * GDN_GROUP)
    group = lambda g: list(range(g * GDN_GROUP, (g + 1) * GDN_GROUP))
    state = [jnp.zeros((HEAD_DIM, HEAD_DIM), F32) for _ in heads]
    preps = prepare(group(0))
    for g in range(n_groups):
        nxt = prepare(group(g + 1)) if g + 1 < n_groups else None
        for sc, prep in zip(group(g), preps):
            state = finish(sc, prep, state)
        preps = nxt


def _gdn(act, gate_t, alog_s, dtb_s, gdn_norm_w, batch, seq_len):
    t = act.shape[0]
    small = lambda shape: pl.BlockSpec(shape, lambda b: (0, 0))
    return pl.pallas_call(
        _gdn_kernel,
        grid=(batch,),
        in_specs=[
            pl.BlockSpec((seq_len, 4 * GDN_WIDTH), lambda b: (b, 0)),
            pl.BlockSpec((N_GATE, seq_len), lambda b: (0, b)),
            small((N_GATE, 1)), small((N_GATE, 1)),
            small((1, HEAD_DIM)),
        ],
        out_specs=pl.BlockSpec((seq_len, GDN_WIDTH), lambda b: (b, 0)),
        out_shape=jax.ShapeDtypeStruct((t, GDN_WIDTH), BF16),
        scratch_shapes=[
            pltpu.VMEM((3, CHUNK, SC), F32),
            pltpu.VMEM((SC, SC), BF16),
            pltpu.VMEM((2, N_GATE, seq_len), F32),
            pltpu.VMEM((seq_len, 3 * N_GATE), F32),
        ],
        compiler_params=pltpu.CompilerParams(
            dimension_semantics=("arbitrary",), vmem_limit_bytes=VMEM_LIMIT),
        name="gdn",
    )(act, gate_t, alog_s, dtb_s, gdn_norm_w)


def _split3(x):
    hi = x.astype(BF16).astype(F32)
    r1 = x - hi
    mid = r1.astype(BF16).astype(F32)
    return hi, mid, r1 - mid


def _attn_kernel(lam_init, q_ref, k_ref, vt_ref, z_ref, lq1_ref, lk1_ref, lq2_ref, lk2_ref,
                 nw_ref, o_ref, bias_ref, kext_ref, vtx_ref):
    h = pl.program_id(0)
    b = pl.program_id(1)
    seq_len = k_ref.shape[0]
    lane = lax.broadcasted_iota(jnp.int32, (1, HEAD_DIM), 1)
    n_bias = 6
    log2e = math.log2(math.e)

    @pl.when(b == 0)
    def _():
        slope = lax.shift_left(jnp.int32(1), 2 * (DIFF_HEADS - 1 - h)).astype(F32) * (log2e / 256.0)
        pos = lax.broadcasted_iota(jnp.int32, (seq_len, 1), 0)
        in_block = (pos & (BK - 1)).astype(F32) * slope
        block_off = (pos & -BK).astype(F32) * slope
        terms = _split3(in_block) + _split3(block_off)
        tile = jnp.zeros((seq_len, HEAD_DIM), F32)
        for j, term in enumerate(terms):
            tile = jnp.where((lane == j) | (lane == DIFF_QK_DIM + j), term, tile)
        bias_ref[...] = tile
        ones_row = lax.broadcasted_iota(jnp.int32, (V_PAD, seq_len), 0) == 0
        vtx_ref[HEAD_DIM:, :] = jnp.where(ones_row, 1.0, 0.0).astype(BF16)

    k = k_ref[...].astype(F32)
    kext_ref[0] = jnp.where(lane < DIFF_QK_DIM, k, bias_ref[...]).astype(BF16)
    kext_ref[1] = jnp.where(lane >= DIFF_QK_DIM, k, bias_ref[...]).astype(BF16)
    vtx_ref[0:HEAD_DIM, :] = vt_ref[...]

    lam = (jnp.exp(jnp.sum(lq1_ref[...] * lk1_ref[...], axis=-1, keepdims=True))
           - jnp.exp(jnp.sum(lq2_ref[...] * lk2_ref[...], axis=-1, keepdims=True)) + lam_init)

    def block_scores(qi):
        q = q_ref[qi * BQ:(qi + 1) * BQ, :].astype(F32) * (DIFF_QK_DIM ** -0.5 * log2e)
        q_ext = (
            jnp.where(lane < DIFF_QK_DIM, q,
                      jnp.where(lane < DIFF_QK_DIM + n_bias, 1.0, 0.0)).astype(BF16),
            jnp.where(lane >= DIFF_QK_DIM, q, jnp.where(lane < n_bias, 1.0, 0.0)).astype(BF16),
        )
        n_keys = (qi + 1) * BQ
        krel = lax.broadcasted_iota(jnp.int32, (BQ, BQ), 0)
        qrel = lax.broadcasted_iota(jnp.int32, (BQ, BQ), 1)
        out = []
        for c in range(2):
            s = _dot_nt(kext_ref[c, 0:n_keys, :], q_ext[c])
            diag = jnp.where(krel <= qrel, s[n_keys - BQ:, :], MASK_VALUE)
            s_max = jnp.max(diag, axis=0, keepdims=True)
            parts = [diag]
            if qi:
                below = s[:n_keys - BQ, :]
                s_max = jnp.maximum(s_max, jnp.max(below, axis=0, keepdims=True))
                parts.append(below)
            out.append((parts, s_max))
        return out

    n_q = seq_len // BQ
    cur = block_scores(0)
    for qi in range(n_q):
        nxt = block_scores(qi + 1) if qi + 1 < n_q else None
        maps = []
        for c in range(2):
            parts, s_max = cur[c]
            acc = None
            k_hi = (qi + 1) * BQ
            for s in parts:
                k_lo = k_hi - s.shape[0]
                term = _dot(vtx_ref[:, k_lo:k_hi], jnp.exp2(s - s_max).astype(BF16))
                acc = term if acc is None else acc + term
                k_hi = k_lo
            maps.append(acc[0:HEAD_DIM, :] / acc[HEAD_DIM:HEAD_DIM + 1, :])
        o = (maps[0] - lam * maps[1]).T
        o = o * lax.rsqrt(jnp.mean(o * o, axis=-1, keepdims=True) + NORM_EPS)
        o = o * nw_ref[...] * (1.0 - lam_init)
        qrows = slice(qi * BQ, (qi + 1) * BQ)
        o_ref[qrows, :] = (o * _silu(z_ref[qrows, :].astype(F32))).astype(BF16)
        cur = nxt


def _attn(act, v_t, lq1, lk1, lq2, lk2, diff_norm_w, lam_init, batch, seq_len):
    t = act.shape[0]
    small = lambda shape: pl.BlockSpec(shape, lambda h, b: (0, 0))
    return pl.pallas_call(
        functools.partial(_attn_kernel, lam_init),
        grid=(DIFF_HEADS, batch),
        in_specs=[
            pl.BlockSpec((seq_len, HEAD_DIM), lambda h, b: (b, COL_DQ + h)),
            pl.BlockSpec((seq_len, HEAD_DIM), lambda h, b: (b, COL_DK + h)),
            pl.BlockSpec((HEAD_DIM, seq_len), lambda h, b: (h, b)),
            pl.BlockSpec((seq_len, HEAD_DIM), lambda h, b: (b, COL_DZ + h)),
            small((1, DIFF_QK_DIM)), small((1, DIFF_QK_DIM)),
            small((1, DIFF_QK_DIM)), small((1, DIFF_QK_DIM)),
            small((1, HEAD_DIM)),
        ],
        out_specs=pl.BlockSpec((seq_len, HEAD_DIM), lambda h, b: (b, h)),
        out_shape=jax.ShapeDtypeStruct((t, DIFF_WIDTH), BF16),
        scratch_shapes=[
            pltpu.VMEM((seq_len, HEAD_DIM), F32),
            pltpu.VMEM((2, seq_len, HEAD_DIM), BF16),
            pltpu.VMEM((HEAD_DIM + V_PAD, seq_len), BF16),
        ],
        compiler_params=pltpu.CompilerParams(
            dimension_semantics=("arbitrary", "arbitrary"), vmem_limit_bytes=VMEM_LIMIT),
        name="diffattn",
    )(act, act, v_t, act, lq1, lk1, lq2, lk2, diff_norm_w)


def _outproj_kernel(oa_ref, ob_ref, x_ref, w_ref, fw_ref, out_ref):
    mix = _dot(oa_ref[...], w_ref[0:GDN_WIDTH, :]) + _dot(ob_ref[...], w_ref[GDN_WIDTH:, :])
    y = x_ref[...] + mix
    ms = jnp.mean(y * y, axis=-1, keepdims=True)
    out_ref[...] = y * lax.rsqrt(ms + NORM_EPS) * fw_ref[...]


def _outproj(o_a, o_b, xf, w_out, final_norm_w):
    t = xf.shape[0]
    return pl.pallas_call(
        _outproj_kernel,
        grid=(t // TM_OUT,),
        in_specs=[
            pl.BlockSpec((TM_OUT, GDN_WIDTH), lambda i: (i, 0)),
            pl.BlockSpec((TM_OUT, DIFF_WIDTH), lambda i: (i, 0)),
            pl.BlockSpec((TM_OUT, D_MODEL), lambda i: (i, 0)),
            pl.BlockSpec((D_MODEL, D_MODEL), lambda i: (0, 0)),
            pl.BlockSpec((1, D_MODEL), lambda i: (0, 0)),
        ],
        out_specs=pl.BlockSpec((TM_OUT, D_MODEL), lambda i: (i, 0)),
        out_shape=jax.ShapeDtypeStruct((t, D_MODEL), F32),
        compiler_params=pltpu.CompilerParams(
            dimension_semantics=("arbitrary",), vmem_limit_bytes=VMEM_LIMIT),
        name="outproj",
    )(o_a, o_b, xf, w_out, final_norm_w)


def kernel(x, norm_w, w_in, conv_w, a_log, dt_bias, gdn_norm_w, lambda_q1, lambda_k1,
           lambda_q2, lambda_k2, diff_norm_w, w_out, final_norm_w):
    batch, seq_len, d_model = x.shape
    depth = norm_w.shape[0]
    assert depth == 1 and d_model == D_MODEL
    assert seq_len % TM_IN == 0 and seq_len % SC == 0 and seq_len % BQ == 0 and BQ % BK == 0
    n_wide_a = 4 * GDN_WIDTH
    xf = x.reshape(batch * seq_len, d_model)

    w = w_in[0]
    c_dq = n_wide_a + N_GATE
    c_dv = c_dq + 2 * DIFF_WIDTH
    c_dz = c_dv + DIFF_WIDTH
    w_main = jnp.concatenate([w[:, :n_wide_a], w[:, c_dq:c_dv], w[:, c_dz:]], axis=1).astype(BF16)
    w_v_t = w[:, c_dv:c_dz].T.astype(BF16)
    w_gate_t = w[:, n_wide_a:n_wide_a + N_GATE].T.astype(BF16)

    act, gate_t, v_t = _inproj(xf, norm_w[0][None, :], w_main, w_gate_t, w_v_t, conv_w[0], seq_len)

    pad_s = lambda vec: jnp.pad(vec, (GDN_HEADS, 0))[:, None]
    o_a = _gdn(act, gate_t, pad_s(a_log[0]), pad_s(dt_bias[0]), gdn_norm_w[0][None, :],
               batch, seq_len)

    lam_init = 0.8 - 0.6 * math.exp(-0.3 * 0)
    o_b = _attn(act, v_t, lambda_q1[0][None, :], lambda_k1[0][None, :], lambda_q2[0][None, :],
                lambda_k2[0][None, :], diff_norm_w[0][None, :], lam_init, batch, seq_len)

    out = _outproj(o_a, o_b, xf, w_out[0].astype(BF16), final_norm_w[None, :])
    return out.reshape(batch, seq_len, d_model)
```

```python
import functools
import math

import jax
import jax.numpy as jnp
from jax import lax
from jax.experimental import pallas as pl
from jax.experimental.pallas import tpu as pltpu

F32 = jnp.float32
BF16 = jnp.bfloat16

D_MODEL = 1024
GDN_HEADS = 4
HEAD_DIM = 128
GDN_WIDTH = GDN_HEADS * HEAD_DIM
DIFF_HEADS = 4
DIFF_QK_DIM = 64
DIFF_WIDTH = DIFF_HEADS * HEAD_DIM
CONV_K = 4
NORM_EPS = 1e-6
N_GATE = 2 * GDN_HEADS
N_MAIN = 4 * GDN_WIDTH + 3 * DIFF_WIDTH
N_CONV = 3 * GDN_WIDTH
LANES = 128
MASK_VALUE = -1e30

COL_GQ, COL_GK, COL_GV, COL_GZ = 0, 4, 8, 12
COL_DQ, COL_DK, COL_DZ = 16, 20, 24

TM_IN = 1024
NC_IN = 256
TAIL = 16
TM_OUT = 1024
SC = 512
CHUNK = 64
GDN_GROUP = 1
BQ = 512
BK = 256
V_PAD = 16
VMEM_LIMIT = 48 * 1024 * 1024


def _sigmoid(x):
    return 1.0 / (1.0 + jnp.exp(-x))


def _silu(x):
    half = 0.5 * x
    return half + half * jnp.tanh(half)


def _softplus(x):
    return jnp.maximum(x, 0.0) + jnp.log1p(jnp.exp(-jnp.abs(x)))


def _dot(a, b):
    return jnp.dot(a, b, preferred_element_type=F32)


def _dot_nt(a, b):
    return lax.dot_general(a, b, (((1,), (1,)), ((), ())), preferred_element_type=F32)


def _dot_tn(a, b):
    return lax.dot_general(a, b, (((0,), (0,)), ((), ())), preferred_element_type=F32)


def _inproj_kernel(tiles_per_seq, x_ref, nw_ref, w_ref, wgt_ref, wvt_ref, cw_ref,
                   main_ref, gate_t_ref, vt_ref, tail_ref):
    i = pl.program_id(0)
    x = x_ref[...]
    ms = jnp.mean(x * x, axis=-1, keepdims=True)
    h = (x * lax.rsqrt(ms + NORM_EPS) * nw_ref[...]).astype(BF16)

    gate_t_ref[...] = _dot_nt(wgt_ref[...], h)
    vt_ref[...] = _dot_nt(wvt_ref[...], h).astype(BF16)

    @pl.when(i % tiles_per_seq == 0)
    def _():
        tail_ref[...] = jnp.zeros_like(tail_ref)

    tm = x.shape[0]
    row = lax.broadcasted_iota(jnp.int32, (TAIL, LANES), 0)

    def conv_silu(acc, n0):
        cols = slice(n0, n0 + LANES)
        last = acc[tm - TAIL:, :]
        delta = tail_ref[:, cols] - last
        tail_ref[:, cols] = last
        y = acc * cw_ref[CONV_K - 1:CONV_K, cols]
        fix = jnp.zeros((TAIL, LANES), F32)
        for s in range(1, CONV_K):
            wk = cw_ref[CONV_K - 1 - s:CONV_K - s, cols]
            y = y + pltpu.roll(acc, s, axis=0) * wk
            fix = fix + jnp.where(row < s, pltpu.roll(delta, s, axis=0), 0.0) * wk
        main_ref[:, cols] = _silu(y).astype(BF16)
        main_ref[0:TAIL, cols] = _silu(y[:TAIL, :] + fix).astype(BF16)

    plain = [slice(n0, n0 + NC_IN) for n0 in range(N_CONV, N_MAIN, NC_IN)]
    pending = []
    for n0 in range(0, N_CONV, NC_IN):
        acc = _dot(h, w_ref[:, n0:n0 + NC_IN])
        if pending:
            conv_silu(*pending.pop(0))
        if plain:
            pc = plain.pop(0)
            main_ref[:, pc] = _dot(h, w_ref[:, pc]).astype(BF16)
        if pending:
            conv_silu(*pending.pop(0))
        pending = [(acc[:, j:j + LANES], n0 + j) for j in range(0, NC_IN, LANES)]
    for piece in pending:
        if plain:
            pc = plain.pop(0)
            main_ref[:, pc] = _dot(h, w_ref[:, pc]).astype(BF16)
        conv_silu(*piece)
    for pc in plain:
        main_ref[:, pc] = _dot(h, w_ref[:, pc]).astype(BF16)


def _inproj(xf, norm_w, w_main, w_gate_t, w_v_t, conv_w, seq_len):
    t = xf.shape[0]
    tiles_per_seq = seq_len // TM_IN
    return pl.pallas_call(
        functools.partial(_inproj_kernel, tiles_per_seq),
        grid=(t // TM_IN,),
        in_specs=[
            pl.BlockSpec((TM_IN, D_MODEL), lambda i: (i, 0)),
            pl.BlockSpec((1, D_MODEL), lambda i: (0, 0)),
            pl.BlockSpec((D_MODEL, N_MAIN), lambda i: (0, 0)),
            pl.BlockSpec((N_GATE, D_MODEL), lambda i: (0, 0)),
            pl.BlockSpec((DIFF_WIDTH, D_MODEL), lambda i: (0, 0)),
            pl.BlockSpec((CONV_K, N_CONV), lambda i: (0, 0)),
        ],
        out_specs=[
            pl.BlockSpec((TM_IN, N_MAIN), lambda i: (i, 0)),
            pl.BlockSpec((N_GATE, TM_IN), lambda i: (0, i)),
            pl.BlockSpec((DIFF_WIDTH, TM_IN), lambda i: (0, i)),
        ],
        out_shape=[
            jax.ShapeDtypeStruct((t, N_MAIN), BF16),
            jax.ShapeDtypeStruct((N_GATE, t), F32),
            jax.ShapeDtypeStruct((DIFF_WIDTH, t), BF16),
        ],
        scratch_shapes=[pltpu.VMEM((TAIL, N_CONV), F32)],
        compiler_params=pltpu.CompilerParams(
            dimension_semantics=("arbitrary",), vmem_limit_bytes=VMEM_LIMIT),
        name="inproj",
    )(xf, norm_w, w_main, w_gate_t, w_v_t, conv_w)


def _gdn_kernel(act_ref, gate_t_ref, alog_s_ref, dtb_s_ref, gnw_ref, o_ref, mask_ref, bd_ref,
                row_ref, col_ref):
    seq_len = act_ref.shape[0]
    n_chunks = SC // CHUNK
    heads = range(GDN_HEADS)
    i_cat = lax.broadcasted_iota(jnp.int32, (CHUNK, SC), 0)
    j_cat = lax.broadcasted_iota(jnp.int32, (CHUNK, SC), 1) & (CHUNK - 1)
    mask_ref[0] = jnp.where(i_cat >= j_cat, 0.0, MASK_VALUE)
    mask_ref[1] = jnp.where(i_cat > j_cat, -1.0, 0.0)
    mask_ref[2] = jnp.where(i_cat == j_cat, 1.0, 0.0)
    r = lax.broadcasted_iota(jnp.int32, (SC, SC), 0)
    c = lax.broadcasted_iota(jnp.int32, (SC, SC), 1)
    bd_ref[...] = jnp.where((r & -CHUNK) == (c & -CHUNK), 1.0, 0.0).astype(BF16)
    lane_chunk = lax.broadcasted_iota(jnp.int32, (1, SC), 1) & -CHUNK

    scale = HEAD_DIM ** -0.5

    lane_in_chunk = lax.broadcasted_iota(jnp.int32, (N_GATE, seq_len), 1) & (CHUNK - 1)
    gate_t = gate_t_ref[...]
    beta = _sigmoid(gate_t)
    log_beta = -_softplus(-gate_t)
    g_step = -jnp.exp(alog_s_ref[...]) * _softplus(gate_t + dtb_s_ref[...])
    gc = g_step
    g_after = jnp.zeros_like(g_step)
    tail = g_step
    step = 1
    while step < CHUNK:
        gc = gc + jnp.where(lane_in_chunk >= step, pltpu.roll(gc, step, axis=1), 0.0)
        ahead = jnp.where(lane_in_chunk + step < CHUNK, pltpu.roll(tail, seq_len - step, axis=1), 0.0)
        g_after = g_after + ahead
        tail = tail + ahead
        step *= 2
    e_gc = jnp.exp(gc)
    is_beta_row = lax.broadcasted_iota(jnp.int32, (N_GATE, seq_len), 0) < GDN_HEADS
    swap = pltpu.roll(e_gc, GDN_HEADS, axis=0)
    row_ref[0] = gc
    row_ref[1] = e_gc
    col_ref[...] = jnp.concatenate([
        jnp.where(is_beta_row, beta, gc),
        jnp.where(is_beta_row, log_beta, e_gc),
        jnp.where(is_beta_row, beta * swap, jnp.exp(g_after))], axis=0).T

    def cat(x):
        out = x[(n_chunks - 1) * CHUNK:, :]
        for ci in range(n_chunks - 2, -1, -1):
            out = jnp.where(lane_chunk == ci * CHUNK, x[ci * CHUNK:(ci + 1) * CHUNK, :], out)
        return out

    def block_diag(x_cat):
        xb = x_cat.astype(BF16)
        zero = jnp.zeros((CHUNK, LANES), BF16)
        row_blocks = []
        for ci in range(n_chunks):
            g = ci * CHUNK // LANES
            grp = slice(g * LANES, (g + 1) * LANES)
            part = xb[:, grp] * bd_ref[ci * CHUNK:(ci + 1) * CHUNK, grp]
            row_blocks.append(jnp.concatenate(
                [part if j == g else zero for j in range(SC // LANES)], axis=1))
        return jnp.concatenate(row_blocks, axis=0)

    def head_cols(col, h):
        return slice((col + h) * HEAD_DIM, (col + h + 1) * HEAD_DIM)

    def prepare(steps):
        causal, neg_strict, eye = mask_ref[0], mask_ref[1], mask_ref[2]
        units = [(sc, h) for sc in steps for h in heads]
        n_pow, a_cat, rhs, q_dec, k_dec, decay_last = [], [], [], [], [], []
        for sc, h in units:
            rows = slice(sc * SC, (sc + 1) * SC)
            col = lambda j: col_ref[rows, j:j + 1]
            q = act_ref[rows, head_cols(COL_GQ, h)].astype(F32)
            k = act_ref[rows, head_cols(COL_GK, h)].astype(F32)
            v = act_ref[rows, head_cols(COL_GV, h)].astype(F32)
            q = q * (lax.rsqrt(jnp.sum(q * q, axis=-1, keepdims=True) + 1e-6) * scale)
            k = k * lax.rsqrt(jnp.sum(k * k, axis=-1, keepdims=True) + 1e-6)
            hb, hg = h, GDN_HEADS + h
            kb = k.astype(BF16)
            gram = _dot_nt(jnp.concatenate([kb, q.astype(BF16)], axis=0), kb)
            e_cat = cat(col(hg)) - row_ref[0, hg:hg + 1, rows] + causal
            n_pow.append(cat(gram[:SC]) * jnp.exp(e_cat + cat(col(N_GATE + hb))) * neg_strict)
            a_cat.append(cat(gram[SC:]) * jnp.exp(e_cat))
            rhs.append(jnp.concatenate([col(hb) * v, col(2 * N_GATE + hb) * k], axis=1).astype(BF16))
            q_dec.append(q * col(N_GATE + hg))
            kd = (k * col(2 * N_GATE + hg)).astype(BF16)
            k_dec.append([kd[ci * CHUNK:(ci + 1) * CHUNK, :] for ci in range(n_chunks)])
            decay_last.append([row_ref[1, hg:hg + 1, sc * SC + (ci + 1) * CHUNK - 1:sc * SC + (ci + 1) * CHUNK]
                               for ci in range(n_chunks)])

        every = range(len(units))
        t_cat = [eye + n_pow[u] for u in every]
        n_pow = [_dot(n_pow[u].astype(BF16), block_diag(n_pow[u])) for u in every]
        span = 2
        while span < CHUNK // 2:
            both = [_dot(jnp.concatenate([t_cat[u], n_pow[u]], axis=0).astype(BF16), block_diag(n_pow[u]))
                    for u in every]
            t_cat = [t_cat[u] + both[u][:CHUNK] for u in every]
            n_pow = [both[u][CHUNK:] for u in every]
            span *= 2
        t_cat = [t_cat[u] + _dot(t_cat[u].astype(BF16), block_diag(n_pow[u])) for u in every]

        uw = [_dot(block_diag(t_cat[u]), rhs[u]) for u in every]
        kt = [[_dot_tn(k_dec[u][ci], uw[u][ci * CHUNK:(ci + 1) * CHUNK, :].astype(BF16))
               for ci in range(n_chunks)] for u in every]
        per_step = lambda xs, i: xs[i * GDN_HEADS:(i + 1) * GDN_HEADS]
        return [tuple(per_step(xs, i) for xs in (uw, kt, q_dec, a_cat, decay_last))
                for i in range(len(steps))]

    def finish(sc, prep, state):
        rows = slice(sc * SC, (sc + 1) * SC)
        uw, kt, q_dec, a_cat, decay_last = prep
        state = list(state)
        state_in = [[None] * n_chunks for _ in heads]
        for ci in range(n_chunks):
            for h in heads:
                sb = state[h].astype(BF16)
                state_in[h][ci] = sb
                state[h] = (state[h] * decay_last[h][ci]
                            - _dot(kt[h][ci][:, HEAD_DIM:].astype(BF16), sb) + kt[h][ci][:, :HEAD_DIM])
        for h in heads:
            inter = []
            for ci in range(n_chunks):
                cr = slice(ci * CHUNK, (ci + 1) * CHUNK)
                lhs = jnp.concatenate([uw[h][cr, HEAD_DIM:], q_dec[h][cr, :]], axis=0).astype(BF16)
                inter.append(_dot(lhs, state_in[h][ci]))
            v_new = jnp.concatenate([uw[h][ci * CHUNK:(ci + 1) * CHUNK, :HEAD_DIM] - inter[ci][:CHUNK]
                                     for ci in range(n_chunks)], axis=0).astype(BF16)
            o = (jnp.concatenate([inter[ci][CHUNK:] for ci in range(n_chunks)], axis=0)
                 + _dot(block_diag(a_cat[h]), v_new))
            o = o * lax.rsqrt(jnp.mean(o * o, axis=-1, keepdims=True) + NORM_EPS)
            z = act_ref[rows, head_cols(COL_GZ, h)].astype(F32)
            o_ref[rows, head_cols(0, h)] = (o * (_silu(z) * gnw_ref[...])).astype(BF16)
        return state

    n_groups = seq_len // (SC * GDN_GROUP)
    group = lambda g: list(range(g * GDN_GROUP, (g + 1) * GDN_GROUP))
    state = [jnp.zeros((HEAD_DIM, HEAD_DIM), F32) for _ in heads]
    preps = prepare(group(0))
    for g in range(n_groups):
        nxt = prepare(group(g + 1)) if g + 1 < n_groups else None
        for sc, prep in zip(group(g), preps):
            state = finish(sc, prep, state)
        preps = nxt


def _gdn(act, gate_t, alog_s, dtb_s, gdn_norm_w, batch, seq_len):
    t = act.shape[0]
    small = lambda shape: pl.BlockSpec(shape, lambda b: (0, 0))
    return pl.pallas_call(
        _gdn_kernel,
        grid=(batch,),
        in_specs=[
            pl.BlockSpec((seq_len, 4 * GDN_WIDTH), lambda b: (b, 0)),
            pl.BlockSpec((N_GATE, seq_len), lambda b: (0, b)),
            small((N_GATE, 1)), small((N_GATE, 1)),
            small((1, HEAD_DIM)),
        ],
        out_specs=pl.BlockSpec((seq_len, GDN_WIDTH), lambda b: (b, 0)),
        out_shape=jax.ShapeDtypeStruct((t, GDN_WIDTH), BF16),
        scratch_shapes=[
            pltpu.VMEM((3, CHUNK, SC), F32),
            pltpu.VMEM((SC, SC), BF16),
            pltpu.VMEM((2, N_GATE, seq_len), F32),
            pltpu.VMEM((seq_len, 3 * N_GATE), F32),
        ],
        compiler_params=pltpu.CompilerParams(
            dimension_semantics=("arbitrary",), vmem_limit_bytes=VMEM_LIMIT),
        name="gdn",
    )(act, gate_t, alog_s, dtb_s, gdn_norm_w)


def _split3(x):
    hi = x.astype(BF16).astype(F32)
    r1 = x - hi
    mid = r1.astype(BF16).astype(F32)
    return hi, mid, r1 - mid


def _attn_kernel(lam_init, q_ref, k_ref, vt_ref, z_ref, lq1_ref, lk1_ref, lq2_ref, lk2_ref,
                 nw_ref, o_ref, bias_ref, kext_ref, vtx_ref):
    h = pl.program_id(0)
    b = pl.program_id(1)
    seq_len = k_ref.shape[0]
    lane = lax.broadcasted_iota(jnp.int32, (1, HEAD_DIM), 1)
    n_bias = 6
    log2e = math.log2(math.e)

    @pl.when(b == 0)
    def _():
        slope = lax.shift_left(jnp.int32(1), 2 * (DIFF_HEADS - 1 - h)).astype(F32) * (log2e / 256.0)
        pos = lax.broadcasted_iota(jnp.int32, (seq_len, 1), 0)
        in_block = (pos & (BK - 1)).astype(F32) * slope
        block_off = (pos & -BK).astype(F32) * slope
        terms = _split3(in_block) + _split3(block_off)
        tile = jnp.zeros((seq_len, HEAD_DIM), F32)
        for j, term in enumerate(terms):
            tile = jnp.where((lane == j) | (lane == DIFF_QK_DIM + j), term, tile)
        bias_ref[...] = tile
        ones_row = lax.broadcasted_iota(jnp.int32, (V_PAD, seq_len), 0) == 0
        vtx_ref[HEAD_DIM:, :] = jnp.where(ones_row, 1.0, 0.0).astype(BF16)

    k = k_ref[...].astype(F32)
    kext_ref[0] = jnp.where(lane < DIFF_QK_DIM, k, bias_ref[...]).astype(BF16)
    kext_ref[1] = jnp.where(lane >= DIFF_QK_DIM, k, bias_ref[...]).astype(BF16)
    vtx_ref[0:HEAD_DIM, :] = vt_ref[...]

    lam = (jnp.exp(jnp.sum(lq1_ref[...] * lk1_ref[...], axis=-1, keepdims=True))
           - jnp.exp(jnp.sum(lq2_ref[...] * lk2_ref[...], axis=-1, keepdims=True)) + lam_init)

    def block_scores(qi):
        q = q_ref[qi * BQ:(qi + 1) * BQ, :].astype(F32) * (DIFF_QK_DIM ** -0.5 * log2e)
        q_ext = (
            jnp.where(lane < DIFF_QK_DIM, q,
                      jnp.where(lane < DIFF_QK_DIM + n_bias, 1.0, 0.0)).astype(BF16),
            jnp.where(lane >= DIFF_QK_DIM, q, jnp.where(lane < n_bias, 1.0, 0.0)).astype(BF16),
        )
        n_keys = (qi + 1) * BQ
        krel = lax.broadcasted_iota(jnp.int32, (BQ, BQ), 0)
        qrel = lax.broadcasted_iota(jnp.int32, (BQ, BQ), 1)
        out = []
        for c in range(2):
            s = _dot_nt(kext_ref[c, 0:n_keys, :], q_ext[c])
            diag = jnp.where(krel <= qrel, s[n_keys - BQ:, :], MASK_VALUE)
            s = jnp.concatenate([s[:n_keys - BQ, :], diag], axis=0) if qi else diag
            out.append((s, jnp.max(s, axis=0, keepdims=True)))
        return out

    n_q = seq_len // BQ
    cur = block_scores(0)
    for qi in range(n_q):
        nxt = block_scores(qi + 1) if qi + 1 < n_q else None
        vt = vtx_ref[:, 0:(qi + 1) * BQ]
        maps = []
        for c in range(2):
            s, s_max = cur[c]
            acc = _dot(vt, jnp.exp2(s - s_max).astype(BF16))
            maps.append(acc[0:HEAD_DIM, :] / acc[HEAD_DIM:HEAD_DIM + 1, :])
        o = (maps[0] - lam * maps[1]).T
        o = o * lax.rsqrt(jnp.mean(o * o, axis=-1, keepdims=True) + NORM_EPS)
        o = o * nw_ref[...] * (1.0 - lam_init)
        qrows = slice(qi * BQ, (qi + 1) * BQ)
        o_ref[qrows, :] = (o * _silu(z_ref[qrows, :].astype(F32))).astype(BF16)
        cur = nxt


def _attn(act, v_t, lq1, lk1, lq2, lk2, diff_norm_w, lam_init, batch, seq_len):
    t = act.shape[0]
    small = lambda shape: pl.BlockSpec(shape, lambda h, b: (0, 0))
    return pl.pallas_call(
        functools.partial(_attn_kernel, lam_init),
        grid=(DIFF_HEADS, batch),
        in_specs=[
            pl.BlockSpec((seq_len, HEAD_DIM), lambda h, b: (b, COL_DQ + h)),
            pl.BlockSpec((seq_len, HEAD_DIM), lambda h, b: (b, COL_DK + h)),
            pl.BlockSpec((HEAD_DIM, seq_len), lambda h, b: (h, b)),
            pl.BlockSpec((seq_len, HEAD_DIM), lambda h, b: (b, COL_DZ + h)),
            small((1, DIFF_QK_DIM)), small((1, DIFF_QK_DIM)),
            small((1, DIFF_QK_DIM)), small((1, DIFF_QK_DIM)),
            small((1, HEAD_DIM)),
        ],
        out_specs=pl.BlockSpec((seq_len, HEAD_DIM), lambda h, b: (b, h)),
        out_shape=jax.ShapeDtypeStruct((t, DIFF_WIDTH), BF16),
        scratch_shapes=[
            pltpu.VMEM((seq_len, HEAD_DIM), F32),
            pltpu.VMEM((2, seq_len, HEAD_DIM), BF16),
            pltpu.VMEM((HEAD_DIM + V_PAD, seq_len), BF16),
        ],
        compiler_params=pltpu.CompilerParams(
            dimension_semantics=("arbitrary", "arbitrary"), vmem_limit_bytes=VMEM_LIMIT),
        name="diffattn",
    )(act, act, v_t, act, lq1, lk1, lq2, lk2, diff_norm_w)


def _outproj_kernel(oa_ref, ob_ref, x_ref, w_ref, fw_ref, out_ref):
    mix = _dot(oa_ref[...], w_ref[0:GDN_WIDTH, :]) + _dot(ob_ref[...], w_ref[GDN_WIDTH:, :])
    y = x_ref[...] + mix
    ms = jnp.mean(y * y, axis=-1, keepdims=True)
    out_ref[...] = y * lax.rsqrt(ms + NORM_EPS) * fw_ref[...]


def _outproj(o_a, o_b, xf, w_out, final_norm_w):
    t = xf.shape[0]
    return pl.pallas_call(
        _outproj_kernel,
        grid=(t // TM_OUT,),
        in_specs=[
            pl.BlockSpec((TM_OUT, GDN_WIDTH), lambda i: (i, 0)),
            pl.BlockSpec((TM_OUT, DIFF_WIDTH), lambda i: (i, 0)),
            pl.BlockSpec((TM_OUT, D_MODEL), lambda i: (i, 0)),
            pl.BlockSpec((D_MODEL, D_MODEL), lambda i: (0, 0)),
            pl.BlockSpec((1, D_MODEL), lambda i: (0, 0)),
        ],
        out_specs=pl.BlockSpec((TM_OUT, D_MODEL), lambda i: (i, 0)),
        out_shape=jax.ShapeDtypeStruct((t, D_MODEL), F32),
        compiler_params=pltpu.CompilerParams(
            dimension_semantics=("arbitrary",), vmem_limit_bytes=VMEM_LIMIT),
        name="outproj",
    )(o_a, o_b, xf, w_out, final_norm_w)


def kernel(x, norm_w, w_in, conv_w, a_log, dt_bias, gdn_norm_w, lambda_q1, lambda_k1,
           lambda_q2, lambda_k2, diff_norm_w, w_out, final_norm_w):
    batch, seq_len, d_model = x.shape
    depth = norm_w.shape[0]
    assert depth == 1 and d_model == D_MODEL
    assert seq_len % TM_IN == 0 and seq_len % SC == 0 and seq_len % BQ == 0 and BQ % BK == 0
    n_wide_a = 4 * GDN_WIDTH
    xf = x.reshape(batch * seq_len, d_model)

    w = w_in[0]
    c_dq = n_wide_a + N_GATE
    c_dv = c_dq + 2 * DIFF_WIDTH
    c_dz = c_dv + DIFF_WIDTH
    w_main = jnp.concatenate([w[:, :n_wide_a], w[:, c_dq:c_dv], w[:, c_dz:]], axis=1).astype(BF16)
    w_v_t = w[:, c_dv:c_dz].T.astype(BF16)
    w_gate_t = w[:, n_wide_a:n_wide_a + N_GATE].T.astype(BF16)

    act, gate_t, v_t = _inproj(xf, norm_w[0][None, :], w_main, w_gate_t, w_v_t, conv_w[0], seq_len)

    pad_s = lambda vec: jnp.pad(vec, (GDN_HEADS, 0))[:, None]
    o_a = _gdn(act, gate_t, pad_s(a_log[0]), pad_s(dt_bias[0]), gdn_norm_w[0][None, :],
               batch, seq_len)

    lam_init = 0.8 - 0.6 * math.exp(-0.3 * 0)
    o_b = _attn(act, v_t, lambda_q1[0][None, :], lambda_k1[0][None, :], lambda_q2[0][None, :],
                lambda_k2[0][None, :], diff_norm_w[0][None, :], lam_init, batch, seq_len)

    out = _outproj(o_a, o_b, xf, w_out[0].astype(BF16), final_norm_w[None, :])
    return out.reshape(batch, seq_len, d_model)
```

```python
import functools
import math

import jax
import jax.numpy as jnp
from jax import lax
from jax.experimental import pallas as pl
from jax.experimental.pallas import tpu as pltpu

F32 = jnp.float32
BF16 = jnp.bfloat16

D_MODEL = 1024
GDN_HEADS = 4
HEAD_DIM = 128
GDN_WIDTH = GDN_HEADS * HEAD_DIM
DIFF_HEADS = 4
DIFF_QK_DIM = 64
DIFF_WIDTH = DIFF_HEADS * HEAD_DIM
CONV_K = 4
NORM_EPS = 1e-6
N_GATE = 2 * GDN_HEADS
N_MAIN = 4 * GDN_WIDTH + 3 * DIFF_WIDTH
N_CONV = 3 * GDN_WIDTH
LANES = 128
MASK_VALUE = -1e30

COL_GQ, COL_GK, COL_GV, COL_GZ = 0, 4, 8, 12
COL_DQ, COL_DK, COL_DZ = 16, 20, 24

TM_IN = 1024
NC_IN = 256
TAIL = 16
TM_OUT = 1024
SC = 256
CHUNK = 64
GDN_GROUP = 2
BQ = 512
BK = 256
V_PAD = 16
VMEM_LIMIT = 48 * 1024 * 1024


def _sigmoid(x):
    return 1.0 / (1.0 + jnp.exp(-x))


def _silu(x):
    half = 0.5 * x
    return half + half * jnp.tanh(half)


def _softplus(x):
    return jnp.maximum(x, 0.0) + jnp.log1p(jnp.exp(-jnp.abs(x)))


def _dot(a, b):
    return jnp.dot(a, b, preferred_element_type=F32)


def _dot_nt(a, b):
    return lax.dot_general(a, b, (((1,), (1,)), ((), ())), preferred_element_type=F32)


def _dot_tn(a, b):
    return lax.dot_general(a, b, (((0,), (0,)), ((), ())), preferred_element_type=F32)


def _inproj_kernel(tiles_per_seq, x_ref, nw_ref, w_ref, wgt_ref, wvt_ref, cw_ref,
                   main_ref, gate_t_ref, vt_ref, tail_ref):
    i = pl.program_id(0)
    x = x_ref[...]
    ms = jnp.mean(x * x, axis=-1, keepdims=True)
    h = (x * lax.rsqrt(ms + NORM_EPS) * nw_ref[...]).astype(BF16)

    gate_t_ref[...] = _dot_nt(wgt_ref[...], h)
    vt_ref[...] = _dot_nt(wvt_ref[...], h).astype(BF16)

    @pl.when(i % tiles_per_seq == 0)
    def _():
        tail_ref[...] = jnp.zeros_like(tail_ref)

    tm = x.shape[0]
    row = lax.broadcasted_iota(jnp.int32, (TAIL, LANES), 0)

    def conv_silu(acc, n0):
        cols = slice(n0, n0 + LANES)
        last = acc[tm - TAIL:, :]
        delta = tail_ref[:, cols] - last
        tail_ref[:, cols] = last
        y = acc * cw_ref[CONV_K - 1:CONV_K, cols]
        fix = jnp.zeros((TAIL, LANES), F32)
        for s in range(1, CONV_K):
            wk = cw_ref[CONV_K - 1 - s:CONV_K - s, cols]
            y = y + pltpu.roll(acc, s, axis=0) * wk
            fix = fix + jnp.where(row < s, pltpu.roll(delta, s, axis=0), 0.0) * wk
        main_ref[:, cols] = _silu(y).astype(BF16)
        main_ref[0:TAIL, cols] = _silu(y[:TAIL, :] + fix).astype(BF16)

    plain = [slice(n0, n0 + NC_IN) for n0 in range(N_CONV, N_MAIN, NC_IN)]
    pending = []
    for n0 in range(0, N_CONV, NC_IN):
        acc = _dot(h, w_ref[:, n0:n0 + NC_IN])
        if pending:
            conv_silu(*pending.pop(0))
        if plain:
            pc = plain.pop(0)
            main_ref[:, pc] = _dot(h, w_ref[:, pc]).astype(BF16)
        if pending:
            conv_silu(*pending.pop(0))
        pending = [(acc[:, j:j + LANES], n0 + j) for j in range(0, NC_IN, LANES)]
    for piece in pending:
        if plain:
            pc = plain.pop(0)
            main_ref[:, pc] = _dot(h, w_ref[:, pc]).astype(BF16)
        conv_silu(*piece)
    for pc in plain:
        main_ref[:, pc] = _dot(h, w_ref[:, pc]).astype(BF16)


def _inproj(xf, norm_w, w_main, w_gate_t, w_v_t, conv_w, seq_len):
    t = xf.shape[0]
    tiles_per_seq = seq_len // TM_IN
    return pl.pallas_call(
        functools.partial(_inproj_kernel, tiles_per_seq),
        grid=(t // TM_IN,),
        in_specs=[
            pl.BlockSpec((TM_IN, D_MODEL), lambda i: (i, 0)),
            pl.BlockSpec((1, D_MODEL), lambda i: (0, 0)),
            pl.BlockSpec((D_MODEL, N_MAIN), lambda i: (0, 0)),
            pl.BlockSpec((N_GATE, D_MODEL), lambda i: (0, 0)),
            pl.BlockSpec((DIFF_WIDTH, D_MODEL), lambda i: (0, 0)),
            pl.BlockSpec((CONV_K, N_CONV), lambda i: (0, 0)),
        ],
        out_specs=[
            pl.BlockSpec((TM_IN, N_MAIN), lambda i: (i, 0)),
            pl.BlockSpec((N_GATE, TM_IN), lambda i: (0, i)),
            pl.BlockSpec((DIFF_WIDTH, TM_IN), lambda i: (0, i)),
        ],
        out_shape=[
            jax.ShapeDtypeStruct((t, N_MAIN), BF16),
            jax.ShapeDtypeStruct((N_GATE, t), F32),
            jax.ShapeDtypeStruct((DIFF_WIDTH, t), BF16),
        ],
        scratch_shapes=[pltpu.VMEM((TAIL, N_CONV), F32)],
        compiler_params=pltpu.CompilerParams(
            dimension_semantics=("arbitrary",), vmem_limit_bytes=VMEM_LIMIT),
        name="inproj",
    )(xf, norm_w, w_main, w_gate_t, w_v_t, conv_w)


def _gdn_kernel(act_ref, gate_t_ref, alog_s_ref, dtb_s_ref, gnw_ref, o_ref, mask_ref, bd_ref,
                row_ref, col_ref):
    seq_len = act_ref.shape[0]
    n_chunks = SC // CHUNK
    heads = range(GDN_HEADS)
    i_cat = lax.broadcasted_iota(jnp.int32, (CHUNK, SC), 0)
    j_cat = lax.broadcasted_iota(jnp.int32, (CHUNK, SC), 1) & (CHUNK - 1)
    mask_ref[0] = jnp.where(i_cat >= j_cat, 0.0, MASK_VALUE)
    mask_ref[1] = jnp.where(i_cat > j_cat, -1.0, 0.0)
    mask_ref[2] = jnp.where(i_cat == j_cat, 1.0, 0.0)
    r = lax.broadcasted_iota(jnp.int32, (SC, SC), 0)
    c = lax.broadcasted_iota(jnp.int32, (SC, SC), 1)
    bd_ref[...] = jnp.where((r & -CHUNK) == (c & -CHUNK), 1.0, 0.0).astype(BF16)
    lane_chunk = lax.broadcasted_iota(jnp.int32, (1, SC), 1) & -CHUNK

    scale = HEAD_DIM ** -0.5

    lane_in_chunk = lax.broadcasted_iota(jnp.int32, (N_GATE, seq_len), 1) & (CHUNK - 1)
    gate_t = gate_t_ref[...]
    beta = _sigmoid(gate_t)
    log_beta = -_softplus(-gate_t)
    g_step = -jnp.exp(alog_s_ref[...]) * _softplus(gate_t + dtb_s_ref[...])
    gc = g_step
    g_after = jnp.zeros_like(g_step)
    tail = g_step
    step = 1
    while step < CHUNK:
        gc = gc + jnp.where(lane_in_chunk >= step, pltpu.roll(gc, step, axis=1), 0.0)
        ahead = jnp.where(lane_in_chunk + step < CHUNK, pltpu.roll(tail, seq_len - step, axis=1), 0.0)
        g_after = g_after + ahead
        tail = tail + ahead
        step *= 2
    e_gc = jnp.exp(gc)
    is_beta_row = lax.broadcasted_iota(jnp.int32, (N_GATE, seq_len), 0) < GDN_HEADS
    swap = pltpu.roll(e_gc, GDN_HEADS, axis=0)
    row_ref[0] = gc
    row_ref[1] = e_gc
    col_ref[...] = jnp.concatenate([
        jnp.where(is_beta_row, beta, gc),
        jnp.where(is_beta_row, log_beta, e_gc),
        jnp.where(is_beta_row, beta * swap, jnp.exp(g_after))], axis=0).T

    def cat(x):
        out = x[(n_chunks - 1) * CHUNK:, :]
        for ci in range(n_chunks - 2, -1, -1):
            out = jnp.where(lane_chunk == ci * CHUNK, x[ci * CHUNK:(ci + 1) * CHUNK, :], out)
        return out

    def block_diag(x_cat):
        xb = x_cat.astype(BF16)
        zero = jnp.zeros((CHUNK, LANES), BF16)
        row_blocks = []
        for ci in range(n_chunks):
            g = ci * CHUNK // LANES
            grp = slice(g * LANES, (g + 1) * LANES)
            part = xb[:, grp] * bd_ref[ci * CHUNK:(ci + 1) * CHUNK, grp]
            row_blocks.append(jnp.concatenate(
                [part if j == g else zero for j in range(SC // LANES)], axis=1))
        return jnp.concatenate(row_blocks, axis=0)

    def head_cols(col, h):
        return slice((col + h) * HEAD_DIM, (col + h + 1) * HEAD_DIM)

    def prepare(steps):
        causal, neg_strict, eye = mask_ref[0], mask_ref[1], mask_ref[2]
        units = [(sc, h) for sc in steps for h in heads]
        n_pow, a_cat, rhs, q_dec, k_dec, decay_last = [], [], [], [], [], []
        for sc, h in units:
            rows = slice(sc * SC, (sc + 1) * SC)
            col = lambda j: col_ref[rows, j:j + 1]
            q = act_ref[rows, head_cols(COL_GQ, h)].astype(F32)
            k = act_ref[rows, head_cols(COL_GK, h)].astype(F32)
            v = act_ref[rows, head_cols(COL_GV, h)].astype(F32)
            q = q * (lax.rsqrt(jnp.sum(q * q, axis=-1, keepdims=True) + 1e-6) * scale)
            k = k * lax.rsqrt(jnp.sum(k * k, axis=-1, keepdims=True) + 1e-6)
            hb, hg = h, GDN_HEADS + h
            kb = k.astype(BF16)
            gram = _dot_nt(jnp.concatenate([kb, q.astype(BF16)], axis=0), kb)
            e_cat = cat(col(hg)) - row_ref[0, hg:hg + 1, rows] + causal
            n_pow.append(cat(gram[:SC]) * jnp.exp(e_cat + cat(col(N_GATE + hb))) * neg_strict)
            a_cat.append(cat(gram[SC:]) * jnp.exp(e_cat))
            rhs.append(jnp.concatenate([col(hb) * v, col(2 * N_GATE + hb) * k], axis=1).astype(BF16))
            q_dec.append(q * col(N_GATE + hg))
            kd = (k * col(2 * N_GATE + hg)).astype(BF16)
            k_dec.append([kd[ci * CHUNK:(ci + 1) * CHUNK, :] for ci in range(n_chunks)])
            decay_last.append([row_ref[1, hg:hg + 1, sc * SC + (ci + 1) * CHUNK - 1:sc * SC + (ci + 1) * CHUNK]
                               for ci in range(n_chunks)])

        every = range(len(units))
        t_cat = [eye + n_pow[u] for u in every]
        n_pow = [_dot(n_pow[u].astype(BF16), block_diag(n_pow[u])) for u in every]
        span = 2
        while span < CHUNK // 2:
            both = [_dot(jnp.concatenate([t_cat[u], n_pow[u]], axis=0).astype(BF16), block_diag(n_pow[u]))
                    for u in every]
            t_cat = [t_cat[u] + both[u][:CHUNK] for u in every]
            n_pow = [both[u][CHUNK:] for u in every]
            span *= 2
        t_cat = [t_cat[u] + _dot(t_cat[u].astype(BF16), block_diag(n_pow[u])) for u in every]

        uw = [_dot(block_diag(t_cat[u]), rhs[u]) for u in every]
        kt = [[_dot_tn(k_dec[u][ci], uw[u][ci * CHUNK:(ci + 1) * CHUNK, :].astype(BF16))
               for ci in range(n_chunks)] for u in every]
        per_step = lambda xs, i: xs[i * GDN_HEADS:(i + 1) * GDN_HEADS]
        return [tuple(per_step(xs, i) for xs in (uw, kt, q_dec, a_cat, decay_last))
                for i in range(len(steps))]

    def finish(sc, prep, state):
        rows = slice(sc * SC, (sc + 1) * SC)
        uw, kt, q_dec, a_cat, decay_last = prep
        state = list(state)
        state_in = [[None] * n_chunks for _ in heads]
        for ci in range(n_chunks):
            for h in heads:
                sb = state[h].astype(BF16)
                state_in[h][ci] = sb
                state[h] = (state[h] * decay_last[h][ci]
                            - _dot(kt[h][ci][:, HEAD_DIM:].astype(BF16), sb) + kt[h][ci][:, :HEAD_DIM])
        for h in heads:
            inter = []
            for ci in range(n_chunks):
                cr = slice(ci * CHUNK, (ci + 1) * CHUNK)
                lhs = jnp.concatenate([uw[h][cr, HEAD_DIM:], q_dec[h][cr, :]], axis=0).astype(BF16)
                inter.append(_dot(lhs, state_in[h][ci]))
            v_new = jnp.concatenate([uw[h][ci * CHUNK:(ci + 1) * CHUNK, :HEAD_DIM] - inter[ci][:CHUNK]
                                     for ci in range(n_chunks)], axis=0).astype(BF16)
            o = (jnp.concatenate([inter[ci][CHUNK:] for ci in range(n_chunks)], axis=0)
                 + _dot(block_diag(a_cat[h]), v_new))
            o = o * lax.rsqrt(jnp.mean(o * o, axis=-1, keepdims=True) + NORM_EPS)
            z = act_ref[rows, head_cols(COL_GZ, h)].astype(F32)
            o_ref[rows, head_cols(0, h)] = (o * (_silu(z) * gnw_ref[...])).astype(BF16)
        return state

    n_groups = seq_len // (SC * GDN_GROUP)
    group = lambda g: list(range(g * GDN_GROUP, (g + 1) * GDN_GROUP))
    state = [jnp.zeros((HEAD_DIM, HEAD_DIM), F32) for _ in heads]
    preps = prepare(group(0))
    for g in range(n_groups):
        nxt = prepare(group(g + 1)) if g + 1 < n_groups else None
        for sc, prep in zip(group(g), preps):
            state = finish(sc, prep, state)
        preps = nxt


def _gdn(act, gate_t, alog_s, dtb_s, gdn_norm_w, batch, seq_len):
    t = act.shape[0]
    small = lambda shape: pl.BlockSpec(shape, lambda b: (0, 0))
    return pl.pallas_call(
        _gdn_kernel,
        grid=(batch,),
        in_specs=[
            pl.BlockSpec((seq_len, 4 * GDN_WIDTH), lambda b: (b, 0)),
            pl.BlockSpec((N_GATE, seq_len), lambda b: (0, b)),
            small((N_GATE, 1)), small((N_GATE, 1)),
            small((1, HEAD_DIM)),
        ],
        out_specs=pl.BlockSpec((seq_len, GDN_WIDTH), lambda b: (b, 0)),
        out_shape=jax.ShapeDtypeStruct((t, GDN_WIDTH), BF16),
        scratch_shapes=[
            pltpu.VMEM((3, CHUNK, SC), F32),
            pltpu.VMEM((SC, SC), BF16),
            pltpu.VMEM((2, N_GATE, seq_len), F32),
            pltpu.VMEM((seq_len, 3 * N_GATE), F32),
        ],
        compiler_params=pltpu.CompilerParams(
            dimension_semantics=("arbitrary",), vmem_limit_bytes=VMEM_LIMIT),
        name="gdn",
    )(act, gate_t, alog_s, dtb_s, gdn_norm_w)


def _split3(x):
    hi = x.astype(BF16).astype(F32)
    r1 = x - hi
    mid = r1.astype(BF16).astype(F32)
    return hi, mid, r1 - mid


def _attn_kernel(lam_init, q_ref, k_ref, vt_ref, z_ref, lq1_ref, lk1_ref, lq2_ref, lk2_ref,
                 nw_ref, o_ref, bias_ref, kext_ref, vtx_ref):
    h = pl.program_id(0)
    b = pl.program_id(1)
    seq_len = k_ref.shape[0]
    lane = lax.broadcasted_iota(jnp.int32, (1, HEAD_DIM), 1)
    n_bias = 6
    log2e = math.log2(math.e)

    @pl.when(b == 0)
    def _():
        slope = lax.shift_left(jnp.int32(1), 2 * (DIFF_HEADS - 1 - h)).astype(F32) * (log2e / 256.0)
        pos = lax.broadcasted_iota(jnp.int32, (seq_len, 1), 0)
        in_block = (pos & (BK - 1)).astype(F32) * slope
        block_off = (pos & -BK).astype(F32) * slope
        terms = _split3(in_block) + _split3(block_off)
        tile = jnp.zeros((seq_len, HEAD_DIM), F32)
        for j, term in enumerate(terms):
            tile = jnp.where((lane == j) | (lane == DIFF_QK_DIM + j), term, tile)
        bias_ref[...] = tile
        ones_row = lax.broadcasted_iota(jnp.int32, (V_PAD, seq_len), 0) == 0
        vtx_ref[HEAD_DIM:, :] = jnp.where(ones_row, 1.0, 0.0).astype(BF16)

    k = k_ref[...].astype(F32)
    kext_ref[0] = jnp.where(lane < DIFF_QK_DIM, k, bias_ref[...]).astype(BF16)
    kext_ref[1] = jnp.where(lane >= DIFF_QK_DIM, k, bias_ref[...]).astype(BF16)
    vtx_ref[0:HEAD_DIM, :] = vt_ref[...]

    lam = (jnp.exp(jnp.sum(lq1_ref[...] * lk1_ref[...], axis=-1, keepdims=True))
           - jnp.exp(jnp.sum(lq2_ref[...] * lk2_ref[...], axis=-1, keepdims=True)) + lam_init)

    def block_scores(qi):
        q = q_ref[qi * BQ:(qi + 1) * BQ, :].astype(F32) * (DIFF_QK_DIM ** -0.5 * log2e)
        q_ext = (
            jnp.where(lane < DIFF_QK_DIM, q,
                      jnp.where(lane < DIFF_QK_DIM + n_bias, 1.0, 0.0)).astype(BF16),
            jnp.where(lane >= DIFF_QK_DIM, q, jnp.where(lane < n_bias, 1.0, 0.0)).astype(BF16),
        )
        n_keys = (qi + 1) * BQ
        krel = lax.broadcasted_iota(jnp.int32, (BQ, BQ), 0)
        qrel = lax.broadcasted_iota(jnp.int32, (BQ, BQ), 1)
        out = []
        for c in range(2):
            s = _dot_nt(kext_ref[c, 0:n_keys, :], q_ext[c])
            diag = jnp.where(krel <= qrel, s[n_keys - BQ:, :], MASK_VALUE)
            s = jnp.concatenate([s[:n_keys - BQ, :], diag], axis=0) if qi else diag
            out.append((s, jnp.max(s, axis=0, keepdims=True)))
        return out

    n_q = seq_len // BQ
    cur = block_scores(0)
    for qi in range(n_q):
        nxt = block_scores(qi + 1) if qi + 1 < n_q else None
        vt = vtx_ref[:, 0:(qi + 1) * BQ]
        maps = []
        for c in range(2):
            s, s_max = cur[c]
            acc = _dot(vt, jnp.exp2(s - s_max).astype(BF16))
            maps.append(acc[0:HEAD_DIM, :] / acc[HEAD_DIM:HEAD_DIM + 1, :])
        o = (maps[0] - lam * maps[1]).T
        o = o * lax.rsqrt(jnp.mean(o * o, axis=-1, keepdims=True) + NORM_EPS)
        o = o * nw_ref[...] * (1.0 - lam_init)
        qrows = slice(qi * BQ, (qi + 1) * BQ)
        o_ref[qrows, :] = (o * _silu(z_ref[qrows, :].astype(F32))).astype(BF16)
        cur = nxt


def _attn(act, v_t, lq1, lk1, lq2, lk2, diff_norm_w, lam_init, batch, seq_len):
    t = act.shape[0]
    small = lambda shape: pl.BlockSpec(shape, lambda h, b: (0, 0))
    return pl.pallas_call(
        functools.partial(_attn_kernel, lam_init),
        grid=(DIFF_HEADS, batch),
        in_specs=[
            pl.BlockSpec((seq_len, HEAD_DIM), lambda h, b: (b, COL_DQ + h)),
            pl.BlockSpec((seq_len, HEAD_DIM), lambda h, b: (b, COL_DK + h)),
            pl.BlockSpec((HEAD_DIM, seq_len), lambda h, b: (h, b)),
            pl.BlockSpec((seq_len, HEAD_DIM), lambda h, b: (b, COL_DZ + h)),
            small((1, DIFF_QK_DIM)), small((1, DIFF_QK_DIM)),
            small((1, DIFF_QK_DIM)), small((1, DIFF_QK_DIM)),
            small((1, HEAD_DIM)),
        ],
        out_specs=pl.BlockSpec((seq_len, HEAD_DIM), lambda h, b: (b, h)),
        out_shape=jax.ShapeDtypeStruct((t, DIFF_WIDTH), BF16),
        scratch_shapes=[
            pltpu.VMEM((seq_len, HEAD_DIM), F32),
            pltpu.VMEM((2, seq_len, HEAD_DIM), BF16),
            pltpu.VMEM((HEAD_DIM + V_PAD, seq_len), BF16),
        ],
        compiler_params=pltpu.CompilerParams(
            dimension_semantics=("arbitrary", "arbitrary"), vmem_limit_bytes=VMEM_LIMIT),
        name="diffattn",
    )(act, act, v_t, act, lq1, lk1, lq2, lk2, diff_norm_w)


def _outproj_kernel(oa_ref, ob_ref, x_ref, w_ref, fw_ref, out_ref):
    mix = _dot(jnp.concatenate([oa_ref[...], ob_ref[...]], axis=1), w_ref[...])
    y = x_ref[...] + mix
    ms = jnp.mean(y * y, axis=-1, keepdims=True)
    out_ref[...] = y * lax.rsqrt(ms + NORM_EPS) * fw_ref[...]


def _outproj(o_a, o_b, xf, w_out, final_norm_w):
    t = xf.shape[0]
    return pl.pallas_call(
        _outproj_kernel,
        grid=(t // TM_OUT,),
        in_specs=[
            pl.BlockSpec((TM_OUT, GDN_WIDTH), lambda i: (i, 0)),
            pl.BlockSpec((TM_OUT, DIFF_WIDTH), lambda i: (i, 0)),
            pl.BlockSpec((TM_OUT, D_MODEL), lambda i: (i, 0)),
            pl.BlockSpec((D_MODEL, D_MODEL), lambda i: (0, 0)),
            pl.BlockSpec((1, D_MODEL), lambda i: (0, 0)),
        ],
        out_specs=pl.BlockSpec((TM_OUT, D_MODEL), lambda i: (i, 0)),
        out_shape=jax.ShapeDtypeStruct((t, D_MODEL), F32),
        compiler_params=pltpu.CompilerParams(
            dimension_semantics=("arbitrary",), vmem_limit_bytes=VMEM_LIMIT),
        name="outproj",
    )(o_a, o_b, xf, w_out, final_norm_w)


def kernel(x, norm_w, w_in, conv_w, a_log, dt_bias, gdn_norm_w, lambda_q1, lambda_k1,
           lambda_q2, lambda_k2, diff_norm_w, w_out, final_norm_w):
    batch, seq_len, d_model = x.shape
    depth = norm_w.shape[0]
    assert depth == 1 and d_model == D_MODEL
    assert seq_len % TM_IN == 0 and seq_len % SC == 0 and seq_len % BQ == 0 and BQ % BK == 0
    n_wide_a = 4 * GDN_WIDTH
    xf = x.reshape(batch * seq_len, d_model)

    w = w_in[0]
    c_dq = n_wide_a + N_GATE
    c_dv = c_dq + 2 * DIFF_WIDTH
    c_dz = c_dv + DIFF_WIDTH
    w_main = jnp.concatenate([w[:, :n_wide_a], w[:, c_dq:c_dv], w[:, c_dz:]], axis=1).astype(BF16)
    w_v_t = w[:, c_dv:c_dz].T.astype(BF16)
    w_gate_t = w[:, n_wide_a:n_wide_a + N_GATE].T.astype(BF16)

    act, gate_t, v_t = _inproj(xf, norm_w[0][None, :], w_main, w_gate_t, w_v_t, conv_w[0], seq_len)

    pad_s = lambda vec: jnp.pad(vec, (GDN_HEADS, 0))[:, None]
    o_a = _gdn(act, gate_t, pad_s(a_log[0]), pad_s(dt_bias[0]), gdn_norm_w[0][None, :],
               batch, seq_len)

    lam_init = 0.8 - 0.6 * math.exp(-0.3 * 0)
    o_b = _attn(act, v_t, lambda_q1[0][None, :], lambda_k1[0][None, :], lambda_q2[0][None, :],
                lambda_k2[0][None, :], diff_norm_w[0][None, :], lam_init, batch, seq_len)

    out = _outproj(o_a, o_b, xf, w_out[0].astype(BF16), final_norm_w[None, :])
    return out.reshape(batch, seq_len, d_model)
```

```python
import functools
import math

import jax
import jax.numpy as jnp
from jax import lax
from jax.experimental import pallas as pl
from jax.experimental.pallas import tpu as pltpu

F32 = jnp.float32
BF16 = jnp.bfloat16

D_MODEL = 1024
GDN_HEADS = 4
HEAD_DIM = 128
GDN_WIDTH = GDN_HEADS * HEAD_DIM
DIFF_HEADS = 4
DIFF_QK_DIM = 64
DIFF_WIDTH = DIFF_HEADS * HEAD_DIM
CONV_K = 4
NORM_EPS = 1e-6
N_GATE = 2 * GDN_HEADS
N_MAIN = 4 * GDN_WIDTH + 3 * DIFF_WIDTH
N_CONV = 3 * GDN_WIDTH
LANES = 128
MASK_VALUE = -1e30

COL_GQ, COL_GK, COL_GV, COL_GZ = 0, 4, 8, 12
COL_DQ, COL_DK, COL_DZ = 16, 20, 24

TM_IN = 1024
NC_IN = 256
TAIL = 16
TM_OUT = 1024
SC = 256
CHUNK = 64
GDN_GROUP = 2
BQ = 512
BK = 256
V_PAD = 16
VMEM_LIMIT = 48 * 1024 * 1024


def _sigmoid(x):
    return 1.0 / (1.0 + jnp.exp(-x))


def _silu(x):
    half = 0.5 * x
    return half + half * jnp.tanh(half)


def _softplus(x):
    return jnp.maximum(x, 0.0) + jnp.log1p(jnp.exp(-jnp.abs(x)))


def _dot(a, b):
    return jnp.dot(a, b, preferred_element_type=F32)


def _dot_nt(a, b):
    return lax.dot_general(a, b, (((1,), (1,)), ((), ())), preferred_element_type=F32)


def _dot_tn(a, b):
    return lax.dot_general(a, b, (((0,), (0,)), ((), ())), preferred_element_type=F32)


def _inproj_kernel(tiles_per_seq, x_ref, nw_ref, w_ref, wgt_ref, wvt_ref, cw_ref,
                   main_ref, gate_t_ref, vt_ref, tail_ref):
    i = pl.program_id(0)
    x = x_ref[...]
    ms = jnp.mean(x * x, axis=-1, keepdims=True)
    h = (x * lax.rsqrt(ms + NORM_EPS) * nw_ref[...]).astype(BF16)

    @pl.when(i % tiles_per_seq == 0)
    def _():
        tail_ref[...] = jnp.zeros_like(tail_ref)

    tm = x.shape[0]
    row = lax.broadcasted_iota(jnp.int32, (TAIL, LANES), 0)

    def conv_silu(acc, n0):
        cols = slice(n0, n0 + LANES)
        last = acc[tm - TAIL:, :]
        delta = tail_ref[:, cols] - last
        tail_ref[:, cols] = last
        y = acc * cw_ref[CONV_K - 1:CONV_K, cols]
        fix = jnp.zeros((TAIL, LANES), F32)
        for s in range(1, CONV_K):
            wk = cw_ref[CONV_K - 1 - s:CONV_K - s, cols]
            y = y + pltpu.roll(acc, s, axis=0) * wk
            fix = fix + jnp.where(row < s, pltpu.roll(delta, s, axis=0), 0.0) * wk
        main_ref[:, cols] = _silu(y).astype(BF16)
        main_ref[0:TAIL, cols] = _silu(y[:TAIL, :] + fix).astype(BF16)

    plain = [slice(n0, n0 + NC_IN) for n0 in range(N_CONV, N_MAIN, NC_IN)]
    pending = []
    for n0 in range(0, N_CONV, NC_IN):
        acc = _dot(h, w_ref[:, n0:n0 + NC_IN])
        if pending:
            conv_silu(*pending.pop(0))
        if plain:
            pc = plain.pop(0)
            main_ref[:, pc] = _dot(h, w_ref[:, pc]).astype(BF16)
        if pending:
            conv_silu(*pending.pop(0))
        pending = [(acc[:, j:j + LANES], n0 + j) for j in range(0, NC_IN, LANES)]
    vt_ref[...] = _dot_nt(wvt_ref[...], h).astype(BF16)
    for piece in pending:
        if plain:
            pc = plain.pop(0)
            main_ref[:, pc] = _dot(h, w_ref[:, pc]).astype(BF16)
        conv_silu(*piece)
    for pc in plain:
        main_ref[:, pc] = _dot(h, w_ref[:, pc]).astype(BF16)
    gate_t_ref[...] = _dot_nt(wgt_ref[...], h)


def _inproj(xf, norm_w, w_main, w_gate_t, w_v_t, conv_w, seq_len):
    t = xf.shape[0]
    tiles_per_seq = seq_len // TM_IN
    return pl.pallas_call(
        functools.partial(_inproj_kernel, tiles_per_seq),
        grid=(t // TM_IN,),
        in_specs=[
            pl.BlockSpec((TM_IN, D_MODEL), lambda i: (i, 0)),
            pl.BlockSpec((1, D_MODEL), lambda i: (0, 0)),
            pl.BlockSpec((D_MODEL, N_MAIN), lambda i: (0, 0)),
            pl.BlockSpec((N_GATE, D_MODEL), lambda i: (0, 0)),
            pl.BlockSpec((DIFF_WIDTH, D_MODEL), lambda i: (0, 0)),
            pl.BlockSpec((CONV_K, N_CONV), lambda i: (0, 0)),
        ],
        out_specs=[
            pl.BlockSpec((TM_IN, N_MAIN), lambda i: (i, 0)),
            pl.BlockSpec((N_GATE, TM_IN), lambda i: (0, i)),
            pl.BlockSpec((DIFF_WIDTH, TM_IN), lambda i: (0, i)),
        ],
        out_shape=[
            jax.ShapeDtypeStruct((t, N_MAIN), BF16),
            jax.ShapeDtypeStruct((N_GATE, t), F32),
            jax.ShapeDtypeStruct((DIFF_WIDTH, t), BF16),
        ],
        scratch_shapes=[pltpu.VMEM((TAIL, N_CONV), F32)],
        compiler_params=pltpu.CompilerParams(
            dimension_semantics=("arbitrary",), vmem_limit_bytes=VMEM_LIMIT),
        name="inproj",
    )(xf, norm_w, w_main, w_gate_t, w_v_t, conv_w)


def _gdn_kernel(act_ref, gate_t_ref, alog_s_ref, dtb_s_ref, gnw_ref, o_ref, mask_ref, bd_ref,
                row_ref, col_ref):
    seq_len = act_ref.shape[0]
    n_chunks = SC // CHUNK
    heads = range(GDN_HEADS)
    i_cat = lax.broadcasted_iota(jnp.int32, (CHUNK, SC), 0)
    j_cat = lax.broadcasted_iota(jnp.int32, (CHUNK, SC), 1) & (CHUNK - 1)
    mask_ref[0] = jnp.where(i_cat >= j_cat, 0.0, MASK_VALUE)
    mask_ref[1] = jnp.where(i_cat > j_cat, -1.0, 0.0)
    mask_ref[2] = jnp.where(i_cat == j_cat, 1.0, 0.0)
    r = lax.broadcasted_iota(jnp.int32, (SC, SC), 0)
    c = lax.broadcasted_iota(jnp.int32, (SC, SC), 1)
    bd_ref[...] = jnp.where((r & -CHUNK) == (c & -CHUNK), 1.0, 0.0).astype(BF16)
    lane_chunk = lax.broadcasted_iota(jnp.int32, (1, SC), 1) & -CHUNK

    scale = HEAD_DIM ** -0.5

    lane_in_chunk = lax.broadcasted_iota(jnp.int32, (N_GATE, seq_len), 1) & (CHUNK - 1)
    gate_t = gate_t_ref[...]
    beta = _sigmoid(gate_t)
    log_beta = -_softplus(-gate_t)
    g_step = -jnp.exp(alog_s_ref[...]) * _softplus(gate_t + dtb_s_ref[...])
    gc = g_step
    g_after = jnp.zeros_like(g_step)
    tail = g_step
    step = 1
    while step < CHUNK:
        gc = gc + jnp.where(lane_in_chunk >= step, pltpu.roll(gc, step, axis=1), 0.0)
        ahead = jnp.where(lane_in_chunk + step < CHUNK, pltpu.roll(tail, seq_len - step, axis=1), 0.0)
        g_after = g_after + ahead
        tail = tail + ahead
        step *= 2
    e_gc = jnp.exp(gc)
    is_beta_row = lax.broadcasted_iota(jnp.int32, (N_GATE, seq_len), 0) < GDN_HEADS
    swap = pltpu.roll(e_gc, GDN_HEADS, axis=0)
    row_ref[0] = gc
    row_ref[1] = e_gc
    col_ref[...] = jnp.concatenate([
        jnp.where(is_beta_row, beta, gc),
        jnp.where(is_beta_row, log_beta, e_gc),
        jnp.where(is_beta_row, beta * swap, jnp.exp(g_after))], axis=0).T

    def cat(x):
        out = x[(n_chunks - 1) * CHUNK:, :]
        for ci in range(n_chunks - 2, -1, -1):
            out = jnp.where(lane_chunk == ci * CHUNK, x[ci * CHUNK:(ci + 1) * CHUNK, :], out)
        return out

    def block_diag(x_cat):
        xb = x_cat.astype(BF16)
        zero = jnp.zeros((CHUNK, LANES), BF16)
        row_blocks = []
        for ci in range(n_chunks):
            g = ci * CHUNK // LANES
            grp = slice(g * LANES, (g + 1) * LANES)
            part = xb[:, grp] * bd_ref[ci * CHUNK:(ci + 1) * CHUNK, grp]
            row_blocks.append(jnp.concatenate(
                [part if j == g else zero for j in range(SC // LANES)], axis=1))
        return jnp.concatenate(row_blocks, axis=0)

    def head_cols(col, h):
        return slice((col + h) * HEAD_DIM, (col + h + 1) * HEAD_DIM)

    def prepare(steps):
        causal, neg_strict, eye = mask_ref[0], mask_ref[1], mask_ref[2]
        units = [(sc, h) for sc in steps for h in heads]
        n_pow, a_cat, rhs, q_dec, k_dec, decay_last = [], [], [], [], [], []
        for sc, h in units:
            rows = slice(sc * SC, (sc + 1) * SC)
            col = lambda j: col_ref[rows, j:j + 1]
            q = act_ref[rows, head_cols(COL_GQ, h)].astype(F32)
            k = act_ref[rows, head_cols(COL_GK, h)].astype(F32)
            v = act_ref[rows, head_cols(COL_GV, h)].astype(F32)
            q = q * (lax.rsqrt(jnp.sum(q * q, axis=-1, keepdims=True) + 1e-6) * scale)
            k = k * lax.rsqrt(jnp.sum(k * k, axis=-1, keepdims=True) + 1e-6)
            hb, hg = h, GDN_HEADS + h
            kb = k.astype(BF16)
            gram = _dot_nt(jnp.concatenate([kb, q.astype(BF16)], axis=0), kb)
            e_cat = cat(col(hg)) - row_ref[0, hg:hg + 1, rows] + causal
            n_pow.append(cat(gram[:SC]) * jnp.exp(e_cat + cat(col(N_GATE + hb))) * neg_strict)
            a_cat.append(cat(gram[SC:]) * jnp.exp(e_cat))
            rhs.append(jnp.concatenate([col(hb) * v, col(2 * N_GATE + hb) * k], axis=1).astype(BF16))
            q_dec.append(q * col(N_GATE + hg))
            kd = (k * col(2 * N_GATE + hg)).astype(BF16)
            k_dec.append([kd[ci * CHUNK:(ci + 1) * CHUNK, :] for ci in range(n_chunks)])
            decay_last.append([row_ref[1, hg:hg + 1, sc * SC + (ci + 1) * CHUNK - 1:sc * SC + (ci + 1) * CHUNK]
                               for ci in range(n_chunks)])

        every = range(len(units))
        t_cat = [eye + n_pow[u] for u in every]
        n_pow = [_dot(n_pow[u].astype(BF16), block_diag(n_pow[u])) for u in every]
        span = 2
        while span < CHUNK // 2:
            both = [_dot(jnp.concatenate([t_cat[u], n_pow[u]], axis=0).astype(BF16), block_diag(n_pow[u]))
                    for u in every]
            t_cat = [t_cat[u] + both[u][:CHUNK] for u in every]
            n_pow = [both[u][CHUNK:] for u in every]
            span *= 2
        t_cat = [t_cat[u] + _dot(t_cat[u].astype(BF16), block_diag(n_pow[u])) for u in every]

        uw = [_dot(block_diag(t_cat[u]), rhs[u]) for u in every]
        kt = [[_dot_tn(k_dec[u][ci], uw[u][ci * CHUNK:(ci + 1) * CHUNK, :].astype(BF16))
               for ci in range(n_chunks)] for u in every]
        per_step = lambda xs, i: xs[i * GDN_HEADS:(i + 1) * GDN_HEADS]
        return [tuple(per_step(xs, i) for xs in (uw, kt, q_dec, a_cat, decay_last))
                for i in range(len(steps))]

    def finish(sc, prep, state):
        rows = slice(sc * SC, (sc + 1) * SC)
        uw, kt, q_dec, a_cat, decay_last = prep
        state = list(state)
        state_in = [[None] * n_chunks for _ in heads]
        for ci in range(n_chunks):
            for h in heads:
                sb = state[h].astype(BF16)
                state_in[h][ci] = sb
                state[h] = (state[h] * decay_last[h][ci]
                            - _dot(kt[h][ci][:, HEAD_DIM:].astype(BF16), sb) + kt[h][ci][:, :HEAD_DIM])
        for h in heads:
            inter = []
            for ci in range(n_chunks):
                cr = slice(ci * CHUNK, (ci + 1) * CHUNK)
                lhs = jnp.concatenate([uw[h][cr, HEAD_DIM:], q_dec[h][cr, :]], axis=0).astype(BF16)
                inter.append(_dot(lhs, state_in[h][ci]))
            v_new = jnp.concatenate([uw[h][ci * CHUNK:(ci + 1) * CHUNK, :HEAD_DIM] - inter[ci][:CHUNK]
                                     for ci in range(n_chunks)], axis=0).astype(BF16)
            o = (jnp.concatenate([inter[ci][CHUNK:] for ci in range(n_chunks)], axis=0)
                 + _dot(block_diag(a_cat[h]), v_new))
            o = o * lax.rsqrt(jnp.mean(o * o, axis=-1, keepdims=True) + NORM_EPS)
            z = act_ref[rows, head_cols(COL_GZ, h)].astype(F32)
            o_ref[rows, head_cols(0, h)] = (o * (_silu(z) * gnw_ref[...])).astype(BF16)
        return state

    n_groups = seq_len // (SC * GDN_GROUP)
    group = lambda g: list(range(g * GDN_GROUP, (g + 1) * GDN_GROUP))
    state = [jnp.zeros((HEAD_DIM, HEAD_DIM), F32) for _ in heads]
    preps = prepare(group(0))
    for g in range(n_groups):
        nxt = prepare(group(g + 1)) if g + 1 < n_groups else None
        for sc, prep in zip(group(g), preps):
            state = finish(sc, prep, state)
        preps = nxt


def _gdn(act, gate_t, alog_s, dtb_s, gdn_norm_w, batch, seq_len):
    t = act.shape[0]
    small = lambda shape: pl.BlockSpec(shape, lambda b: (0, 0))
    return pl.pallas_call(
        _gdn_kernel,
        grid=(batch,),
        in_specs=[
            pl.BlockSpec((seq_len, 4 * GDN_WIDTH), lambda b: (b, 0)),
            pl.BlockSpec((N_GATE, seq_len), lambda b: (0, b)),
            small((N_GATE, 1)), small((N_GATE, 1)),
            small((1, HEAD_DIM)),
        ],
        out_specs=pl.BlockSpec((seq_len, GDN_WIDTH), lambda b: (b, 0)),
        out_shape=jax.ShapeDtypeStruct((t, GDN_WIDTH), BF16),
        scratch_shapes=[
            pltpu.VMEM((3, CHUNK, SC), F32),
            pltpu.VMEM((SC, SC), BF16),
            pltpu.VMEM((2, N_GATE, seq_len), F32),
            pltpu.VMEM((seq_len, 3 * N_GATE), F32),
        ],
        compiler_params=pltpu.CompilerParams(
            dimension_semantics=("arbitrary",), vmem_limit_bytes=VMEM_LIMIT),
        name="gdn",
    )(act, gate_t, alog_s, dtb_s, gdn_norm_w)


def _split3(x):
    hi = x.astype(BF16).astype(F32)
    r1 = x - hi
    mid = r1.astype(BF16).astype(F32)
    return hi, mid, r1 - mid


def _attn_kernel(lam_init, q_ref, k_ref, vt_ref, z_ref, lq1_ref, lk1_ref, lq2_ref, lk2_ref,
                 nw_ref, o_ref, bias_ref, kext_ref, vtx_ref):
    h = pl.program_id(0)
    b = pl.program_id(1)
    seq_len = k_ref.shape[0]
    lane = lax.broadcasted_iota(jnp.int32, (1, HEAD_DIM), 1)
    n_bias = 6
    log2e = math.log2(math.e)

    @pl.when(b == 0)
    def _():
        slope = lax.shift_left(jnp.int32(1), 2 * (DIFF_HEADS - 1 - h)).astype(F32) * (log2e / 256.0)
        pos = lax.broadcasted_iota(jnp.int32, (seq_len, 1), 0)
        in_block = (pos & (BK - 1)).astype(F32) * slope
        block_off = (pos & -BK).astype(F32) * slope
        terms = _split3(in_block) + _split3(block_off)
        tile = jnp.zeros((seq_len, HEAD_DIM), F32)
        for j, term in enumerate(terms):
            tile = jnp.where((lane == j) | (lane == DIFF_QK_DIM + j), term, tile)
        bias_ref[...] = tile
        ones_row = lax.broadcasted_iota(jnp.int32, (V_PAD, seq_len), 0) == 0
        vtx_ref[HEAD_DIM:, :] = jnp.where(ones_row, 1.0, 0.0).astype(BF16)

    k = k_ref[...].astype(F32)
    kext_ref[0] = jnp.where(lane < DIFF_QK_DIM, k, bias_ref[...]).astype(BF16)
    kext_ref[1] = jnp.where(lane >= DIFF_QK_DIM, k, bias_ref[...]).astype(BF16)
    vtx_ref[0:HEAD_DIM, :] = vt_ref[...]

    lam = (jnp.exp(jnp.sum(lq1_ref[...] * lk1_ref[...], axis=-1, keepdims=True))
           - jnp.exp(jnp.sum(lq2_ref[...] * lk2_ref[...], axis=-1, keepdims=True)) + lam_init)

    def block_scores(qi):
        q = q_ref[qi * BQ:(qi + 1) * BQ, :].astype(F32) * (DIFF_QK_DIM ** -0.5 * log2e)
        q_ext = (
            jnp.where(lane < DIFF_QK_DIM, q,
                      jnp.where(lane < DIFF_QK_DIM + n_bias, 1.0, 0.0)).astype(BF16),
            jnp.where(lane >= DIFF_QK_DIM, q, jnp.where(lane < n_bias, 1.0, 0.0)).astype(BF16),
        )
        n_keys = (qi + 1) * BQ
        krel = lax.broadcasted_iota(jnp.int32, (BQ, BQ), 0)
        qrel = lax.broadcasted_iota(jnp.int32, (BQ, BQ), 1)
        out = []
        for c in range(2):
            s = _dot_nt(kext_ref[c, 0:n_keys, :], q_ext[c])
            diag = jnp.where(krel <= qrel, s[n_keys - BQ:, :], MASK_VALUE)
            s = jnp.concatenate([s[:n_keys - BQ, :], diag], axis=0) if qi else diag
            out.append((s, jnp.max(s, axis=0, keepdims=True)))
        return out

    n_q = seq_len // BQ
    cur = block_scores(0)
    for qi in range(n_q):
        nxt = block_scores(qi + 1) if qi + 1 < n_q else None
        vt = vtx_ref[:, 0:(qi + 1) * BQ]
        maps = []
        for c in range(2):
            s, s_max = cur[c]
            acc = _dot(vt, jnp.exp2(s - s_max).astype(BF16))
            maps.append(acc[0:HEAD_DIM, :] / acc[HEAD_DIM:HEAD_DIM + 1, :])
        o = (maps[0] - lam * maps[1]).T
        o = o * lax.rsqrt(jnp.mean(o * o, axis=-1, keepdims=True) + NORM_EPS)
        o = o * nw_ref[...] * (1.0 - lam_init)
        qrows = slice(qi * BQ, (qi + 1) * BQ)
        o_ref[qrows, :] = (o * _silu(z_ref[qrows, :].astype(F32))).astype(BF16)
        cur = nxt


def _attn(act, v_t, lq1, lk1, lq2, lk2, diff_norm_w, lam_init, batch, seq_len):
    t = act.shape[0]
    small = lambda shape: pl.BlockSpec(shape, lambda h, b: (0, 0))
    return pl.pallas_call(
        functools.partial(_attn_kernel, lam_init),
        grid=(DIFF_HEADS, batch),
        in_specs=[
            pl.BlockSpec((seq_len, HEAD_DIM), lambda h, b: (b, COL_DQ + h)),
            pl.BlockSpec((seq_len, HEAD_DIM), lambda h, b: (b, COL_DK + h)),
            pl.BlockSpec((HEAD_DIM, seq_len), lambda h, b: (h, b)),
            pl.BlockSpec((seq_len, HEAD_DIM), lambda h, b: (b, COL_DZ + h)),
            small((1, DIFF_QK_DIM)), small((1, DIFF_QK_DIM)),
            small((1, DIFF_QK_DIM)), small((1, DIFF_QK_DIM)),
            small((1, HEAD_DIM)),
        ],
        out_specs=pl.BlockSpec((seq_len, HEAD_DIM), lambda h, b: (b, h)),
        out_shape=jax.ShapeDtypeStruct((t, DIFF_WIDTH), BF16),
        scratch_shapes=[
            pltpu.VMEM((seq_len, HEAD_DIM), F32),
            pltpu.VMEM((2, seq_len, HEAD_DIM), BF16),
            pltpu.VMEM((HEAD_DIM + V_PAD, seq_len), BF16),
        ],
        compiler_params=pltpu.CompilerParams(
            dimension_semantics=("arbitrary", "arbitrary"), vmem_limit_bytes=VMEM_LIMIT),
        name="diffattn",
    )(act, act, v_t, act, lq1, lk1, lq2, lk2, diff_norm_w)


def _outproj_kernel(oa_ref, ob_ref, x_ref, w_ref, fw_ref, out_ref):
    mix = _dot(oa_ref[...], w_ref[0:GDN_WIDTH, :]) + _dot(ob_ref[...], w_ref[GDN_WIDTH:, :])
    y = x_ref[...] + mix
    ms = jnp.mean(y * y, axis=-1, keepdims=True)
    out_ref[...] = y * lax.rsqrt(ms + NORM_EPS) * fw_ref[...]


def _outproj(o_a, o_b, xf, w_out, final_norm_w):
    t = xf.shape[0]
    return pl.pallas_call(
        _outproj_kernel,
        grid=(t // TM_OUT,),
        in_specs=[
            pl.BlockSpec((TM_OUT, GDN_WIDTH), lambda i: (i, 0)),
            pl.BlockSpec((TM_OUT, DIFF_WIDTH), lambda i: (i, 0)),
            pl.BlockSpec((TM_OUT, D_MODEL), lambda i: (i, 0)),
            pl.BlockSpec((D_MODEL, D_MODEL), lambda i: (0, 0)),
            pl.BlockSpec((1, D_MODEL), lambda i: (0, 0)),
        ],
        out_specs=pl.BlockSpec((TM_OUT, D_MODEL), lambda i: (i, 0)),
        out_shape=jax.ShapeDtypeStruct((t, D_MODEL), F32),
        compiler_params=pltpu.CompilerParams(
            dimension_semantics=("arbitrary",), vmem_limit_bytes=VMEM_LIMIT),
        name="outproj",
    )(o_a, o_b, xf, w_out, final_norm_w)


def kernel(x, norm_w, w_in, conv_w, a_log, dt_bias, gdn_norm_w, lambda_q1, lambda_k1,
           lambda_q2, lambda_k2, diff_norm_w, w_out, final_norm_w):
    batch, seq_len, d_model = x.shape
    depth = norm_w.shape[0]
    assert depth == 1 and d_model == D_MODEL
    assert seq_len % TM_IN == 0 and seq_len % SC == 0 and seq_len % BQ == 0 and BQ % BK == 0
    n_wide_a = 4 * GDN_WIDTH
    xf = x.reshape(batch * seq_len, d_model)

    w = w_in[0]
    c_dq = n_wide_a + N_GATE
    c_dv = c_dq + 2 * DIFF_WIDTH
    c_dz = c_dv + DIFF_WIDTH
    w_main = jnp.concatenate([w[:, :n_wide_a], w[:, c_dq:c_dv], w[:, c_dz:]], axis=1).astype(BF16)
    w_v_t = w[:, c_dv:c_dz].T.astype(BF16)
    w_gate_t = w[:, n_wide_a:n_wide_a + N_GATE].T.astype(BF16)

    act, gate_t, v_t = _inproj(xf, norm_w[0][None, :], w_main, w_gate_t, w_v_t, conv_w[0], seq_len)

    pad_s = lambda vec: jnp.pad(vec, (GDN_HEADS, 0))[:, None]
    o_a = _gdn(act, gate_t, pad_s(a_log[0]), pad_s(dt_bias[0]), gdn_norm_w[0][None, :],
               batch, seq_len)

    lam_init = 0.8 - 0.6 * math.exp(-0.3 * 0)
    o_b = _attn(act, v_t, lambda_q1[0][None, :], lambda_k1[0][None, :], lambda_q2[0][None, :],
                lambda_k2[0][None, :], diff_norm_w[0][None, :], lam_init, batch, seq_len)

    out = _outproj(o_a, o_b, xf, w_out[0].astype(BF16), final_norm_w[None, :])
    return out.reshape(batch, seq_len, d_model)
```

```python
import functools
import math

import jax
import jax.numpy as jnp
from jax import lax
from jax.experimental import pallas as pl
from jax.experimental.pallas import tpu as pltpu

F32 = jnp.float32
BF16 = jnp.bfloat16

D_MODEL = 1024
GDN_HEADS = 4
HEAD_DIM = 128
GDN_WIDTH = GDN_HEADS * HEAD_DIM
DIFF_HEADS = 4
DIFF_QK_DIM = 64
DIFF_WIDTH = DIFF_HEADS * HEAD_DIM
CONV_K = 4
NORM_EPS = 1e-6
N_GATE = 2 * GDN_HEADS
N_MAIN = 4 * GDN_WIDTH + 3 * DIFF_WIDTH
N_CONV = 3 * GDN_WIDTH
LANES = 128
MASK_VALUE = -1e30

COL_GQ, COL_GK, COL_GV, COL_GZ = 0, 4, 8, 12
COL_DQ, COL_DK, COL_DZ = 16, 20, 24

TM_IN = 1024
NC_IN = 256
TAIL = 16
TM_OUT = 1024
SC = 256
CHUNK = 64
GDN_GROUP = 2
BQ = 512
BK = 256
V_PAD = 16
N_SHIFT = 3
VMEM_LIMIT = 48 * 1024 * 1024


def _sigmoid(x):
    return 1.0 / (1.0 + jnp.exp(-x))


def _silu(x):
    half = 0.5 * x
    return half + half * jnp.tanh(half)


def _softplus(x):
    return jnp.maximum(x, 0.0) + jnp.log1p(jnp.exp(-jnp.abs(x)))


def _dot(a, b):
    return jnp.dot(a, b, preferred_element_type=F32)


def _dot_nt(a, b):
    return lax.dot_general(a, b, (((1,), (1,)), ((), ())), preferred_element_type=F32)


def _dot_tn(a, b):
    return lax.dot_general(a, b, (((0,), (0,)), ((), ())), preferred_element_type=F32)


def _inproj_kernel(tiles_per_seq, x_ref, nw_ref, w_ref, wgt_ref, wvt_ref, cw_ref,
                   main_ref, gate_t_ref, vt_ref, tail_ref):
    i = pl.program_id(0)
    x = x_ref[...]
    ms = jnp.mean(x * x, axis=-1, keepdims=True)
    h = (x * lax.rsqrt(ms + NORM_EPS) * nw_ref[...]).astype(BF16)

    gate_t_ref[...] = _dot_nt(wgt_ref[...], h)
    vt_ref[...] = _dot_nt(wvt_ref[...], h).astype(BF16)

    @pl.when(i % tiles_per_seq == 0)
    def _():
        tail_ref[...] = jnp.zeros_like(tail_ref)

    tm = x.shape[0]
    row = lax.broadcasted_iota(jnp.int32, (TAIL, LANES), 0)

    def conv_silu(acc, n0):
        cols = slice(n0, n0 + LANES)
        last = acc[tm - TAIL:, :]
        delta = tail_ref[:, cols] - last
        tail_ref[:, cols] = last
        y = acc * cw_ref[CONV_K - 1:CONV_K, cols]
        fix = jnp.zeros((TAIL, LANES), F32)
        for s in range(1, CONV_K):
            wk = cw_ref[CONV_K - 1 - s:CONV_K - s, cols]
            y = y + pltpu.roll(acc, s, axis=0) * wk
            fix = fix + jnp.where(row < s, pltpu.roll(delta, s, axis=0), 0.0) * wk
        main_ref[:, cols] = _silu(y).astype(BF16)
        main_ref[0:TAIL, cols] = _silu(y[:TAIL, :] + fix).astype(BF16)

    plain = [slice(n0, n0 + NC_IN) for n0 in range(N_CONV, N_MAIN, NC_IN)]
    pending = []
    for n0 in range(0, N_CONV, NC_IN):
        acc = _dot(h, w_ref[:, n0:n0 + NC_IN])
        if pending:
            conv_silu(*pending.pop(0))
        if plain:
            pc = plain.pop(0)
            main_ref[:, pc] = _dot(h, w_ref[:, pc]).astype(BF16)
        if pending:
            conv_silu(*pending.pop(0))
        pending = [(acc[:, j:j + LANES], n0 + j) for j in range(0, NC_IN, LANES)]
    for piece in pending:
        if plain:
            pc = plain.pop(0)
            main_ref[:, pc] = _dot(h, w_ref[:, pc]).astype(BF16)
        conv_silu(*piece)
    for pc in plain:
        main_ref[:, pc] = _dot(h, w_ref[:, pc]).astype(BF16)


def _inproj(xf, norm_w, w_main, w_gate_t, w_v_t, conv_w, seq_len):
    t = xf.shape[0]
    tiles_per_seq = seq_len // TM_IN
    return pl.pallas_call(
        functools.partial(_inproj_kernel, tiles_per_seq),
        grid=(t // TM_IN,),
        in_specs=[
            pl.BlockSpec((TM_IN, D_MODEL), lambda i: (i, 0)),
            pl.BlockSpec((1, D_MODEL), lambda i: (0, 0)),
            pl.BlockSpec((D_MODEL, N_MAIN), lambda i: (0, 0)),
            pl.BlockSpec((N_GATE, D_MODEL), lambda i: (0, 0)),
            pl.BlockSpec((DIFF_WIDTH, D_MODEL), lambda i: (0, 0)),
            pl.BlockSpec((CONV_K, N_CONV), lambda i: (0, 0)),
        ],
        out_specs=[
            pl.BlockSpec((TM_IN, N_MAIN), lambda i: (i, 0)),
            pl.BlockSpec((N_GATE, TM_IN), lambda i: (0, i)),
            pl.BlockSpec((DIFF_WIDTH, TM_IN), lambda i: (0, i)),
        ],
        out_shape=[
            jax.ShapeDtypeStruct((t, N_MAIN), BF16),
            jax.ShapeDtypeStruct((N_GATE, t), F32),
            jax.ShapeDtypeStruct((DIFF_WIDTH, t), BF16),
        ],
        scratch_shapes=[pltpu.VMEM((TAIL, N_CONV), F32)],
        compiler_params=pltpu.CompilerParams(
            dimension_semantics=("arbitrary",), vmem_limit_bytes=VMEM_LIMIT),
        name="inproj",
    )(xf, norm_w, w_main, w_gate_t, w_v_t, conv_w)


def _gdn_kernel(act_ref, gate_t_ref, alog_s_ref, dtb_s_ref, gnw_ref, o_ref, mask_ref, bd_ref,
                row_ref, col_ref):
    seq_len = act_ref.shape[0]
    n_chunks = SC // CHUNK
    heads = range(GDN_HEADS)
    i_cat = lax.broadcasted_iota(jnp.int32, (CHUNK, SC), 0)
    j_cat = lax.broadcasted_iota(jnp.int32, (CHUNK, SC), 1) & (CHUNK - 1)
    mask_ref[0] = jnp.where(i_cat >= j_cat, 0.0, MASK_VALUE)
    mask_ref[1] = jnp.where(i_cat > j_cat, -1.0, 0.0)
    mask_ref[2] = jnp.where(i_cat == j_cat, 1.0, 0.0)
    r = lax.broadcasted_iota(jnp.int32, (SC, SC), 0)
    c = lax.broadcasted_iota(jnp.int32, (SC, SC), 1)
    bd_ref[...] = jnp.where((r & -CHUNK) == (c & -CHUNK), 1.0, 0.0).astype(BF16)
    lane_chunk = lax.broadcasted_iota(jnp.int32, (1, SC), 1) & -CHUNK

    scale = HEAD_DIM ** -0.5

    lane_in_chunk = lax.broadcasted_iota(jnp.int32, (N_GATE, seq_len), 1) & (CHUNK - 1)
    gate_t = gate_t_ref[...]
    beta = _sigmoid(gate_t)
    log_beta = -_softplus(-gate_t)
    g_step = -jnp.exp(alog_s_ref[...]) * _softplus(gate_t + dtb_s_ref[...])
    gc = g_step
    g_after = jnp.zeros_like(g_step)
    tail = g_step
    step = 1
    while step < CHUNK:
        gc = gc + jnp.where(lane_in_chunk >= step, pltpu.roll(gc, step, axis=1), 0.0)
        ahead = jnp.where(lane_in_chunk + step < CHUNK, pltpu.roll(tail, seq_len - step, axis=1), 0.0)
        g_after = g_after + ahead
        tail = tail + ahead
        step *= 2
    e_gc = jnp.exp(gc)
    is_beta_row = lax.broadcasted_iota(jnp.int32, (N_GATE, seq_len), 0) < GDN_HEADS
    swap = pltpu.roll(e_gc, GDN_HEADS, axis=0)
    row_ref[0] = gc
    row_ref[1] = e_gc
    col_ref[...] = jnp.concatenate([
        jnp.where(is_beta_row, beta, gc),
        jnp.where(is_beta_row, log_beta, e_gc),
        jnp.where(is_beta_row, beta * swap, jnp.exp(g_after))], axis=0).T

    def cat(x):
        out = x[(n_chunks - 1) * CHUNK:, :]
        for ci in range(n_chunks - 2, -1, -1):
            out = jnp.where(lane_chunk == ci * CHUNK, x[ci * CHUNK:(ci + 1) * CHUNK, :], out)
        return out

    def block_diag(x_cat):
        xb = x_cat.astype(BF16)
        zero = jnp.zeros((CHUNK, LANES), BF16)
        row_blocks = []
        for ci in range(n_chunks):
            g = ci * CHUNK // LANES
            grp = slice(g * LANES, (g + 1) * LANES)
            part = xb[:, grp] * bd_ref[ci * CHUNK:(ci + 1) * CHUNK, grp]
            row_blocks.append(jnp.concatenate(
                [part if j == g else zero for j in range(SC // LANES)], axis=1))
        return jnp.concatenate(row_blocks, axis=0)

    def head_cols(col, h):
        return slice((col + h) * HEAD_DIM, (col + h + 1) * HEAD_DIM)

    def prepare(steps):
        causal, neg_strict, eye = mask_ref[0], mask_ref[1], mask_ref[2]
        units = [(sc, h) for sc in steps for h in heads]
        n_pow, a_cat, rhs, q_dec, k_dec, decay_last = [], [], [], [], [], []
        for sc, h in units:
            rows = slice(sc * SC, (sc + 1) * SC)
            col = lambda j: col_ref[rows, j:j + 1]
            q = act_ref[rows, head_cols(COL_GQ, h)].astype(F32)
            k = act_ref[rows, head_cols(COL_GK, h)].astype(F32)
            v = act_ref[rows, head_cols(COL_GV, h)].astype(F32)
            q = q * (lax.rsqrt(jnp.sum(q * q, axis=-1, keepdims=True) + 1e-6) * scale)
            k = k * lax.rsqrt(jnp.sum(k * k, axis=-1, keepdims=True) + 1e-6)
            hb, hg = h, GDN_HEADS + h
            kb = k.astype(BF16)
            gram = _dot_nt(jnp.concatenate([kb, q.astype(BF16)], axis=0), kb)
            e_cat = cat(col(hg)) - row_ref[0, hg:hg + 1, rows] + causal
            n_pow.append(cat(gram[:SC]) * jnp.exp(e_cat + cat(col(N_GATE + hb))) * neg_strict)
            a_cat.append(cat(gram[SC:]) * jnp.exp(e_cat))
            rhs.append(jnp.concatenate([col(hb) * v, col(2 * N_GATE + hb) * k], axis=1).astype(BF16))
            q_dec.append(q * col(N_GATE + hg))
            kd = (k * col(2 * N_GATE + hg)).astype(BF16)
            k_dec.append([kd[ci * CHUNK:(ci + 1) * CHUNK, :] for ci in range(n_chunks)])
            decay_last.append([row_ref[1, hg:hg + 1, sc * SC + (ci + 1) * CHUNK - 1:sc * SC + (ci + 1) * CHUNK]
                               for ci in range(n_chunks)])

        every = range(len(units))
        t_cat = [eye + n_pow[u] for u in every]
        n_pow = [_dot(n_pow[u].astype(BF16), block_diag(n_pow[u])) for u in every]
        span = 2
        while span < CHUNK // 2:
            both = [_dot(jnp.concatenate([t_cat[u], n_pow[u]], axis=0).astype(BF16), block_diag(n_pow[u]))
                    for u in every]
            t_cat = [t_cat[u] + both[u][:CHUNK] for u in every]
            n_pow = [both[u][CHUNK:] for u in every]
            span *= 2
        t_cat = [t_cat[u] + _dot(t_cat[u].astype(BF16), block_diag(n_pow[u])) for u in every]

        uw = [_dot(block_diag(t_cat[u]), rhs[u]) for u in every]
        kt = [[_dot_tn(k_dec[u][ci], uw[u][ci * CHUNK:(ci + 1) * CHUNK, :].astype(BF16))
               for ci in range(n_chunks)] for u in every]
        per_step = lambda xs, i: xs[i * GDN_HEADS:(i + 1) * GDN_HEADS]
        return [tuple(per_step(xs, i) for xs in (uw, kt, q_dec, a_cat, decay_last))
                for i in range(len(steps))]

    def finish(sc, prep, state):
        rows = slice(sc * SC, (sc + 1) * SC)
        uw, kt, q_dec, a_cat, decay_last = prep
        state = list(state)
        state_in = [[None] * n_chunks for _ in heads]
        for ci in range(n_chunks):
            for h in heads:
                sb = state[h].astype(BF16)
                state_in[h][ci] = sb
                state[h] = (state[h] * decay_last[h][ci]
                            - _dot(kt[h][ci][:, HEAD_DIM:].astype(BF16), sb) + kt[h][ci][:, :HEAD_DIM])
        for h in heads:
            inter = []
            for ci in range(n_chunks):
                cr = slice(ci * CHUNK, (ci + 1) * CHUNK)
                lhs = jnp.concatenate([uw[h][cr, HEAD_DIM:], q_dec[h][cr, :]], axis=0).astype(BF16)
                inter.append(_dot(lhs, state_in[h][ci]))
            v_new = jnp.concatenate([uw[h][ci * CHUNK:(ci + 1) * CHUNK, :HEAD_DIM] - inter[ci][:CHUNK]
                                     for ci in range(n_chunks)], axis=0).astype(BF16)
            o = (jnp.concatenate([inter[ci][CHUNK:] for ci in range(n_chunks)], axis=0)
                 + _dot(block_diag(a_cat[h]), v_new))
            o = o * lax.rsqrt(jnp.mean(o * o, axis=-1, keepdims=True) + NORM_EPS)
            z = act_ref[rows, head_cols(COL_GZ, h)].astype(F32)
            o_ref[rows, head_cols(0, h)] = (o * (_silu(z) * gnw_ref[...])).astype(BF16)
        return state

    n_groups = seq_len // (SC * GDN_GROUP)
    group = lambda g: list(range(g * GDN_GROUP, (g + 1) * GDN_GROUP))
    state = [jnp.zeros((HEAD_DIM, HEAD_DIM), F32) for _ in heads]
    preps = prepare(group(0))
    for g in range(n_groups):
        nxt = prepare(group(g + 1)) if g + 1 < n_groups else None
        for sc, prep in zip(group(g), preps):
            state = finish(sc, prep, state)
        preps = nxt


def _gdn(act, gate_t, alog_s, dtb_s, gdn_norm_w, batch, seq_len):
    t = act.shape[0]
    small = lambda shape: pl.BlockSpec(shape, lambda b: (0, 0))
    return pl.pallas_call(
        _gdn_kernel,
        grid=(batch,),
        in_specs=[
            pl.BlockSpec((seq_len, 4 * GDN_WIDTH), lambda b: (b, 0)),
            pl.BlockSpec((N_GATE, seq_len), lambda b: (0, b)),
            small((N_GATE, 1)), small((N_GATE, 1)),
            small((1, HEAD_DIM)),
        ],
        out_specs=pl.BlockSpec((seq_len, GDN_WIDTH), lambda b: (b, 0)),
        out_shape=jax.ShapeDtypeStruct((t, GDN_WIDTH), BF16),
        scratch_shapes=[
            pltpu.VMEM((3, CHUNK, SC), F32),
            pltpu.VMEM((SC, SC), BF16),
            pltpu.VMEM((2, N_GATE, seq_len), F32),
            pltpu.VMEM((seq_len, 3 * N_GATE), F32),
        ],
        compiler_params=pltpu.CompilerParams(
            dimension_semantics=("arbitrary",), vmem_limit_bytes=VMEM_LIMIT),
        name="gdn",
    )(act, gate_t, alog_s, dtb_s, gdn_norm_w)


def _split3(x):
    hi = x.astype(BF16).astype(F32)
    r1 = x - hi
    mid = r1.astype(BF16).astype(F32)
    return hi, mid, r1 - mid


def _attn_kernel(lam_init, q_ref, k_ref, vt_ref, z_ref, lq1_ref, lk1_ref, lq2_ref, lk2_ref,
                 nw_ref, o_ref, bias_ref, kext_ref, vtx_ref):
    h = pl.program_id(0)
    b = pl.program_id(1)
    seq_len = k_ref.shape[0]
    lane = lax.broadcasted_iota(jnp.int32, (1, HEAD_DIM), 1)
    n_bias = 6
    log2e = math.log2(math.e)

    @pl.when(b == 0)
    def _():
        slope = lax.shift_left(jnp.int32(1), 2 * (DIFF_HEADS - 1 - h)).astype(F32) * (log2e / 256.0)
        pos = lax.broadcasted_iota(jnp.int32, (seq_len, 1), 0)
        in_block = (pos & (BK - 1)).astype(F32) * slope
        block_off = (pos & -BK).astype(F32) * slope
        terms = _split3(in_block) + _split3(block_off)
        tile = jnp.zeros((seq_len, HEAD_DIM), F32)
        for j, term in enumerate(terms):
            tile = jnp.where((lane == j) | (lane == DIFF_QK_DIM + j), term, tile)
        for j in range(N_SHIFT):
            tile = jnp.where((lane == n_bias + j) | (lane == DIFF_QK_DIM + n_bias + j), 1.0, tile)
        bias_ref[...] = tile
        ones_row = lax.broadcasted_iota(jnp.int32, (V_PAD, seq_len), 0) == 0
        vtx_ref[HEAD_DIM:, :] = jnp.where(ones_row, 1.0, 0.0).astype(BF16)

    k = k_ref[...].astype(F32)
    kext_ref[0] = jnp.where(lane < DIFF_QK_DIM, k, bias_ref[...]).astype(BF16)
    kext_ref[1] = jnp.where(lane >= DIFF_QK_DIM, k, bias_ref[...]).astype(BF16)
    vtx_ref[0:HEAD_DIM, :] = vt_ref[...]

    lam = (jnp.exp(jnp.sum(lq1_ref[...] * lk1_ref[...], axis=-1, keepdims=True))
           - jnp.exp(jnp.sum(lq2_ref[...] * lk2_ref[...], axis=-1, keepdims=True)) + lam_init)

    krel = lax.broadcasted_iota(jnp.int32, (BQ, BQ), 0)
    qrel = lax.broadcasted_iota(jnp.int32, (BQ, BQ), 1)

    def masked_scores(q_ext_c, c, qi):
        n_keys = (qi + 1) * BQ
        s = _dot_nt(kext_ref[c, 0:n_keys, :], q_ext_c.astype(BF16))
        diag = jnp.where(krel <= qrel, s[n_keys - BQ:, :], MASK_VALUE)
        return jnp.concatenate([s[:n_keys - BQ, :], diag], axis=0) if qi else diag

    def block_maxima(qi):
        q = q_ref[qi * BQ:(qi + 1) * BQ, :].astype(F32) * (DIFF_QK_DIM ** -0.5 * log2e)
        q_ext = (
            jnp.where(lane < DIFF_QK_DIM, q, jnp.where(lane < DIFF_QK_DIM + n_bias, 1.0, 0.0)),
            jnp.where(lane >= DIFF_QK_DIM, q, jnp.where(lane < n_bias, 1.0, 0.0)),
        )
        shifted = []
        for c in range(2):
            s_max = jnp.max(masked_scores(q_ext[c], c, qi), axis=0, keepdims=True)
            neg_col = jnp.broadcast_to(-s_max, (8, BQ)).T[:, 0:1]
            base = (DIFF_QK_DIM if c == 0 else 0) + n_bias
            qs = q_ext[c]
            for j, term in enumerate(_split3(neg_col)):
                qs = jnp.where(lane == base + j, term, qs)
            shifted.append(qs)
        return shifted

    n_q = seq_len // BQ
    cur = block_maxima(0)
    for qi in range(n_q):
        nxt = block_maxima(qi + 1) if qi + 1 < n_q else None
        vt = vtx_ref[:, 0:(qi + 1) * BQ]
        maps = []
        for c in range(2):
            p = jnp.exp2(masked_scores(cur[c], c, qi)).astype(BF16)
            acc = _dot(vt, p)
            maps.append(acc[0:HEAD_DIM, :] / acc[HEAD_DIM:HEAD_DIM + 1, :])
        o = (maps[0] - lam * maps[1]).T
        o = o * lax.rsqrt(jnp.mean(o * o, axis=-1, keepdims=True) + NORM_EPS)
        o = o * nw_ref[...] * (1.0 - lam_init)
        qrows = slice(qi * BQ, (qi + 1) * BQ)
        o_ref[qrows, :] = (o * _silu(z_ref[qrows, :].astype(F32))).astype(BF16)
        cur = nxt


def _attn(act, v_t, lq1, lk1, lq2, lk2, diff_norm_w, lam_init, batch, seq_len):
    t = act.shape[0]
    small = lambda shape: pl.BlockSpec(shape, lambda h, b: (0, 0))
    return pl.pallas_call(
        functools.partial(_attn_kernel, lam_init),
        grid=(DIFF_HEADS, batch),
        in_specs=[
            pl.BlockSpec((seq_len, HEAD_DIM), lambda h, b: (b, COL_DQ + h)),
            pl.BlockSpec((seq_len, HEAD_DIM), lambda h, b: (b, COL_DK + h)),
            pl.BlockSpec((HEAD_DIM, seq_len), lambda h, b: (h, b)),
            pl.BlockSpec((seq_len, HEAD_DIM), lambda h, b: (b, COL_DZ + h)),
            small((1, DIFF_QK_DIM)), small((1, DIFF_QK_DIM)),
            small((1, DIFF_QK_DIM)), small((1, DIFF_QK_DIM)),
            small((1, HEAD_DIM)),
        ],
        out_specs=pl.BlockSpec((seq_len, HEAD_DIM), lambda h, b: (b, h)),
        out_shape=jax.ShapeDtypeStruct((t, DIFF_WIDTH), BF16),
        scratch_shapes=[
            pltpu.VMEM((seq_len, HEAD_DIM), F32),
            pltpu.VMEM((2, seq_len, HEAD_DIM), BF16),
            pltpu.VMEM((HEAD_DIM + V_PAD, seq_len), BF16),
        ],
        compiler_params=pltpu.CompilerParams(
            dimension_semantics=("arbitrary", "arbitrary"), vmem_limit_bytes=VMEM_LIMIT),
        name="diffattn",
    )(act, act, v_t, act, lq1, lk1, lq2, lk2, diff_norm_w)


def _outproj_kernel(oa_ref, ob_ref, x_ref, w_ref, fw_ref, out_ref):
    mix = _dot(oa_ref[...], w_ref[0:GDN_WIDTH, :]) + _dot(ob_ref[...], w_ref[GDN_WIDTH:, :])
    y = x_ref[...] + mix
    ms = jnp.mean(y * y, axis=-1, keepdims=True)
    out_ref[...] = y * lax.rsqrt(ms + NORM_EPS) * fw_ref[...]


def _outproj(o_a, o_b, xf, w_out, final_norm_w):
    t = xf.shape[0]
    return pl.pallas_call(
        _outproj_kernel,
        grid=(t // TM_OUT,),
        in_specs=[
            pl.BlockSpec((TM_OUT, GDN_WIDTH), lambda i: (i, 0)),
            pl.BlockSpec((TM_OUT, DIFF_WIDTH), lambda i: (i, 0)),
            pl.BlockSpec((TM_OUT, D_MODEL), lambda i: (i, 0)),
            pl.BlockSpec((D_MODEL, D_MODEL), lambda i: (0, 0)),
            pl.BlockSpec((1, D_MODEL), lambda i: (0, 0)),
        ],
        out_specs=pl.BlockSpec((TM_OUT, D_MODEL), lambda i: (i, 0)),
        out_shape=jax.ShapeDtypeStruct((t, D_MODEL), F32),
        compiler_params=pltpu.CompilerParams(
            dimension_semantics=("arbitrary",), vmem_limit_bytes=VMEM_LIMIT),
        name="outproj",
    )(o_a, o_b, xf, w_out, final_norm_w)


def kernel(x, norm_w, w_in, conv_w, a_log, dt_bias, gdn_norm_w, lambda_q1, lambda_k1,
           lambda_q2, lambda_k2, diff_norm_w, w_out, final_norm_w):
    batch, seq_len, d_model = x.shape
    depth = norm_w.shape[0]
    assert depth == 1 and d_model == D_MODEL
    assert seq_len % TM_IN == 0 and seq_len % SC == 0 and seq_len % BQ == 0 and BQ % BK == 0
    n_wide_a = 4 * GDN_WIDTH
    xf = x.reshape(batch * seq_len, d_model)

    w = w_in[0]
    c_dq = n_wide_a + N_GATE
    c_dv = c_dq + 2 * DIFF_WIDTH
    c_dz = c_dv + DIFF_WIDTH
    w_main = jnp.concatenate([w[:, :n_wide_a], w[:, c_dq:c_dv], w[:, c_dz:]], axis=1).astype(BF16)
    w_v_t = w[:, c_dv:c_dz].T.astype(BF16)
    w_gate_t = w[:, n_wide_a:n_wide_a + N_GATE].T.astype(BF16)

    act, gate_t, v_t = _inproj(xf, norm_w[0][None, :], w_main, w_gate_t, w_v_t, conv_w[0], seq_len)

    pad_s = lambda vec: jnp.pad(vec, (GDN_HEADS, 0))[:, None]
    o_a = _gdn(act, gate_t, pad_s(a_log[0]), pad_s(dt_bias[0]), gdn_norm_w[0][None, :],
               batch, seq_len)

    lam_init = 0.8 - 0.6 * math.exp(-0.3 * 0)
    o_b = _attn(act, v_t, lambda_q1[0][None, :], lambda_k1[0][None, :], lambda_q2[0][None, :],
                lambda_k2[0][None, :], diff_norm_w[0][None, :], lam_init, batch, seq_len)

    out = _outproj(o_a, o_b, xf, w_out[0].astype(BF16), final_norm_w[None, :])
    return out.reshape(batch, seq_len, d_model)
```

```python
import functools
import math

import jax
import jax.numpy as jnp
from jax import lax
from jax.experimental import pallas as pl
from jax.experimental.pallas import tpu as pltpu

F32 = jnp.float32
BF16 = jnp.bfloat16

D_MODEL = 1024
GDN_HEADS = 4
HEAD_DIM = 128
GDN_WIDTH = GDN_HEADS * HEAD_DIM
DIFF_HEADS = 4
DIFF_QK_DIM = 64
DIFF_WIDTH = DIFF_HEADS * HEAD_DIM
CONV_K = 4
NORM_EPS = 1e-6
N_GATE = 2 * GDN_HEADS
N_MAIN = 4 * GDN_WIDTH + 3 * DIFF_WIDTH
N_CONV = 3 * GDN_WIDTH
LANES = 128
MASK_VALUE = -1e30

COL_GQ, COL_GK, COL_GV, COL_GZ = 0, 4, 8, 12
COL_DQ, COL_DK, COL_DZ = 16, 20, 24

TM_IN = 1024
NC_IN = 256
TAIL = 16
TM_OUT = 1024
SC = 256
CHUNK = 64
GDN_GROUP = 2
BQ = 512
BK = 256
V_PAD = 16
VMEM_LIMIT = 48 * 1024 * 1024


def _sigmoid(x):
    return 1.0 / (1.0 + jnp.exp(-x))


def _silu(x):
    half = 0.5 * x
    return half + half * jnp.tanh(half)


def _softplus(x):
    return jnp.maximum(x, 0.0) + jnp.log1p(jnp.exp(-jnp.abs(x)))


def _dot(a, b):
    return jnp.dot(a, b, preferred_element_type=F32)


def _dot_nt(a, b):
    return lax.dot_general(a, b, (((1,), (1,)), ((), ())), preferred_element_type=F32)


def _dot_tn(a, b):
    return lax.dot_general(a, b, (((0,), (0,)), ((), ())), preferred_element_type=F32)


def _inproj_kernel(tiles_per_seq, x_ref, nw_ref, w_ref, wgt_ref, wvt_ref, cw_ref,
                   main_ref, gate_t_ref, vt_ref, tail_ref):
    i = pl.program_id(0)
    x = x_ref[...]
    ms = jnp.mean(x * x, axis=-1, keepdims=True)
    h = (x * lax.rsqrt(ms + NORM_EPS) * nw_ref[...]).astype(BF16)

    gate_t_ref[...] = _dot_nt(wgt_ref[...], h)
    vt_ref[...] = _dot_nt(wvt_ref[...], h).astype(BF16)

    @pl.when(i % tiles_per_seq == 0)
    def _():
        tail_ref[...] = jnp.zeros_like(tail_ref)

    tm = x.shape[0]
    row = lax.broadcasted_iota(jnp.int32, (TAIL, LANES), 0)

    def conv_silu(acc, n0):
        cols = slice(n0, n0 + LANES)
        last = acc[tm - TAIL:, :]
        delta = tail_ref[:, cols] - last
        tail_ref[:, cols] = last
        y = acc * cw_ref[CONV_K - 1:CONV_K, cols]
        fix = jnp.zeros((TAIL, LANES), F32)
        for s in range(1, CONV_K):
            wk = cw_ref[CONV_K - 1 - s:CONV_K - s, cols]
            y = y + pltpu.roll(acc, s, axis=0) * wk
            fix = fix + jnp.where(row < s, pltpu.roll(delta, s, axis=0), 0.0) * wk
        main_ref[:, cols] = _silu(y).astype(BF16)
        main_ref[0:TAIL, cols] = _silu(y[:TAIL, :] + fix).astype(BF16)

    plain = [slice(n0, n0 + NC_IN) for n0 in range(N_CONV, N_MAIN, NC_IN)]
    pending = []
    for n0 in range(0, N_CONV, NC_IN):
        acc = _dot(h, w_ref[:, n0:n0 + NC_IN])
        if pending:
            conv_silu(*pending.pop(0))
        if plain:
            pc = plain.pop(0)
            main_ref[:, pc] = _dot(h, w_ref[:, pc]).astype(BF16)
        if pending:
            conv_silu(*pending.pop(0))
        pending = [(acc[:, j:j + LANES], n0 + j) for j in range(0, NC_IN, LANES)]
    for piece in pending:
        if plain:
            pc = plain.pop(0)
            main_ref[:, pc] = _dot(h, w_ref[:, pc]).astype(BF16)
        conv_silu(*piece)
    for pc in plain:
        main_ref[:, pc] = _dot(h, w_ref[:, pc]).astype(BF16)


def _inproj(xf, norm_w, w_main, w_gate_t, w_v_t, conv_w, seq_len):
    t = xf.shape[0]
    tiles_per_seq = seq_len // TM_IN
    return pl.pallas_call(
        functools.partial(_inproj_kernel, tiles_per_seq),
        grid=(t // TM_IN,),
        in_specs=[
            pl.BlockSpec((TM_IN, D_MODEL), lambda i: (i, 0)),
            pl.BlockSpec((1, D_MODEL), lambda i: (0, 0)),
            pl.BlockSpec((D_MODEL, N_MAIN), lambda i: (0, 0)),
            pl.BlockSpec((N_GATE, D_MODEL), lambda i: (0, 0)),
            pl.BlockSpec((DIFF_WIDTH, D_MODEL), lambda i: (0, 0)),
            pl.BlockSpec((CONV_K, N_CONV), lambda i: (0, 0)),
        ],
        out_specs=[
            pl.BlockSpec((TM_IN, N_MAIN), lambda i: (i, 0)),
            pl.BlockSpec((N_GATE, TM_IN), lambda i: (0, i)),
            pl.BlockSpec((DIFF_WIDTH, TM_IN), lambda i: (0, i)),
        ],
        out_shape=[
            jax.ShapeDtypeStruct((t, N_MAIN), BF16),
            jax.ShapeDtypeStruct((N_GATE, t), F32),
            jax.ShapeDtypeStruct((DIFF_WIDTH, t), BF16),
        ],
        scratch_shapes=[pltpu.VMEM((TAIL, N_CONV), F32)],
        compiler_params=pltpu.CompilerParams(
            dimension_semantics=("arbitrary",), vmem_limit_bytes=VMEM_LIMIT),
        name="inproj",
    )(xf, norm_w, w_main, w_gate_t, w_v_t, conv_w)


def _gdn_kernel(act_ref, gate_t_ref, alog_s_ref, dtb_s_ref, gnw_ref, o_ref, mask_ref, bd_ref,
                row_ref, col_ref):
    seq_len = act_ref.shape[0]
    n_chunks = SC // CHUNK
    heads = range(GDN_HEADS)
    i_cat = lax.broadcasted_iota(jnp.int32, (CHUNK, SC), 0)
    j_cat = lax.broadcasted_iota(jnp.int32, (CHUNK, SC), 1) & (CHUNK - 1)
    mask_ref[0] = jnp.where(i_cat >= j_cat, 0.0, MASK_VALUE)
    mask_ref[1] = jnp.where(i_cat > j_cat, -1.0, 0.0)
    mask_ref[2] = jnp.where(i_cat == j_cat, 1.0, 0.0)
    r = lax.broadcasted_iota(jnp.int32, (SC, SC), 0)
    c = lax.broadcasted_iota(jnp.int32, (SC, SC), 1)
    bd_ref[...] = jnp.where((r & -CHUNK) == (c & -CHUNK), 1.0, 0.0).astype(BF16)
    lane_chunk = lax.broadcasted_iota(jnp.int32, (1, SC), 1) & -CHUNK

    scale = HEAD_DIM ** -0.5

    lane_in_chunk = lax.broadcasted_iota(jnp.int32, (N_GATE, seq_len), 1) & (CHUNK - 1)
    gate_t = gate_t_ref[...]
    beta = _sigmoid(gate_t)
    log_beta = -_softplus(-gate_t)
    g_step = -jnp.exp(alog_s_ref[...]) * _softplus(gate_t + dtb_s_ref[...])
    gc = g_step
    g_after = jnp.zeros_like(g_step)
    tail = g_step
    step = 1
    while step < CHUNK:
        gc = gc + jnp.where(lane_in_chunk >= step, pltpu.roll(gc, step, axis=1), 0.0)
        ahead = jnp.where(lane_in_chunk + step < CHUNK, pltpu.roll(tail, seq_len - step, axis=1), 0.0)
        g_after = g_after + ahead
        tail = tail + ahead
        step *= 2
    e_gc = jnp.exp(gc)
    is_beta_row = lax.broadcasted_iota(jnp.int32, (N_GATE, seq_len), 0) < GDN_HEADS
    swap = pltpu.roll(e_gc, GDN_HEADS, axis=0)
    row_ref[0] = gc
    row_ref[1] = e_gc
    col_ref[...] = jnp.concatenate([
        jnp.where(is_beta_row, beta, gc),
        jnp.where(is_beta_row, log_beta, e_gc),
        jnp.where(is_beta_row, beta * swap, jnp.exp(g_after))], axis=0).T

    def cat(x):
        out = x[(n_chunks - 1) * CHUNK:, :]
        for ci in range(n_chunks - 2, -1, -1):
            out = jnp.where(lane_chunk == ci * CHUNK, x[ci * CHUNK:(ci + 1) * CHUNK, :], out)
        return out

    def block_diag(x_cat):
        xb = x_cat.astype(BF16)
        zero = jnp.zeros((CHUNK, LANES), BF16)
        row_blocks = []
        for ci in range(n_chunks):
            g = ci * CHUNK // LANES
            grp = slice(g * LANES, (g + 1) * LANES)
            part = xb[:, grp] * bd_ref[ci * CHUNK:(ci + 1) * CHUNK, grp]
            row_blocks.append(jnp.concatenate(
                [part if j == g else zero for j in range(SC // LANES)], axis=1))
        return jnp.concatenate(row_blocks, axis=0)

    def head_cols(col, h):
        return slice((col + h) * HEAD_DIM, (col + h + 1) * HEAD_DIM)

    def prepare(steps):
        causal, neg_strict, eye = mask_ref[0], mask_ref[1], mask_ref[2]
        units = [(sc, h) for sc in steps for h in heads]
        n_pow, a_cat, rhs, q_dec, k_dec, decay_last = [], [], [], [], [], []
        for sc, h in units:
            rows = slice(sc * SC, (sc + 1) * SC)
            col = lambda j: col_ref[rows, j:j + 1]
            q = act_ref[rows, head_cols(COL_GQ, h)].astype(F32)
            k = act_ref[rows, head_cols(COL_GK, h)].astype(F32)
            v = act_ref[rows, head_cols(COL_GV, h)].astype(F32)
            q = q * (lax.rsqrt(jnp.sum(q * q, axis=-1, keepdims=True) + 1e-6) * scale)
            k = k * lax.rsqrt(jnp.sum(k * k, axis=-1, keepdims=True) + 1e-6)
            hb, hg = h, GDN_HEADS + h
            kb = k.astype(BF16)
            gram = _dot_nt(jnp.concatenate([kb, q.astype(BF16)], axis=0), kb)
            e_cat = cat(col(hg)) - row_ref[0, hg:hg + 1, rows] + causal
            n_pow.append(cat(gram[:SC]) * jnp.exp(e_cat + cat(col(N_GATE + hb))) * neg_strict)
            a_cat.append(cat(gram[SC:]) * jnp.exp(e_cat))
            rhs.append(jnp.concatenate([col(hb) * v, col(2 * N_GATE + hb) * k], axis=1).astype(BF16))
            q_dec.append(q * col(N_GATE + hg))
            kd = (k * col(2 * N_GATE + hg)).astype(BF16)
            k_dec.append([kd[ci * CHUNK:(ci + 1) * CHUNK, :] for ci in range(n_chunks)])
            decay_last.append([row_ref[1, hg:hg + 1, sc * SC + (ci + 1) * CHUNK - 1:sc * SC + (ci + 1) * CHUNK]
                               for ci in range(n_chunks)])

        every = range(len(units))
        t_cat = [eye + n_pow[u] for u in every]
        n_pow = [_dot(n_pow[u].astype(BF16), block_diag(n_pow[u])) for u in every]
        span = 2
        while span < CHUNK // 2:
            both = [_dot(jnp.concatenate([t_cat[u], n_pow[u]], axis=0).astype(BF16), block_diag(n_pow[u]))
                    for u in every]
            t_cat = [t_cat[u] + both[u][:CHUNK] for u in every]
            n_pow = [both[u][CHUNK:] for u in every]
            span *= 2
        t_cat = [t_cat[u] + _dot(t_cat[u].astype(BF16), block_diag(n_pow[u])) for u in every]

        uw = [_dot(block_diag(t_cat[u]), rhs[u]) for u in every]
        kt = [[_dot_tn(k_dec[u][ci], uw[u][ci * CHUNK:(ci + 1) * CHUNK, :].astype(BF16))
               for ci in range(n_chunks)] for u in every]
        per_step = lambda xs, i: xs[i * GDN_HEADS:(i + 1) * GDN_HEADS]
        return [tuple(per_step(xs, i) for xs in (uw, kt, q_dec, a_cat, decay_last))
                for i in range(len(steps))]

    def finish(sc, prep, state):
        rows = slice(sc * SC, (sc + 1) * SC)
        uw, kt, q_dec, a_cat, decay_last = prep
        state = list(state)
        state_in = [[None] * n_chunks for _ in heads]
        for ci in range(n_chunks):
            for h in heads:
                sb = state[h].astype(BF16)
                state_in[h][ci] = sb
                state[h] = (state[h] * decay_last[h][ci]
                            - _dot(kt[h][ci][:, HEAD_DIM:].astype(BF16), sb) + kt[h][ci][:, :HEAD_DIM])
        for h in heads:
            inter = []
            for ci in range(n_chunks):
                cr = slice(ci * CHUNK, (ci + 1) * CHUNK)
                lhs = jnp.concatenate([uw[h][cr, HEAD_DIM:], q_dec[h][cr, :]], axis=0).astype(BF16)
                inter.append(_dot(lhs, state_in[h][ci]))
            v_new = jnp.concatenate([uw[h][ci * CHUNK:(ci + 1) * CHUNK, :HEAD_DIM] - inter[ci][:CHUNK]
                                     for ci in range(n_chunks)], axis=0).astype(BF16)
            o = (jnp.concatenate([inter[ci][CHUNK:] for ci in range(n_chunks)], axis=0)
                 + _dot(block_diag(a_cat[h]), v_new))
            o = o * lax.rsqrt(jnp.mean(o * o, axis=-1, keepdims=True) + NORM_EPS)
            z = act_ref[rows, head_cols(COL_GZ, h)].astype(F32)
            o_ref[rows, head_cols(0, h)] = (o * (_silu(z) * gnw_ref[...])).astype(BF16)
        return state

    n_groups = seq_len // (SC * GDN_GROUP)
    group = lambda g: list(range(g * GDN_GROUP, (g + 1) * GDN_GROUP))
    state = [jnp.zeros((HEAD_DIM, HEAD_DIM), F32) for _ in heads]
    preps = prepare(group(0))
    for g in range(n_groups):
        nxt = prepare(group(g + 1)) if g + 1 < n_groups else None
        for sc, prep in zip(group(g), preps):
            state = finish(sc, prep, state)
        preps = nxt


def _gdn(act, gate_t, alog_s, dtb_s, gdn_norm_w, batch, seq_len):
    t = act.shape[0]
    small = lambda shape: pl.BlockSpec(shape, lambda b: (0, 0))
    return pl.pallas_call(
        _gdn_kernel,
        grid=(batch,),
        in_specs=[
            pl.BlockSpec((seq_len, 4 * GDN_WIDTH), lambda b: (b, 0)),
            pl.BlockSpec((N_GATE, seq_len), lambda b: (0, b)),
            small((N_GATE, 1)), small((N_GATE, 1)),
            small((1, HEAD_DIM)),
        ],
        out_specs=pl.BlockSpec((seq_len, GDN_WIDTH), lambda b: (b, 0)),
        out_shape=jax.ShapeDtypeStruct((t, GDN_WIDTH), BF16),
        scratch_shapes=[
            pltpu.VMEM((3, CHUNK, SC), F32),
            pltpu.VMEM((SC, SC), BF16),
            pltpu.VMEM((2, N_GATE, seq_len), F32),
            pltpu.VMEM((seq_len, 3 * N_GATE), F32),
        ],
        compiler_params=pltpu.CompilerParams(
            dimension_semantics=("arbitrary",), vmem_limit_bytes=VMEM_LIMIT),
        name="gdn",
    )(act, gate_t, alog_s, dtb_s, gdn_norm_w)


def _split3(x):
    hi = x.astype(BF16).astype(F32)
    r1 = x - hi
    mid = r1.astype(BF16).astype(F32)
    return hi, mid, r1 - mid


def _attn_kernel(lam_init, q_ref, k_ref, vt_ref, z_ref, lq1_ref, lk1_ref, lq2_ref, lk2_ref,
                 nw_ref, o_ref, bias_ref, kext_ref, vtx_ref):
    h = pl.program_id(0)
    b = pl.program_id(1)
    seq_len = k_ref.shape[0]
    lane = lax.broadcasted_iota(jnp.int32, (1, HEAD_DIM), 1)
    n_bias = 6
    log2e = math.log2(math.e)

    @pl.when(b == 0)
    def _():
        slope = lax.shift_left(jnp.int32(1), 2 * (DIFF_HEADS - 1 - h)).astype(F32) * (log2e / 256.0)
        pos = lax.broadcasted_iota(jnp.int32, (seq_len, 1), 0)
        in_block = (pos & (BK - 1)).astype(F32) * slope
        block_off = (pos & -BK).astype(F32) * slope
        terms = _split3(in_block) + _split3(block_off)
        tile = jnp.zeros((seq_len, HEAD_DIM), F32)
        for j, term in enumerate(terms):
            tile = jnp.where((lane == j) | (lane == DIFF_QK_DIM + j), term, tile)
        bias_ref[...] = tile
        ones_row = lax.broadcasted_iota(jnp.int32, (V_PAD, seq_len), 0) == 0
        vtx_ref[HEAD_DIM:, :] = jnp.where(ones_row, 1.0, 0.0).astype(BF16)

    k = k_ref[...].astype(F32)
    kext_ref[0] = jnp.where(lane < DIFF_QK_DIM, k, bias_ref[...]).astype(BF16)
    kext_ref[1] = jnp.where(lane >= DIFF_QK_DIM, k, bias_ref[...]).astype(BF16)
    vtx_ref[0:HEAD_DIM, :] = vt_ref[...]

    lam = (jnp.exp(jnp.sum(lq1_ref[...] * lk1_ref[...], axis=-1, keepdims=True))
           - jnp.exp(jnp.sum(lq2_ref[...] * lk2_ref[...], axis=-1, keepdims=True)) + lam_init)

    def block_scores(qi):
        q = q_ref[qi * BQ:(qi + 1) * BQ, :].astype(F32) * (DIFF_QK_DIM ** -0.5 * log2e)
        q_ext = (
            jnp.where(lane < DIFF_QK_DIM, q,
                      jnp.where(lane < DIFF_QK_DIM + n_bias, 1.0, 0.0)).astype(BF16),
            jnp.where(lane >= DIFF_QK_DIM, q, jnp.where(lane < n_bias, 1.0, 0.0)).astype(BF16),
        )
        n_keys = (qi + 1) * BQ
        krel = lax.broadcasted_iota(jnp.int32, (BQ, BQ), 0)
        qrel = lax.broadcasted_iota(jnp.int32, (BQ, BQ), 1)
        out = []
        for c in range(2):
            s = _dot_nt(kext_ref[c, 0:n_keys, :], q_ext[c])
            diag = jnp.where(krel <= qrel, s[n_keys - BQ:, :], MASK_VALUE)
            s = jnp.concatenate([s[:n_keys - BQ, :], diag], axis=0) if qi else diag
            out.append((s, jnp.max(s, axis=0, keepdims=True)))
        return out

    n_q = seq_len // BQ
    cur = block_scores(0)
    for qi in range(n_q):
        nxt = block_scores(qi + 1) if qi + 1 < n_q else None
        vt = vtx_ref[:, 0:(qi + 1) * BQ]
        maps = []
        for c in range(2):
            s, s_max = cur[c]
            halves = []
            for lo in range(0, BQ, BQ // 2):
                ql = slice(lo, lo + BQ // 2)
                acc = _dot(vt, jnp.exp2(s[:, ql] - s_max[:, ql]).astype(BF16))
                halves.append(acc[0:HEAD_DIM, :] / acc[HEAD_DIM:HEAD_DIM + 1, :])
            maps.append(jnp.concatenate(halves, axis=1))
        o = (maps[0] - lam * maps[1]).T
        o = o * lax.rsqrt(jnp.mean(o * o, axis=-1, keepdims=True) + NORM_EPS)
        o = o * nw_ref[...] * (1.0 - lam_init)
        qrows = slice(qi * BQ, (qi + 1) * BQ)
        o_ref[qrows, :] = (o * _silu(z_ref[qrows, :].astype(F32))).astype(BF16)
        cur = nxt


def _attn(act, v_t, lq1, lk1, lq2, lk2, diff_norm_w, lam_init, batch, seq_len):
    t = act.shape[0]
    small = lambda shape: pl.BlockSpec(shape, lambda h, b: (0, 0))
    return pl.pallas_call(
        functools.partial(_attn_kernel, lam_init),
        grid=(DIFF_HEADS, batch),
        in_specs=[
            pl.BlockSpec((seq_len, HEAD_DIM), lambda h, b: (b, COL_DQ + h)),
            pl.BlockSpec((seq_len, HEAD_DIM), lambda h, b: (b, COL_DK + h)),
            pl.BlockSpec((HEAD_DIM, seq_len), lambda h, b: (h, b)),
            pl.BlockSpec((seq_len, HEAD_DIM), lambda h, b: (b, COL_DZ + h)),
            small((1, DIFF_QK_DIM)), small((1, DIFF_QK_DIM)),
            small((1, DIFF_QK_DIM)), small((1, DIFF_QK_DIM)),
            small((1, HEAD_DIM)),
        ],
        out_specs=pl.BlockSpec((seq_len, HEAD_DIM), lambda h, b: (b, h)),
        out_shape=jax.ShapeDtypeStruct((t, DIFF_WIDTH), BF16),
        scratch_shapes=[
            pltpu.VMEM((seq_len, HEAD_DIM), F32),
            pltpu.VMEM((2, seq_len, HEAD_DIM), BF16),
            pltpu.VMEM((HEAD_DIM + V_PAD, seq_len), BF16),
        ],
        compiler_params=pltpu.CompilerParams(
            dimension_semantics=("arbitrary", "arbitrary"), vmem_limit_bytes=VMEM_LIMIT),
        name="diffattn",
    )(act, act, v_t, act, lq1, lk1, lq2, lk2, diff_norm_w)


def _outproj_kernel(oa_ref, ob_ref, x_ref, w_ref, fw_ref, out_ref):
    mix = _dot(oa_ref[...], w_ref[0:GDN_WIDTH, :]) + _dot(ob_ref[...], w_ref[GDN_WIDTH:, :])
    y = x_ref[...] + mix
    ms = jnp.mean(y * y, axis=-1, keepdims=True)
    out_ref[...] = y * lax.rsqrt(ms + NORM_EPS) * fw_ref[...]


def _outproj(o_a, o_b, xf, w_out, final_norm_w):
    t = xf.shape[0]
    return pl.pallas_call(
        _outproj_kernel,
        grid=(t // TM_OUT,),
        in_specs=[
            pl.BlockSpec((TM_OUT, GDN_WIDTH), lambda i: (i, 0)),
            pl.BlockSpec((TM_OUT, DIFF_WIDTH), lambda i: (i, 0)),
            pl.BlockSpec((TM_OUT, D_MODEL), lambda i: (i, 0)),
            pl.BlockSpec((D_MODEL, D_MODEL), lambda i: (0, 0)),
            pl.BlockSpec((1, D_MODEL), lambda i: (0, 0)),
        ],
        out_specs=pl.BlockSpec((TM_OUT, D_MODEL), lambda i: (i, 0)),
        out_shape=jax.ShapeDtypeStruct((t, D_MODEL), F32),
        compiler_params=pltpu.CompilerParams(
            dimension_semantics=("arbitrary",), vmem_limit_bytes=VMEM_LIMIT),
        name="outproj",
    )(o_a, o_b, xf, w_out, final_norm_w)


def kernel(x, norm_w, w_in, conv_w, a_log, dt_bias, gdn_norm_w, lambda_q1, lambda_k1,
           lambda_q2, lambda_k2, diff_norm_w, w_out, final_norm_w):
    batch, seq_len, d_model = x.shape
    depth = norm_w.shape[0]
    assert depth == 1 and d_model == D_MODEL
    assert seq_len % TM_IN == 0 and seq_len % SC == 0 and seq_len % BQ == 0 and BQ % BK == 0
    n_wide_a = 4 * GDN_WIDTH
    xf = x.reshape(batch * seq_len, d_model)

    w = w_in[0]
    c_dq = n_wide_a + N_GATE
    c_dv = c_dq + 2 * DIFF_WIDTH
    c_dz = c_dv + DIFF_WIDTH
    w_main = jnp.concatenate([w[:, :n_wide_a], w[:, c_dq:c_dv], w[:, c_dz:]], axis=1).astype(BF16)
    w_v_t = w[:, c_dv:c_dz].T.astype(BF16)
    w_gate_t = w[:, n_wide_a:n_wide_a + N_GATE].T.astype(BF16)

    act, gate_t, v_t = _inproj(xf, norm_w[0][None, :], w_main, w_gate_t, w_v_t, conv_w[0], seq_len)

    pad_s = lambda vec: jnp.pad(vec, (GDN_HEADS, 0))[:, None]
    o_a = _gdn(act, gate_t, pad_s(a_log[0]), pad_s(dt_bias[0]), gdn_norm_w[0][None, :],
               batch, seq_len)

    lam_init = 0.8 - 0.6 * math.exp(-0.3 * 0)
    o_b = _attn(act, v_t, lambda_q1[0][None, :], lambda_k1[0][None, :], lambda_q2[0][None, :],
                lambda_k2[0][None, :], diff_norm_w[0][None, :], lam_init, batch, seq_len)

    out = _outproj(o_a, o_b, xf, w_out[0].astype(BF16), final_norm_w[None, :])
    return out.reshape(batch, seq_len, d_model)
```

```python
import functools
import math

import jax
import jax.numpy as jnp
from jax import lax
from jax.experimental import pallas as pl
from jax.experimental.pallas import tpu as pltpu

F32 = jnp.float32
BF16 = jnp.bfloat16

D_MODEL = 1024
GDN_HEADS = 4
HEAD_DIM = 128
GDN_WIDTH = GDN_HEADS * HEAD_DIM
DIFF_HEADS = 4
DIFF_QK_DIM = 64
DIFF_WIDTH = DIFF_HEADS * HEAD_DIM
CONV_K = 4
NORM_EPS = 1e-6
N_GATE = 2 * GDN_HEADS
N_MAIN = 4 * GDN_WIDTH + 3 * DIFF_WIDTH
N_CONV = 3 * GDN_WIDTH
LANES = 128
MASK_VALUE = -1e30

COL_GQ, COL_GK, COL_GV, COL_GZ = 0, 4, 8, 12
COL_DQ, COL_DK, COL_DZ = 16, 20, 24

TM_IN = 1024
NC_IN = 256
TAIL = 16
TM_OUT = 1024
SC = 256
CHUNK = 64
GDN_GROUP = 2
BQ = 512
BK = 256
V_PAD = 16
VMEM_LIMIT = 48 * 1024 * 1024


def _sigmoid(x):
    return 1.0 / (1.0 + jnp.exp(-x))


def _silu(x):
    half = 0.5 * x
    return half + half * jnp.tanh(half)


def _softplus(x):
    return jnp.maximum(x, 0.0) + jnp.log1p(jnp.exp(-jnp.abs(x)))


def _dot(a, b):
    return jnp.dot(a, b, preferred_element_type=F32)


def _dot_nt(a, b):
    return lax.dot_general(a, b, (((1,), (1,)), ((), ())), preferred_element_type=F32)


def _dot_tn(a, b):
    return lax.dot_general(a, b, (((0,), (0,)), ((), ())), preferred_element_type=F32)


def _inproj_kernel(tiles_per_seq, x_ref, nw_ref, w_ref, wgt_ref, wvt_ref, cw_ref,
                   main_ref, gate_t_ref, vt_ref, tail_ref):
    i = pl.program_id(0)
    x = x_ref[...]
    ms = jnp.mean(x * x, axis=-1, keepdims=True)
    h = (x * lax.rsqrt(ms + NORM_EPS) * nw_ref[...]).astype(BF16)

    gate_t_ref[...] = _dot_nt(wgt_ref[...], h)
    vt_ref[...] = _dot_nt(wvt_ref[...], h).astype(BF16)

    @pl.when(i % tiles_per_seq == 0)
    def _():
        tail_ref[...] = jnp.zeros_like(tail_ref)

    tm = x.shape[0]
    row = lax.broadcasted_iota(jnp.int32, (TAIL, LANES), 0)

    def conv_silu(acc, n0):
        cols = slice(n0, n0 + LANES)
        last = acc[tm - TAIL:, :]
        delta = tail_ref[:, cols] - last
        tail_ref[:, cols] = last
        y = acc * cw_ref[CONV_K - 1:CONV_K, cols]
        fix = jnp.zeros((TAIL, LANES), F32)
        for s in range(1, CONV_K):
            wk = cw_ref[CONV_K - 1 - s:CONV_K - s, cols]
            y = y + pltpu.roll(acc, s, axis=0) * wk
            fix = fix + jnp.where(row < s, pltpu.roll(delta, s, axis=0), 0.0) * wk
        main_ref[:, cols] = _silu(y).astype(BF16)
        main_ref[0:TAIL, cols] = _silu(y[:TAIL, :] + fix).astype(BF16)

    plain = [slice(n0, n0 + NC_IN) for n0 in range(N_CONV, N_MAIN, NC_IN)]
    pending = []
    for n0 in range(0, N_CONV, NC_IN):
        acc = _dot(h, w_ref[:, n0:n0 + NC_IN])
        if pending:
            conv_silu(*pending.pop(0))
        if plain:
            pc = plain.pop(0)
            main_ref[:, pc] = _dot(h, w_ref[:, pc]).astype(BF16)
        if pending:
            conv_silu(*pending.pop(0))
        pending = [(acc[:, j:j + LANES], n0 + j) for j in range(0, NC_IN, LANES)]
    for piece in pending:
        if plain:
            pc = plain.pop(0)
            main_ref[:, pc] = _dot(h, w_ref[:, pc]).astype(BF16)
        conv_silu(*piece)
    for pc in plain:
        main_ref[:, pc] = _dot(h, w_ref[:, pc]).astype(BF16)


def _inproj(xf, norm_w, w_main, w_gate_t, w_v_t, conv_w, seq_len):
    t = xf.shape[0]
    tiles_per_seq = seq_len // TM_IN
    return pl.pallas_call(
        functools.partial(_inproj_kernel, tiles_per_seq),
        grid=(t // TM_IN,),
        in_specs=[
            pl.BlockSpec((TM_IN, D_MODEL), lambda i: (i, 0)),
            pl.BlockSpec((1, D_MODEL), lambda i: (0, 0)),
            pl.BlockSpec((D_MODEL, N_MAIN), lambda i: (0, 0)),
            pl.BlockSpec((N_GATE, D_MODEL), lambda i: (0, 0)),
            pl.BlockSpec((DIFF_WIDTH, D_MODEL), lambda i: (0, 0)),
            pl.BlockSpec((CONV_K, N_CONV), lambda i: (0, 0)),
        ],
        out_specs=[
            pl.BlockSpec((TM_IN, N_MAIN), lambda i: (i, 0)),
            pl.BlockSpec((N_GATE, TM_IN), lambda i: (0, i)),
            pl.BlockSpec((DIFF_WIDTH, TM_IN), lambda i: (0, i)),
        ],
        out_shape=[
            jax.ShapeDtypeStruct((t, N_MAIN), BF16),
            jax.ShapeDtypeStruct((N_GATE, t), F32),
            jax.ShapeDtypeStruct((DIFF_WIDTH, t), BF16),
        ],
        scratch_shapes=[pltpu.VMEM((TAIL, N_CONV), F32)],
        compiler_params=pltpu.CompilerParams(
            dimension_semantics=("arbitrary",), vmem_limit_bytes=VMEM_LIMIT),
        name="inproj",
    )(xf, norm_w, w_main, w_gate_t, w_v_t, conv_w)


def _gdn_kernel(act_ref, gate_t_ref, alog_s_ref, dtb_s_ref, gnw_ref, o_ref, mask_ref, bd_ref,
                row_ref, col_ref):
    seq_len = act_ref.shape[0]
    n_chunks = SC // CHUNK
    heads = range(GDN_HEADS)
    i_cat = lax.broadcasted_iota(jnp.int32, (CHUNK, SC), 0)
    j_cat = lax.broadcasted_iota(jnp.int32, (CHUNK, SC), 1) & (CHUNK - 1)
    mask_ref[0] = jnp.where(i_cat >= j_cat, 0.0, MASK_VALUE)
    mask_ref[1] = jnp.where(i_cat > j_cat, -1.0, 0.0)
    mask_ref[2] = jnp.where(i_cat == j_cat, 1.0, 0.0)
    r = lax.broadcasted_iota(jnp.int32, (SC, SC), 0)
    c = lax.broadcasted_iota(jnp.int32, (SC, SC), 1)
    bd_ref[...] = jnp.where((r & -CHUNK) == (c & -CHUNK), 1.0, 0.0).astype(BF16)
    lane_chunk = lax.broadcasted_iota(jnp.int32, (1, SC), 1) & -CHUNK

    scale = HEAD_DIM ** -0.5

    lane_in_chunk = lax.broadcasted_iota(jnp.int32, (N_GATE, seq_len), 1) & (CHUNK - 1)
    gate_t = gate_t_ref[...]
    beta = _sigmoid(gate_t)
    log_beta = -_softplus(-gate_t)
    g_step = -jnp.exp(alog_s_ref[...]) * _softplus(gate_t + dtb_s_ref[...])
    gc = g_step
    g_after = jnp.zeros_like(g_step)
    tail = g_step
    step = 1
    while step < CHUNK:
        gc = gc + jnp.where(lane_in_chunk >= step, pltpu.roll(gc, step, axis=1), 0.0)
        ahead = jnp.where(lane_in_chunk + step < CHUNK, pltpu.roll(tail, seq_len - step, axis=1), 0.0)
        g_after = g_after + ahead
        tail = tail + ahead
        step *= 2
    e_gc = jnp.exp(gc)
    is_beta_row = lax.broadcasted_iota(jnp.int32, (N_GATE, seq_len), 0) < GDN_HEADS
    swap = pltpu.roll(e_gc, GDN_HEADS, axis=0)
    row_ref[0] = gc
    row_ref[1] = e_gc
    col_ref[...] = jnp.concatenate([
        jnp.where(is_beta_row, beta, gc),
        jnp.where(is_beta_row, log_beta, e_gc),
        jnp.where(is_beta_row, beta * swap, jnp.exp(g_after))], axis=0).T

    def cat(x):
        out = x[(n_chunks - 1) * CHUNK:, :]
        for ci in range(n_chunks - 2, -1, -1):
            out = jnp.where(lane_chunk == ci * CHUNK, x[ci * CHUNK:(ci + 1) * CHUNK, :], out)
        return out

    def block_diag(x_cat):
        xb = x_cat.astype(BF16)
        zero = jnp.zeros((CHUNK, LANES), BF16)
        row_blocks = []
        for ci in range(n_chunks):
            g = ci * CHUNK // LANES
            grp = slice(g * LANES, (g + 1) * LANES)
            part = xb[:, grp] * bd_ref[ci * CHUNK:(ci + 1) * CHUNK, grp]
            row_blocks.append(jnp.concatenate(
                [part if j == g else zero for j in range(SC // LANES)], axis=1))
        return jnp.concatenate(row_blocks, axis=0)

    def head_cols(col, h):
        return slice((col + h) * HEAD_DIM, (col + h + 1) * HEAD_DIM)

    def prepare(steps):
        units = [(sc, h) for sc in steps for h in heads]
        n_pow, a_cat, rhs, q_dec, k_dec, decay_last = [], [], [], [], [], []
        for sc, h in units:
            rows = slice(sc * SC, (sc + 1) * SC)
            col = lambda j: col_ref[rows, j:j + 1]
            q = act_ref[rows, head_cols(COL_GQ, h)].astype(F32)
            k = act_ref[rows, head_cols(COL_GK, h)].astype(F32)
            v = act_ref[rows, head_cols(COL_GV, h)].astype(F32)
            q = q * (lax.rsqrt(jnp.sum(q * q, axis=-1, keepdims=True) + 1e-6) * scale)
            k = k * lax.rsqrt(jnp.sum(k * k, axis=-1, keepdims=True) + 1e-6)
            hb, hg = h, GDN_HEADS + h
            kb = k.astype(BF16)
            gram = _dot_nt(jnp.concatenate([kb, q.astype(BF16)], axis=0), kb)
            e_cat = cat(col(hg)) - row_ref[0, hg:hg + 1, rows] + mask_ref[0]
            n_pow.append(cat(gram[:SC]) * jnp.exp(e_cat + cat(col(N_GATE + hb))) * mask_ref[1])
            a_cat.append(cat(gram[SC:]) * jnp.exp(e_cat))
            rhs.append(jnp.concatenate([col(hb) * v, col(2 * N_GATE + hb) * k], axis=1).astype(BF16))
            q_dec.append(q * col(N_GATE + hg))
            kd = (k * col(2 * N_GATE + hg)).astype(BF16)
            k_dec.append([kd[ci * CHUNK:(ci + 1) * CHUNK, :] for ci in range(n_chunks)])
            decay_last.append([row_ref[1, hg:hg + 1, sc * SC + (ci + 1) * CHUNK - 1:sc * SC + (ci + 1) * CHUNK]
                               for ci in range(n_chunks)])

        every = range(len(units))
        t_cat = [mask_ref[2] + n_pow[u] for u in every]
        n_pow = [_dot(n_pow[u].astype(BF16), block_diag(n_pow[u])) for u in every]
        span = 2
        while span < CHUNK // 2:
            both = [_dot(jnp.concatenate([t_cat[u], n_pow[u]], axis=0).astype(BF16), block_diag(n_pow[u]))
                    for u in every]
            t_cat = [t_cat[u] + both[u][:CHUNK] for u in every]
            n_pow = [both[u][CHUNK:] for u in every]
            span *= 2
        t_cat = [t_cat[u] + _dot(t_cat[u].astype(BF16), block_diag(n_pow[u])) for u in every]

        uw = [_dot(block_diag(t_cat[u]), rhs[u]) for u in every]
        kt = [[_dot_tn(k_dec[u][ci], uw[u][ci * CHUNK:(ci + 1) * CHUNK, :].astype(BF16))
               for ci in range(n_chunks)] for u in every]
        per_step = lambda xs, i: xs[i * GDN_HEADS:(i + 1) * GDN_HEADS]
        return [tuple(per_step(xs, i) for xs in (uw, kt, q_dec, a_cat, decay_last))
                for i in range(len(steps))]

    def finish(sc, prep, state):
        rows = slice(sc * SC, (sc + 1) * SC)
        uw, kt, q_dec, a_cat, decay_last = prep
        state = list(state)
        state_in = [[None] * n_chunks for _ in heads]
        for ci in range(n_chunks):
            for h in heads:
                sb = state[h].astype(BF16)
                state_in[h][ci] = sb
                state[h] = (state[h] * decay_last[h][ci]
                            - _dot(kt[h][ci][:, HEAD_DIM:].astype(BF16), sb) + kt[h][ci][:, :HEAD_DIM])
        for h in heads:
            inter = []
            for ci in range(n_chunks):
                cr = slice(ci * CHUNK, (ci + 1) * CHUNK)
                lhs = jnp.concatenate([uw[h][cr, HEAD_DIM:], q_dec[h][cr, :]], axis=0).astype(BF16)
                inter.append(_dot(lhs, state_in[h][ci]))
            v_new = jnp.concatenate([uw[h][ci * CHUNK:(ci + 1) * CHUNK, :HEAD_DIM] - inter[ci][:CHUNK]
                                     for ci in range(n_chunks)], axis=0).astype(BF16)
            o = (jnp.concatenate([inter[ci][CHUNK:] for ci in range(n_chunks)], axis=0)
                 + _dot(block_diag(a_cat[h]), v_new))
            o = o * lax.rsqrt(jnp.mean(o * o, axis=-1, keepdims=True) + NORM_EPS)
            z = act_ref[rows, head_cols(COL_GZ, h)].astype(F32)
            o_ref[rows, head_cols(0, h)] = (o * (_silu(z) * gnw_ref[...])).astype(BF16)
        return state

    n_groups = seq_len // (SC * GDN_GROUP)
    group = lambda g: list(range(g * GDN_GROUP, (g + 1) * GDN_GROUP))
    state = [jnp.zeros((HEAD_DIM, HEAD_DIM), F32) for _ in heads]
    preps = prepare(group(0))
    for g in range(n_groups):
        nxt = prepare(group(g + 1)) if g + 1 < n_groups else None
        for sc, prep in zip(group(g), preps):
            state = finish(sc, prep, state)
        preps = nxt


def _gdn(act, gate_t, alog_s, dtb_s, gdn_norm_w, batch, seq_len):
    t = act.shape[0]
    small = lambda shape: pl.BlockSpec(shape, lambda b: (0, 0))
    return pl.pallas_call(
        _gdn_kernel,
        grid=(batch,),
        in_specs=[
            pl.BlockSpec((seq_len, 4 * GDN_WIDTH), lambda b: (b, 0)),
            pl.BlockSpec((N_GATE, seq_len), lambda b: (0, b)),
            small((N_GATE, 1)), small((N_GATE, 1)),
            small((1, HEAD_DIM)),
        ],
        out_specs=pl.BlockSpec((seq_len, GDN_WIDTH), lambda b: (b, 0)),
        out_shape=jax.ShapeDtypeStruct((t, GDN_WIDTH), BF16),
        scratch_shapes=[
            pltpu.VMEM((3, CHUNK, SC), F32),
            pltpu.VMEM((SC, SC), BF16),
            pltpu.VMEM((2, N_GATE, seq_len), F32),
            pltpu.VMEM((seq_len, 3 * N_GATE), F32),
        ],
        compiler_params=pltpu.CompilerParams(
            dimension_semantics=("arbitrary",), vmem_limit_bytes=VMEM_LIMIT),
        name="gdn",
    )(act, gate_t, alog_s, dtb_s, gdn_norm_w)


def _split3(x):
    hi = x.astype(BF16).astype(F32)
    r1 = x - hi
    mid = r1.astype(BF16).astype(F32)
    return hi, mid, r1 - mid


def _attn_kernel(lam_init, q_ref, k_ref, vt_ref, z_ref, lq1_ref, lk1_ref, lq2_ref, lk2_ref,
                 nw_ref, o_ref, bias_ref, kext_ref, vtx_ref):
    h = pl.program_id(0)
    b = pl.program_id(1)
    seq_len = k_ref.shape[0]
    lane = lax.broadcasted_iota(jnp.int32, (1, HEAD_DIM), 1)
    n_bias = 6
    log2e = math.log2(math.e)

    @pl.when(b == 0)
    def _():
        slope = lax.shift_left(jnp.int32(1), 2 * (DIFF_HEADS - 1 - h)).astype(F32) * (log2e / 256.0)
        pos = lax.broadcasted_iota(jnp.int32, (seq_len, 1), 0)
        in_block = (pos & (BK - 1)).astype(F32) * slope
        block_off = (pos & -BK).astype(F32) * slope
        terms = _split3(in_block) + _split3(block_off)
        tile = jnp.zeros((seq_len, HEAD_DIM), F32)
        for j, term in enumerate(terms):
            tile = jnp.where((lane == j) | (lane == DIFF_QK_DIM + j), term, tile)
        bias_ref[...] = tile
        ones_row = lax.broadcasted_iota(jnp.int32, (V_PAD, seq_len), 0) == 0
        vtx_ref[HEAD_DIM:, :] = jnp.where(ones_row, 1.0, 0.0).astype(BF16)

    k = k_ref[...].astype(F32)
    kext_ref[0] = jnp.where(lane < DIFF_QK_DIM, k, bias_ref[...]).astype(BF16)
    kext_ref[1] = jnp.where(lane >= DIFF_QK_DIM, k, bias_ref[...]).astype(BF16)
    vtx_ref[0:HEAD_DIM, :] = vt_ref[...]

    lam = (jnp.exp(jnp.sum(lq1_ref[...] * lk1_ref[...], axis=-1, keepdims=True))
           - jnp.exp(jnp.sum(lq2_ref[...] * lk2_ref[...], axis=-1, keepdims=True)) + lam_init)

    def block_scores(qi):
        q = q_ref[qi * BQ:(qi + 1) * BQ, :].astype(F32) * (DIFF_QK_DIM ** -0.5 * log2e)
        q_ext = (
            jnp.where(lane < DIFF_QK_DIM, q,
                      jnp.where(lane < DIFF_QK_DIM + n_bias, 1.0, 0.0)).astype(BF16),
            jnp.where(lane >= DIFF_QK_DIM, q, jnp.where(lane < n_bias, 1.0, 0.0)).astype(BF16),
        )
        n_keys = (qi + 1) * BQ
        krel = lax.broadcasted_iota(jnp.int32, (BQ, BQ), 0)
        qrel = lax.broadcasted_iota(jnp.int32, (BQ, BQ), 1)
        out = []
        for c in range(2):
            s = _dot_nt(kext_ref[c, 0:n_keys, :], q_ext[c])
            diag = jnp.where(krel <= qrel, s[n_keys - BQ:, :], MASK_VALUE)
            s = jnp.concatenate([s[:n_keys - BQ, :], diag], axis=0) if qi else diag
            out.append((s, jnp.max(s, axis=0, keepdims=True)))
        return out

    n_q = seq_len // BQ
    cur = block_scores(0)
    for qi in range(n_q):
        nxt = block_scores(qi + 1) if qi + 1 < n_q else None
        vt = vtx_ref[:, 0:(qi + 1) * BQ]
        maps = []
        for c in range(2):
            s, s_max = cur[c]
            acc = _dot(vt, jnp.exp2(s - s_max).astype(BF16))
            maps.append(acc[0:HEAD_DIM, :] / acc[HEAD_DIM:HEAD_DIM + 1, :])
        o = (maps[0] - lam * maps[1]).T
        o = o * lax.rsqrt(jnp.mean(o * o, axis=-1, keepdims=True) + NORM_EPS)
        o = o * nw_ref[...] * (1.0 - lam_init)
        qrows = slice(qi * BQ, (qi + 1) * BQ)
        o_ref[qrows, :] = (o * _silu(z_ref[qrows, :].astype(F32))).astype(BF16)
        cur = nxt


def _attn(act, v_t, lq1, lk1, lq2, lk2, diff_norm_w, lam_init, batch, seq_len):
    t = act.shape[0]
    small = lambda shape: pl.BlockSpec(shape, lambda h, b: (0, 0))
    return pl.pallas_call(
        functools.partial(_attn_kernel, lam_init),
        grid=(DIFF_HEADS, batch),
        in_specs=[
            pl.BlockSpec((seq_len, HEAD_DIM), lambda h, b: (b, COL_DQ + h)),
            pl.BlockSpec((seq_len, HEAD_DIM), lambda h, b: (b, COL_DK + h)),
            pl.BlockSpec((HEAD_DIM, seq_len), lambda h, b: (h, b)),
            pl.BlockSpec((seq_len, HEAD_DIM), lambda h, b: (b, COL_DZ + h)),
            small((1, DIFF_QK_DIM)), small((1, DIFF_QK_DIM)),
            small((1, DIFF_QK_DIM)), small((1, DIFF_QK_DIM)),
            small((1, HEAD_DIM)),
        ],
        out_specs=pl.BlockSpec((seq_len, HEAD_DIM), lambda h, b: (b, h)),
        out_shape=jax.ShapeDtypeStruct((t, DIFF_WIDTH), BF16),
        scratch_shapes=[
            pltpu.VMEM((seq_len, HEAD_DIM), F32),
            pltpu.VMEM((2, seq_len, HEAD_DIM), BF16),
            pltpu.VMEM((HEAD_DIM + V_PAD, seq_len), BF16),
        ],
        compiler_params=pltpu.CompilerParams(
            dimension_semantics=("arbitrary", "arbitrary"), vmem_limit_bytes=VMEM_LIMIT),
        name="diffattn",
    )(act, act, v_t, act, lq1, lk1, lq2, lk2, diff_norm_w)


def _outproj_kernel(oa_ref, ob_ref, x_ref, w_ref, fw_ref, out_ref):
    mix = _dot(oa_ref[...], w_ref[0:GDN_WIDTH, :]) + _dot(ob_ref[...], w_ref[GDN_WIDTH:, :])
    y = x_ref[...] + mix
    ms = jnp.mean(y * y, axis=-1, keepdims=True)
    out_ref[...] = y * lax.rsqrt(ms + NORM_EPS) * fw_ref[...]


def _outproj(o_a, o_b, xf, w_out, final_norm_w):
    t = xf.shape[0]
    return pl.pallas_call(
        _outproj_kernel,
        grid=(t // TM_OUT,),
        in_specs=[
            pl.BlockSpec((TM_OUT, GDN_WIDTH), lambda i: (i, 0)),
            pl.BlockSpec((TM_OUT, DIFF_WIDTH), lambda i: (i, 0)),
            pl.BlockSpec((TM_OUT, D_MODEL), lambda i: (i, 0)),
            pl.BlockSpec((D_MODEL, D_MODEL), lambda i: (0, 0)),
            pl.BlockSpec((1, D_MODEL), lambda i: (0, 0)),
        ],
        out_specs=pl.BlockSpec((TM_OUT, D_MODEL), lambda i: (i, 0)),
        out_shape=jax.ShapeDtypeStruct((t, D_MODEL), F32),
        compiler_params=pltpu.CompilerParams(
            dimension_semantics=("arbitrary",), vmem_limit_bytes=VMEM_LIMIT),
        name="outproj",
    )(o_a, o_b, xf, w_out, final_norm_w)


def kernel(x, norm_w, w_in, conv_w, a_log, dt_bias, gdn_norm_w, lambda_q1, lambda_k1,
           lambda_q2, lambda_k2, diff_norm_w, w_out, final_norm_w):
    batch, seq_len, d_model = x.shape
    depth = norm_w.shape[0]
    assert depth == 1 and d_model == D_MODEL
    assert seq_len % TM_IN == 0 and seq_len % SC == 0 and seq_len % BQ == 0 and BQ % BK == 0
    n_wide_a = 4 * GDN_WIDTH
    xf = x.reshape(batch * seq_len, d_model)

    w = w_in[0]
    c_dq = n_wide_a + N_GATE
    c_dv = c_dq + 2 * DIFF_WIDTH
    c_dz = c_dv + DIFF_WIDTH
    w_main = jnp.concatenate([w[:, :n_wide_a], w[:, c_dq:c_dv], w[:, c_dz:]], axis=1).astype(BF16)
    w_v_t = w[:, c_dv:c_dz].T.astype(BF16)
    w_gate_t = w[:, n_wide_a:n_wide_a + N_GATE].T.astype(BF16)

    act, gate_t, v_t = _inproj(xf, norm_w[0][None, :], w_main, w_gate_t, w_v_t, conv_w[0], seq_len)

    pad_s = lambda vec: jnp.pad(vec, (GDN_HEADS, 0))[:, None]
    o_a = _gdn(act, gate_t, pad_s(a_log[0]), pad_s(dt_bias[0]), gdn_norm_w[0][None, :],
               batch, seq_len)

    lam_init = 0.8 - 0.6 * math.exp(-0.3 * 0)
    o_b = _attn(act, v_t, lambda_q1[0][None, :], lambda_k1[0][None, :], lambda_q2[0][None, :],
                lambda_k2[0][None, :], diff_norm_w[0][None, :], lam_init, batch, seq_len)

    out = _outproj(o_a, o_b, xf, w_out[0].astype(BF16), final_norm_w[None, :])
    return out.reshape(batch, seq_len, d_model)
```
